```python
import jax, jax.numpy as jnp
from jax import lax
import numpy as np

D_MODEL = 1024
BATCH = 8
SEQ = 2048
DEPTH = 2

GRID_W = 64
CTX_LEN = 256
MIX_W = D_MODEL
FN_W = MIX_W // 4
FN_GROUPS = 4
FN_GD = FN_W // FN_GROUPS
NA_HD = 64
NA_W = 3 * MIX_W // 8
NA_HEADS = NA_W // NA_HD
NA_KH = 8
NA_KW = 16
GLA_HEADS = 4
GLA_V_W = MIX_W - FN_W - NA_W
GLA_DV = GLA_V_W // GLA_HEADS
GLA_QK_W = GLA_V_W // 2
GLA_DK = GLA_QK_W // GLA_HEADS
GLA_RANK = 16
GLA_GATE_NORM = 16.0
GLA_CHUNK = 64
ROPE_BASE = 10000.0
D_FF = ((8 * D_MODEL // 3 + 255) // 256) * 256
N_IN = FN_W + 3 * NA_W + 2 * GLA_QK_W + 2 * GLA_V_W + 2 * GLA_RANK
EPS = 1e-6

kernel_name = "hybrid_fnet_natten_gla_prefix_block"

F32 = jnp.float32


def rms_norm(x, gain):
    xf = x.astype(F32)
    y = xf * lax.rsqrt(jnp.mean(xf * xf, axis=-1, keepdims=True) + EPS)
    return (y * gain.astype(F32)).astype(x.dtype)


def ada_modulate(x, gain, shift, scale):
    return rms_norm(x, gain) * (1.0 + scale) + shift


def split_in(z):
    sizes = (FN_W, NA_W, NA_W, NA_W, GLA_QK_W, GLA_QK_W, GLA_V_W, GLA_V_W, 2 * GLA_RANK)
    points = [int(p) for p in np.cumsum(sizes)[:-1]]
    return jnp.split(z, points, axis=-1)


def rope_axis(x, pos):
    m = x.shape[-1] // 2
    inv = ROPE_BASE ** (-jnp.arange(m, dtype=F32) / m)
    ang = pos[:, None] * inv[None, :]
    cos = jnp.cos(ang)[None, :, None, :]
    sin = jnp.sin(ang)[None, :, None, :]
    x1, x2 = x[..., :m], x[..., m:]
    return jnp.concatenate([x1 * cos - x2 * sin, x2 * cos + x1 * sin], axis=-1)


def rope_2d(x):
    L = x.shape[1]
    t = jnp.arange(L)
    row = (t // GRID_W).astype(F32)
    col = (t % GRID_W).astype(F32)
    h = x.shape[-1] // 2
    xf = x.astype(F32)
    return jnp.concatenate([rope_axis(xf[..., :h], row), rope_axis(xf[..., h:], col)], axis=-1).astype(x.dtype)


def fourier_mix(u, w):
    B, L, _ = u.shape
    z = u.astype(F32).reshape(B, L, FN_GROUPS, FN_GD)
    f = jnp.fft.fftn(z, axes=(1, 3), norm="ortho").real
    return f.reshape(B, L, FN_W).astype(u.dtype) @ w


def na_latent(q, k, v, kc, vc, rpb):
    B, L, H, d = q.shape
    R = L // GRID_W
    kh = min(NA_KH, R)
    scale = d ** -0.5
    to_grid = lambda t: t.reshape(B, R, GRID_W, H, d).transpose(0, 3, 1, 2, 4)
    qg, kg, vg = to_grid(q), to_grid(k), to_grid(v)
    r = jnp.arange(R)
    r0 = jnp.clip(r - kh // 2, 0, R - kh)
    row_idx = r0[:, None] + jnp.arange(kh)[None, :]
    kb = kg[:, :, row_idx]
    vb = vg[:, :, row_idx]
    cq = jnp.arange(GRID_W)
    c0 = jnp.clip(cq - NA_KW // 2, 0, GRID_W - NA_KW)
    col_ok = (cq[None, :] >= c0[:, None]) & (cq[None, :] < c0[:, None] + NA_KW)
    dr = row_idx - r[:, None] + NA_KH - 1
    dc = jnp.clip(cq[None, :] - cq[:, None], -(NA_KW - 1), NA_KW - 1) + NA_KW - 1
    bias = rpb[:, dr[:, None, :, None], dc[None, :, None, :]]
    s_loc = jnp.einsum('bhrqd,bhrakd->bhrqak', qg, kb, preferred_element_type=F32) * scale + bias.astype(F32)
    s_loc = jnp.where(col_ok[:, None, :], s_loc, -jnp.inf)
    s_ctx = jnp.einsum('bhrqd,bnhd->bhrqn', qg, kc, preferred_element_type=F32) * scale
    n_loc = kh * GRID_W
    s = jnp.concatenate([s_loc.reshape(B, H, R, GRID_W, n_loc), s_ctx], axis=-1)
    p = jax.nn.softmax(s, axis=-1).astype(v.dtype)
    p_loc = p[..., :n_loc].reshape(B, H, R, GRID_W, kh, GRID_W)
    o = (jnp.einsum('bhrqak,bhrakd->bhrqd', p_loc, vb)
         + jnp.einsum('bhrqn,bnhd->bhrqd', p[..., n_loc:], vc))
    return o.transpose(0, 2, 3, 1, 4).reshape(B, L, H * d)


def ctx_attn(q, k, v):
    B, N, H, d = q.shape
    s = jnp.einsum('bnhd,bmhd->bhnm', q, k, preferred_element_type=F32) * (d ** -0.5)
    p = jax.nn.softmax(s, axis=-1).astype(v.dtype)
    return jnp.einsum('bhnm,bmhd->bnhd', p, v).reshape(B, N, H * d)


def gla_chunked(q, k, v, glog, s0, need_out):
    B, L, H, dk = k.shape
    dv = v.shape[-1]
    n = L // GLA_CHUNK
    blk = lambda t: t.reshape(B, n, GLA_CHUNK, H, t.shape[-1]).transpose(1, 0, 3, 2, 4)
    kb, vb, gb = blk(k), blk(v), blk(glog)
    b = jnp.cumsum(gb, axis=3)
    b_last = b[:, :, :, -1:, :]
    decay = jnp.exp(b_last)
    k_end = kb * jnp.exp(b_last - b)
    if not need_out:
        def step_state(S, xs):
            ke, vi, dl = xs
            return S * jnp.swapaxes(dl, -1, -2) + jnp.einsum('bhcd,bhcv->bhdv', ke, vi), None
        S, _ = lax.scan(step_state, s0, (k_end, vb, decay))
        return None, S
    qb = blk(q)
    q_in = qb * jnp.exp(b)
    k_in = kb * jnp.exp(-b)
    mask = jnp.tril(jnp.ones((GLA_CHUNK, GLA_CHUNK), dtype=bool))
    att = jnp.where(mask, jnp.einsum('nbhcd,nbhsd->nbhcs', q_in, k_in), 0.0)
    o_intra = jnp.einsum('nbhcs,nbhsv->nbhcv', att, vb)

    def step(S, xs):
        qi, ke, vi, dl = xs
        o = jnp.einsum('bhcd,bhdv->bhcv', qi, S)
        return S * jnp.swapaxes(dl, -1, -2) + jnp.einsum('bhcd,bhcv->bhdv', ke, vi), o

    S, o_inter = lax.scan(step, s0, (q_in, k_end, vb, decay))
    o = (o_intra + o_inter).transpose(1, 0, 3, 2, 4).reshape(B, L, H, dv)
    return o, S


def gla_prep(q, k, v, a, alpha_w, alpha_b, rope):
    B, L, _ = q.shape
    q = q.astype(F32).reshape(B, L, GLA_HEADS, GLA_DK)
    k = k.astype(F32).reshape(B, L, GLA_HEADS, GLA_DK)
    v = v.astype(F32).reshape(B, L, GLA_HEADS, GLA_DV)
    if rope:
        q = rope_2d(q)
        k = rope_2d(k)
    q = q * (GLA_DK ** -0.5)
    a = a.astype(F32).reshape(B, L, 2, GLA_RANK)
    logits = jnp.einsum('bldr,drk->bldk', a, alpha_w.astype(F32)) + alpha_b.astype(F32)
    glog = (jax.nn.log_sigmoid(logits) / GLA_GATE_NORM).reshape(B, L, 2, GLA_HEADS, GLA_DK)
    return q, k, v, glog[:, :, 0], glog[:, :, 1]


def gla_out(o, g, gain):
    B, L = o.shape[:2]
    on = rms_norm(o, gain).reshape(B, L, GLA_V_W)
    return (on * jax.nn.silu(g.astype(F32))).astype(g.dtype)


def gla_mix(zx, zc, alpha_w, alpha_b, o_norm, need_ctx):
    qx, kx, vx, gx, ax = zx
    qc, kc, vc, gc, ac = zc
    q, k, v, gf, gb = gla_prep(qx, kx, vx, ax, alpha_w, alpha_b, rope=True)
    q_c, k_c, v_c, gfc, gbc = gla_prep(qc, kc, vc, ac, alpha_w, alpha_b, rope=False)
    s0 = jnp.zeros((q.shape[0], GLA_HEADS, GLA_DK, GLA_DV), F32)
    rev = lambda t: t[:, ::-1]
    oc_f, sc_f = gla_chunked(q_c, k_c, v_c, gfc, s0, need_ctx)
    oc_b, sc_b = gla_chunked(rev(q_c), rev(k_c), rev(v_c), rev(gbc), s0, need_ctx)
    ox_f, _ = gla_chunked(q, k, v, gf, sc_f, True)
    ox_b, _ = gla_chunked(rev(q), rev(k), rev(v), rev(gb), sc_b, True)
    y_x = gla_out(ox_f + rev(ox_b), gx, o_norm)
    y_c = gla_out(oc_f + rev(oc_b), gc, o_norm) if need_ctx else None
    return y_x, y_c


def swiglu(h, w1, w3, w2):
    return (jax.nn.silu(h @ w1) * (h @ w3)) @ w2


def hybrid_layer(x, cx, c, c_ctx, ada_w, ada_b, norm_mix, norm_ffn, w_in, fnet_w,
                 na_q_norm, na_k_norm, na_rpb, gla_alpha_w, gla_alpha_b, gla_o_norm,
                 w_out, ffn_w1, ffn_w3, ffn_w2, need_ctx):
    mod_x = (jax.nn.silu(c) @ ada_w + ada_b)[:, None, :]
    mod_c = (jax.nn.silu(c_ctx) @ ada_w + ada_b)[None, None, :]
    sh_m, sc_m, g_m, sh_f, sc_f, g_f = jnp.split(mod_x, 6, axis=-1)
    csh_m, csc_m, cg_m, csh_f, csc_f, cg_f = jnp.split(mod_c, 6, axis=-1)

    hx = ada_modulate(x, norm_mix, sh_m, sc_m)
    hc = ada_modulate(cx, norm_mix, csh_m, csc_m)
    fx, nqx, nkx, nvx, gqx, gkx, gvx, ggx, gax = split_in(hx @ w_in)
    fc, nqc, nkc, nvc, gqc, gkc, gvc, ggc, gac = split_in(hc @ w_in)

    heads = lambda t: t.reshape(t.shape[0], t.shape[1], NA_HEADS, NA_HD)
    y_fn = fourier_mix(fx, fnet_w)
    qxh, kxh, vxh = rms_norm(heads(nqx), na_q_norm), rms_norm(heads(nkx), na_k_norm), heads(nvx)
    qch, kch, vch = rms_norm(heads(nqc), na_q_norm), rms_norm(heads(nkc), na_k_norm), heads(nvc)
    y_na = na_latent(qxh, kxh, vxh, kch, vch, na_rpb)
    y_gla, yc_gla = gla_mix((gqx, gkx, gvx, ggx, gax), (gqc, gkc, gvc, ggc, gac),
                            gla_alpha_w, gla_alpha_b, gla_o_norm, need_ctx)

    y = jnp.concatenate([y_fn, y_na, y_gla.astype(y_fn.dtype)], axis=-1) @ w_out
    x = x + (g_m * y).astype(x.dtype)
    x = x + (g_f * swiglu(ada_modulate(x, norm_ffn, sh_f, sc_f), ffn_w1, ffn_w3, ffn_w2)).astype(x.dtype)

    if need_ctx:
        yc = jnp.concatenate([fourier_mix(fc, fnet_w), ctx_attn(qch, kch, vch),
                              yc_gla.astype(fc.dtype)], axis=-1) @ w_out
        cx = cx + (cg_m * yc).astype(cx.dtype)
        cx = cx + (cg_f * swiglu(ada_modulate(cx, norm_ffn, csh_f, csc_f), ffn_w1, ffn_w3, ffn_w2)).astype(cx.dtype)
    return x, cx


def setup_inputs(seed: int = 0) -> dict:
    key = jax.random.key(seed)
    ks = jax.random.split(key, 20)
    nrm = lambda k, shape, s: jax.random.normal(k, shape, F32) * s
    return {
        "x": nrm(ks[0], (BATCH, SEQ, D_MODEL), 1.0),
        "c": nrm(ks[1], (BATCH, D_MODEL), 1.0),
        "ctx": nrm(ks[2], (BATCH, CTX_LEN, D_MODEL), 1.0),
        "c_ctx": nrm(ks[3], (D_MODEL,), 1.0),
        "ada_w": nrm(ks[4], (DEPTH, D_MODEL, 6 * D_MODEL), 0.5 * D_MODEL ** -0.5),
        "ada_b": nrm(ks[5], (DEPTH, 6 * D_MODEL), 0.01),
        "norm_mix": 1.0 + nrm(ks[6], (DEPTH, D_MODEL), 0.02),
        "norm_ffn": 1.0 + nrm(ks[7], (DEPTH, D_MODEL), 0.02),
        "w_in": nrm(ks[8], (DEPTH, D_MODEL, N_IN), D_MODEL ** -0.5),
        "fnet_w": nrm(ks[9], (DEPTH, FN_W, FN_W), FN_W ** -0.5),
        "na_q_norm": 1.0 + nrm(ks[10], (DEPTH, NA_HD), 0.02),
        "na_k_norm": 1.0 + nrm(ks[11], (DEPTH, NA_HD), 0.02),
        "na_rpb": nrm(ks[12], (DEPTH, NA_HEADS, 2 * NA_KH - 1, 2 * NA_KW - 1), 0.1),
        "gla_alpha_w": nrm(ks[13], (DEPTH, 2, GLA_RANK, GLA_QK_W), GLA_RANK ** -0.5),
        "gla_alpha_b": nrm(ks[14], (DEPTH, 2, GLA_QK_W), 0.1),
        "gla_o_norm": 1.0 + nrm(ks[15], (DEPTH, GLA_DV), 0.02),
        "w_out": nrm(ks[16], (DEPTH, MIX_W, D_MODEL), MIX_W ** -0.5),
        "ffn_w1": nrm(ks[17], (DEPTH, D_MODEL, D_FF), D_MODEL ** -0.5),
        "ffn_w3": nrm(ks[18], (DEPTH, D_MODEL, D_FF), D_MODEL ** -0.5),
        "ffn_w2": nrm(ks[19], (DEPTH, D_FF, D_MODEL), D_FF ** -0.5),
    }


def reference(x, c, ctx, c_ctx, ada_w, ada_b, norm_mix, norm_ffn, w_in, fnet_w,
              na_q_norm, na_k_norm, na_rpb, gla_alpha_w, gla_alpha_b, gla_o_norm,
              w_out, ffn_w1, ffn_w3, ffn_w2):
    cx = ctx
    for l in range(DEPTH):
        x, cx = hybrid_layer(x, cx, c, c_ctx, ada_w[l], ada_b[l], norm_mix[l], norm_ffn[l],
                             w_in[l], fnet_w[l], na_q_norm[l], na_k_norm[l], na_rpb[l],
                             gla_alpha_w[l], gla_alpha_b[l], gla_o_norm[l], w_out[l],
                             ffn_w1[l], ffn_w3[l], ffn_w2[l], need_ctx=(l < DEPTH - 1))
    return x
```

```python
import functools

import numpy as np
import jax
import jax.numpy as jnp
from jax import lax
from jax.experimental import pallas as pl
from jax.experimental.pallas import tpu as pltpu

F32 = jnp.float32
BF16 = jnp.bfloat16

D_MODEL = 1024
GRID_W = 64
FN_W = 256
FN_GROUPS = 4
FN_GD = 64
NA_HD = 64
NA_W = 384
NA_HEADS = 6
NA_KH = 8
NA_KW = 16
GLA_HEADS = 4
GLA_DK = 48
GLA_DV = 96
GLA_QK_W = GLA_HEADS * GLA_DK
GLA_V_W = GLA_HEADS * GLA_DV
GLA_RANK = 16
GLA_GATE_NORM = 16.0
GLA_CHUNK = 64
ROPE_BASE = 10000.0
D_FF = 2816
EPS = 1e-6

LANES = 128
MXU_W = 256
VMEM_LIMIT_BYTES = 56 * 1024 * 1024

GLA_DK_PAD = 64
GLA_DV_PAD = 128
GLA_QK_PAD = GLA_HEADS * GLA_DK_PAD
GLA_V_PAD = GLA_HEADS * GLA_DV_PAD
GLA_A_PAD = LANES
MIX_PAD = FN_W + NA_W + GLA_V_PAD

C_FX = 0
C_NQ = C_FX + FN_W
C_NK = C_NQ + NA_W
C_NV = C_NK + NA_W
C_GQ = C_NV + NA_W
C_GK = C_GQ + GLA_QK_PAD
C_GV = C_GK + GLA_QK_PAD
C_GG = C_GV + GLA_V_PAD
C_GA = C_GG + GLA_V_PAD
N_IN_PAD = C_GA + GLA_A_PAD

SUB = 256


def _cparams(sem):
    return pltpu.CompilerParams(dimension_semantics=sem, vmem_limit_bytes=VMEM_LIMIT_BYTES)


def _const_spec(shape):
    nd = len(shape)
    return pl.BlockSpec(shape, lambda *_: (0,) * nd, pipeline_mode=pl.Buffered(1))


def _dot(a, b):
    return jnp.dot(a, b, preferred_element_type=F32)


def _dot_nt(a, b):
    return lax.dot_general(a, b, (((1,), (1,)), ((), ())), preferred_element_type=F32)


def _dot_tn(a, b):
    return lax.dot_general(a, b, (((0,), (0,)), ((), ())), preferred_element_type=F32)


def _ada_kernel(c_ref, w_ref, b_ref, o_ref):
    a = c_ref[...]
    a = (a * jax.nn.sigmoid(a)).astype(BF16)
    o_ref[...] = _dot(a, w_ref[...].astype(BF16)) + b_ref[...]


def _ada_mod(cc, ada_w, ada_b):
    depth, d, n = ada_w.shape
    rows = cc.shape[0]
    tn = 1024
    return pl.pallas_call(
        _ada_kernel,
        grid=(depth, n // tn),
        in_specs=[
            pl.BlockSpec((rows, d), lambda l, j: (0, 0)),
            pl.BlockSpec((None, d, tn), lambda l, j: (l, 0, j)),
            pl.BlockSpec((None, 1, tn), lambda l, j: (l, 0, j)),
        ],
        out_specs=pl.BlockSpec((None, rows, tn), lambda l, j: (l, 0, j)),
        out_shape=jax.ShapeDtypeStruct((depth, rows, n), F32),
        compiler_params=_cparams(("arbitrary", "arbitrary")),
        name="ada_mod",
    )(cc, ada_w, ada_b.reshape(depth, 1, n))


def _modulated_norm(x, gain, shift, scale):
    ms = jnp.mean(x * x, axis=-1, keepdims=True)
    return (x * lax.rsqrt(ms + EPS) * gain) * (1.0 + scale) + shift


def _log_sigmoid(x):
    return jnp.minimum(x, 0.0) - jnp.log1p(jnp.exp(-jnp.abs(x)))


def _inproj_kernel(x_ref, mod_ref, gain_ref, w_ref, cs_ref, bd_ref, qkg_ref, rope_ref,
                   aw_ref, ab_ref, tri_ref,
                   uv_ref, nqkv_ref, gqk_ref, gvg_ref, gb_ref, *, rope, tm):
    x = x_ref[...]
    h = _modulated_norm(x, gain_ref[...], mod_ref[0:1, :], mod_ref[1:2, :]).astype(BF16)

    zf = _dot(h, w_ref[:, C_FX:C_NQ]).astype(BF16)
    uv_ref[...] = _dot(zf, cs_ref[...]).astype(BF16)

    bd = bd_ref[...]
    for i, c0 in enumerate((C_NQ, C_NK)):
        z = _dot(h, w_ref[:, c0:c0 + NA_W])
        ms = _dot((z * z).astype(BF16), bd) * (1.0 / NA_HD)
        z = z * lax.rsqrt(ms + EPS) * qkg_ref[i:i + 1, :]
        nqkv_ref[:, i * NA_W:(i + 1) * NA_W] = z.astype(BF16)
    nqkv_ref[:, 2 * NA_W:3 * NA_W] = _dot(h, w_ref[:, C_NV:C_GQ]).astype(BF16)

    for i, c0 in enumerate((C_GQ, C_GK)):
        z = _dot(h, w_ref[:, c0:c0 + GLA_QK_PAD])
        if rope:
            z = (z * rope_ref[0]
                 + pltpu.roll(z, GLA_QK_PAD - GLA_DK // 4, 1) * rope_ref[1]
                 + pltpu.roll(z, GLA_DK // 4, 1) * rope_ref[2])
        if i == 0:
            z = z * (GLA_DK ** -0.5)
        gqk_ref[:, i * GLA_QK_PAD:(i + 1) * GLA_QK_PAD] = z.astype(BF16)

    gvg_ref[...] = _dot(h, w_ref[:, C_GV:C_GA]).astype(BF16)

    a = _dot(h, w_ref[:, C_GA:N_IN_PAD]).astype(BF16)
    logits = _dot(a, aw_ref[...]) + ab_ref[...]
    glog = (_log_sigmoid(logits) * (1.0 / GLA_GATE_NORM)).astype(BF16)
    for s in range(tm // SUB):
        rows = slice(s * SUB, (s + 1) * SUB)
        gb_ref[rows, 0:GLA_QK_PAD] = _dot(tri_ref[0], glog[rows, 0:GLA_QK_PAD])
        gb_ref[rows, GLA_QK_PAD:] = _dot(tri_ref[1], glog[rows, GLA_QK_PAD:])


def _inproj(x, mod, mod_row, gain, w, consts, rope_tab, *, tm):
    bsz, t, d = x.shape
    rope = rope_tab is not None
    if not rope:
        rope_tab = jnp.zeros((3, tm, GLA_QK_PAD), F32)
    mod_idx = (lambda b, i: (b, 0, 0)) if mod_row is None else (lambda b, i: (mod_row, 0, 0))
    rope_idx = (lambda b, i: (0, i, 0)) if rope else (lambda b, i: (0, 0, 0))
    row = lambda w_: pl.BlockSpec((None, tm, w_), lambda b, i: (b, i, 0))
    out_w = (2 * FN_W, 3 * NA_W, 2 * GLA_QK_PAD, 2 * GLA_V_PAD, 2 * GLA_QK_PAD)
    out_dt = (BF16, BF16, BF16, BF16, F32)
    return pl.pallas_call(
        functools.partial(_inproj_kernel, rope=rope, tm=tm),
        grid=(bsz, t // tm),
        in_specs=[
            row(d),
            pl.BlockSpec((None, 6, d), mod_idx),
            _const_spec((1, d)),
            _const_spec((d, N_IN_PAD)),
            _const_spec(consts["cs"].shape),
            _const_spec(consts["bd"].shape),
            _const_spec((2, NA_W)),
            pl.BlockSpec((3, tm, GLA_QK_PAD), rope_idx),
            _const_spec(consts["aw"].shape),
            _const_spec((1, 2 * GLA_QK_PAD)),
            _const_spec((2, SUB, SUB)),
        ],
        out_specs=[row(w_) for w_ in out_w],
        out_shape=[jax.ShapeDtypeStruct((bsz, t, w_), dt) for w_, dt in zip(out_w, out_dt)],
        compiler_params=_cparams(("arbitrary", "arbitrary")),
        name="inproj_rope" if rope else "inproj_ctx",
    )(x, mod, gain, w, consts["cs"], consts["bd"], consts["qkg"], rope_tab,
      consts["aw"], consts["ab"], consts["tri"])


def _fourier_kernel(tc_ref, ts_ref, uv_ref, w_ref, o_ref, *, scale):
    f = _dot(tc_ref[...], uv_ref[:, 0:FN_W]) - _dot(ts_ref[...], uv_ref[:, FN_W:])
    o_ref[...] = _dot((f * scale).astype(BF16), w_ref[...]).astype(BF16)


def _fourier(uv, tab_c, tab_s, fnet_w, *, tk):
    bsz, t, _ = uv.shape
    scale = float((t * FN_GD) ** -0.5)
    return pl.pallas_call(
        functools.partial(_fourier_kernel, scale=scale),
        grid=(t // tk, bsz),
        in_specs=[
            pl.BlockSpec((tk, t), lambda k, b: (k, 0)),
            pl.BlockSpec((tk, t), lambda k, b: (k, 0)),
            pl.BlockSpec((None, t, 2 * FN_W), lambda k, b: (b, 0, 0)),
            _const_spec((FN_W, FN_W)),
        ],
        out_specs=pl.BlockSpec((None, tk, FN_W), lambda k, b: (b, k, 0)),
        out_shape=jax.ShapeDtypeStruct((bsz, t, FN_W), BF16),
        compiler_params=_cparams(("arbitrary", "arbitrary")),
        name=f"fourier_{t}",
    )(tab_c, tab_s, uv, fnet_w)


def _softmax_pv(parts):
    m = functools.reduce(jnp.maximum, [jnp.max(s, axis=-1, keepdims=True) for s, _ in parts])
    ps = [jnp.exp(s - m) for s, _ in parts]
    l = functools.reduce(jnp.add, [jnp.sum(p, axis=-1, keepdims=True) for p in ps])
    o = functools.reduce(jnp.add, [_dot(p.astype(BF16), v) for p, (_, v) in zip(ps, parts)])
    return o / l


def _na_kernel(q_ref, k_ref, v_ref, qc_ref, kc_ref, vc_ref, bias_ref, o_ref, oc_ref, ksplit_ref,
               *, n_rows, need_ctx):
    lane = lax.broadcasted_iota(jnp.int32, (1, LANES), 1)
    first = lane < NA_HD
    zero = jnp.zeros((), BF16)
    k = k_ref[...]
    ksplit_ref[0] = jnp.where(first, k, zero)
    ksplit_ref[1] = jnp.where(first, zero, k)
    kc = kc_ref[...]
    kc_split = (jnp.where(first, kc, zero), jnp.where(first, zero, kc))
    vc = vc_ref[...]

    def row(r, carry):
        r0 = jnp.clip(r - NA_KH // 2, 0, n_rows - NA_KH)
        var = r - r0
        qs = pl.ds(pl.multiple_of(r * GRID_W, GRID_W), GRID_W)
        ks = pl.ds(pl.multiple_of(r0 * GRID_W, GRID_W), NA_KH * GRID_W)
        q = q_ref[qs, :]
        vb = v_ref[ks, :]
        outs = []
        for hh in range(2):
            s_loc = _dot_nt(q, ksplit_ref[hh, ks, :]) + bias_ref[hh, var]
            s_ctx = _dot_nt(q, kc_split[hh])
            outs.append(_softmax_pv([(s_loc, vb), (s_ctx, vc)]))
        o_ref[qs, :] = jnp.where(first, outs[0], outs[1]).astype(BF16)
        return carry

    lax.fori_loop(0, n_rows, row, 0)

    if need_ctx:
        qc = qc_ref[...]
        outs = [_softmax_pv([(_dot_nt(qc, kc_split[hh]), vc)]) for hh in range(2)]
        oc_ref[...] = jnp.where(first, outs[0], outs[1]).astype(BF16)
    else:
        oc_ref[...] = jnp.zeros(oc_ref.shape, oc_ref.dtype)


def _na(nqkv, nqkv_c, bias, *, need_ctx):
    bsz, t, _ = nqkv.shape
    n = nqkv_c.shape[1]
    n_rows = t // GRID_W
    pairs = NA_HEADS // 2
    nb = NA_W // LANES
    lat = lambda off: pl.BlockSpec((None, t, LANES), lambda j, b: (b, 0, off + j))
    ctx = lambda off: pl.BlockSpec((None, n, LANES), lambda j, b: (b, 0, off + j))
    return pl.pallas_call(
        functools.partial(_na_kernel, n_rows=n_rows, need_ctx=need_ctx),
        grid=(pairs, bsz),
        in_specs=[lat(0), lat(nb), lat(2 * nb), ctx(0), ctx(nb), ctx(2 * nb),
                  pl.BlockSpec((2,) + bias.shape[1:], lambda j, b: (j, 0, 0, 0))],
        out_specs=[pl.BlockSpec((None, t, LANES), lambda j, b: (b, 0, j)),
                   pl.BlockSpec((None, n, LANES), lambda j, b: (b, 0, j))],
        out_shape=[jax.ShapeDtypeStruct((bsz, t, NA_W), BF16),
                   jax.ShapeDtypeStruct((bsz, n, NA_W), BF16)],
        scratch_shapes=[pltpu.VMEM((2, t, LANES), BF16)],
        compiler_params=_cparams(("arbitrary", "arbitrary")),
        name="na_attn",
    )(nqkv, nqkv, nqkv, nqkv_c, nqkv_c, nqkv_c, bias)


def _gla_chunk(qk_ref, vg_ref, b_ref, st_ref, bdmask_ref, out_ref, ci, d):
    c = GLA_CHUNK
    rows = pl.ds(pl.multiple_of(ci * c, c), c)
    q = qk_ref[rows, 0:GLA_QK_PAD].astype(F32)
    k = qk_ref[rows, GLA_QK_PAD:].astype(F32)
    v = vg_ref[rows, 0:GLA_V_PAD]
    b = b_ref[rows, d * GLA_QK_PAD:(d + 1) * GLA_QK_PAD]
    b_tot = b[c - 1:c, :] if d == 0 else b[0:1, :]
    st = st_ref[d]
    if out_ref is not None:
        q_in = (q * jnp.exp(b)).astype(BF16)
        k_in = k * jnp.exp(-b)
        ti = lax.broadcasted_iota(jnp.int32, (c, c), 0)
        si = lax.broadcasted_iota(jnp.int32, (c, c), 1)
        causal = (si <= ti) if d == 0 else (si >= ti)
        head = lax.broadcasted_iota(jnp.int32, (1, GLA_QK_PAD), 1) // GLA_DK_PAD
        intra = []
        for h in range(GLA_HEADS):
            km = jnp.where(head == h, k_in, 0.0).astype(BF16)
            att = jnp.where(causal, _dot_nt(q_in, km), 0.0).astype(BF16)
            intra.append(_dot(att, v[:, h * GLA_DV_PAD:(h + 1) * GLA_DV_PAD]))
        out_ref[rows, :] = _dot_nt(q_in, st.astype(BF16)) + jnp.concatenate(intra, axis=1)
    k_end = (k * jnp.exp(b_tot - b)).astype(BF16)
    st_ref[d] = st * jnp.exp(b_tot) + bdmask_ref[...] * _dot_tn(v, k_end)


def _gla_finish(of_ref, ob_ref, vg_ref, gain_ref, y_ref, t):
    for s in range(t // SUB):
        rows = slice(s * SUB, (s + 1) * SUB)
        for h in range(GLA_HEADS):
            cols = slice(h * GLA_DV_PAD, (h + 1) * GLA_DV_PAD)
            o = of_ref[rows, cols] + ob_ref[rows, cols]
            ms = jnp.sum(o * o, axis=-1, keepdims=True) * (1.0 / GLA_DV)
            on = o * lax.rsqrt(ms + EPS) * gain_ref[:, cols]
            g = vg_ref[rows, GLA_V_PAD + h * GLA_DV_PAD:GLA_V_PAD + (h + 1) * GLA_DV_PAD].astype(F32)
            y_ref[rows, cols] = (on * (g * jax.nn.sigmoid(g))).astype(BF16)


def _gla_kernel(qk_ref, vg_ref, b_ref, qkc_ref, vgc_ref, bc_ref, gain_ref, bdmask_ref,
                y_ref, yc_ref, st_ref, of_ref, ob_ref, ocf_ref, ocb_ref, *, t, n, need_ctx):
    st_ref[...] = jnp.zeros(st_ref.shape, F32)
    nc_ctx = n // GLA_CHUNK
    nc_lat = t // GLA_CHUNK
    for i in range(nc_ctx):
        _gla_chunk(qkc_ref, vgc_ref, bc_ref, st_ref, bdmask_ref, ocf_ref if need_ctx else None, i, 0)
        _gla_chunk(qkc_ref, vgc_ref, bc_ref, st_ref, bdmask_ref, ocb_ref if need_ctx else None,
                   nc_ctx - 1 - i, 1)

    def step(i, carry):
        _gla_chunk(qk_ref, vg_ref, b_ref, st_ref, bdmask_ref, of_ref, i, 0)
        _gla_chunk(qk_ref, vg_ref, b_ref, st_ref, bdmask_ref, ob_ref, nc_lat - 1 - i, 1)
        return carry

    lax.fori_loop(0, nc_lat, step, 0)
    _gla_finish(of_ref, ob_ref, vg_ref, gain_ref, y_ref, t)
    if need_ctx:
        _gla_finish(ocf_ref, ocb_ref, vgc_ref, gain_ref, yc_ref, n)
    else:
        yc_ref[...] = jnp.zeros(yc_ref.shape, yc_ref.dtype)


def _gla(gqk, gvg, gb, gqk_c, gvg_c, gb_c, gain, bdmask, *, need_ctx):
    bsz, t, _ = gqk.shape
    n = gqk_c.shape[1]
    row = lambda rows, w_: pl.BlockSpec((None, rows, w_), lambda b: (b, 0, 0))
    return pl.pallas_call(
        functools.partial(_gla_kernel, t=t, n=n, need_ctx=need_ctx),
        grid=(bsz,),
        in_specs=[row(t, 2 * GLA_QK_PAD), row(t, 2 * GLA_V_PAD), row(t, 2 * GLA_QK_PAD),
                  row(n, 2 * GLA_QK_PAD), row(n, 2 * GLA_V_PAD), row(n, 2 * GLA_QK_PAD),
                  _const_spec((1, GLA_V_PAD)), _const_spec((GLA_V_PAD, GLA_QK_PAD))],
        out_specs=[row(t, GLA_V_PAD), row(n, GLA_V_PAD)],
        out_shape=[jax.ShapeDtypeStruct((bsz, t, GLA_V_PAD), BF16),
                   jax.ShapeDtypeStruct((bsz, n, GLA_V_PAD), BF16)],
        scratch_shapes=[pltpu.VMEM((2, GLA_V_PAD, GLA_QK_PAD), F32),
                        pltpu.VMEM((t, GLA_V_PAD), F32), pltpu.VMEM((t, GLA_V_PAD), F32),
                        pltpu.VMEM((n, GLA_V_PAD), F32), pltpu.VMEM((n, GLA_V_PAD), F32)],
        compiler_params=_cparams(("arbitrary",)),
        name="gla_scan",
    )(gqk, gvg, gb, gqk_c, gvg_c, gb_c, gain, bdmask)


def _post_kernel(x_ref, yfn_ref, yna_ref, ygla_ref, mod_ref, gain_ref, wo_ref, w1_ref, w3_ref, w2_ref,
                 o_ref):
    y = (_dot(yfn_ref[...], wo_ref[0:FN_W, :])
         + _dot(yna_ref[...], wo_ref[FN_W:FN_W + NA_W, :])
         + _dot(ygla_ref[...], wo_ref[FN_W + NA_W:, :]))
    x1 = x_ref[...] + mod_ref[2:3, :] * y
    h = _modulated_norm(x1, gain_ref[...], mod_ref[3:4, :], mod_ref[4:5, :]).astype(BF16)
    acc = jnp.zeros(x1.shape, F32)
    for c in range(D_FF // MXU_W):
        cols = slice(c * MXU_W, (c + 1) * MXU_W)
        u = _dot(h, w1_ref[:, cols])
        g = _dot(h, w3_ref[:, cols])
        a = (u * jax.nn.sigmoid(u) * g).astype(BF16)
        acc = acc + _dot(a, w2_ref[cols, :])
    o_ref[...] = x1 + mod_ref[5:6, :] * acc


def _post(x, yfn, yna, ygla, mod, mod_row, gain, wo, w1, w3, w2, *, tm):
    bsz, t, d = x.shape
    mod_idx = (lambda b, i: (b, 0, 0)) if mod_row is None else (lambda b, i: (mod_row, 0, 0))
    row = lambda w_: pl.BlockSpec((None, tm, w_), lambda b, i: (b, i, 0))
    return pl.pallas_call(
        _post_kernel,
        grid=(bsz, t // tm),
        in_specs=[row(d), row(FN_W), row(NA_W), row(GLA_V_PAD),
                  pl.BlockSpec((None, 6, d), mod_idx),
                  _const_spec((1, d)), _const_spec(wo.shape), _const_spec(w1.shape),
                  _const_spec(w3.shape), _const_spec(w2.shape)],
        out_specs=row(d),
        out_shape=jax.ShapeDtypeStruct((bsz, t, d), F32),
        compiler_params=_cparams(("arbitrary", "arbitrary")),
        name=f"post_{t}",
    )(x, yfn, yna, ygla, mod, gain, wo, w1, w3, w2)


def _dft_tables(n):
    idx = (np.arange(n)[:, None] * np.arange(n)[None, :]) % n
    ang = 2.0 * np.pi * idx / n
    return np.cos(ang), np.sin(ang)


def _shape_consts(t, n):
    c64, s64 = _dft_tables(FN_GD)
    eye = np.eye(FN_GROUPS)
    cs = np.concatenate([np.kron(eye, c64), np.kron(eye, s64)], axis=1)
    bd = np.kron(np.eye(NA_HEADS), np.ones((NA_HD, NA_HD)))
    chunk_id = np.arange(SUB) // GLA_CHUNK
    same = chunk_id[:, None] == chunk_id[None, :]
    pos = np.arange(SUB)
    tri = np.stack([same & (pos[None, :] <= pos[:, None]), same & (pos[None, :] >= pos[:, None])])
    bdmask = (np.arange(GLA_V_PAD)[:, None] // GLA_DV_PAD) == (np.arange(GLA_QK_PAD)[None, :] // GLA_DK_PAD)

    m = GLA_DK // 4
    inv = ROPE_BASE ** (-np.arange(m) / m)
    tok = np.arange(t)
    cos_h = np.ones((t, GLA_DK_PAD))
    sin_lo = np.zeros((t, GLA_DK_PAD))
    sin_hi = np.zeros((t, GLA_DK_PAD))
    for blk, p in enumerate((tok // GRID_W, tok % GRID_W)):
        ang = p[:, None] * inv[None, :]
        o = blk * 2 * m
        cos_h[:, o:o + m] = np.cos(ang)
        cos_h[:, o + m:o + 2 * m] = np.cos(ang)
        sin_lo[:, o:o + m] = -np.sin(ang)
        sin_hi[:, o + m:o + 2 * m] = np.sin(ang)
    rope = np.stack([np.tile(a, (1, GLA_HEADS)) for a in (cos_h, sin_lo, sin_hi)])

    tc, ts = _dft_tables(t)
    tcc, tsc = _dft_tables(n)
    to_bf16 = lambda a: jnp.asarray(a, F32).astype(BF16)
    return {
        "cs": to_bf16(cs), "bd": jnp.asarray(bd, BF16), "tri": jnp.asarray(tri, BF16),
        "bdmask": jnp.asarray(bdmask, F32), "rope": jnp.asarray(rope, F32),
        "tc": to_bf16(tc), "ts": to_bf16(ts), "tcc": to_bf16(tcc), "tsc": to_bf16(tsc),
    }


def _pad_heads(w, heads, dim, pad):
    lead = w.shape[:-1]
    w = w.reshape(lead + (heads, dim))
    w = jnp.pad(w, [(0, 0)] * len(lead) + [(0, 0), (0, pad - dim)])
    return w.reshape(lead + (heads * pad,))


def _layer_params(w_in, na_q_norm, na_k_norm, na_rpb, alpha_w, alpha_b, o_norm, w_out, n_rows):
    o = FN_W + 3 * NA_W
    gq = w_in[:, o:o + GLA_QK_W]
    gk = w_in[:, o + GLA_QK_W:o + 2 * GLA_QK_W]
    o2 = o + 2 * GLA_QK_W
    gv = w_in[:, o2:o2 + GLA_V_W]
    gg = w_in[:, o2 + GLA_V_W:o2 + 2 * GLA_V_W]
    ga = w_in[:, o2 + 2 * GLA_V_W:]
    w = jnp.concatenate([
        w_in[:, :o],
        _pad_heads(gq, GLA_HEADS, GLA_DK, GLA_DK_PAD), _pad_heads(gk, GLA_HEADS, GLA_DK, GLA_DK_PAD),
        _pad_heads(gv, GLA_HEADS, GLA_DV, GLA_DV_PAD), _pad_heads(gg, GLA_HEADS, GLA_DV, GLA_DV_PAD),
        jnp.pad(ga, ((0, 0), (0, GLA_A_PAD - 2 * GLA_RANK))),
    ], axis=1).astype(BF16)

    qkg = jnp.stack([jnp.tile(na_q_norm, NA_HEADS) * (NA_HD ** -0.5), jnp.tile(na_k_norm, NA_HEADS)])

    aw = jnp.zeros((GLA_A_PAD, 2 * GLA_QK_PAD), F32)
    for d in range(2):
        aw = aw.at[d * GLA_RANK:(d + 1) * GLA_RANK, d * GLA_QK_PAD:(d + 1) * GLA_QK_PAD].set(
            _pad_heads(alpha_w[d], GLA_HEADS, GLA_DK, GLA_DK_PAD))
    ab = jnp.concatenate([_pad_heads(alpha_b[d], GLA_HEADS, GLA_DK, GLA_DK_PAD) for d in range(2)])[None, :]

    cq = np.arange(GRID_W)
    c0 = np.clip(cq - NA_KW // 2, 0, GRID_W - NA_KW)
    col_ok = (cq[None, :] >= c0[:, None]) & (cq[None, :] < c0[:, None] + NA_KW)
    dc = np.clip(cq[None, :] - cq[:, None], -(NA_KW - 1), NA_KW - 1) + NA_KW - 1
    kh = min(NA_KH, n_rows)
    var = np.arange(kh)
    dr = np.arange(kh)[None, :] - var[:, None] + NA_KH - 1
    bias = na_rpb[:, dr[:, None, :, None], dc[None, :, None, :]]
    bias = jnp.where(col_ok[None, None, :, None, :], bias, -1e30)
    bias = bias.reshape(NA_HEADS, kh, GRID_W, kh * GRID_W).astype(F32)

    gain_o = _pad_heads(jnp.tile(o_norm, GLA_HEADS), GLA_HEADS, GLA_DV, GLA_DV_PAD)[None, :]
    wo = jnp.concatenate([w_out[:FN_W + NA_W],
                          _pad_heads(w_out[FN_W + NA_W:].T, GLA_HEADS, GLA_DV, GLA_DV_PAD).T]).astype(BF16)
    return {"w": w, "qkg": qkg, "aw": aw.astype(BF16), "ab": ab, "bias": bias, "gain_o": gain_o, "wo": wo}


def kernel(x, c, ctx, c_ctx, ada_w, ada_b, norm_mix, norm_ffn, w_in, fnet_w, na_q_norm, na_k_norm, na_rpb,
           gla_alpha_w, gla_alpha_b, gla_o_norm, w_out, ffn_w1, ffn_w3, ffn_w2):
    bsz, t, d = x.shape
    n = ctx.shape[1]
    depth = ada_w.shape[0]
    ctx_row = bsz
    consts = _shape_consts(t, n)

    cc = jnp.zeros((16, d), F32).at[:bsz].set(c).at[ctx_row].set(c_ctx)
    mod = _ada_mod(cc, ada_w, ada_b).reshape(depth, 16, 6, d)

    cx = ctx
    for l in range(depth):
        need_ctx = l < depth - 1
        p = _layer_params(w_in[l], na_q_norm[l], na_k_norm[l], na_rpb[l], gla_alpha_w[l], gla_alpha_b[l],
                          gla_o_norm[l], w_out[l], t // GRID_W)
        lc = dict(consts, **p)
        gain_m = norm_mix[l][None, :]
        gain_f = norm_ffn[l][None, :]
        fw = fnet_w[l].astype(BF16)
        w1, w3, w2 = ffn_w1[l].astype(BF16), ffn_w3[l].astype(BF16), ffn_w2[l].astype(BF16)

        uv, nqkv, gqk, gvg, gb = _inproj(x, mod[l], None, gain_m, p["w"], lc, consts["rope"], tm=512)
        uv_c, nqkv_c, gqk_c, gvg_c, gb_c = _inproj(cx, mod[l], ctx_row, gain_m, p["w"], lc, None, tm=256)

        y_fn = _fourier(uv, consts["tc"], consts["ts"], fw, tk=1024)
        y_na, y_na_c = _na(nqkv, nqkv_c, p["bias"], need_ctx=need_ctx)
        y_gla, y_gla_c = _gla(gqk, gvg, gb, gqk_c, gvg_c, gb_c, p["gain_o"], consts["bdmask"],
                              need_ctx=need_ctx)
        x = _post(x, y_fn, y_na, y_gla, mod[l], None, gain_f, p["wo"], w1, w3, w2, tm=512)
        if need_ctx:
            y_fn_c = _fourier(uv_c, consts["tcc"], consts["tsc"], fw, tk=n)
            cx = _post(cx, y_fn_c, y_na_c, y_gla_c, mod[l], ctx_row, gain_f, p["wo"], w1, w3, w2, tm=n)
    return x
```

```python
import functools

import numpy as np
import jax
import jax.numpy as jnp
from jax import lax
from jax.experimental import pallas as pl
from jax.experimental.pallas import tpu as pltpu

F32 = jnp.float32
BF16 = jnp.bfloat16

D_MODEL = 1024
GRID_W = 64
FN_W = 256
FN_GROUPS = 4
FN_GD = 64
NA_HD = 64
NA_W = 384
NA_HEADS = 6
NA_KH = 8
NA_KW = 16
NA_GROUP = 4
NA_WIN = NA_GROUP + NA_KH - 1
NA_MASKED = -1e30
GLA_HEADS = 4
GLA_DK = 48
GLA_DV = 96
GLA_QK_W = GLA_HEADS * GLA_DK
GLA_V_W = GLA_HEADS * GLA_DV
GLA_RANK = 16
GLA_GATE_NORM = 16.0
GLA_CHUNK = 64
ROPE_BASE = 10000.0
D_FF = 2816
EPS = 1e-6

LANES = 128
MXU_W = 256
VMEM_LIMIT_BYTES = 56 * 1024 * 1024

GLA_DK_PAD = 64
GLA_DV_PAD = 128
GLA_QK_PAD = GLA_HEADS * GLA_DK_PAD
GLA_V_PAD = GLA_HEADS * GLA_DV_PAD
GLA_A_PAD = LANES
MIX_PAD = FN_W + NA_W + GLA_V_PAD

C_FX = 0
C_NQ = C_FX + FN_W
C_NK = C_NQ + NA_W
C_NV = C_NK + NA_W
C_GQ = C_NV + NA_W
C_GK = C_GQ + GLA_QK_PAD
C_GV = C_GK + GLA_QK_PAD
C_GG = C_GV + GLA_V_PAD
C_GA = C_GG + GLA_V_PAD
N_IN_PAD = C_GA + GLA_A_PAD

SUB = 256


def _cparams(sem):
    return pltpu.CompilerParams(dimension_semantics=sem, vmem_limit_bytes=VMEM_LIMIT_BYTES)


def _const_spec(shape):
    nd = len(shape)
    return pl.BlockSpec(shape, lambda *_: (0,) * nd, pipeline_mode=pl.Buffered(1))


def _dot(a, b):
    return jnp.dot(a, b, preferred_element_type=F32)


def _dot_nt(a, b):
    return lax.dot_general(a, b, (((1,), (1,)), ((), ())), preferred_element_type=F32)


def _dot_tn(a, b):
    return lax.dot_general(a, b, (((0,), (0,)), ((), ())), preferred_element_type=F32)


def _ada_kernel(c_ref, w_ref, b_ref, o_ref):
    a = c_ref[...]
    a = (a * jax.nn.sigmoid(a)).astype(BF16)
    o_ref[...] = _dot(a, w_ref[...].astype(BF16)) + b_ref[...]


def _ada_mod(cc, ada_w, ada_b):
    depth, d, n = ada_w.shape
    rows = cc.shape[0]
    tn = 1024
    return pl.pallas_call(
        _ada_kernel,
        grid=(depth, n // tn),
        in_specs=[
            pl.BlockSpec((rows, d), lambda l, j: (0, 0)),
            pl.BlockSpec((None, d, tn), lambda l, j: (l, 0, j)),
            pl.BlockSpec((None, 1, tn), lambda l, j: (l, 0, j)),
        ],
        out_specs=pl.BlockSpec((None, rows, tn), lambda l, j: (l, 0, j)),
        out_shape=jax.ShapeDtypeStruct((depth, rows, n), F32),
        compiler_params=_cparams(("arbitrary", "arbitrary")),
        name="ada_mod",
    )(cc, ada_w, ada_b.reshape(depth, 1, n))


def _modulated_norm(x, gain, shift, scale):
    ms = jnp.mean(x * x, axis=-1, keepdims=True)
    return (x * lax.rsqrt(ms + EPS) * gain) * (1.0 + scale) + shift


def _log_sigmoid(x):
    return jnp.minimum(x, 0.0) - jnp.log1p(jnp.exp(-jnp.abs(x)))


def _inproj_kernel(x_ref, mod_ref, gain_ref, w_ref, cs_ref, bd_ref, qkg_ref, rope_ref,
                   aw_ref, ab_ref, tri_ref,
                   uv_ref, nqkv_ref, gqk_ref, gvg_ref, gb_ref, *, rope, tm):
    x = x_ref[...]
    h = _modulated_norm(x, gain_ref[...], mod_ref[0:1, :], mod_ref[1:2, :]).astype(BF16)

    zf = _dot(h, w_ref[:, C_FX:C_NQ]).astype(BF16)
    uv_ref[...] = _dot(zf, cs_ref[...]).astype(BF16)

    bd = bd_ref[...]
    for i, c0 in enumerate((C_NQ, C_NK)):
        z = _dot(h, w_ref[:, c0:c0 + NA_W])
        ms = _dot((z * z).astype(BF16), bd) * (1.0 / NA_HD)
        z = z * lax.rsqrt(ms + EPS) * qkg_ref[i:i + 1, :]
        nqkv_ref[:, i * NA_W:(i + 1) * NA_W] = z.astype(BF16)
    nqkv_ref[:, 2 * NA_W:3 * NA_W] = _dot(h, w_ref[:, C_NV:C_GQ]).astype(BF16)

    for i, c0 in enumerate((C_GQ, C_GK)):
        z = _dot(h, w_ref[:, c0:c0 + GLA_QK_PAD])
        if rope:
            z = (z * rope_ref[0]
                 + pltpu.roll(z, GLA_QK_PAD - GLA_DK // 4, 1) * rope_ref[1]
                 + pltpu.roll(z, GLA_DK // 4, 1) * rope_ref[2])
        if i == 0:
            z = z * (GLA_DK ** -0.5)
        gqk_ref[:, i * GLA_QK_PAD:(i + 1) * GLA_QK_PAD] = z.astype(BF16)

    gvg_ref[...] = _dot(h, w_ref[:, C_GV:C_GA]).astype(BF16)

    a = _dot(h, w_ref[:, C_GA:N_IN_PAD]).astype(BF16)
    logits = _dot(a, aw_ref[...]) + ab_ref[...]
    glog = (_log_sigmoid(logits) * (1.0 / GLA_GATE_NORM)).astype(BF16)
    for s in range(tm // SUB):
        rows = slice(s * SUB, (s + 1) * SUB)
        gb_ref[rows, 0:GLA_QK_PAD] = _dot(tri_ref[0], glog[rows, 0:GLA_QK_PAD])
        gb_ref[rows, GLA_QK_PAD:] = _dot(tri_ref[1], glog[rows, GLA_QK_PAD:])


def _inproj(x, mod, mod_row, gain, w, consts, rope_tab, *, tm):
    bsz, t, d = x.shape
    rope = rope_tab is not None
    if not rope:
        rope_tab = jnp.zeros((3, tm, GLA_QK_PAD), F32)
    mod_idx = (lambda b, i: (b, 0, 0)) if mod_row is None else (lambda b, i: (mod_row, 0, 0))
    rope_idx = (lambda b, i: (0, i, 0)) if rope else (lambda b, i: (0, 0, 0))
    row = lambda w_: pl.BlockSpec((None, tm, w_), lambda b, i: (b, i, 0))
    out_w = (2 * FN_W, 3 * NA_W, 2 * GLA_QK_PAD, 2 * GLA_V_PAD, 2 * GLA_QK_PAD)
    out_dt = (BF16, BF16, BF16, BF16, F32)
    return pl.pallas_call(
        functools.partial(_inproj_kernel, rope=rope, tm=tm),
        grid=(bsz, t // tm),
        in_specs=[
            row(d),
            pl.BlockSpec((None, 6, d), mod_idx),
            _const_spec((1, d)),
            _const_spec((d, N_IN_PAD)),
            _const_spec(consts["cs"].shape),
            _const_spec(consts["bd"].shape),
            _const_spec((2, NA_W)),
            pl.BlockSpec((3, tm, GLA_QK_PAD), rope_idx),
            _const_spec(consts["aw"].shape),
            _const_spec((1, 2 * GLA_QK_PAD)),
            _const_spec((2, SUB, SUB)),
        ],
        out_specs=[row(w_) for w_ in out_w],
        out_shape=[jax.ShapeDtypeStruct((bsz, t, w_), dt) for w_, dt in zip(out_w, out_dt)],
        compiler_params=_cparams(("arbitrary", "arbitrary")),
        name="inproj_rope" if rope else "inproj_ctx",
    )(x, mod, gain, w, consts["cs"], consts["bd"], consts["qkg"], rope_tab,
      consts["aw"], consts["ab"], consts["tri"])


def _fourier_kernel(tc_ref, ts_ref, uv_ref, w_ref, o_ref, *, scale):
    f = _dot(tc_ref[...], uv_ref[:, 0:FN_W]) - _dot(ts_ref[...], uv_ref[:, FN_W:])
    o_ref[...] = _dot((f * scale).astype(BF16), w_ref[...]).astype(BF16)


def _fourier(uv, tab_c, tab_s, fnet_w, *, tk):
    bsz, t, _ = uv.shape
    scale = float((t * FN_GD) ** -0.5)
    return pl.pallas_call(
        functools.partial(_fourier_kernel, scale=scale),
        grid=(t // tk, bsz),
        in_specs=[
            pl.BlockSpec((tk, t), lambda k, b: (k, 0)),
            pl.BlockSpec((tk, t), lambda k, b: (k, 0)),
            pl.BlockSpec((None, t, 2 * FN_W), lambda k, b: (b, 0, 0)),
            _const_spec((FN_W, FN_W)),
        ],
        out_specs=pl.BlockSpec((None, tk, FN_W), lambda k, b: (b, k, 0)),
        out_shape=jax.ShapeDtypeStruct((bsz, t, FN_W), BF16),
        compiler_params=_cparams(("arbitrary", "arbitrary")),
        name=f"fourier_{t}",
    )(tab_c, tab_s, uv, fnet_w)


def _softmax_pv(parts):
    m = functools.reduce(jnp.maximum, [jnp.max(s, axis=-1, keepdims=True) for s, _ in parts])
    return functools.reduce(jnp.add, [_dot(jnp.exp(s - m).astype(BF16), v) for s, v in parts])


def _na_heads_out(o0, o1, first):
    o = jnp.where(first, o0, o1)
    l = jnp.where(first, pltpu.roll(o0, NA_HD, 1), pltpu.roll(o1, NA_HD, 1))
    return (o / l).astype(BF16)


def _na_kernel(q_ref, k_ref, v_ref, qc_ref, kc_ref, vc_ref, bias_ref, o_ref, oc_ref, ksplit_ref, vsplit_ref,
               *, n_rows, need_ctx):
    lane = lax.broadcasted_iota(jnp.int32, (1, LANES), 1)
    first = lane < NA_HD
    zero = jnp.zeros((), BF16)
    one = jnp.ones((), BF16)
    k = k_ref[...]
    ksplit_ref[0] = jnp.where(first, k, zero)
    ksplit_ref[1] = jnp.where(first, zero, k)
    v = v_ref[...]
    vsplit_ref[0] = jnp.where(first, v, one)
    vsplit_ref[1] = jnp.where(first, one, v)
    kc = kc_ref[...]
    kc_split = (jnp.where(first, kc, zero), jnp.where(first, zero, kc))
    vc = vc_ref[...]
    vc_split = (jnp.where(first, vc, one), jnp.where(first, one, vc))

    n_groups = n_rows // NA_GROUP
    gq = NA_GROUP * GRID_W

    def group(g, carry):
        ws = jnp.clip(g * NA_GROUP - NA_KH // 2, 0, n_rows - NA_WIN)
        var = jnp.where(g == 0, 0, jnp.where(g == n_groups - 1, 2, 1))
        qs = pl.ds(pl.multiple_of(g * gq, gq), gq)
        ks = pl.ds(pl.multiple_of(ws * GRID_W, GRID_W), NA_WIN * GRID_W)
        q = q_ref[qs, :]
        s_loc = [_dot_nt(q, ksplit_ref[hh, ks, :]) + bias_ref[hh, var] for hh in range(2)]
        s_ctx = [_dot_nt(q, kc_split[hh]) for hh in range(2)]
        outs = [_softmax_pv([(s_loc[hh], vsplit_ref[hh, ks, :]), (s_ctx[hh], vc_split[hh])])
                for hh in range(2)]
        o_ref[qs, :] = _na_heads_out(outs[0], outs[1], first)
        return carry

    lax.fori_loop(0, n_groups, group, 0)

    if need_ctx:
        qc = qc_ref[...]
        outs = [_softmax_pv([(_dot_nt(qc, kc_split[hh]), vc_split[hh])]) for hh in range(2)]
        oc_ref[...] = _na_heads_out(outs[0], outs[1], first)
    else:
        oc_ref[...] = jnp.zeros(oc_ref.shape, oc_ref.dtype)


def _na(nqkv, nqkv_c, bias, *, need_ctx):
    bsz, t, _ = nqkv.shape
    n = nqkv_c.shape[1]
    n_rows = t // GRID_W
    assert n_rows % NA_GROUP == 0 and n_rows >= NA_WIN + NA_GROUP
    pairs = NA_HEADS // 2
    nb = NA_W // LANES
    lat = lambda off: pl.BlockSpec((None, t, LANES), lambda j, b: (b, 0, off + j))
    ctx = lambda off: pl.BlockSpec((None, n, LANES), lambda j, b: (b, 0, off + j))
    return pl.pallas_call(
        functools.partial(_na_kernel, n_rows=n_rows, need_ctx=need_ctx),
        grid=(pairs, bsz),
        in_specs=[lat(0), lat(nb), lat(2 * nb), ctx(0), ctx(nb), ctx(2 * nb),
                  pl.BlockSpec((2,) + bias.shape[1:], lambda j, b: (j, 0, 0, 0))],
        out_specs=[pl.BlockSpec((None, t, LANES), lambda j, b: (b, 0, j)),
                   pl.BlockSpec((None, n, LANES), lambda j, b: (b, 0, j))],
        out_shape=[jax.ShapeDtypeStruct((bsz, t, NA_W), BF16),
                   jax.ShapeDtypeStruct((bsz, n, NA_W), BF16)],
        scratch_shapes=[pltpu.VMEM((2, t, LANES), BF16), pltpu.VMEM((2, t, LANES), BF16)],
        compiler_params=_cparams(("arbitrary", "arbitrary")),
        name="na_attn",
    )(nqkv, nqkv, nqkv, nqkv_c, nqkv_c, nqkv_c, bias)


def _gla_steps(jobs, st_ref, causal_ref, vmask_ref):
    c = GLA_CHUNK
    head = lax.broadcasted_iota(jnp.int32, (1, GLA_QK_PAD), 1) // GLA_DK_PAD
    prep = []
    for qk_ref, vg_ref, b_ref, out_ref, ci, d in jobs:
        rows = pl.ds(pl.multiple_of(ci * c, c), c)
        k = qk_ref[rows, GLA_QK_PAD:].astype(F32)
        v = vg_ref[rows, 0:GLA_V_PAD]
        b = b_ref[rows, d * GLA_QK_PAD:(d + 1) * GLA_QK_PAD]
        decay = jnp.exp(b[c - 1:c, :] if d == 0 else b[0:1, :])
        k_in = k * jnp.exp(-b)
        k_heads = [jnp.where(head == h, k_in, 0.0) for h in range(GLA_HEADS)]
        q_in = None
        if out_ref is not None:
            q_in = (qk_ref[rows, 0:GLA_QK_PAD].astype(F32) * jnp.exp(b)).astype(BF16)
        prep.append((rows, q_in, k_heads, v, decay))

    att, inter = [], []
    for (qk_ref, vg_ref, b_ref, out_ref, ci, d), (rows, q_in, k_heads, v, decay) in zip(jobs, prep):
        if out_ref is None:
            att.append(None)
            inter.append(None)
            continue
        k_stack = jnp.concatenate(k_heads, axis=0).astype(BF16)
        att.append(_dot_nt(q_in, k_stack))
        inter.append(_dot_nt(q_in, st_ref[d].astype(BF16)))

    kv = [[_dot_tn(v[:, h * GLA_DV_PAD:(h + 1) * GLA_DV_PAD], (k_heads[h] * decay).astype(BF16))
           for h in range(GLA_HEADS)] for (rows, q_in, k_heads, v, decay) in prep]

    for j, ((qk_ref, vg_ref, b_ref, out_ref, ci, d), (rows, q_in, k_heads, v, decay)) in enumerate(
            zip(jobs, prep)):
        if out_ref is not None:
            a = (att[j] * causal_ref[d]).astype(BF16)
            v_blk = jnp.concatenate([v] * GLA_HEADS, axis=0) * vmask_ref[...]
            out_ref[rows, :] = inter[j] + _dot(a, v_blk)
        for h in range(GLA_HEADS):
            blk = slice(h * GLA_DV_PAD, (h + 1) * GLA_DV_PAD)
            st_ref[d, blk, :] = st_ref[d, blk, :] * decay + kv[j][h]


def _gla_finish(of_ref, ob_ref, vg_ref, gain_ref, y_ref, t):
    for s in range(t // SUB):
        rows = slice(s * SUB, (s + 1) * SUB)
        for h in range(GLA_HEADS):
            cols = slice(h * GLA_DV_PAD, (h + 1) * GLA_DV_PAD)
            o = of_ref[rows, cols] + ob_ref[rows, cols]
            ms = jnp.sum(o * o, axis=-1, keepdims=True) * (1.0 / GLA_DV)
            on = o * lax.rsqrt(ms + EPS) * gain_ref[:, cols]
            g = vg_ref[rows, GLA_V_PAD + h * GLA_DV_PAD:GLA_V_PAD + (h + 1) * GLA_DV_PAD].astype(F32)
            y_ref[rows, cols] = (on * (g * jax.nn.sigmoid(g))).astype(BF16)


def _gla_kernel(qk_ref, vg_ref, b_ref, qkc_ref, vgc_ref, bc_ref, gain_ref, causal_ref, vmask_ref,
                y_ref, yc_ref, st_ref, of_ref, ob_ref, ocf_ref, ocb_ref, *, t, n, need_ctx):
    st_ref[...] = jnp.zeros(st_ref.shape, F32)
    nc_ctx = n // GLA_CHUNK
    nc_lat = t // GLA_CHUNK
    for i in range(nc_ctx):
        _gla_steps([(qkc_ref, vgc_ref, bc_ref, ocf_ref if need_ctx else None, i, 0),
                    (qkc_ref, vgc_ref, bc_ref, ocb_ref if need_ctx else None, nc_ctx - 1 - i, 1)],
                   st_ref, causal_ref, vmask_ref)

    def step(i, carry):
        _gla_steps([(qk_ref, vg_ref, b_ref, of_ref, i, 0),
                    (qk_ref, vg_ref, b_ref, ob_ref, nc_lat - 1 - i, 1)],
                   st_ref, causal_ref, vmask_ref)
        return carry

    lax.fori_loop(0, nc_lat, step, 0)
    _gla_finish(of_ref, ob_ref, vg_ref, gain_ref, y_ref, t)
    if need_ctx:
        _gla_finish(ocf_ref, ocb_ref, vgc_ref, gain_ref, yc_ref, n)
    else:
        yc_ref[...] = jnp.zeros(yc_ref.shape, yc_ref.dtype)


def _gla(gqk, gvg, gb, gqk_c, gvg_c, gb_c, gain, causal, vmask, *, need_ctx):
    bsz, t, _ = gqk.shape
    n = gqk_c.shape[1]
    row = lambda rows, w_: pl.BlockSpec((None, rows, w_), lambda b: (b, 0, 0))
    return pl.pallas_call(
        functools.partial(_gla_kernel, t=t, n=n, need_ctx=need_ctx),
        grid=(bsz,),
        in_specs=[row(t, 2 * GLA_QK_PAD), row(t, 2 * GLA_V_PAD), row(t, 2 * GLA_QK_PAD),
                  row(n, 2 * GLA_QK_PAD), row(n, 2 * GLA_V_PAD), row(n, 2 * GLA_QK_PAD),
                  _const_spec((1, GLA_V_PAD)), _const_spec(causal.shape), _const_spec(vmask.shape)],
        out_specs=[row(t, GLA_V_PAD), row(n, GLA_V_PAD)],
        out_shape=[jax.ShapeDtypeStruct((bsz, t, GLA_V_PAD), BF16),
                   jax.ShapeDtypeStruct((bsz, n, GLA_V_PAD), BF16)],
        scratch_shapes=[pltpu.VMEM((2, GLA_V_PAD, GLA_QK_PAD), F32),
                        pltpu.VMEM((t, GLA_V_PAD), F32), pltpu.VMEM((t, GLA_V_PAD), F32),
                        pltpu.VMEM((n, GLA_V_PAD), F32), pltpu.VMEM((n, GLA_V_PAD), F32)],
        compiler_params=_cparams(("arbitrary",)),
        name="gla_scan",
    )(gqk, gvg, gb, gqk_c, gvg_c, gb_c, gain, causal, vmask)


def _post_kernel(x_ref, yfn_ref, yna_ref, ygla_ref, mod_ref, gain_ref, wo_ref, w1_ref, w3_ref, w2_ref,
                 o_ref):
    y = (_dot(yfn_ref[...], wo_ref[0:FN_W, :])
         + _dot(yna_ref[...], wo_ref[FN_W:FN_W + NA_W, :])
         + _dot(ygla_ref[...], wo_ref[FN_W + NA_W:, :]))
    x1 = x_ref[...] + mod_ref[2:3, :] * y
    h = _modulated_norm(x1, gain_ref[...], mod_ref[3:4, :], mod_ref[4:5, :]).astype(BF16)
    acc = jnp.zeros(x1.shape, F32)
    for c in range(D_FF // MXU_W):
        cols = slice(c * MXU_W, (c + 1) * MXU_W)
        u = _dot(h, w1_ref[:, cols])
        g = _dot(h, w3_ref[:, cols])
        a = (u * jax.nn.sigmoid(u) * g).astype(BF16)
        acc = acc + _dot(a, w2_ref[cols, :])
    o_ref[...] = x1 + mod_ref[5:6, :] * acc


def _post(x, yfn, yna, ygla, mod, mod_row, gain, wo, w1, w3, w2, *, tm):
    bsz, t, d = x.shape
    mod_idx = (lambda b, i: (b, 0, 0)) if mod_row is None else (lambda b, i: (mod_row, 0, 0))
    row = lambda w_: pl.BlockSpec((None, tm, w_), lambda b, i: (b, i, 0))
    return pl.pallas_call(
        _post_kernel,
        grid=(bsz, t // tm),
        in_specs=[row(d), row(FN_W), row(NA_W), row(GLA_V_PAD),
                  pl.BlockSpec((None, 6, d), mod_idx),
                  _const_spec((1, d)), _const_spec(wo.shape), _const_spec(w1.shape),
                  _const_spec(w3.shape), _const_spec(w2.shape)],
        out_specs=row(d),
        out_shape=jax.ShapeDtypeStruct((bsz, t, d), F32),
        compiler_params=_cparams(("arbitrary", "arbitrary")),
        name=f"post_{t}",
    )(x, yfn, yna, ygla, mod, gain, wo, w1, w3, w2)


def _dft_tables(n):
    idx = (np.arange(n)[:, None] * np.arange(n)[None, :]) % n
    ang = 2.0 * np.pi * idx / n
    return np.cos(ang), np.sin(ang)


def _shape_consts(t, n):
    c64, s64 = _dft_tables(FN_GD)
    eye = np.eye(FN_GROUPS)
    cs = np.concatenate([np.kron(eye, c64), np.kron(eye, s64)], axis=1)
    bd = np.kron(np.eye(NA_HEADS), np.ones((NA_HD, NA_HD)))
    chunk_id = np.arange(SUB) // GLA_CHUNK
    same = chunk_id[:, None] == chunk_id[None, :]
    pos = np.arange(SUB)
    tri = np.stack([same & (pos[None, :] <= pos[:, None]), same & (pos[None, :] >= pos[:, None])])
    cpos = np.arange(GLA_CHUNK)
    fwd = cpos[None, :] <= cpos[:, None]
    gla_causal = np.stack([np.tile(fwd, (1, GLA_HEADS)), np.tile(fwd.T, (1, GLA_HEADS))])
    gla_vmask = (np.arange(GLA_HEADS * GLA_CHUNK)[:, None] // GLA_CHUNK) == (np.arange(GLA_V_PAD)[None, :] // GLA_DV_PAD)

    m = GLA_DK // 4
    inv = ROPE_BASE ** (-np.arange(m) / m)
    tok = np.arange(t)
    cos_h = np.ones((t, GLA_DK_PAD))
    sin_lo = np.zeros((t, GLA_DK_PAD))
    sin_hi = np.zeros((t, GLA_DK_PAD))
    for blk, p in enumerate((tok // GRID_W, tok % GRID_W)):
        ang = p[:, None] * inv[None, :]
        o = blk * 2 * m
        cos_h[:, o:o + m] = np.cos(ang)
        cos_h[:, o + m:o + 2 * m] = np.cos(ang)
        sin_lo[:, o:o + m] = -np.sin(ang)
        sin_hi[:, o + m:o + 2 * m] = np.sin(ang)
    rope = np.stack([np.tile(a, (1, GLA_HEADS)) for a in (cos_h, sin_lo, sin_hi)])

    tc, ts = _dft_tables(t)
    tcc, tsc = _dft_tables(n)
    to_bf16 = lambda a: jnp.asarray(a, F32).astype(BF16)
    return {
        "cs": to_bf16(cs), "bd": jnp.asarray(bd, BF16), "tri": jnp.asarray(tri, BF16),
        "gla_causal": jnp.asarray(gla_causal, F32), "gla_vmask": jnp.asarray(gla_vmask, BF16),
        "rope": jnp.asarray(rope, F32),
        "tc": to_bf16(tc), "ts": to_bf16(ts), "tcc": to_bf16(tcc), "tsc": to_bf16(tsc),
    }


def _bias_kernel(rpb_ref, onehot_ref, ok_ref, o_ref):
    r = rpb_ref[...]
    onehot = onehot_ref[...]
    t = jnp.zeros(o_ref.shape, F32)
    for _ in range(3):
        piece = r.astype(BF16)
        t = t + _dot(piece, onehot)
        r = r - piece.astype(F32)
    o_ref[...] = jnp.where(ok_ref[...] > 0.0, t, NA_MASKED)


def _window_bias(na_rpb):
    heads, n_dr, n_dc = na_rpb.shape
    cq = np.arange(GRID_W)
    c0 = np.clip(cq - NA_KW // 2, 0, GRID_W - NA_KW)
    col_ok = (cq[None, :] >= c0[:, None]) & (cq[None, :] < c0[:, None] + NA_KW)
    dc = np.clip(cq[None, :] - cq[:, None], -(NA_KW - 1), NA_KW - 1) + NA_KW - 1
    n_dc_pad = n_dc + 1
    rows_pad = -(-heads * n_dr // 8) * 8
    onehot = (np.arange(n_dc_pad)[:, None, None] == dc[None]).reshape(n_dc_pad, GRID_W * GRID_W)
    rpb2 = jnp.pad(na_rpb.reshape(heads * n_dr, n_dc), ((0, rows_pad - heads * n_dr), (0, 1)))
    t = pl.pallas_call(
        _bias_kernel,
        out_shape=jax.ShapeDtypeStruct((rows_pad, GRID_W * GRID_W), F32),
        name="na_bias",
    )(rpb2, jnp.asarray(onehot, BF16), jnp.asarray(col_ok.reshape(1, -1), F32))
    t = t[:heads * n_dr].reshape(heads, n_dr, GRID_W, GRID_W)
    t = t.transpose(0, 2, 1, 3).reshape(heads, GRID_W, n_dr * GRID_W)
    kinds = []
    for q_off, a_lo_of in ((0, lambda i: 0), (NA_KH // 2, lambda i: i), (NA_WIN - NA_GROUP, lambda i: NA_WIN - NA_KH)):
        rows = []
        for i in range(NA_GROUP):
            a_lo = a_lo_of(i)
            dr_lo = a_lo - q_off - i + NA_KH - 1
            band = t[:, :, dr_lo * GRID_W:(dr_lo + NA_KH) * GRID_W]
            rows.append(jnp.pad(band, ((0, 0), (0, 0), (a_lo * GRID_W, (NA_WIN - NA_KH - a_lo) * GRID_W)),
                                constant_values=NA_MASKED))
        kinds.append(jnp.concatenate(rows, axis=1))
    return jnp.stack(kinds, axis=1)


def _pad_heads(w, heads, dim, pad):
    lead = w.shape[:-1]
    w = w.reshape(lead + (heads, dim))
    w = jnp.pad(w, [(0, 0)] * len(lead) + [(0, 0), (0, pad - dim)])
    return w.reshape(lead + (heads * pad,))


def _layer_params(w_in, na_q_norm, na_k_norm, na_rpb, alpha_w, alpha_b, o_norm, w_out):
    o = FN_W + 3 * NA_W
    gq = w_in[:, o:o + GLA_QK_W]
    gk = w_in[:, o + GLA_QK_W:o + 2 * GLA_QK_W]
    o2 = o + 2 * GLA_QK_W
    gv = w_in[:, o2:o2 + GLA_V_W]
    gg = w_in[:, o2 + GLA_V_W:o2 + 2 * GLA_V_W]
    ga = w_in[:, o2 + 2 * GLA_V_W:]
    w = jnp.concatenate([
        w_in[:, :o],
        _pad_heads(gq, GLA_HEADS, GLA_DK, GLA_DK_PAD), _pad_heads(gk, GLA_HEADS, GLA_DK, GLA_DK_PAD),
        _pad_heads(gv, GLA_HEADS, GLA_DV, GLA_DV_PAD), _pad_heads(gg, GLA_HEADS, GLA_DV, GLA_DV_PAD),
        jnp.pad(ga, ((0, 0), (0, GLA_A_PAD - 2 * GLA_RANK))),
    ], axis=1).astype(BF16)

    qkg = jnp.stack([jnp.tile(na_q_norm, NA_HEADS) * (NA_HD ** -0.5), jnp.tile(na_k_norm, NA_HEADS)])

    aw = jnp.zeros((GLA_A_PAD, 2 * GLA_QK_PAD), F32)
    for d in range(2):
        aw = aw.at[d * GLA_RANK:(d + 1) * GLA_RANK, d * GLA_QK_PAD:(d + 1) * GLA_QK_PAD].set(
            _pad_heads(alpha_w[d], GLA_HEADS, GLA_DK, GLA_DK_PAD))
    ab = jnp.concatenate([_pad_heads(alpha_b[d], GLA_HEADS, GLA_DK, GLA_DK_PAD) for d in range(2)])[None, :]

    bias = _window_bias(na_rpb)

    gain_o = _pad_heads(jnp.tile(o_norm, GLA_HEADS), GLA_HEADS, GLA_DV, GLA_DV_PAD)[None, :]
    wo = jnp.concatenate([w_out[:FN_W + NA_W],
                          _pad_heads(w_out[FN_W + NA_W:].T, GLA_HEADS, GLA_DV, GLA_DV_PAD).T]).astype(BF16)
    return {"w": w, "qkg": qkg, "aw": aw.astype(BF16), "ab": ab, "bias": bias, "gain_o": gain_o, "wo": wo}


def kernel(x, c, ctx, c_ctx, ada_w, ada_b, norm_mix, norm_ffn, w_in, fnet_w, na_q_norm, na_k_norm, na_rpb,
           gla_alpha_w, gla_alpha_b, gla_o_norm, w_out, ffn_w1, ffn_w3, ffn_w2):
    bsz, t, d = x.shape
    n = ctx.shape[1]
    depth = ada_w.shape[0]
    ctx_row = bsz
    consts = _shape_consts(t, n)

    cc = jnp.zeros((16, d), F32).at[:bsz].set(c).at[ctx_row].set(c_ctx)
    mod = _ada_mod(cc, ada_w, ada_b).reshape(depth, 16, 6, d)

    cx = ctx
    for l in range(depth):
        need_ctx = l < depth - 1
        p = _layer_params(w_in[l], na_q_norm[l], na_k_norm[l], na_rpb[l], gla_alpha_w[l], gla_alpha_b[l],
                          gla_o_norm[l], w_out[l])
        lc = dict(consts, **p)
        gain_m = norm_mix[l][None, :]
        gain_f = norm_ffn[l][None, :]
        fw = fnet_w[l].astype(BF16)
        w1, w3, w2 = ffn_w1[l].astype(BF16), ffn_w3[l].astype(BF16), ffn_w2[l].astype(BF16)

        uv, nqkv, gqk, gvg, gb = _inproj(x, mod[l], None, gain_m, p["w"], lc, consts["rope"], tm=512)
        uv_c, nqkv_c, gqk_c, gvg_c, gb_c = _inproj(cx, mod[l], ctx_row, gain_m, p["w"], lc, None, tm=256)

        y_fn = _fourier(uv, consts["tc"], consts["ts"], fw, tk=1024)
        y_na, y_na_c = _na(nqkv, nqkv_c, p["bias"], need_ctx=need_ctx)
        y_gla, y_gla_c = _gla(gqk, gvg, gb, gqk_c, gvg_c, gb_c, p["gain_o"],
                              consts["gla_causal"], consts["gla_vmask"],
                              need_ctx=need_ctx)
        x = _post(x, y_fn, y_na, y_gla, mod[l], None, gain_f, p["wo"], w1, w3, w2, tm=512)
        if need_ctx:
            y_fn_c = _fourier(uv_c, consts["tcc"], consts["tsc"], fw, tk=n)
            cx = _post(cx, y_fn_c, y_na_c, y_gla_c, mod[l], ctx_row, gain_f, p["wo"], w1, w3, w2, tm=n)
    return x
```

```python
import functools

import numpy as np
import jax
import jax.numpy as jnp
from jax import lax
from jax.experimental import pallas as pl
from jax.experimental.pallas import tpu as pltpu

F32 = jnp.float32
BF16 = jnp.bfloat16

D_MODEL = 1024
GRID_W = 64
FN_W = 256
FN_GROUPS = 4
FN_GD = 64
NA_HD = 64
NA_W = 384
NA_HEADS = 6
NA_KH = 8
NA_KW = 16
NA_GROUP = 4
NA_WIN = NA_GROUP + NA_KH - 1
NA_MASKED = -1e30
GLA_HEADS = 4
GLA_DK = 48
GLA_DV = 96
GLA_QK_W = GLA_HEADS * GLA_DK
GLA_V_W = GLA_HEADS * GLA_DV
GLA_RANK = 16
GLA_GATE_NORM = 16.0
GLA_CHUNK = 64
ROPE_BASE = 10000.0
D_FF = 2816
EPS = 1e-6
MOD_ROWS = 16

LANES = 128
MXU_W = 256
VMEM_LIMIT_BYTES = 56 * 1024 * 1024

GLA_DK_PAD = 64
GLA_DV_PAD = 128
GLA_QK_PAD = GLA_HEADS * GLA_DK_PAD
GLA_V_PAD = GLA_HEADS * GLA_DV_PAD
GLA_A_PAD = LANES
MIX_PAD = FN_W + NA_W + GLA_V_PAD

C_FX = 0
C_NQ = C_FX + FN_W
C_NK = C_NQ + NA_W
C_NV = C_NK + NA_W
C_GA = C_NV + NA_W
C_GQ = C_GA + GLA_A_PAD
C_GK = C_GQ + GLA_QK_PAD
C_GV = C_GK + GLA_QK_PAD
C_GG = C_GV + GLA_V_PAD
N_IN_PAD = C_GG + GLA_V_PAD

SUB = 256


def _cparams(sem):
    return pltpu.CompilerParams(dimension_semantics=sem, vmem_limit_bytes=VMEM_LIMIT_BYTES)


def _const_spec(shape):
    nd = len(shape)
    return pl.BlockSpec(tuple(shape), lambda *_: (0,) * nd, pipeline_mode=pl.Buffered(1))


def _layer_spec(shape, l):
    nd = len(shape)
    return pl.BlockSpec((None,) + tuple(shape), lambda *_: (l,) + (0,) * nd, pipeline_mode=pl.Buffered(1))


def _mod_spec(d, l, mod_row):
    if mod_row is None:
        return pl.BlockSpec((None, None, 6, d), lambda b, i: (l, b, 0, 0))
    return pl.BlockSpec((None, None, 6, d), lambda b, i: (l, mod_row, 0, 0))


def _dot(a, b):
    return jnp.dot(a, b, preferred_element_type=F32)


def _dot_nt(a, b):
    return lax.dot_general(a, b, (((1,), (1,)), ((), ())), preferred_element_type=F32)


def _dot_tn(a, b):
    return lax.dot_general(a, b, (((0,), (0,)), ((), ())), preferred_element_type=F32)


def _ada_kernel(c_ref, w_ref, b_ref, o_ref):
    a = c_ref[...]
    a = (a * jax.nn.sigmoid(a)).astype(BF16)
    o_ref[...] = _dot(a, w_ref[...].astype(BF16)) + b_ref[...]


def _ada_mod(cc, ada_w, ada_b):
    depth, d, n = ada_w.shape
    rows = cc.shape[0]
    tn = 1024
    return pl.pallas_call(
        _ada_kernel,
        grid=(depth, n // tn),
        in_specs=[
            pl.BlockSpec((rows, d), lambda l, j: (0, 0)),
            pl.BlockSpec((None, d, tn), lambda l, j: (l, 0, j)),
            pl.BlockSpec((None, 1, tn), lambda l, j: (l, 0, j)),
        ],
        out_specs=pl.BlockSpec((None, rows, tn), lambda l, j: (l, 0, j)),
        out_shape=jax.ShapeDtypeStruct((depth, rows, n), F32),
        compiler_params=_cparams(("arbitrary", "arbitrary")),
        name="ada_mod",
    )(cc, ada_w, ada_b.reshape(depth, 1, n))


def _modulated_norm(x, gain, shift, scale):
    ms = jnp.mean(x * x, axis=-1, keepdims=True)
    return (x * lax.rsqrt(ms + EPS) * gain) * (1.0 + scale) + shift


def _log_sigmoid(x):
    return jnp.minimum(x, 0.0) - jnp.log1p(jnp.exp(-jnp.abs(x)))


def _head_mean_sq(z):
    first = lax.broadcasted_iota(jnp.int32, (1, LANES), 1) < NA_HD
    out = []
    for t in range(NA_W // LANES):
        sq = z[:, t * LANES:(t + 1) * LANES]
        sq = sq * sq
        lo = jnp.sum(jnp.where(first, sq, 0.0), axis=-1, keepdims=True)
        hi = jnp.sum(jnp.where(first, 0.0, sq), axis=-1, keepdims=True)
        out.append(jnp.where(first, lo, hi))
    return jnp.concatenate(out, axis=1) * (1.0 / NA_HD)


def _inproj_kernel(x_ref, mod_ref, gain_ref, w_ref, cs_ref, qkg_ref, rope_ref, aw_ref, ab_ref, tri_ref,
                   uv_ref, nqkv_ref, gqk_ref, gvg_ref, gb_ref, *, rope, tm):
    x = x_ref[...]
    h = _modulated_norm(x, gain_ref[...], mod_ref[0:1, :], mod_ref[1:2, :]).astype(BF16)

    zf = _dot(h, w_ref[:, C_FX:C_NQ]).astype(BF16)
    zva = _dot(h, w_ref[:, C_NV:C_GQ])
    nqkv_ref[:, 2 * NA_W:3 * NA_W] = zva[:, :NA_W].astype(BF16)
    a = zva[:, NA_W:].astype(BF16)
    zqk = _dot(h, w_ref[:, C_NQ:C_NV])
    logits = _dot(a, aw_ref[...]) + ab_ref[...]
    zg = _dot(h, w_ref[:, C_GQ:C_GV])
    uv_ref[...] = _dot(zf, cs_ref[...]).astype(BF16)
    gvg_ref[...] = _dot(h, w_ref[:, C_GV:N_IN_PAD]).astype(BF16)

    glog = (_log_sigmoid(logits) * (1.0 / GLA_GATE_NORM)).astype(BF16)
    for s in range(tm // SUB):
        rows = slice(s * SUB, (s + 1) * SUB)
        gb_ref[rows, 0:GLA_QK_PAD] = _dot(tri_ref[0], glog[rows, 0:GLA_QK_PAD])
        gb_ref[rows, GLA_QK_PAD:] = _dot(tri_ref[1], glog[rows, GLA_QK_PAD:])

    for i in range(2):
        z = zqk[:, i * NA_W:(i + 1) * NA_W]
        z = z * lax.rsqrt(_head_mean_sq(z) + EPS) * qkg_ref[i:i + 1, :]
        nqkv_ref[:, i * NA_W:(i + 1) * NA_W] = z.astype(BF16)

    for i in range(2):
        z = zg[:, i * GLA_QK_PAD:(i + 1) * GLA_QK_PAD]
        if rope:
            z = (z * rope_ref[0]
                 + pltpu.roll(z, GLA_QK_PAD - GLA_DK // 4, 1) * rope_ref[1]
                 + pltpu.roll(z, GLA_DK // 4, 1) * rope_ref[2])
        if i == 0:
            z = z * (GLA_DK ** -0.5)
        gqk_ref[:, i * GLA_QK_PAD:(i + 1) * GLA_QK_PAD] = z.astype(BF16)


def _inproj(x, l, mod, mod_row, p, consts, rope_tab, *, tm):
    bsz, t, d = x.shape
    assert t % tm == 0 and tm % SUB == 0
    rope = rope_tab is not None
    if not rope:
        rope_tab = jnp.zeros((3, tm, GLA_QK_PAD), F32)
    rope_idx = (lambda b, i: (0, i, 0)) if rope else (lambda b, i: (0, 0, 0))
    row = lambda w_: pl.BlockSpec((None, tm, w_), lambda b, i: (b, i, 0))
    out_w = (2 * FN_W, 3 * NA_W, 2 * GLA_QK_PAD, 2 * GLA_V_PAD, 2 * GLA_QK_PAD)
    out_dt = (BF16, BF16, BF16, BF16, F32)
    return pl.pallas_call(
        functools.partial(_inproj_kernel, rope=rope, tm=tm),
        grid=(bsz, t // tm),
        in_specs=[
            row(d),
            _mod_spec(d, l, mod_row),
            _layer_spec((1, d), l),
            _layer_spec((d, N_IN_PAD), l),
            _const_spec(consts["cs"].shape),
            _layer_spec((2, NA_W), l),
            pl.BlockSpec((3, tm, GLA_QK_PAD), rope_idx),
            _layer_spec((GLA_A_PAD, 2 * GLA_QK_PAD), l),
            _layer_spec((1, 2 * GLA_QK_PAD), l),
            _const_spec((2, SUB, SUB)),
        ],
        out_specs=[row(w_) for w_ in out_w],
        out_shape=[jax.ShapeDtypeStruct((bsz, t, w_), dt) for w_, dt in zip(out_w, out_dt)],
        compiler_params=_cparams(("arbitrary", "arbitrary")),
        name="inproj_rope" if rope else "inproj_ctx",
    )(x, mod, p["gain_m"], p["w"], consts["cs"], p["qkg"], rope_tab, p["aw"], p["ab"], consts["tri"])


def _fourier_kernel(tc_ref, ts_ref, uv_ref, w_ref, o_ref, *, scale):
    f = _dot(tc_ref[...], uv_ref[:, 0:FN_W]) - _dot(ts_ref[...], uv_ref[:, FN_W:])
    o_ref[...] = _dot((f * scale).astype(BF16), w_ref[...]).astype(BF16)


def _fourier(uv, l, tab_c, tab_s, fnet_w, *, tk):
    bsz, t, _ = uv.shape
    scale = float((t * FN_GD) ** -0.5)
    return pl.pallas_call(
        functools.partial(_fourier_kernel, scale=scale),
        grid=(t // tk, bsz),
        in_specs=[
            pl.BlockSpec((tk, t), lambda k, b: (k, 0)),
            pl.BlockSpec((tk, t), lambda k, b: (k, 0)),
            pl.BlockSpec((None, t, 2 * FN_W), lambda k, b: (b, 0, 0)),
            _layer_spec((FN_W, FN_W), l),
        ],
        out_specs=pl.BlockSpec((None, tk, FN_W), lambda k, b: (b, k, 0)),
        out_shape=jax.ShapeDtypeStruct((bsz, t, FN_W), BF16),
        compiler_params=_cparams(("arbitrary", "arbitrary")),
        name=f"fourier_{t}",
    )(tab_c, tab_s, uv, fnet_w)


def _softmax_pv(parts):
    m = functools.reduce(jnp.maximum, [jnp.max(s, axis=-1, keepdims=True) for s, _ in parts])
    return functools.reduce(jnp.add, [_dot(jnp.exp(s - m).astype(BF16), v) for s, v in parts])


def _na_heads_out(o0, o1, first):
    o = jnp.where(first, o0, o1)
    l = jnp.where(first, pltpu.roll(o0, NA_HD, 1), pltpu.roll(o1, NA_HD, 1))
    return (o / l).astype(BF16)


def _na_kernel(q_ref, k_ref, v_ref, qc_ref, kc_ref, vc_ref, bias_ref, o_ref, oc_ref,
               ksplit_ref, vsplit_ref, sc_ref, *, n_rows, need_ctx):
    lane = lax.broadcasted_iota(jnp.int32, (1, LANES), 1)
    first = lane < NA_HD
    zero = jnp.zeros((), BF16)
    one = jnp.ones((), BF16)
    k = k_ref[...]
    ksplit_ref[0] = jnp.where(first, k, zero)
    ksplit_ref[1] = jnp.where(first, zero, k)
    v = v_ref[...]
    vsplit_ref[0] = jnp.where(first, v, one)
    vsplit_ref[1] = jnp.where(first, one, v)
    kc = kc_ref[...]
    kc_split = (jnp.where(first, kc, zero), jnp.where(first, zero, kc))
    vc = vc_ref[...]
    vc_split = (jnp.where(first, vc, one), jnp.where(first, one, vc))

    n_groups = n_rows // NA_GROUP
    gq = NA_GROUP * GRID_W
    n_loc = NA_WIN * GRID_W
    n_loc_pad = sc_ref.shape[-1] - kc_ref.shape[0]

    def window(g):
        ws = jnp.clip(g * NA_GROUP - NA_KH // 2, 0, n_rows - NA_WIN)
        return (pl.ds(pl.multiple_of(g * gq, gq), gq), pl.ds(pl.multiple_of(ws * GRID_W, GRID_W), n_loc))

    def scores(g, slot):
        qs, ks = window(g)
        kind = jnp.where(g == 0, 0, jnp.where(g == n_groups - 1, 2, 1))
        q = q_ref[qs, :]
        for hh in range(2):
            sc_ref[slot, hh, :, 0:n_loc] = _dot_nt(q, ksplit_ref[hh, ks, :]) + bias_ref[hh, kind]
            sc_ref[slot, hh, :, n_loc_pad:] = _dot_nt(q, kc_split[hh])

    def finish(g, slot):
        qs, ks = window(g)
        outs = [_softmax_pv([(sc_ref[slot, hh, :, 0:n_loc], vsplit_ref[hh, ks, :]),
                             (sc_ref[slot, hh, :, n_loc_pad:], vc_split[hh])]) for hh in range(2)]
        o_ref[qs, :] = _na_heads_out(outs[0], outs[1], first)

    scores(0, 0)

    def pair(i, carry):
        scores(2 * i + 1, 1)
        finish(2 * i, 0)
        scores(2 * i + 2, 0)
        finish(2 * i + 1, 1)
        return carry

    lax.fori_loop(0, n_groups // 2 - 1, pair, 0)
    scores(n_groups - 1, 1)
    finish(n_groups - 2, 0)
    finish(n_groups - 1, 1)

    if need_ctx:
        qc = qc_ref[...]
        outs = [_softmax_pv([(_dot_nt(qc, kc_split[hh]), vc_split[hh])]) for hh in range(2)]
        oc_ref[...] = _na_heads_out(outs[0], outs[1], first)
    else:
        oc_ref[...] = jnp.zeros(oc_ref.shape, oc_ref.dtype)


def _na(nqkv, nqkv_c, l, bias, *, need_ctx):
    bsz, t, _ = nqkv.shape
    n = nqkv_c.shape[1]
    n_rows = t // GRID_W
    assert n_rows % (2 * NA_GROUP) == 0 and n_rows >= NA_WIN + NA_GROUP
    pairs = NA_HEADS // 2
    n_loc_pad = pl.cdiv(NA_WIN * GRID_W, LANES) * LANES
    nb = NA_W // LANES
    lat = lambda off: pl.BlockSpec((None, t, LANES), lambda j, b: (b, 0, off + j))
    ctx = lambda off: pl.BlockSpec((None, n, LANES), lambda j, b: (b, 0, off + j))
    return pl.pallas_call(
        functools.partial(_na_kernel, n_rows=n_rows, need_ctx=need_ctx),
        grid=(pairs, bsz),
        in_specs=[lat(0), lat(nb), lat(2 * nb), ctx(0), ctx(nb), ctx(2 * nb),
                  pl.BlockSpec((None, 2) + bias.shape[2:], lambda j, b: (l, j, 0, 0, 0))],
        out_specs=[pl.BlockSpec((None, t, LANES), lambda j, b: (b, 0, j)),
                   pl.BlockSpec((None, n, LANES), lambda j, b: (b, 0, j))],
        out_shape=[jax.ShapeDtypeStruct((bsz, t, NA_W), BF16),
                   jax.ShapeDtypeStruct((bsz, n, NA_W), BF16)],
        scratch_shapes=[pltpu.VMEM((2, t, LANES), BF16), pltpu.VMEM((2, t, LANES), BF16),
                        pltpu.VMEM((2, 2, NA_GROUP * GRID_W, n_loc_pad + n), F32)],
        compiler_params=_cparams(("arbitrary", "arbitrary")),
        name="na_attn",
    )(nqkv, nqkv, nqkv, nqkv_c, nqkv_c, nqkv_c, bias)


def _gla_steps(jobs, st_ref, causal_ref, vmask_ref):
    c = GLA_CHUNK
    head = lax.broadcasted_iota(jnp.int32, (1, GLA_QK_PAD), 1) // GLA_DK_PAD
    prep = []
    for qk_ref, vg_ref, b_ref, out_ref, ci, d in jobs:
        rows = pl.ds(pl.multiple_of(ci * c, c), c)
        k = qk_ref[rows, GLA_QK_PAD:].astype(F32)
        v = vg_ref[rows, 0:GLA_V_PAD]
        b = b_ref[rows, d * GLA_QK_PAD:(d + 1) * GLA_QK_PAD]
        decay = jnp.exp(b[c - 1:c, :] if d == 0 else b[0:1, :])
        k_in = k * jnp.exp(-b)
        k_heads = [jnp.where(head == h, k_in, 0.0) for h in range(GLA_HEADS)]
        q_in = None
        if out_ref is not None:
            q_in = (qk_ref[rows, 0:GLA_QK_PAD].astype(F32) * jnp.exp(b)).astype(BF16)
        prep.append((rows, q_in, k_heads, v, decay))

    att = [None if q_in is None else _dot_nt(q_in, jnp.concatenate(k_heads, axis=0).astype(BF16))
           for (rows, q_in, k_heads, v, decay) in prep]

    kv = [[_dot_tn(v[:, h * GLA_DV_PAD:(h + 1) * GLA_DV_PAD], (k_heads[h] * decay).astype(BF16))
           for h in range(GLA_HEADS)] for (rows, q_in, k_heads, v, decay) in prep]

    inter = []
    for j, ((qk_ref, vg_ref, b_ref, out_ref, ci, d), (rows, q_in, k_heads, v, decay)) in enumerate(
            zip(jobs, prep)):
        inter.append(None if q_in is None else _dot_nt(q_in, st_ref[d].astype(BF16)))
        for h in range(GLA_HEADS):
            blk = slice(h * GLA_DV_PAD, (h + 1) * GLA_DV_PAD)
            st_ref[d, blk, :] = st_ref[d, blk, :] * decay + kv[j][h]

    for j, ((qk_ref, vg_ref, b_ref, out_ref, ci, d), (rows, q_in, k_heads, v, decay)) in enumerate(
            zip(jobs, prep)):
        if out_ref is not None:
            a = (att[j] * causal_ref[d]).astype(BF16)
            v_blk = jnp.concatenate([v] * GLA_HEADS, axis=0) * vmask_ref[...]
            out_ref[rows, :] = inter[j] + _dot(a, v_blk)


def _gla_finish(of_ref, ob_ref, vg_ref, gain_ref, y_ref, t):
    for s in range(t // SUB):
        rows = slice(s * SUB, (s + 1) * SUB)
        for h in range(GLA_HEADS):
            cols = slice(h * GLA_DV_PAD, (h + 1) * GLA_DV_PAD)
            o = of_ref[rows, cols] + ob_ref[rows, cols]
            ms = jnp.sum(o * o, axis=-1, keepdims=True) * (1.0 / GLA_DV)
            on = o * lax.rsqrt(ms + EPS) * gain_ref[:, cols]
            g = vg_ref[rows, GLA_V_PAD + h * GLA_DV_PAD:GLA_V_PAD + (h + 1) * GLA_DV_PAD].astype(F32)
            y_ref[rows, cols] = (on * (g * jax.nn.sigmoid(g))).astype(BF16)


def _gla_kernel(qk_ref, vg_ref, b_ref, qkc_ref, vgc_ref, bc_ref, gain_ref, causal_ref, vmask_ref,
                y_ref, yc_ref, st_ref, of_ref, ob_ref, ocf_ref, ocb_ref, *, t, n, need_ctx):
    st_ref[...] = jnp.zeros(st_ref.shape, F32)
    nc_ctx = n // GLA_CHUNK
    nc_lat = t // GLA_CHUNK
    def jobs(refs, outs, n_chunks, i):
        fwd = [refs + (outs[0], 2 * i + s, 0) for s in range(2)]
        bwd = [refs + (outs[1], n_chunks - 1 - 2 * i - s, 1) for s in range(2)]
        return [fwd[0], bwd[0], fwd[1], bwd[1]]

    ctx_outs = (ocf_ref, ocb_ref) if need_ctx else (None, None)
    for i in range(nc_ctx // 2):
        _gla_steps(jobs((qkc_ref, vgc_ref, bc_ref), ctx_outs, nc_ctx, i), st_ref, causal_ref, vmask_ref)

    def step(i, carry):
        _gla_steps(jobs((qk_ref, vg_ref, b_ref), (of_ref, ob_ref), nc_lat, i), st_ref, causal_ref, vmask_ref)
        return carry

    lax.fori_loop(0, nc_lat // 2, step, 0)
    _gla_finish(of_ref, ob_ref, vg_ref, gain_ref, y_ref, t)
    if need_ctx:
        _gla_finish(ocf_ref, ocb_ref, vgc_ref, gain_ref, yc_ref, n)
    else:
        yc_ref[...] = jnp.zeros(yc_ref.shape, yc_ref.dtype)


def _gla(gqk, gvg, gb, gqk_c, gvg_c, gb_c, l, gain, causal, vmask, *, need_ctx):
    bsz, t, _ = gqk.shape
    n = gqk_c.shape[1]
    row = lambda rows, w_: pl.BlockSpec((None, rows, w_), lambda b: (b, 0, 0))
    return pl.pallas_call(
        functools.partial(_gla_kernel, t=t, n=n, need_ctx=need_ctx),
        grid=(bsz,),
        in_specs=[row(t, 2 * GLA_QK_PAD), row(t, 2 * GLA_V_PAD), row(t, 2 * GLA_QK_PAD),
                  row(n, 2 * GLA_QK_PAD), row(n, 2 * GLA_V_PAD), row(n, 2 * GLA_QK_PAD),
                  _layer_spec((1, GLA_V_PAD), l), _const_spec(causal.shape), _const_spec(vmask.shape)],
        out_specs=[row(t, GLA_V_PAD), row(n, GLA_V_PAD)],
        out_shape=[jax.ShapeDtypeStruct((bsz, t, GLA_V_PAD), BF16),
                   jax.ShapeDtypeStruct((bsz, n, GLA_V_PAD), BF16)],
        scratch_shapes=[pltpu.VMEM((2, GLA_V_PAD, GLA_QK_PAD), F32),
                        pltpu.VMEM((t, GLA_V_PAD), F32), pltpu.VMEM((t, GLA_V_PAD), F32),
                        pltpu.VMEM((n, GLA_V_PAD), F32), pltpu.VMEM((n, GLA_V_PAD), F32)],
        compiler_params=_cparams(("arbitrary",)),
        name="gla_scan",
    )(gqk, gvg, gb, gqk_c, gvg_c, gb_c, gain, causal, vmask)


def _post_kernel(x_ref, yfn_ref, yna_ref, ygla_ref, mod_ref, gain_ref, wo_ref, w1_ref, w3_ref, w2_ref,
                 o_ref):
    y = (_dot(yfn_ref[...], wo_ref[0:FN_W, :])
         + _dot(yna_ref[...], wo_ref[FN_W:FN_W + NA_W, :])
         + _dot(ygla_ref[...], wo_ref[FN_W + NA_W:, :]))
    x1 = x_ref[...] + mod_ref[2:3, :] * y
    h = _modulated_norm(x1, gain_ref[...], mod_ref[3:4, :], mod_ref[4:5, :]).astype(BF16)
    acc = jnp.zeros(x1.shape, F32)
    for c in range(D_FF // MXU_W):
        cols = slice(c * MXU_W, (c + 1) * MXU_W)
        u = _dot(h, w1_ref[:, cols])
        g = _dot(h, w3_ref[:, cols])
        a = (u * jax.nn.sigmoid(u) * g).astype(BF16)
        acc = acc + _dot(a, w2_ref[cols, :])
    o_ref[...] = x1 + mod_ref[5:6, :] * acc


def _post(x, yfn, yna, ygla, l, mod, mod_row, p, *, tm):
    bsz, t, d = x.shape
    assert t % tm == 0
    row = lambda w_: pl.BlockSpec((None, tm, w_), lambda b, i: (b, i, 0))
    return pl.pallas_call(
        _post_kernel,
        grid=(bsz, t // tm),
        in_specs=[row(d), row(FN_W), row(NA_W), row(GLA_V_PAD),
                  _mod_spec(d, l, mod_row),
                  _layer_spec((1, d), l), _layer_spec((MIX_PAD, d), l), _layer_spec((d, D_FF), l),
                  _layer_spec((d, D_FF), l), _layer_spec((D_FF, d), l)],
        out_specs=row(d),
        out_shape=jax.ShapeDtypeStruct((bsz, t, d), F32),
        compiler_params=_cparams(("arbitrary", "arbitrary")),
        name=f"post_{t}",
    )(x, yfn, yna, ygla, mod, p["gain_f"], p["wo"], p["w1"], p["w3"], p["w2"])


def _dft_tables(n):
    idx = (np.arange(n)[:, None] * np.arange(n)[None, :]) % n
    ang = 2.0 * np.pi * idx / n
    return np.cos(ang), np.sin(ang)


def _shape_consts(t, n):
    c64, s64 = _dft_tables(FN_GD)
    eye = np.eye(FN_GROUPS)
    cs = np.concatenate([np.kron(eye, c64), np.kron(eye, s64)], axis=1)
    chunk_id = np.arange(SUB) // GLA_CHUNK
    same = chunk_id[:, None] == chunk_id[None, :]
    pos = np.arange(SUB)
    tri = np.stack([same & (pos[None, :] <= pos[:, None]), same & (pos[None, :] >= pos[:, None])])
    cpos = np.arange(GLA_CHUNK)
    fwd = cpos[None, :] <= cpos[:, None]
    gla_causal = np.stack([np.tile(fwd, (1, GLA_HEADS)), np.tile(fwd.T, (1, GLA_HEADS))])
    gla_vmask = ((np.arange(GLA_HEADS * GLA_CHUNK)[:, None] // GLA_CHUNK)
                 == (np.arange(GLA_V_PAD)[None, :] // GLA_DV_PAD))

    m = GLA_DK // 4
    inv = ROPE_BASE ** (-np.arange(m) / m)
    tok = np.arange(t)
    cos_h = np.ones((t, GLA_DK_PAD))
    sin_lo = np.zeros((t, GLA_DK_PAD))
    sin_hi = np.zeros((t, GLA_DK_PAD))
    for blk, p in enumerate((tok // GRID_W, tok % GRID_W)):
        ang = p[:, None] * inv[None, :]
        o = blk * 2 * m
        cos_h[:, o:o + m] = np.cos(ang)
        cos_h[:, o + m:o + 2 * m] = np.cos(ang)
        sin_lo[:, o:o + m] = -np.sin(ang)
        sin_hi[:, o + m:o + 2 * m] = np.sin(ang)
    rope = np.stack([np.tile(a, (1, GLA_HEADS)) for a in (cos_h, sin_lo, sin_hi)])

    tc, ts = _dft_tables(t)
    tcc, tsc = _dft_tables(n)
    to_bf16 = lambda a: jnp.asarray(a, F32).astype(BF16)
    return {
        "cs": to_bf16(cs), "tri": jnp.asarray(tri, BF16),
        "gla_causal": jnp.asarray(gla_causal, F32), "gla_vmask": jnp.asarray(gla_vmask, BF16),
        "rope": jnp.asarray(rope, F32),
        "tc": to_bf16(tc), "ts": to_bf16(ts), "tcc": to_bf16(tcc), "tsc": to_bf16(tsc),
    }


def _bias_kernel(rpb_ref, onehot_ref, ok_ref, o_ref):
    r = rpb_ref[...]
    onehot = onehot_ref[...]
    t = jnp.zeros(o_ref.shape, F32)
    for _ in range(3):
        piece = r.astype(BF16)
        t = t + _dot(piece, onehot)
        r = r - piece.astype(F32)
    o_ref[...] = jnp.where(ok_ref[...] > 0.0, t, NA_MASKED)


def _window_bias(na_rpb):
    depth, heads, n_dr, n_dc = na_rpb.shape
    cq = np.arange(GRID_W)
    c0 = np.clip(cq - NA_KW // 2, 0, GRID_W - NA_KW)
    col_ok = (cq[None, :] >= c0[:, None]) & (cq[None, :] < c0[:, None] + NA_KW)
    dc = np.clip(cq[None, :] - cq[:, None], -(NA_KW - 1), NA_KW - 1) + NA_KW - 1
    n_dc_pad = n_dc + 1
    n_tab = depth * heads * n_dr
    rows_pad = -(-n_tab // 8) * 8
    onehot = (np.arange(n_dc_pad)[:, None, None] == dc[None]).reshape(n_dc_pad, GRID_W * GRID_W)
    rpb2 = jnp.pad(na_rpb.reshape(n_tab, n_dc), ((0, rows_pad - n_tab), (0, 1)))
    t = pl.pallas_call(
        _bias_kernel,
        out_shape=jax.ShapeDtypeStruct((rows_pad, GRID_W * GRID_W), F32),
        name="na_bias",
    )(rpb2, jnp.asarray(onehot, BF16), jnp.asarray(col_ok.reshape(1, -1), F32))
    t = t[:n_tab].reshape(depth * heads, n_dr, GRID_W, GRID_W)
    t = t.transpose(0, 2, 1, 3).reshape(depth * heads, GRID_W, n_dr * GRID_W)
    margin = NA_WIN - NA_KH
    t = jnp.pad(t, ((0, 0), (0, 0), (margin * GRID_W, margin * GRID_W)))
    blocks, band = [], np.zeros((3, NA_GROUP, NA_WIN), bool)
    for kind, (q_off, a_lo_of) in enumerate(((0, lambda i: 0), (NA_KH // 2, lambda i: i),
                                             (margin + NA_KH - NA_GROUP, lambda i: margin))):
        for i in range(NA_GROUP):
            dr0 = NA_KH - 1 - q_off - i
            blocks.append(t[:, :, (dr0 + margin) * GRID_W:(dr0 + margin + NA_WIN) * GRID_W])
            band[kind, i, a_lo_of(i):a_lo_of(i) + NA_KH] = True
    tiles = jnp.stack(blocks, axis=1).reshape(depth, heads, 3, NA_GROUP, GRID_W, NA_WIN * GRID_W)
    band = np.repeat(band, GRID_W, axis=2)[None, None, :, :, None, :]
    tiles = jnp.where(jnp.asarray(band), tiles, NA_MASKED)
    return tiles.reshape(depth, heads, 3, NA_GROUP * GRID_W, NA_WIN * GRID_W)


def _pad_heads(w, heads, dim, pad, axis=-1):
    axis = axis % w.ndim
    shape = w.shape
    w = w.reshape(shape[:axis] + (heads, dim) + shape[axis + 1:])
    widths = [(0, 0)] * w.ndim
    widths[axis + 1] = (0, pad - dim)
    return jnp.pad(w, widths).reshape(shape[:axis] + (heads * pad,) + shape[axis + 1:])


def _params(norm_mix, norm_ffn, w_in, fnet_w, na_q_norm, na_k_norm, na_rpb, alpha_w, alpha_b, o_norm, w_out,
            ffn_w1, ffn_w3, ffn_w2):
    depth = w_in.shape[0]
    o = FN_W + 3 * NA_W
    gq = w_in[..., o:o + GLA_QK_W]
    gk = w_in[..., o + GLA_QK_W:o + 2 * GLA_QK_W]
    o2 = o + 2 * GLA_QK_W
    gv = w_in[..., o2:o2 + GLA_V_W]
    gg = w_in[..., o2 + GLA_V_W:o2 + 2 * GLA_V_W]
    ga = w_in[..., o2 + 2 * GLA_V_W:]
    w = jnp.concatenate([
        w_in[..., :o],
        jnp.pad(ga, ((0, 0), (0, 0), (0, GLA_A_PAD - 2 * GLA_RANK))),
        _pad_heads(gq, GLA_HEADS, GLA_DK, GLA_DK_PAD), _pad_heads(gk, GLA_HEADS, GLA_DK, GLA_DK_PAD),
        _pad_heads(gv, GLA_HEADS, GLA_DV, GLA_DV_PAD), _pad_heads(gg, GLA_HEADS, GLA_DV, GLA_DV_PAD),
    ], axis=-1).astype(BF16)

    qkg = jnp.stack([jnp.tile(na_q_norm, (1, NA_HEADS)) * (NA_HD ** -0.5),
                     jnp.tile(na_k_norm, (1, NA_HEADS))], axis=1)

    aw_pad = _pad_heads(alpha_w, GLA_HEADS, GLA_DK, GLA_DK_PAD)
    aw = jnp.concatenate([jnp.pad(aw_pad[:, 0], ((0, 0), (0, 0), (0, GLA_QK_PAD))),
                          jnp.pad(aw_pad[:, 1], ((0, 0), (0, 0), (GLA_QK_PAD, 0)))], axis=1)
    aw = jnp.pad(aw, ((0, 0), (0, GLA_A_PAD - 2 * GLA_RANK), (0, 0))).astype(BF16)
    ab = _pad_heads(alpha_b, GLA_HEADS, GLA_DK, GLA_DK_PAD).reshape(depth, 1, 2 * GLA_QK_PAD)

    gain_o = _pad_heads(jnp.tile(o_norm, (1, GLA_HEADS)), GLA_HEADS, GLA_DV, GLA_DV_PAD)[:, None, :]
    wo = jnp.concatenate([w_out[:, :FN_W + NA_W],
                          _pad_heads(w_out[:, FN_W + NA_W:], GLA_HEADS, GLA_DV, GLA_DV_PAD, axis=1)],
                         axis=1).astype(BF16)
    return {
        "gain_m": norm_mix[:, None, :], "gain_f": norm_ffn[:, None, :], "w": w, "qkg": qkg, "aw": aw, "ab": ab,
        "bias": _window_bias(na_rpb), "gain_o": gain_o, "wo": wo, "fw": fnet_w.astype(BF16),
        "w1": ffn_w1.astype(BF16), "w3": ffn_w3.astype(BF16), "w2": ffn_w2.astype(BF16),
    }


def kernel(x, c, ctx, c_ctx, ada_w, ada_b, norm_mix, norm_ffn, w_in, fnet_w, na_q_norm, na_k_norm, na_rpb,
           gla_alpha_w, gla_alpha_b, gla_o_norm, w_out, ffn_w1, ffn_w3, ffn_w2):
    bsz, t, d = x.shape
    n = ctx.shape[1]
    depth = ada_w.shape[0]
    ctx_row = bsz
    assert bsz < MOD_ROWS
    consts = _shape_consts(t, n)
    p = _params(norm_mix, norm_ffn, w_in, fnet_w, na_q_norm, na_k_norm, na_rpb, gla_alpha_w, gla_alpha_b,
                gla_o_norm, w_out, ffn_w1, ffn_w3, ffn_w2)

    cc = jnp.concatenate([c, c_ctx[None, :], jnp.zeros((MOD_ROWS - bsz - 1, d), F32)])
    mod = _ada_mod(cc, ada_w, ada_b).reshape(depth, MOD_ROWS, 6, d)

    cx = ctx
    for l in range(depth):
        need_ctx = l < depth - 1
        uv, nqkv, gqk, gvg, gb = _inproj(x, l, mod, None, p, consts, consts["rope"], tm=1024)
        uv_c, nqkv_c, gqk_c, gvg_c, gb_c = _inproj(cx, l, mod, ctx_row, p, consts, None, tm=n)

        y_fn = _fourier(uv, l, consts["tc"], consts["ts"], p["fw"], tk=1024)
        y_na, y_na_c = _na(nqkv, nqkv_c, l, p["bias"], need_ctx=need_ctx)
        y_gla, y_gla_c = _gla(gqk, gvg, gb, gqk_c, gvg_c, gb_c, l, p["gain_o"],
                              consts["gla_causal"], consts["gla_vmask"], need_ctx=need_ctx)
        x = _post(x, y_fn, y_na, y_gla, l, mod, None, p, tm=512)
        if need_ctx:
            y_fn_c = _fourier(uv_c, l, consts["tcc"], consts["tsc"], p["fw"], tk=n)
            cx = _post(cx, y_fn_c, y_na_c, y_gla_c, l, mod, ctx_row, p, tm=n)
    return x
```

```python
import functools

import numpy as np
import jax
import jax.numpy as jnp
from jax import lax
from jax.experimental import pallas as pl
from jax.experimental.pallas import tpu as pltpu

F32 = jnp.float32
BF16 = jnp.bfloat16

D_MODEL = 1024
GRID_W = 64
FN_W = 256
FN_GROUPS = 4
FN_GD = 64
NA_HD = 64
NA_W = 384
NA_HEADS = 6
NA_KH = 8
NA_KW = 16
NA_GROUP = 4
NA_WIN = NA_GROUP + NA_KH - 1
NA_MASKED = -1e30
GLA_HEADS = 4
GLA_DK = 48
GLA_DV = 96
GLA_QK_W = GLA_HEADS * GLA_DK
GLA_V_W = GLA_HEADS * GLA_DV
GLA_RANK = 16
GLA_GATE_NORM = 16.0
GLA_CHUNK = 64
GLA_BLOCK = 2 * GLA_CHUNK
ROPE_BASE = 10000.0
D_FF = 2816
EPS = 1e-6
MOD_ROWS = 16

LANES = 128
MXU_W = 256
VMEM_LIMIT_BYTES = 56 * 1024 * 1024

GLA_DK_PAD = 64
GLA_DV_PAD = 128
GLA_QK_PAD = GLA_HEADS * GLA_DK_PAD
GLA_V_PAD = GLA_HEADS * GLA_DV_PAD
GLA_A_PAD = LANES
MIX_PAD = FN_W + NA_W + GLA_V_PAD

C_FX = 0
C_NQ = C_FX + FN_W
C_NK = C_NQ + NA_W
C_NV = C_NK + NA_W
C_GA = C_NV + NA_W
C_GQ = C_GA + GLA_A_PAD
C_GK = C_GQ + GLA_QK_PAD
C_GV = C_GK + GLA_QK_PAD
C_GG = C_GV + GLA_V_PAD
N_IN_PAD = C_GG + GLA_V_PAD

SUB = 256
CTX_TM = 512


def _cparams(sem):
    return pltpu.CompilerParams(dimension_semantics=sem, vmem_limit_bytes=VMEM_LIMIT_BYTES)


def _const_spec(shape):
    nd = len(shape)
    return pl.BlockSpec(tuple(shape), lambda *_: (0,) * nd, pipeline_mode=pl.Buffered(1))


def _layer_spec(shape, l):
    nd = len(shape)
    return pl.BlockSpec((None,) + tuple(shape), lambda *_: (l,) + (0,) * nd, pipeline_mode=pl.Buffered(1))


def _mod_spec(d, l, mod_row):
    if mod_row is None:
        return pl.BlockSpec((None, None, 6, d), lambda b, i: (l, b, 0, 0))
    return pl.BlockSpec((None, None, 6, d), lambda b, i: (l, mod_row, 0, 0))


def _dot(a, b):
    return jnp.dot(a, b, preferred_element_type=F32)


def _dot_nt(a, b):
    return lax.dot_general(a, b, (((1,), (1,)), ((), ())), preferred_element_type=F32)


def _dot_tn(a, b):
    return lax.dot_general(a, b, (((0,), (0,)), ((), ())), preferred_element_type=F32)


def _ada_kernel(c_ref, w_ref, b_ref, o_ref):
    a = c_ref[...]
    a = (a * jax.nn.sigmoid(a)).astype(BF16)
    o_ref[...] = _dot(a, w_ref[...].astype(BF16)) + b_ref[...]


def _ada_mod(cc, ada_w, ada_b):
    depth, d, n = ada_w.shape
    rows = cc.shape[0]
    tn = 1024
    return pl.pallas_call(
        _ada_kernel,
        grid=(depth, n // tn),
        in_specs=[
            pl.BlockSpec((rows, d), lambda l, j: (0, 0)),
            pl.BlockSpec((None, d, tn), lambda l, j: (l, 0, j)),
            pl.BlockSpec((None, 1, tn), lambda l, j: (l, 0, j)),
        ],
        out_specs=pl.BlockSpec((None, rows, tn), lambda l, j: (l, 0, j)),
        out_shape=jax.ShapeDtypeStruct((depth, rows, n), F32),
        compiler_params=_cparams(("arbitrary", "arbitrary")),
        name="ada_mod",
    )(cc, ada_w, ada_b.reshape(depth, 1, n))


def _modulated_norm(x, gain, shift, scale):
    ms = jnp.mean(x * x, axis=-1, keepdims=True)
    return (x * lax.rsqrt(ms + EPS) * gain) * (1.0 + scale) + shift


def _log_sigmoid(x):
    return jnp.minimum(x, 0.0) - jnp.log1p(jnp.exp(-jnp.abs(x)))


def _head_mean_sq(z):
    first = lax.broadcasted_iota(jnp.int32, (1, LANES), 1) < NA_HD
    out = []
    for t in range(NA_W // LANES):
        sq = z[:, t * LANES:(t + 1) * LANES]
        sq = sq * sq
        lo = jnp.sum(jnp.where(first, sq, 0.0), axis=-1, keepdims=True)
        hi = jnp.sum(jnp.where(first, 0.0, sq), axis=-1, keepdims=True)
        out.append(jnp.where(first, lo, hi))
    return jnp.concatenate(out, axis=1) * (1.0 / NA_HD)


def _inproj_kernel(x_ref, mod_ref, gain_ref, w_ref, cs_ref, qkg_ref, rope_ref, aw_ref, ab_ref, tri_ref,
                   uv_ref, nqkv_ref, gqk_ref, gvg_ref, gb_ref, *, rope, tm):
    x = x_ref[...]
    h = _modulated_norm(x, gain_ref[...], mod_ref[0:1, :], mod_ref[1:2, :]).astype(BF16)

    zf = _dot(h, w_ref[:, C_FX:C_NQ]).astype(BF16)
    zva = _dot(h, w_ref[:, C_NV:C_GQ])
    nqkv_ref[:, 2 * NA_W:3 * NA_W] = zva[:, :NA_W].astype(BF16)
    a = zva[:, NA_W:].astype(BF16)
    zqk = _dot(h, w_ref[:, C_NQ:C_NV])
    logits = _dot(a, aw_ref[...]) + ab_ref[...]
    zg = _dot(h, w_ref[:, C_GQ:C_GV])
    uv_ref[...] = _dot(zf, cs_ref[...]).astype(BF16)
    gvg_ref[...] = _dot(h, w_ref[:, C_GV:N_IN_PAD]).astype(BF16)

    glog = (_log_sigmoid(logits) * (1.0 / GLA_GATE_NORM)).astype(BF16)
    for s in range(tm // SUB):
        rows = slice(s * SUB, (s + 1) * SUB)
        gb_ref[rows, 0:GLA_QK_PAD] = _dot(tri_ref[0], glog[rows, 0:GLA_QK_PAD])
        gb_ref[rows, GLA_QK_PAD:] = _dot(tri_ref[1], glog[rows, GLA_QK_PAD:])

    for i in range(2):
        z = zqk[:, i * NA_W:(i + 1) * NA_W]
        z = z * lax.rsqrt(_head_mean_sq(z) + EPS) * qkg_ref[i:i + 1, :]
        nqkv_ref[:, i * NA_W:(i + 1) * NA_W] = z.astype(BF16)

    for i in range(2):
        z = zg[:, i * GLA_QK_PAD:(i + 1) * GLA_QK_PAD]
        if rope:
            z = (z * rope_ref[0]
                 + pltpu.roll(z, GLA_QK_PAD - GLA_DK // 4, 1) * rope_ref[1]
                 + pltpu.roll(z, GLA_DK // 4, 1) * rope_ref[2])
        if i == 0:
            z = z * (GLA_DK ** -0.5)
        gqk_ref[:, i * GLA_QK_PAD:(i + 1) * GLA_QK_PAD] = z.astype(BF16)


def _inproj(x, l, mod, mod_row, p, consts, rope_tab, *, tm):
    bsz, t, d = x.shape
    assert t % tm == 0 and tm % SUB == 0
    rope = rope_tab is not None
    if not rope:
        rope_tab = jnp.zeros((3, tm, GLA_QK_PAD), F32)
    rope_idx = (lambda b, i: (0, i, 0)) if rope else (lambda b, i: (0, 0, 0))
    row = lambda w_: pl.BlockSpec((None, tm, w_), lambda b, i: (b, i, 0))
    out_w = (2 * FN_W, 3 * NA_W, 2 * GLA_QK_PAD, 2 * GLA_V_PAD, 2 * GLA_QK_PAD)
    out_dt = (BF16, BF16, BF16, BF16, F32)
    return pl.pallas_call(
        functools.partial(_inproj_kernel, rope=rope, tm=tm),
        grid=(bsz, t // tm),
        in_specs=[
            row(d),
            _mod_spec(d, l, mod_row),
            _layer_spec((1, d), l),
            _layer_spec((d, N_IN_PAD), l),
            _const_spec(consts["cs"].shape),
            _layer_spec((2, NA_W), l),
            pl.BlockSpec((3, tm, GLA_QK_PAD), rope_idx),
            _layer_spec((GLA_A_PAD, 2 * GLA_QK_PAD), l),
            _layer_spec((1, 2 * GLA_QK_PAD), l),
            _const_spec((2, SUB, SUB)),
        ],
        out_specs=[row(w_) for w_ in out_w],
        out_shape=[jax.ShapeDtypeStruct((bsz, t, w_), dt) for w_, dt in zip(out_w, out_dt)],
        compiler_params=_cparams(("arbitrary", "arbitrary")),
        name="inproj_rope" if rope else "inproj_ctx",
    )(x, mod, p["gain_m"], p["w"], consts["cs"], p["qkg"], rope_tab, p["aw"], p["ab"], consts["tri"])


def _fourier_kernel(tc_ref, ts_ref, uv_ref, w_ref, o_ref, *, scale):
    f = _dot(tc_ref[...], uv_ref[:, 0:FN_W]) - _dot(ts_ref[...], uv_ref[:, FN_W:])
    o_ref[...] = _dot((f * scale).astype(BF16), w_ref[...]).astype(BF16)


def _fourier(uv, l, tab_c, tab_s, fnet_w, *, tk):
    bsz, t, _ = uv.shape
    scale = float((t * FN_GD) ** -0.5)
    return pl.pallas_call(
        functools.partial(_fourier_kernel, scale=scale),
        grid=(t // tk, bsz),
        in_specs=[
            pl.BlockSpec((tk, t), lambda k, b: (k, 0)),
            pl.BlockSpec((tk, t), lambda k, b: (k, 0)),
            pl.BlockSpec((None, t, 2 * FN_W), lambda k, b: (b, 0, 0)),
            _layer_spec((FN_W, FN_W), l),
        ],
        out_specs=pl.BlockSpec((None, tk, FN_W), lambda k, b: (b, k, 0)),
        out_shape=jax.ShapeDtypeStruct((bsz, t, FN_W), BF16),
        compiler_params=_cparams(("arbitrary", "arbitrary")),
        name=f"fourier_{t}",
    )(tab_c, tab_s, uv, fnet_w)


def _softmax_pv(parts):
    m = functools.reduce(jnp.maximum, [jnp.max(s, axis=-1, keepdims=True) for s, _ in parts])
    return functools.reduce(jnp.add, [_dot(jnp.exp(s - m).astype(BF16), v) for s, v in parts])


def _na_heads_out(o0, o1, first):
    o = jnp.where(first, o0, o1)
    l = jnp.where(first, pltpu.roll(o0, NA_HD, 1), pltpu.roll(o1, NA_HD, 1))
    return (o / l).astype(BF16)


def _na_group_kinds():
    margin = NA_WIN - NA_KH
    return ((0, lambda i: 0), (NA_KH // 2, lambda i: i), (margin + NA_KH - NA_GROUP, lambda i: margin))


def _na_kernel(q_ref, k_ref, v_ref, qc_ref, kc_ref, vc_ref, rpb_ref, band_ref, o_ref, oc_ref,
               ksplit_ref, vsplit_ref, sc_ref, bias_ref, *, n_rows, need_ctx):
    lane = lax.broadcasted_iota(jnp.int32, (1, LANES), 1)
    first = lane < NA_HD
    zero = jnp.zeros((), BF16)
    one = jnp.ones((), BF16)
    k = k_ref[...]
    ksplit_ref[0] = jnp.where(first, k, zero)
    ksplit_ref[1] = jnp.where(first, zero, k)
    v = v_ref[...]
    vsplit_ref[0] = jnp.where(first, v, one)
    vsplit_ref[1] = jnp.where(first, one, v)
    kc = kc_ref[...]
    kc_split = (jnp.where(first, kc, zero), jnp.where(first, zero, kc))
    vc = vc_ref[...]
    vc_split = (jnp.where(first, vc, one), jnp.where(first, one, vc))

    n_groups = n_rows // NA_GROUP
    gq = NA_GROUP * GRID_W
    n_loc = NA_WIN * GRID_W
    n_loc_pad = sc_ref.shape[-1] - kc_ref.shape[0]

    @pl.when(pl.program_id(1) == 0)
    def _():
        for hh in range(2):
            for kind, (q_off, _) in enumerate(_na_group_kinds()):
                for i in range(NA_GROUP):
                    start = (NA_KH - 1 - q_off - i + NA_WIN - NA_KH) * GRID_W
                    tile = rpb_ref[hh, :, start:start + n_loc]
                    bias_ref[hh, kind, i * GRID_W:(i + 1) * GRID_W, :] = jnp.where(
                        band_ref[kind * NA_GROUP + i] > 0.0, tile, NA_MASKED)

    def window(g):
        ws = jnp.clip(g * NA_GROUP - NA_KH // 2, 0, n_rows - NA_WIN)
        return (pl.ds(pl.multiple_of(g * gq, gq), gq), pl.ds(pl.multiple_of(ws * GRID_W, GRID_W), n_loc))

    def scores(g, slot):
        qs, ks = window(g)
        kind = jnp.where(g == 0, 0, jnp.where(g == n_groups - 1, 2, 1))
        q = q_ref[qs, :]
        for hh in range(2):
            sc_ref[slot, hh, :, 0:n_loc] = _dot_nt(q, ksplit_ref[hh, ks, :]) + bias_ref[hh, kind]
            sc_ref[slot, hh, :, n_loc_pad:] = _dot_nt(q, kc_split[hh])

    def finish(g, slot):
        qs, ks = window(g)
        outs = [_softmax_pv([(sc_ref[slot, hh, :, 0:n_loc], vsplit_ref[hh, ks, :]),
                             (sc_ref[slot, hh, :, n_loc_pad:], vc_split[hh])]) for hh in range(2)]
        o_ref[qs, :] = _na_heads_out(outs[0], outs[1], first)

    scores(0, 0)

    def pair(i, carry):
        scores(2 * i + 1, 1)
        finish(2 * i, 0)
        scores(2 * i + 2, 0)
        finish(2 * i + 1, 1)
        return carry

    lax.fori_loop(0, n_groups // 2 - 1, pair, 0)
    scores(n_groups - 1, 1)
    finish(n_groups - 2, 0)
    finish(n_groups - 1, 1)

    if need_ctx:
        qc = qc_ref[...]
        outs = [_softmax_pv([(_dot_nt(qc, kc_split[hh]), vc_split[hh])]) for hh in range(2)]
        oc_ref[...] = _na_heads_out(outs[0], outs[1], first)
    else:
        oc_ref[...] = jnp.zeros(oc_ref.shape, oc_ref.dtype)


def _na(nqkv, nqkv_c, l, rpb_rows, band, *, need_ctx):
    bsz, t, _ = nqkv.shape
    n = nqkv_c.shape[1]
    n_rows = t // GRID_W
    assert n_rows % (2 * NA_GROUP) == 0 and n_rows >= NA_WIN + NA_GROUP
    pairs = NA_HEADS // 2
    n_loc = NA_WIN * GRID_W
    n_loc_pad = pl.cdiv(n_loc, LANES) * LANES
    nb = NA_W // LANES
    lat = lambda off: pl.BlockSpec((None, t, LANES), lambda j, b: (b, 0, off + j))
    ctx = lambda off: pl.BlockSpec((None, n, LANES), lambda j, b: (b, 0, off + j))
    return pl.pallas_call(
        functools.partial(_na_kernel, n_rows=n_rows, need_ctx=need_ctx),
        grid=(pairs, bsz),
        in_specs=[lat(0), lat(nb), lat(2 * nb), ctx(0), ctx(nb), ctx(2 * nb),
                  pl.BlockSpec((None, 2) + rpb_rows.shape[2:], lambda j, b: (l, j, 0, 0)),
                  _const_spec(band.shape)],
        out_specs=[pl.BlockSpec((None, t, LANES), lambda j, b: (b, 0, j)),
                   pl.BlockSpec((None, n, LANES), lambda j, b: (b, 0, j))],
        out_shape=[jax.ShapeDtypeStruct((bsz, t, NA_W), BF16),
                   jax.ShapeDtypeStruct((bsz, n, NA_W), BF16)],
        scratch_shapes=[pltpu.VMEM((2, t, LANES), BF16), pltpu.VMEM((2, t, LANES), BF16),
                        pltpu.VMEM((2, 2, NA_GROUP * GRID_W, n_loc_pad + n), F32),
                        pltpu.VMEM((2, len(_na_group_kinds()), NA_GROUP * GRID_W, n_loc), F32)],
        compiler_params=_cparams(("arbitrary", "arbitrary")),
        name="na_attn",
    )(nqkv, nqkv, nqkv, nqkv_c, nqkv_c, nqkv_c, rpb_rows, band)


def _gla_steps(jobs, st_ref, causal_ref, vmask_ref):
    c = GLA_CHUNK
    head = lax.broadcasted_iota(jnp.int32, (1, GLA_QK_PAD), 1) // GLA_DK_PAD
    prep = []
    for qk_ref, vg_ref, b_ref, out_ref, ci, d in jobs:
        rows = pl.ds(pl.multiple_of(ci * GLA_BLOCK, GLA_BLOCK), GLA_BLOCK)
        k = qk_ref[rows, GLA_QK_PAD:].astype(F32)
        v = vg_ref[rows, 0:GLA_V_PAD]
        b = b_ref[rows, d * GLA_QK_PAD:(d + 1) * GLA_QK_PAD]
        lo, hi = b[0:c, :], b[c:, :]
        if d == 0:
            tot1, tot2 = lo[c - 1:c, :], hi[c - 1:c, :]
            b_rel = jnp.concatenate([lo - tot1, hi], axis=0)
            b_abs = jnp.concatenate([lo, hi + tot1], axis=0)
        else:
            tot1, tot2 = hi[0:1, :], lo[0:1, :]
            b_rel = jnp.concatenate([lo, hi - tot1], axis=0)
            b_abs = jnp.concatenate([lo + tot1, hi], axis=0)
        tot = tot1 + tot2
        k_in = k * jnp.exp(-b_rel)
        k_end = k * jnp.exp(tot - b_abs)
        q_in = q_st = None
        if out_ref is not None:
            q = qk_ref[rows, 0:GLA_QK_PAD].astype(F32)
            q_in = (q * jnp.exp(b_rel)).astype(BF16)
            q_st = (q * jnp.exp(b_abs)).astype(BF16)
        prep.append((rows, q_in, q_st, k_in, k_end, v, jnp.exp(tot)))

    att = [None if q_in is None else
           _dot_nt(q_in, jnp.concatenate([jnp.where(head == h, k_in, 0.0) for h in range(GLA_HEADS)],
                                         axis=0).astype(BF16))
           for (rows, q_in, q_st, k_in, k_end, v, decay) in prep]

    kv = [[_dot_tn(v[:, h * GLA_DV_PAD:(h + 1) * GLA_DV_PAD], jnp.where(head == h, k_end, 0.0).astype(BF16))
           for h in range(GLA_HEADS)] for (rows, q_in, q_st, k_in, k_end, v, decay) in prep]

    inter = []
    for j, ((qk_ref, vg_ref, b_ref, out_ref, ci, d), (rows, q_in, q_st, k_in, k_end, v, decay)) in enumerate(
            zip(jobs, prep)):
        inter.append(None if q_st is None else _dot_nt(q_st, st_ref[d].astype(BF16)))
        for h in range(GLA_HEADS):
            blk = slice(h * GLA_DV_PAD, (h + 1) * GLA_DV_PAD)
            st_ref[d, blk, :] = st_ref[d, blk, :] * decay + kv[j][h]

    for j, ((qk_ref, vg_ref, b_ref, out_ref, ci, d), (rows, q_in, q_st, k_in, k_end, v, decay)) in enumerate(
            zip(jobs, prep)):
        if out_ref is not None:
            a = (att[j] * causal_ref[d]).astype(BF16)
            v_blk = jnp.concatenate([v] * GLA_HEADS, axis=0) * vmask_ref[...]
            out_ref[rows, :] = inter[j] + _dot(a, v_blk)


def _gla_finish(of_ref, ob_ref, vg_ref, gain_ref, y_ref, t):
    for s in range(t // SUB):
        rows = slice(s * SUB, (s + 1) * SUB)
        for h in range(GLA_HEADS):
            cols = slice(h * GLA_DV_PAD, (h + 1) * GLA_DV_PAD)
            o = of_ref[rows, cols] + ob_ref[rows, cols]
            ms = jnp.sum(o * o, axis=-1, keepdims=True) * (1.0 / GLA_DV)
            on = o * lax.rsqrt(ms + EPS) * gain_ref[:, cols]
            g = vg_ref[rows, GLA_V_PAD + h * GLA_DV_PAD:GLA_V_PAD + (h + 1) * GLA_DV_PAD].astype(F32)
            y_ref[rows, cols] = (on * (g * jax.nn.sigmoid(g))).astype(BF16)


def _gla_kernel(qk_ref, vg_ref, b_ref, qkc_ref, vgc_ref, bc_ref, gain_ref, causal_ref, vmask_ref,
                y_ref, yc_ref, st_ref, of_ref, ob_ref, ocf_ref, ocb_ref, *, t, n, need_ctx):
    st_ref[...] = jnp.zeros(st_ref.shape, F32)
    nb_ctx = n // GLA_BLOCK
    nb_lat = t // GLA_BLOCK

    def jobs(refs, outs, n_blocks, i):
        return [refs + (outs[0], i, 0), refs + (outs[1], n_blocks - 1 - i, 1)]

    ctx_outs = (ocf_ref, ocb_ref) if need_ctx else (None, None)
    for i in range(nb_ctx):
        _gla_steps(jobs((qkc_ref, vgc_ref, bc_ref), ctx_outs, nb_ctx, i), st_ref, causal_ref, vmask_ref)

    def step(i, carry):
        _gla_steps(jobs((qk_ref, vg_ref, b_ref), (of_ref, ob_ref), nb_lat, i), st_ref, causal_ref, vmask_ref)
        return carry

    lax.fori_loop(0, nb_lat, step, 0)
    _gla_finish(of_ref, ob_ref, vg_ref, gain_ref, y_ref, t)
    if need_ctx:
        _gla_finish(ocf_ref, ocb_ref, vgc_ref, gain_ref, yc_ref, n)
    else:
        yc_ref[...] = jnp.zeros(yc_ref.shape, yc_ref.dtype)


def _gla(gqk, gvg, gb, gqk_c, gvg_c, gb_c, l, gain, causal, vmask, *, need_ctx):
    bsz, t, _ = gqk.shape
    n = gqk_c.shape[1]
    row = lambda rows, w_: pl.BlockSpec((None, rows, w_), lambda b: (b, 0, 0))
    return pl.pallas_call(
        functools.partial(_gla_kernel, t=t, n=n, need_ctx=need_ctx),
        grid=(bsz,),
        in_specs=[row(t, 2 * GLA_QK_PAD), row(t, 2 * GLA_V_PAD), row(t, 2 * GLA_QK_PAD),
                  row(n, 2 * GLA_QK_PAD), row(n, 2 * GLA_V_PAD), row(n, 2 * GLA_QK_PAD),
                  _layer_spec((1, GLA_V_PAD), l), _const_spec(causal.shape), _const_spec(vmask.shape)],
        out_specs=[row(t, GLA_V_PAD), row(n, GLA_V_PAD)],
        out_shape=[jax.ShapeDtypeStruct((bsz, t, GLA_V_PAD), BF16),
                   jax.ShapeDtypeStruct((bsz, n, GLA_V_PAD), BF16)],
        scratch_shapes=[pltpu.VMEM((2, GLA_V_PAD, GLA_QK_PAD), F32),
                        pltpu.VMEM((t, GLA_V_PAD), F32), pltpu.VMEM((t, GLA_V_PAD), F32),
                        pltpu.VMEM((n, GLA_V_PAD), F32), pltpu.VMEM((n, GLA_V_PAD), F32)],
        compiler_params=_cparams(("arbitrary",)),
        name="gla_scan",
    )(gqk, gvg, gb, gqk_c, gvg_c, gb_c, gain, causal, vmask)


def _post_kernel(x_ref, yfn_ref, yna_ref, ygla_ref, mod_ref, gain_ref, wo_ref, w1_ref, w3_ref, w2_ref,
                 o_ref):
    y = (_dot(yfn_ref[...], wo_ref[0:FN_W, :])
         + _dot(yna_ref[...], wo_ref[FN_W:FN_W + NA_W, :])
         + _dot(ygla_ref[...], wo_ref[FN_W + NA_W:, :]))
    x1 = x_ref[...] + mod_ref[2:3, :] * y
    h = _modulated_norm(x1, gain_ref[...], mod_ref[3:4, :], mod_ref[4:5, :]).astype(BF16)
    acc = jnp.zeros(x1.shape, F32)
    for c in range(D_FF // MXU_W):
        cols = slice(c * MXU_W, (c + 1) * MXU_W)
        u = _dot(h, w1_ref[:, cols])
        g = _dot(h, w3_ref[:, cols])
        a = (u * jax.nn.sigmoid(u) * g).astype(BF16)
        acc = acc + _dot(a, w2_ref[cols, :])
    o_ref[...] = x1 + mod_ref[5:6, :] * acc


def _post(x, yfn, yna, ygla, l, mod, mod_row, p, *, tm):
    bsz, t, d = x.shape
    assert t % tm == 0
    row = lambda w_: pl.BlockSpec((None, tm, w_), lambda b, i: (b, i, 0))
    return pl.pallas_call(
        _post_kernel,
        grid=(bsz, t // tm),
        in_specs=[row(d), row(FN_W), row(NA_W), row(GLA_V_PAD),
                  _mod_spec(d, l, mod_row),
                  _layer_spec((1, d), l), _layer_spec((MIX_PAD, d), l), _layer_spec((d, D_FF), l),
                  _layer_spec((d, D_FF), l), _layer_spec((D_FF, d), l)],
        out_specs=row(d),
        out_shape=jax.ShapeDtypeStruct((bsz, t, d), F32),
        compiler_params=_cparams(("arbitrary", "arbitrary")),
        name=f"post_{t}",
    )(x, yfn, yna, ygla, mod, p["gain_f"], p["wo"], p["w1"], p["w3"], p["w2"])


def _dft_tables(n):
    idx = (np.arange(n)[:, None] * np.arange(n)[None, :]) % n
    ang = 2.0 * np.pi * idx / n
    return np.cos(ang), np.sin(ang)


def _shape_consts(t, n):
    c64, s64 = _dft_tables(FN_GD)
    eye = np.eye(FN_GROUPS)
    cs = np.concatenate([np.kron(eye, c64), np.kron(eye, s64)], axis=1)
    chunk_id = np.arange(SUB) // GLA_CHUNK
    same = chunk_id[:, None] == chunk_id[None, :]
    pos = np.arange(SUB)
    tri = np.stack([same & (pos[None, :] <= pos[:, None]), same & (pos[None, :] >= pos[:, None])])
    cpos = np.arange(GLA_BLOCK)
    fwd = cpos[None, :] <= cpos[:, None]
    gla_causal = np.stack([np.tile(fwd, (1, GLA_HEADS)), np.tile(fwd.T, (1, GLA_HEADS))])
    gla_vmask = ((np.arange(GLA_HEADS * GLA_BLOCK)[:, None] // GLA_BLOCK)
                 == (np.arange(GLA_V_PAD)[None, :] // GLA_DV_PAD))

    m = GLA_DK // 4
    inv = ROPE_BASE ** (-np.arange(m) / m)
    tok = np.arange(t)
    cos_h = np.ones((t, GLA_DK_PAD))
    sin_lo = np.zeros((t, GLA_DK_PAD))
    sin_hi = np.zeros((t, GLA_DK_PAD))
    for blk, p in enumerate((tok // GRID_W, tok % GRID_W)):
        ang = p[:, None] * inv[None, :]
        o = blk * 2 * m
        cos_h[:, o:o + m] = np.cos(ang)
        cos_h[:, o + m:o + 2 * m] = np.cos(ang)
        sin_lo[:, o:o + m] = -np.sin(ang)
        sin_hi[:, o + m:o + 2 * m] = np.sin(ang)
    rope = np.stack([np.tile(a, (1, GLA_HEADS)) for a in (cos_h, sin_lo, sin_hi)])

    tc, ts = _dft_tables(t)
    tcc, tsc = _dft_tables(n)
    to_bf16 = lambda a: jnp.asarray(a, F32).astype(BF16)
    return {
        "cs": to_bf16(cs), "tri": jnp.asarray(tri, BF16),
        "gla_causal": jnp.asarray(gla_causal, F32), "gla_vmask": jnp.asarray(gla_vmask, BF16),
        "rope": jnp.asarray(rope, F32), "na_band": jnp.asarray(_na_band_mask(), F32),
        "tc": to_bf16(tc), "ts": to_bf16(ts), "tcc": to_bf16(tcc), "tsc": to_bf16(tsc),
    }


def _bias_kernel(rpb_ref, onehot_ref, ok_ref, o_ref):
    r = rpb_ref[...]
    onehot = onehot_ref[...]
    t = jnp.zeros(o_ref.shape, F32)
    for _ in range(3):
        piece = r.astype(BF16)
        t = t + _dot(piece, onehot)
        r = r - piece.astype(F32)
    o_ref[...] = jnp.where(ok_ref[...] > 0.0, t, NA_MASKED)


def _window_bias_rows(na_rpb):
    depth, heads, n_dr, n_dc = na_rpb.shape
    cq = np.arange(GRID_W)
    c0 = np.clip(cq - NA_KW // 2, 0, GRID_W - NA_KW)
    col_ok = (cq[None, :] >= c0[:, None]) & (cq[None, :] < c0[:, None] + NA_KW)
    dc = np.clip(cq[None, :] - cq[:, None], -(NA_KW - 1), NA_KW - 1) + NA_KW - 1
    n_dc_pad = n_dc + 1
    n_tab = depth * heads * n_dr
    rows_pad = -(-n_tab // 8) * 8
    onehot = (np.arange(n_dc_pad)[:, None, None] == dc[None]).reshape(n_dc_pad, GRID_W * GRID_W)
    rpb2 = jnp.pad(na_rpb.reshape(n_tab, n_dc), ((0, rows_pad - n_tab), (0, 1)))
    t = pl.pallas_call(
        _bias_kernel,
        out_shape=jax.ShapeDtypeStruct((rows_pad, GRID_W * GRID_W), F32),
        name="na_bias",
    )(rpb2, jnp.asarray(onehot, BF16), jnp.asarray(col_ok.reshape(1, -1), F32))
    t = t[:n_tab].reshape(depth * heads, n_dr, GRID_W, GRID_W)
    t = t.transpose(0, 2, 1, 3).reshape(depth * heads, GRID_W, n_dr * GRID_W)
    margin = NA_WIN - NA_KH
    t = jnp.pad(t, ((0, 0), (0, 0), (margin * GRID_W, margin * GRID_W)))
    return t.reshape(depth, heads, GRID_W, (n_dr + 2 * margin) * GRID_W)


def _na_band_mask():
    kinds = _na_group_kinds()
    band = np.zeros((len(kinds), NA_GROUP, NA_WIN), np.float32)
    for kind, (_, a_lo_of) in enumerate(kinds):
        for i in range(NA_GROUP):
            band[kind, i, a_lo_of(i):a_lo_of(i) + NA_KH] = 1.0
    return np.repeat(band, GRID_W, axis=2).reshape(len(kinds) * NA_GROUP, 1, NA_WIN * GRID_W)


def _pad_heads(w, heads, dim, pad, axis=-1):
    axis = axis % w.ndim
    shape = w.shape
    w = w.reshape(shape[:axis] + (heads, dim) + shape[axis + 1:])
    widths = [(0, 0)] * w.ndim
    widths[axis + 1] = (0, pad - dim)
    return jnp.pad(w, widths).reshape(shape[:axis] + (heads * pad,) + shape[axis + 1:])


def _params(norm_mix, norm_ffn, w_in, fnet_w, na_q_norm, na_k_norm, na_rpb, alpha_w, alpha_b, o_norm, w_out,
            ffn_w1, ffn_w3, ffn_w2):
    depth = w_in.shape[0]
    o = FN_W + 3 * NA_W
    gq = w_in[..., o:o + GLA_QK_W]
    gk = w_in[..., o + GLA_QK_W:o + 2 * GLA_QK_W]
    o2 = o + 2 * GLA_QK_W
    gv = w_in[..., o2:o2 + GLA_V_W]
    gg = w_in[..., o2 + GLA_V_W:o2 + 2 * GLA_V_W]
    ga = w_in[..., o2 + 2 * GLA_V_W:]
    w = jnp.concatenate([
        w_in[..., :o],
        jnp.pad(ga, ((0, 0), (0, 0), (0, GLA_A_PAD - 2 * GLA_RANK))),
        _pad_heads(gq, GLA_HEADS, GLA_DK, GLA_DK_PAD), _pad_heads(gk, GLA_HEADS, GLA_DK, GLA_DK_PAD),
        _pad_heads(gv, GLA_HEADS, GLA_DV, GLA_DV_PAD), _pad_heads(gg, GLA_HEADS, GLA_DV, GLA_DV_PAD),
    ], axis=-1).astype(BF16)

    qkg = jnp.stack([jnp.tile(na_q_norm, (1, NA_HEADS)) * (NA_HD ** -0.5),
                     jnp.tile(na_k_norm, (1, NA_HEADS))], axis=1)

    aw_pad = _pad_heads(alpha_w, GLA_HEADS, GLA_DK, GLA_DK_PAD)
    aw = jnp.concatenate([jnp.pad(aw_pad[:, 0], ((0, 0), (0, 0), (0, GLA_QK_PAD))),
                          jnp.pad(aw_pad[:, 1], ((0, 0), (0, 0), (GLA_QK_PAD, 0)))], axis=1)
    aw = jnp.pad(aw, ((0, 0), (0, GLA_A_PAD - 2 * GLA_RANK), (0, 0))).astype(BF16)
    ab = _pad_heads(alpha_b, GLA_HEADS, GLA_DK, GLA_DK_PAD).reshape(depth, 1, 2 * GLA_QK_PAD)

    gain_o = _pad_heads(jnp.tile(o_norm, (1, GLA_HEADS)), GLA_HEADS, GLA_DV, GLA_DV_PAD)[:, None, :]
    wo = jnp.concatenate([w_out[:, :FN_W + NA_W],
                          _pad_heads(w_out[:, FN_W + NA_W:], GLA_HEADS, GLA_DV, GLA_DV_PAD, axis=1)],
                         axis=1).astype(BF16)
    return {
        "gain_m": norm_mix[:, None, :], "gain_f": norm_ffn[:, None, :], "w": w, "qkg": qkg, "aw": aw, "ab": ab,
        "rpb_rows": _window_bias_rows(na_rpb), "gain_o": gain_o, "wo": wo, "fw": fnet_w.astype(BF16),
        "w1": ffn_w1.astype(BF16), "w3": ffn_w3.astype(BF16), "w2": ffn_w2.astype(BF16),
    }


def kernel(x, c, ctx, c_ctx, ada_w, ada_b, norm_mix, norm_ffn, w_in, fnet_w, na_q_norm, na_k_norm, na_rpb,
           gla_alpha_w, gla_alpha_b, gla_o_norm, w_out, ffn_w1, ffn_w3, ffn_w2):
    bsz, t, d = x.shape
    n = ctx.shape[1]
    depth = ada_w.shape[0]
    ctx_row = bsz
    assert bsz < MOD_ROWS and n % SUB == 0 and (bsz * n) % CTX_TM == 0
    consts = _shape_consts(t, n)
    p = _params(norm_mix, norm_ffn, w_in, fnet_w, na_q_norm, na_k_norm, na_rpb, gla_alpha_w, gla_alpha_b,
                gla_o_norm, w_out, ffn_w1, ffn_w3, ffn_w2)

    cc = jnp.concatenate([c, c_ctx[None, :], jnp.zeros((MOD_ROWS - bsz - 1, d), F32)])
    mod = _ada_mod(cc, ada_w, ada_b).reshape(depth, MOD_ROWS, 6, d)

    cx = ctx.reshape(1, bsz * n, d)
    for l in range(depth):
        need_ctx = l < depth - 1
        uv, nqkv, gqk, gvg, gb = _inproj(x, l, mod, None, p, consts, consts["rope"], tm=1024)
        uv_c, nqkv_c, gqk_c, gvg_c, gb_c = [a.reshape(bsz, n, a.shape[-1]) for a in
                                             _inproj(cx, l, mod, ctx_row, p, consts, None, tm=CTX_TM)]

        y_fn = _fourier(uv, l, consts["tc"], consts["ts"], p["fw"], tk=1024)
        y_na, y_na_c = _na(nqkv, nqkv_c, l, p["rpb_rows"], consts["na_band"], need_ctx=need_ctx)
        y_gla, y_gla_c = _gla(gqk, gvg, gb, gqk_c, gvg_c, gb_c, l, p["gain_o"],
                              consts["gla_causal"], consts["gla_vmask"], need_ctx=need_ctx)
        x = _post(x, y_fn, y_na, y_gla, l, mod, None, p, tm=512)
        if need_ctx:
            y_fn_c = _fourier(uv_c, l, consts["tcc"], consts["tsc"], p["fw"], tk=n)
            flat = lambda a: a.reshape(1, bsz * n, a.shape[-1])
            cx = _post(cx, flat(y_fn_c), flat(y_na_c), flat(y_gla_c), l, mod, ctx_row, p, tm=CTX_TM)
    return x
```

```python
import functools

import numpy as np
import jax
import jax.numpy as jnp
from jax import lax
from jax.experimental import pallas as pl
from jax.experimental.pallas import tpu as pltpu

F32 = jnp.float32
BF16 = jnp.bfloat16

D_MODEL = 1024
GRID_W = 64
FN_W = 256
FN_GROUPS = 4
FN_GD = 64
NA_HD = 64
NA_W = 384
NA_HEADS = 6
NA_KH = 8
NA_KW = 16
NA_GROUP = 4
NA_WIN = NA_GROUP + NA_KH - 1
NA_MASKED = -1e30
GLA_HEADS = 4
GLA_DK = 48
GLA_DV = 96
GLA_QK_W = GLA_HEADS * GLA_DK
GLA_V_W = GLA_HEADS * GLA_DV
GLA_RANK = 16
GLA_GATE_NORM = 16.0
GLA_CHUNK = 64
GLA_BLOCK = 2 * GLA_CHUNK
ROPE_BASE = 10000.0
D_FF = 2816
EPS = 1e-6
MOD_ROWS = 16

LANES = 128
MXU_W = 256
VMEM_LIMIT_BYTES = 56 * 1024 * 1024

GLA_DK_PAD = 64
GLA_DV_PAD = 128
GLA_QK_PAD = GLA_HEADS * GLA_DK_PAD
GLA_V_PAD = GLA_HEADS * GLA_DV_PAD
GLA_A_PAD = LANES
MIX_PAD = FN_W + NA_W + GLA_V_PAD

C_FX = 0
C_NQ = C_FX + FN_W
C_NK = C_NQ + NA_W
C_NV = C_NK + NA_W
C_GA = C_NV + NA_W
C_GQ = C_GA + GLA_A_PAD
C_GK = C_GQ + GLA_QK_PAD
C_GV = C_GK + GLA_QK_PAD
C_GG = C_GV + GLA_V_PAD
N_IN_PAD = C_GG + GLA_V_PAD

SUB = 256
CTX_TM = 512


def _cparams(sem):
    return pltpu.CompilerParams(dimension_semantics=sem, vmem_limit_bytes=VMEM_LIMIT_BYTES)


def _const_spec(shape):
    nd = len(shape)
    return pl.BlockSpec(tuple(shape), lambda *_: (0,) * nd, pipeline_mode=pl.Buffered(1))


def _layer_spec(shape, l):
    nd = len(shape)
    return pl.BlockSpec((None,) + tuple(shape), lambda *_: (l,) + (0,) * nd, pipeline_mode=pl.Buffered(1))


def _mod_spec(d, l, mod_row):
    if mod_row is None:
        return pl.BlockSpec((None, None, 6, d), lambda b, i: (l, b, 0, 0))
    return pl.BlockSpec((None, None, 6, d), lambda b, i: (l, mod_row, 0, 0))


def _dot(a, b):
    return jnp.dot(a, b, preferred_element_type=F32)


def _dot_nt(a, b):
    return lax.dot_general(a, b, (((1,), (1,)), ((), ())), preferred_element_type=F32)


def _dot_tn(a, b):
    return lax.dot_general(a, b, (((0,), (0,)), ((), ())), preferred_element_type=F32)


def _ada_kernel(c_ref, w_ref, b_ref, o_ref):
    a = c_ref[...]
    a = (a * jax.nn.sigmoid(a)).astype(BF16)
    o_ref[...] = _dot(a, w_ref[...].astype(BF16)) + b_ref[...]


def _ada_mod(cc, ada_w, ada_b):
    depth, d, n = ada_w.shape
    rows = cc.shape[0]
    tn = 1024
    return pl.pallas_call(
        _ada_kernel,
        grid=(depth, n // tn),
        in_specs=[
            pl.BlockSpec((rows, d), lambda l, j: (0, 0)),
            pl.BlockSpec((None, d, tn), lambda l, j: (l, 0, j)),
            pl.BlockSpec((None, 1, tn), lambda l, j: (l, 0, j)),
        ],
        out_specs=pl.BlockSpec((None, rows, tn), lambda l, j: (l, 0, j)),
        out_shape=jax.ShapeDtypeStruct((depth, rows, n), F32),
        compiler_params=_cparams(("arbitrary", "arbitrary")),
        name="ada_mod",
    )(cc, ada_w, ada_b.reshape(depth, 1, n))


def _modulated_norm(x, gain, shift, scale):
    ms = jnp.mean(x * x, axis=-1, keepdims=True)
    return (x * lax.rsqrt(ms + EPS) * gain) * (1.0 + scale) + shift


def _log_sigmoid(x):
    return jnp.minimum(x, 0.0) - jnp.log1p(jnp.exp(-jnp.abs(x)))


def _head_mean_sq(z):
    first = lax.broadcasted_iota(jnp.int32, (1, LANES), 1) < NA_HD
    out = []
    for t in range(NA_W // LANES):
        sq = z[:, t * LANES:(t + 1) * LANES]
        sq = sq * sq
        lo = jnp.sum(jnp.where(first, sq, 0.0), axis=-1, keepdims=True)
        hi = jnp.sum(jnp.where(first, 0.0, sq), axis=-1, keepdims=True)
        out.append(jnp.where(first, lo, hi))
    return jnp.concatenate(out, axis=1) * (1.0 / NA_HD)


def _inproj_kernel(x_ref, mod_ref, gain_ref, w_ref, cs_ref, qkg_ref, rope_ref, aw_ref, ab_ref, tri_ref,
                   uv_ref, nqkv_ref, gqk_ref, gvg_ref, gb_ref, *, rope, tm):
    x = x_ref[...]
    h = _modulated_norm(x, gain_ref[...], mod_ref[0:1, :], mod_ref[1:2, :]).astype(BF16)

    zf = _dot(h, w_ref[:, C_FX:C_NQ]).astype(BF16)
    zva = _dot(h, w_ref[:, C_NV:C_GQ])
    nqkv_ref[:, 2 * NA_W:3 * NA_W] = zva[:, :NA_W].astype(BF16)
    a = zva[:, NA_W:].astype(BF16)
    zqk = _dot(h, w_ref[:, C_NQ:C_NV])
    logits = _dot(a, aw_ref[...]) + ab_ref[...]
    zg = _dot(h, w_ref[:, C_GQ:C_GV])
    uv = _dot(zf, cs_ref[...])
    for j in range(uv_ref.shape[0]):
        uv_ref[j] = uv[:, j * LANES:(j + 1) * LANES]
    gvg_ref[...] = _dot(h, w_ref[:, C_GV:N_IN_PAD]).astype(BF16)

    glog = (_log_sigmoid(logits) * (1.0 / GLA_GATE_NORM)).astype(BF16)
    for s in range(tm // SUB):
        rows = slice(s * SUB, (s + 1) * SUB)
        gb_ref[rows, 0:GLA_QK_PAD] = _dot(tri_ref[0], glog[rows, 0:GLA_QK_PAD])
        gb_ref[rows, GLA_QK_PAD:] = _dot(tri_ref[1], glog[rows, GLA_QK_PAD:])

    for i in range(2):
        z = zqk[:, i * NA_W:(i + 1) * NA_W]
        z = z * lax.rsqrt(_head_mean_sq(z) + EPS) * qkg_ref[i:i + 1, :]
        nqkv_ref[:, i * NA_W:(i + 1) * NA_W] = z.astype(BF16)

    for i in range(2):
        z = zg[:, i * GLA_QK_PAD:(i + 1) * GLA_QK_PAD]
        if rope:
            z = (z * rope_ref[0]
                 + pltpu.roll(z, GLA_QK_PAD - GLA_DK // 4, 1) * rope_ref[1]
                 + pltpu.roll(z, GLA_DK // 4, 1) * rope_ref[2])
        if i == 0:
            z = z * (GLA_DK ** -0.5)
        gqk_ref[:, i * GLA_QK_PAD:(i + 1) * GLA_QK_PAD] = z.astype(BF16)


def _inproj(x, l, mod, mod_row, p, consts, rope_tab, *, tm):
    bsz, t, d = x.shape
    assert t % tm == 0 and tm % SUB == 0
    rope = rope_tab is not None
    if not rope:
        rope_tab = jnp.zeros((3, tm, GLA_QK_PAD), F32)
    rope_idx = (lambda b, i: (0, i, 0)) if rope else (lambda b, i: (0, 0, 0))
    row = lambda w_: pl.BlockSpec((None, tm, w_), lambda b, i: (b, i, 0))
    uv_tiles = 2 * FN_W // LANES
    out_w = (3 * NA_W, 2 * GLA_QK_PAD, 2 * GLA_V_PAD, 2 * GLA_QK_PAD)
    out_dt = (BF16, BF16, BF16, F32)
    return pl.pallas_call(
        functools.partial(_inproj_kernel, rope=rope, tm=tm),
        grid=(bsz, t // tm),
        in_specs=[
            row(d),
            _mod_spec(d, l, mod_row),
            _layer_spec((1, d), l),
            _layer_spec((d, N_IN_PAD), l),
            _const_spec(consts["cs"].shape),
            _layer_spec((2, NA_W), l),
            pl.BlockSpec((3, tm, GLA_QK_PAD), rope_idx),
            _layer_spec((GLA_A_PAD, 2 * GLA_QK_PAD), l),
            _layer_spec((1, 2 * GLA_QK_PAD), l),
            _const_spec((2, SUB, SUB)),
        ],
        out_specs=[pl.BlockSpec((None, uv_tiles, tm, LANES), lambda b, i: (b, 0, i, 0))]
        + [row(w_) for w_ in out_w],
        out_shape=[jax.ShapeDtypeStruct((bsz, uv_tiles, t, LANES), F32)]
        + [jax.ShapeDtypeStruct((bsz, t, w_), dt) for w_, dt in zip(out_w, out_dt)],
        compiler_params=_cparams(("arbitrary", "arbitrary")),
        name="inproj_rope" if rope else "inproj_ctx",
    )(x, mod, p["gain_m"], p["w"], consts["cs"], p["qkg"], rope_tab, p["aw"], p["ab"], consts["tri"])


def _fourier_kernel(tab_ref, uv_ref, w_ref, o_ref, *, scale):
    half = uv_ref.shape[1] // 2

    def positions(parity):
        tiles = [uv_ref[j, pl.ds(parity, half, stride=2), :].astype(BF16) for j in range(uv_ref.shape[0])]
        return jnp.concatenate(tiles[:2], axis=1), jnp.concatenate(tiles[2:], axis=1)

    ue, ve = positions(0)
    uo, vo = positions(1)
    even = _dot(tab_ref[0], ue) - _dot(tab_ref[1], ve)
    odd = _dot(tab_ref[2], uo) - _dot(tab_ref[3], vo)
    w = w_ref[...]
    o_ref[0:half, :] = _dot(((even + odd) * scale).astype(BF16), w).astype(BF16)
    o_ref[half:, :] = _dot(((even - odd) * scale).astype(BF16), w).astype(BF16)


def _fourier(uv, l, tabs, fnet_w):
    bsz, tiles, t, _ = uv.shape
    scale = float((t * FN_GD) ** -0.5)
    return pl.pallas_call(
        functools.partial(_fourier_kernel, scale=scale),
        grid=(bsz,),
        in_specs=[
            _const_spec(tabs.shape),
            pl.BlockSpec((None, tiles, t, LANES), lambda b: (b, 0, 0, 0)),
            _layer_spec((FN_W, FN_W), l),
        ],
        out_specs=pl.BlockSpec((None, t, FN_W), lambda b: (b, 0, 0)),
        out_shape=jax.ShapeDtypeStruct((bsz, t, FN_W), BF16),
        compiler_params=_cparams(("arbitrary",)),
        name=f"fourier_{t}",
    )(tabs, uv, fnet_w)


def _softmax_pv(parts):
    m = functools.reduce(jnp.maximum, [jnp.max(s, axis=-1, keepdims=True) for s, _ in parts])
    return functools.reduce(jnp.add, [_dot(jnp.exp(s - m).astype(BF16), v) for s, v in parts])


def _na_heads_out(o0, o1, first):
    o = jnp.where(first, o0, o1)
    l = jnp.where(first, pltpu.roll(o0, NA_HD, 1), pltpu.roll(o1, NA_HD, 1))
    return (o / l).astype(BF16)


def _na_group_kinds():
    margin = NA_WIN - NA_KH
    return ((0, lambda i: 0), (NA_KH // 2, lambda i: i), (margin + NA_KH - NA_GROUP, lambda i: margin))


def _na_kernel(q_ref, k_ref, v_ref, qc_ref, kc_ref, vc_ref, rpb_ref, band_ref, o_ref, oc_ref,
               ksplit_ref, vsplit_ref, sc_ref, bias_ref, *, n_rows, need_ctx):
    lane = lax.broadcasted_iota(jnp.int32, (1, LANES), 1)
    first = lane < NA_HD
    zero = jnp.zeros((), BF16)
    one = jnp.ones((), BF16)
    k = k_ref[...]
    ksplit_ref[0] = jnp.where(first, k, zero)
    ksplit_ref[1] = jnp.where(first, zero, k)
    v = v_ref[...]
    vsplit_ref[0] = jnp.where(first, v, one)
    vsplit_ref[1] = jnp.where(first, one, v)
    kc = kc_ref[...]
    kc_split = (jnp.where(first, kc, zero), jnp.where(first, zero, kc))
    vc = vc_ref[...]
    vc_split = (jnp.where(first, vc, one), jnp.where(first, one, vc))

    n_groups = n_rows // NA_GROUP
    gq = NA_GROUP * GRID_W
    n_loc = NA_WIN * GRID_W
    n_loc_pad = sc_ref.shape[-1] - kc_ref.shape[0]

    @pl.when(pl.program_id(1) == 0)
    def _():
        for hh in range(2):
            for kind, (q_off, _) in enumerate(_na_group_kinds()):
                for i in range(NA_GROUP):
                    start = (NA_KH - 1 - q_off - i + NA_WIN - NA_KH) * GRID_W
                    tile = rpb_ref[hh, :, start:start + n_loc]
                    bias_ref[hh, kind, i * GRID_W:(i + 1) * GRID_W, :] = jnp.where(
                        band_ref[kind * NA_GROUP + i] > 0.0, tile, NA_MASKED)

    def window(g):
        ws = jnp.clip(g * NA_GROUP - NA_KH // 2, 0, n_rows - NA_WIN)
        return (pl.ds(pl.multiple_of(g * gq, gq), gq), pl.ds(pl.multiple_of(ws * GRID_W, GRID_W), n_loc))

    def scores(g, slot):
        qs, ks = window(g)
        kind = jnp.where(g == 0, 0, jnp.where(g == n_groups - 1, 2, 1))
        q = q_ref[qs, :]
        for hh in range(2):
            sc_ref[slot, hh, :, 0:n_loc] = _dot_nt(q, ksplit_ref[hh, ks, :]) + bias_ref[hh, kind]
            sc_ref[slot, hh, :, n_loc_pad:] = _dot_nt(q, kc_split[hh])

    def finish(g, slot):
        qs, ks = window(g)
        outs = [_softmax_pv([(sc_ref[slot, hh, :, 0:n_loc], vsplit_ref[hh, ks, :]),
                             (sc_ref[slot, hh, :, n_loc_pad:], vc_split[hh])]) for hh in range(2)]
        o_ref[qs, :] = _na_heads_out(outs[0], outs[1], first)

    scores(0, 0)

    def pair(i, carry):
        scores(2 * i + 1, 1)
        finish(2 * i, 0)
        scores(2 * i + 2, 0)
        finish(2 * i + 1, 1)
        return carry

    lax.fori_loop(0, n_groups // 2 - 1, pair, 0)
    scores(n_groups - 1, 1)
    finish(n_groups - 2, 0)
    finish(n_groups - 1, 1)

    if need_ctx:
        qc = qc_ref[...]
        outs = [_softmax_pv([(_dot_nt(qc, kc_split[hh]), vc_split[hh])]) for hh in range(2)]
        oc_ref[...] = _na_heads_out(outs[0], outs[1], first)
    else:
        oc_ref[...] = jnp.zeros(oc_ref.shape, oc_ref.dtype)


def _na(nqkv, nqkv_c, l, rpb_rows, band, *, need_ctx):
    bsz, t, _ = nqkv.shape
    n = nqkv_c.shape[1]
    n_rows = t // GRID_W
    assert n_rows % (2 * NA_GROUP) == 0 and n_rows >= NA_WIN + NA_GROUP
    pairs = NA_HEADS // 2
    n_loc = NA_WIN * GRID_W
    n_loc_pad = pl.cdiv(n_loc, LANES) * LANES
    nb = NA_W // LANES
    lat = lambda off: pl.BlockSpec((None, t, LANES), lambda j, b: (b, 0, off + j))
    ctx = lambda off: pl.BlockSpec((None, n, LANES), lambda j, b: (b, 0, off + j))
    return pl.pallas_call(
        functools.partial(_na_kernel, n_rows=n_rows, need_ctx=need_ctx),
        grid=(pairs, bsz),
        in_specs=[lat(0), lat(nb), lat(2 * nb), ctx(0), ctx(nb), ctx(2 * nb),
                  pl.BlockSpec((None, 2) + rpb_rows.shape[2:], lambda j, b: (l, j, 0, 0)),
                  _const_spec(band.shape)],
        out_specs=[pl.BlockSpec((None, t, LANES), lambda j, b: (b, 0, j)),
                   pl.BlockSpec((None, n, LANES), lambda j, b: (b, 0, j))],
        out_shape=[jax.ShapeDtypeStruct((bsz, t, NA_W), BF16),
                   jax.ShapeDtypeStruct((bsz, n, NA_W), BF16)],
        scratch_shapes=[pltpu.VMEM((2, t, LANES), BF16), pltpu.VMEM((2, t, LANES), BF16),
                        pltpu.VMEM((2, 2, NA_GROUP * GRID_W, n_loc_pad + n), F32),
                        pltpu.VMEM((2, len(_na_group_kinds()), NA_GROUP * GRID_W, n_loc), F32)],
        compiler_params=_cparams(("arbitrary", "arbitrary")),
        name="na_attn",
    )(nqkv, nqkv, nqkv, nqkv_c, nqkv_c, nqkv_c, rpb_rows, band)


def _gla_steps(jobs, st_ref, causal_ref, vmask_ref):
    c = GLA_CHUNK
    head = lax.broadcasted_iota(jnp.int32, (1, GLA_QK_PAD), 1) // GLA_DK_PAD
    prep = []
    for qk_ref, vg_ref, b_ref, out_ref, ci, d in jobs:
        rows = pl.ds(pl.multiple_of(ci * GLA_BLOCK, GLA_BLOCK), GLA_BLOCK)
        k = qk_ref[rows, GLA_QK_PAD:].astype(F32)
        v = vg_ref[rows, 0:GLA_V_PAD]
        b = b_ref[rows, d * GLA_QK_PAD:(d + 1) * GLA_QK_PAD]
        lo, hi = b[0:c, :], b[c:, :]
        if d == 0:
            tot1, tot2 = lo[c - 1:c, :], hi[c - 1:c, :]
            b_rel = jnp.concatenate([lo - tot1, hi], axis=0)
            b_abs = jnp.concatenate([lo, hi + tot1], axis=0)
        else:
            tot1, tot2 = hi[0:1, :], lo[0:1, :]
            b_rel = jnp.concatenate([lo, hi - tot1], axis=0)
            b_abs = jnp.concatenate([lo + tot1, hi], axis=0)
        tot = tot1 + tot2
        k_in = k * jnp.exp(-b_rel)
        k_end = k * jnp.exp(tot - b_abs)
        q_in = q_st = None
        if out_ref is not None:
            q = qk_ref[rows, 0:GLA_QK_PAD].astype(F32)
            q_in = (q * jnp.exp(b_rel)).astype(BF16)
            q_st = (q * jnp.exp(b_abs)).astype(BF16)
        prep.append((rows, q_in, q_st, k_in, k_end, v, jnp.exp(tot)))

    att = [None if q_in is None else
           _dot_nt(q_in, jnp.concatenate([jnp.where(head == h, k_in, 0.0) for h in range(GLA_HEADS)],
                                         axis=0).astype(BF16))
           for (rows, q_in, q_st, k_in, k_end, v, decay) in prep]

    kv = [[_dot_tn(v[:, h * GLA_DV_PAD:(h + 1) * GLA_DV_PAD], jnp.where(head == h, k_end, 0.0).astype(BF16))
           for h in range(GLA_HEADS)] for (rows, q_in, q_st, k_in, k_end, v, decay) in prep]

    inter = []
    for j, ((qk_ref, vg_ref, b_ref, out_ref, ci, d), (rows, q_in, q_st, k_in, k_end, v, decay)) in enumerate(
            zip(jobs, prep)):
        inter.append(None if q_st is None else _dot_nt(q_st, st_ref[d].astype(BF16)))
        for h in range(GLA_HEADS):
            blk = slice(h * GLA_DV_PAD, (h + 1) * GLA_DV_PAD)
            st_ref[d, blk, :] = st_ref[d, blk, :] * decay + kv[j][h]

    for j, ((qk_ref, vg_ref, b_ref, out_ref, ci, d), (rows, q_in, q_st, k_in, k_end, v, decay)) in enumerate(
            zip(jobs, prep)):
        if out_ref is not None:
            a = (att[j] * causal_ref[d]).astype(BF16)
            v_blk = jnp.concatenate([v] * GLA_HEADS, axis=0) * vmask_ref[...]
            out_ref[rows, :] = (inter[j] + _dot(a, v_blk)).astype(out_ref.dtype)


def _gla_kernel(qk_ref, vg_ref, b_ref, qkc_ref, vgc_ref, bc_ref, causal_ref, vmask_ref, *rest, t, n, need_ctx):
    if need_ctx:
        of_ref, ob_ref, ocf_ref, ocb_ref, st_ref = rest
    else:
        of_ref, ob_ref, st_ref = rest
    st_ref[...] = jnp.zeros(st_ref.shape, F32)
    nb_ctx = n // GLA_BLOCK
    nb_lat = t // GLA_BLOCK

    def jobs(refs, outs, n_blocks, i):
        return [refs + (outs[0], i, 0), refs + (outs[1], n_blocks - 1 - i, 1)]

    ctx_outs = (ocf_ref, ocb_ref) if need_ctx else (None, None)
    for i in range(nb_ctx):
        _gla_steps(jobs((qkc_ref, vgc_ref, bc_ref), ctx_outs, nb_ctx, i), st_ref, causal_ref, vmask_ref)

    def step(i, carry):
        _gla_steps(jobs((qk_ref, vg_ref, b_ref), (of_ref, ob_ref), nb_lat, i), st_ref, causal_ref, vmask_ref)
        return carry

    lax.fori_loop(0, nb_lat, step, 0)


def _gla(gqk, gvg, gb, gqk_c, gvg_c, gb_c, causal, vmask, *, need_ctx):
    bsz, t, _ = gqk.shape
    n = gqk_c.shape[1]
    assert t % GLA_BLOCK == 0 and n % GLA_BLOCK == 0
    row = lambda rows, w_: pl.BlockSpec((None, rows, w_), lambda b: (b, 0, 0))
    out_rows = (t, t, n, n) if need_ctx else (t, t)
    return pl.pallas_call(
        functools.partial(_gla_kernel, t=t, n=n, need_ctx=need_ctx),
        grid=(bsz,),
        in_specs=[row(t, 2 * GLA_QK_PAD), row(t, GLA_V_PAD), row(t, 2 * GLA_QK_PAD),
                  row(n, 2 * GLA_QK_PAD), row(n, GLA_V_PAD), row(n, 2 * GLA_QK_PAD),
                  _const_spec(causal.shape), _const_spec(vmask.shape)],
        out_specs=[row(r, GLA_V_PAD) for r in out_rows],
        out_shape=[jax.ShapeDtypeStruct((bsz, r, GLA_V_PAD), BF16) for r in out_rows],
        scratch_shapes=[pltpu.VMEM((2, GLA_V_PAD, GLA_QK_PAD), F32)],
        compiler_params=_cparams(("arbitrary",)),
        name="gla_scan",
    )(gqk, gvg, gb, gqk_c, gvg_c, gb_c, causal, vmask)


def _gla_out(of_ref, ob_ref, g_ref, gain_ref):
    heads = []
    for h in range(GLA_HEADS):
        cols = slice(h * GLA_DV_PAD, (h + 1) * GLA_DV_PAD)
        o = of_ref[:, cols].astype(F32) + ob_ref[:, cols].astype(F32)
        ms = jnp.sum(o * o, axis=-1, keepdims=True) * (1.0 / GLA_DV)
        g = g_ref[:, cols].astype(F32)
        heads.append((o * lax.rsqrt(ms + EPS) * gain_ref[:, cols] * (g * jax.nn.sigmoid(g))).astype(BF16))
    return jnp.concatenate(heads, axis=1)


def _post_kernel(x_ref, yfn_ref, yna_ref, of_ref, ob_ref, g_ref, mod_ref, gain_o_ref, gain_ref,
                 wo_ref, w1_ref, w3_ref, w2_ref, o_ref):
    y = (_dot(yfn_ref[...], wo_ref[0:FN_W, :])
         + _dot(yna_ref[...], wo_ref[FN_W:FN_W + NA_W, :])
         + _dot(_gla_out(of_ref, ob_ref, g_ref, gain_o_ref), wo_ref[FN_W + NA_W:, :]))
    x1 = x_ref[...] + mod_ref[2:3, :] * y
    h = _modulated_norm(x1, gain_ref[...], mod_ref[3:4, :], mod_ref[4:5, :]).astype(BF16)
    acc = jnp.zeros(x1.shape, F32)
    for c in range(D_FF // MXU_W):
        cols = slice(c * MXU_W, (c + 1) * MXU_W)
        u = _dot(h, w1_ref[:, cols])
        g = _dot(h, w3_ref[:, cols])
        a = (u * jax.nn.sigmoid(u) * g).astype(BF16)
        acc = acc + _dot(a, w2_ref[cols, :])
    o_ref[...] = x1 + mod_ref[5:6, :] * acc


def _post(x, yfn, yna, o_fwd, o_bwd, gvg, l, mod, mod_row, p, *, tm):
    bsz, t, d = x.shape
    assert t % tm == 0
    row = lambda w_: pl.BlockSpec((None, tm, w_), lambda b, i: (b, i, 0))
    gate = pl.BlockSpec((None, tm, GLA_V_PAD), lambda b, i: (b, i, 1))
    return pl.pallas_call(
        _post_kernel,
        grid=(bsz, t // tm),
        in_specs=[row(d), row(FN_W), row(NA_W), row(GLA_V_PAD), row(GLA_V_PAD), gate,
                  _mod_spec(d, l, mod_row), _layer_spec((1, GLA_V_PAD), l),
                  _layer_spec((1, d), l), _layer_spec((MIX_PAD, d), l), _layer_spec((d, D_FF), l),
                  _layer_spec((d, D_FF), l), _layer_spec((D_FF, d), l)],
        out_specs=row(d),
        out_shape=jax.ShapeDtypeStruct((bsz, t, d), F32),
        compiler_params=_cparams(("arbitrary", "arbitrary")),
        name=f"post_{t}",
    )(x, yfn, yna, o_fwd, o_bwd, gvg, mod, p["gain_o"], p["gain_f"], p["wo"], p["w1"], p["w3"], p["w2"])


def _dft_tables(n):
    idx = (np.arange(n)[:, None] * np.arange(n)[None, :]) % n
    ang = 2.0 * np.pi * idx / n
    return np.cos(ang), np.sin(ang)


def _shape_consts(t, n):
    c64, s64 = _dft_tables(FN_GD)
    eye = np.eye(FN_GROUPS)
    cs = np.concatenate([np.kron(eye, c64), np.kron(eye, s64)], axis=1)
    chunk_id = np.arange(SUB) // GLA_CHUNK
    same = chunk_id[:, None] == chunk_id[None, :]
    pos = np.arange(SUB)
    tri = np.stack([same & (pos[None, :] <= pos[:, None]), same & (pos[None, :] >= pos[:, None])])
    cpos = np.arange(GLA_BLOCK)
    fwd = cpos[None, :] <= cpos[:, None]
    gla_causal = np.stack([np.tile(fwd, (1, GLA_HEADS)), np.tile(fwd.T, (1, GLA_HEADS))])
    gla_vmask = ((np.arange(GLA_HEADS * GLA_BLOCK)[:, None] // GLA_BLOCK)
                 == (np.arange(GLA_V_PAD)[None, :] // GLA_DV_PAD))

    m = GLA_DK // 4
    inv = ROPE_BASE ** (-np.arange(m) / m)
    tok = np.arange(t)
    cos_h = np.ones((t, GLA_DK_PAD))
    sin_lo = np.zeros((t, GLA_DK_PAD))
    sin_hi = np.zeros((t, GLA_DK_PAD))
    for blk, p in enumerate((tok // GRID_W, tok % GRID_W)):
        ang = p[:, None] * inv[None, :]
        o = blk * 2 * m
        cos_h[:, o:o + m] = np.cos(ang)
        cos_h[:, o + m:o + 2 * m] = np.cos(ang)
        sin_lo[:, o:o + m] = -np.sin(ang)
        sin_hi[:, o + m:o + 2 * m] = np.sin(ang)
    rope = np.stack([np.tile(a, (1, GLA_HEADS)) for a in (cos_h, sin_lo, sin_hi)])

    def position_tables(length):
        cos, sin = _dft_tables(length)
        half = length // 2
        return np.stack([cos[:half, 0::2], sin[:half, 0::2], cos[:half, 1::2], sin[:half, 1::2]])

    to_bf16 = lambda a: jnp.asarray(a, F32).astype(BF16)
    return {
        "cs": to_bf16(cs), "tri": jnp.asarray(tri, BF16),
        "gla_causal": jnp.asarray(gla_causal, F32), "gla_vmask": jnp.asarray(gla_vmask, BF16),
        "rope": jnp.asarray(rope, F32), "na_band": jnp.asarray(_na_band_mask(), F32),
        "dft_lat": to_bf16(position_tables(t)), "dft_ctx": to_bf16(position_tables(n)),
    }


def _bias_kernel(rpb_ref, onehot_ref, ok_ref, o_ref):
    r = rpb_ref[...]
    onehot = onehot_ref[...]
    t = jnp.zeros(o_ref.shape, F32)
    for _ in range(3):
        piece = r.astype(BF16)
        t = t + _dot(piece, onehot)
        r = r - piece.astype(F32)
    o_ref[...] = jnp.where(ok_ref[...] > 0.0, t, NA_MASKED)


def _window_bias_rows(na_rpb):
    depth, heads, n_dr, n_dc = na_rpb.shape
    cq = np.arange(GRID_W)
    c0 = np.clip(cq - NA_KW // 2, 0, GRID_W - NA_KW)
    col_ok = (cq[None, :] >= c0[:, None]) & (cq[None, :] < c0[:, None] + NA_KW)
    dc = np.clip(cq[None, :] - cq[:, None], -(NA_KW - 1), NA_KW - 1) + NA_KW - 1
    n_dc_pad = n_dc + 1
    n_tab = depth * heads * n_dr
    rows_pad = -(-n_tab // 8) * 8
    onehot = (np.arange(n_dc_pad)[:, None, None] == dc[None]).reshape(n_dc_pad, GRID_W * GRID_W)
    rpb2 = jnp.pad(na_rpb.reshape(n_tab, n_dc), ((0, rows_pad - n_tab), (0, 1)))
    t = pl.pallas_call(
        _bias_kernel,
        out_shape=jax.ShapeDtypeStruct((rows_pad, GRID_W * GRID_W), F32),
        name="na_bias",
    )(rpb2, jnp.asarray(onehot, BF16), jnp.asarray(col_ok.reshape(1, -1), F32))
    t = t[:n_tab].reshape(depth * heads, n_dr, GRID_W, GRID_W)
    t = t.transpose(0, 2, 1, 3).reshape(depth * heads, GRID_W, n_dr * GRID_W)
    margin = NA_WIN - NA_KH
    t = jnp.pad(t, ((0, 0), (0, 0), (margin * GRID_W, margin * GRID_W)))
    return t.reshape(depth, heads, GRID_W, (n_dr + 2 * margin) * GRID_W)


def _na_band_mask():
    kinds = _na_group_kinds()
    band = np.zeros((len(kinds), NA_GROUP, NA_WIN), np.float32)
    for kind, (_, a_lo_of) in enumerate(kinds):
        for i in range(NA_GROUP):
            band[kind, i, a_lo_of(i):a_lo_of(i) + NA_KH] = 1.0
    return np.repeat(band, GRID_W, axis=2).reshape(len(kinds) * NA_GROUP, 1, NA_WIN * GRID_W)


def _pad_heads(w, heads, dim, pad, axis=-1):
    axis = axis % w.ndim
    shape = w.shape
    w = w.reshape(shape[:axis] + (heads, dim) + shape[axis + 1:])
    widths = [(0, 0)] * w.ndim
    widths[axis + 1] = (0, pad - dim)
    return jnp.pad(w, widths).reshape(shape[:axis] + (heads * pad,) + shape[axis + 1:])


def _params(norm_mix, norm_ffn, w_in, fnet_w, na_q_norm, na_k_norm, na_rpb, alpha_w, alpha_b, o_norm, w_out,
            ffn_w1, ffn_w3, ffn_w2):
    depth = w_in.shape[0]
    o = FN_W + 3 * NA_W
    gq = w_in[..., o:o + GLA_QK_W]
    gk = w_in[..., o + GLA_QK_W:o + 2 * GLA_QK_W]
    o2 = o + 2 * GLA_QK_W
    gv = w_in[..., o2:o2 + GLA_V_W]
    gg = w_in[..., o2 + GLA_V_W:o2 + 2 * GLA_V_W]
    ga = w_in[..., o2 + 2 * GLA_V_W:]
    w = jnp.concatenate([
        w_in[..., :o],
        jnp.pad(ga, ((0, 0), (0, 0), (0, GLA_A_PAD - 2 * GLA_RANK))),
        _pad_heads(gq, GLA_HEADS, GLA_DK, GLA_DK_PAD), _pad_heads(gk, GLA_HEADS, GLA_DK, GLA_DK_PAD),
        _pad_heads(gv, GLA_HEADS, GLA_DV, GLA_DV_PAD), _pad_heads(gg, GLA_HEADS, GLA_DV, GLA_DV_PAD),
    ], axis=-1).astype(BF16)

    qkg = jnp.stack([jnp.tile(na_q_norm, (1, NA_HEADS)) * (NA_HD ** -0.5),
                     jnp.tile(na_k_norm, (1, NA_HEADS))], axis=1)

    aw_pad = _pad_heads(alpha_w, GLA_HEADS, GLA_DK, GLA_DK_PAD)
    aw = jnp.concatenate([jnp.pad(aw_pad[:, 0], ((0, 0), (0, 0), (0, GLA_QK_PAD))),
                          jnp.pad(aw_pad[:, 1], ((0, 0), (0, 0), (GLA_QK_PAD, 0)))], axis=1)
    aw = jnp.pad(aw, ((0, 0), (0, GLA_A_PAD - 2 * GLA_RANK), (0, 0))).astype(BF16)
    ab = _pad_heads(alpha_b, GLA_HEADS, GLA_DK, GLA_DK_PAD).reshape(depth, 1, 2 * GLA_QK_PAD)

    gain_o = _pad_heads(jnp.tile(o_norm, (1, GLA_HEADS)), GLA_HEADS, GLA_DV, GLA_DV_PAD)[:, None, :]
    wo = jnp.concatenate([w_out[:, :FN_W + NA_W],
                          _pad_heads(w_out[:, FN_W + NA_W:], GLA_HEADS, GLA_DV, GLA_DV_PAD, axis=1)],
                         axis=1).astype(BF16)
    return {
        "gain_m": norm_mix[:, None, :], "gain_f": norm_ffn[:, None, :], "w": w, "qkg": qkg, "aw": aw, "ab": ab,
        "rpb_rows": _window_bias_rows(na_rpb), "gain_o": gain_o, "wo": wo, "fw": fnet_w.astype(BF16),
        "w1": ffn_w1.astype(BF16), "w3": ffn_w3.astype(BF16), "w2": ffn_w2.astype(BF16),
    }


def kernel(x, c, ctx, c_ctx, ada_w, ada_b, norm_mix, norm_ffn, w_in, fnet_w, na_q_norm, na_k_norm, na_rpb,
           gla_alpha_w, gla_alpha_b, gla_o_norm, w_out, ffn_w1, ffn_w3, ffn_w2):
    bsz, t, d = x.shape
    n = ctx.shape[1]
    depth = ada_w.shape[0]
    ctx_row = bsz
    assert bsz < MOD_ROWS and n % SUB == 0 and (bsz * n) % CTX_TM == 0
    consts = _shape_consts(t, n)
    p = _params(norm_mix, norm_ffn, w_in, fnet_w, na_q_norm, na_k_norm, na_rpb, gla_alpha_w, gla_alpha_b,
                gla_o_norm, w_out, ffn_w1, ffn_w3, ffn_w2)

    cc = jnp.concatenate([c, c_ctx[None, :], jnp.zeros((MOD_ROWS - bsz - 1, d), F32)])
    mod = _ada_mod(cc, ada_w, ada_b).reshape(depth, MOD_ROWS, 6, d)

    cx = ctx
    flat = lambda a: a.reshape(1, bsz * n, a.shape[-1])
    for l in range(depth):
        need_ctx = l < depth - 1
        uv, nqkv, gqk, gvg, gb = _inproj(x, l, mod, None, p, consts, consts["rope"], tm=1024)
        uv_c, nqkv_c, gqk_c, gvg_c, gb_c = _inproj(cx, l, mod, ctx_row, p, consts, None, tm=n)

        y_fn = _fourier(uv, l, consts["dft_lat"], p["fw"])
        y_na, y_na_c = _na(nqkv, nqkv_c, l, p["rpb_rows"], consts["na_band"], need_ctx=need_ctx)
        o_gla = _gla(gqk, gvg, gb, gqk_c, gvg_c, gb_c, consts["gla_causal"], consts["gla_vmask"],
                     need_ctx=need_ctx)
        x = _post(x, y_fn, y_na, o_gla[0], o_gla[1], gvg, l, mod, None, p, tm=512)
        if need_ctx:
            y_fn_c = _fourier(uv_c, l, consts["dft_ctx"], p["fw"])
            cx = _post(flat(cx), flat(y_fn_c), flat(y_na_c), flat(o_gla[2]), flat(o_gla[3]), flat(gvg_c),
                       l, mod, ctx_row, p, tm=CTX_TM).reshape(bsz, n, d)
    return x
```

```python
import functools

import numpy as np
import jax
import jax.numpy as jnp
from jax import lax
from jax.experimental import pallas as pl
from jax.experimental.pallas import tpu as pltpu

F32 = jnp.float32
BF16 = jnp.bfloat16

D_MODEL = 1024
GRID_W = 64
FN_W = 256
FN_GROUPS = 4
FN_GD = 64
NA_HD = 64
NA_W = 384
NA_HEADS = 6
NA_KH = 8
NA_KW = 16
NA_GROUP = 4
NA_WIN = NA_GROUP + NA_KH - 1
NA_MASKED = -1e30
GLA_HEADS = 4
GLA_DK = 48
GLA_DV = 96
GLA_QK_W = GLA_HEADS * GLA_DK
GLA_V_W = GLA_HEADS * GLA_DV
GLA_RANK = 16
GLA_GATE_NORM = 16.0
GLA_CHUNK = 64
GLA_BLOCK = 2 * GLA_CHUNK
GLA_UNROLL = 4
ROPE_BASE = 10000.0
D_FF = 2816
EPS = 1e-6
MOD_ROWS = 16

LANES = 128
MXU_W = 256
VMEM_LIMIT_BYTES = 56 * 1024 * 1024

GLA_DK_PAD = 64
GLA_DV_PAD = 128
GLA_QK_PAD = GLA_HEADS * GLA_DK_PAD
GLA_V_PAD = GLA_HEADS * GLA_DV_PAD
GLA_A_PAD = LANES
MIX_PAD = FN_W + NA_W + GLA_V_PAD

C_FX = 0
C_NQ = C_FX + FN_W
C_NK = C_NQ + NA_W
C_NV = C_NK + NA_W
C_GA = C_NV + NA_W
C_GQ = C_GA + GLA_A_PAD
C_GK = C_GQ + GLA_QK_PAD
C_GV = C_GK + GLA_QK_PAD
C_GG = C_GV + GLA_V_PAD
N_IN_PAD = C_GG + GLA_V_PAD

SUB = 256
CTX_TM = 512


def _cparams(sem):
    return pltpu.CompilerParams(dimension_semantics=sem, vmem_limit_bytes=VMEM_LIMIT_BYTES)


def _const_spec(shape):
    nd = len(shape)
    return pl.BlockSpec(tuple(shape), lambda *_: (0,) * nd, pipeline_mode=pl.Buffered(1))


def _layer_spec(shape, l):
    nd = len(shape)
    return pl.BlockSpec((None,) + tuple(shape), lambda *_: (l,) + (0,) * nd, pipeline_mode=pl.Buffered(1))


def _mod_spec(d, l, mod_row):
    if mod_row is None:
        return pl.BlockSpec((None, None, 6, d), lambda b, i: (l, b, 0, 0))
    return pl.BlockSpec((None, None, 6, d), lambda b, i: (l, mod_row, 0, 0))


def _dot(a, b):
    return jnp.dot(a, b, preferred_element_type=F32)


def _dot_nt(a, b):
    return lax.dot_general(a, b, (((1,), (1,)), ((), ())), preferred_element_type=F32)


def _dot_tn(a, b):
    return lax.dot_general(a, b, (((0,), (0,)), ((), ())), preferred_element_type=F32)


def _ada_kernel(c_ref, w_ref, b_ref, o_ref):
    a = c_ref[...]
    a = (a * jax.nn.sigmoid(a)).astype(BF16)
    o_ref[...] = _dot(a, w_ref[...].astype(BF16)) + b_ref[...]


def _ada_mod(cc, ada_w, ada_b):
    depth, d, n = ada_w.shape
    rows = cc.shape[0]
    tn = 1024
    return pl.pallas_call(
        _ada_kernel,
        grid=(depth, n // tn),
        in_specs=[
            pl.BlockSpec((rows, d), lambda l, j: (0, 0)),
            pl.BlockSpec((None, d, tn), lambda l, j: (l, 0, j)),
            pl.BlockSpec((None, 1, tn), lambda l, j: (l, 0, j)),
        ],
        out_specs=pl.BlockSpec((None, rows, tn), lambda l, j: (l, 0, j)),
        out_shape=jax.ShapeDtypeStruct((depth, rows, n), F32),
        compiler_params=_cparams(("arbitrary", "arbitrary")),
        name="ada_mod",
    )(cc, ada_w, ada_b.reshape(depth, 1, n))


def _modulated_norm(x, gain, shift, scale):
    ms = jnp.mean(x * x, axis=-1, keepdims=True)
    return (x * lax.rsqrt(ms + EPS) * gain) * (1.0 + scale) + shift


def _log_sigmoid(x):
    return jnp.minimum(x, 0.0) - jnp.log1p(jnp.exp(-jnp.abs(x)))


def _head_mean_sq(z):
    first = lax.broadcasted_iota(jnp.int32, (1, LANES), 1) < NA_HD
    out = []
    for t in range(NA_W // LANES):
        sq = z[:, t * LANES:(t + 1) * LANES]
        sq = sq * sq
        lo = jnp.sum(jnp.where(first, sq, 0.0), axis=-1, keepdims=True)
        hi = jnp.sum(jnp.where(first, 0.0, sq), axis=-1, keepdims=True)
        out.append(jnp.where(first, lo, hi))
    return jnp.concatenate(out, axis=1) * (1.0 / NA_HD)


def _inproj_kernel(x_ref, mod_ref, gain_ref, w_ref, cs_ref, qkg_ref, rope_ref, aw_ref, ab_ref, tri_ref,
                   uv_ref, nqkv_ref, gqk_ref, gvg_ref, gb_ref, *, rope, tm):
    x = x_ref[...]
    h = _modulated_norm(x, gain_ref[...], mod_ref[0:1, :], mod_ref[1:2, :]).astype(BF16)

    zf = _dot(h, w_ref[:, C_FX:C_NQ]).astype(BF16)
    zva = _dot(h, w_ref[:, C_NV:C_GQ])
    nqkv_ref[:, 2 * NA_W:3 * NA_W] = zva[:, :NA_W].astype(BF16)
    a = zva[:, NA_W:].astype(BF16)
    zqk = _dot(h, w_ref[:, C_NQ:C_NV])
    logits = _dot(a, aw_ref[...]) + ab_ref[...]
    zg = _dot(h, w_ref[:, C_GQ:C_GV])
    uv = _dot(zf, cs_ref[...])
    for j in range(uv_ref.shape[0]):
        uv_ref[j] = uv[:, j * LANES:(j + 1) * LANES]
    gvg_ref[...] = _dot(h, w_ref[:, C_GV:N_IN_PAD]).astype(BF16)

    glog = (_log_sigmoid(logits) * (1.0 / GLA_GATE_NORM)).astype(BF16)
    for s in range(tm // SUB):
        rows = slice(s * SUB, (s + 1) * SUB)
        gb_ref[rows, 0:GLA_QK_PAD] = _dot(tri_ref[0], glog[rows, 0:GLA_QK_PAD])
        gb_ref[rows, GLA_QK_PAD:] = _dot(tri_ref[1], glog[rows, GLA_QK_PAD:])

    for i in range(2):
        z = zqk[:, i * NA_W:(i + 1) * NA_W]
        z = z * lax.rsqrt(_head_mean_sq(z) + EPS) * qkg_ref[i:i + 1, :]
        nqkv_ref[:, i * NA_W:(i + 1) * NA_W] = z.astype(BF16)

    for i in range(2):
        z = zg[:, i * GLA_QK_PAD:(i + 1) * GLA_QK_PAD]
        if rope:
            z = (z * rope_ref[0]
                 + pltpu.roll(z, GLA_QK_PAD - GLA_DK // 4, 1) * rope_ref[1]
                 + pltpu.roll(z, GLA_DK // 4, 1) * rope_ref[2])
        if i == 0:
            z = z * (GLA_DK ** -0.5)
        gqk_ref[:, i * GLA_QK_PAD:(i + 1) * GLA_QK_PAD] = z.astype(BF16)


def _inproj(x, l, mod, mod_row, p, consts, rope_tab, *, tm):
    bsz, t, d = x.shape
    assert t % tm == 0 and tm % SUB == 0
    rope = rope_tab is not None
    if not rope:
        rope_tab = jnp.zeros((3, tm, GLA_QK_PAD), F32)
    rope_idx = (lambda b, i: (0, i, 0)) if rope else (lambda b, i: (0, 0, 0))
    row = lambda w_: pl.BlockSpec((None, tm, w_), lambda b, i: (b, i, 0))
    uv_tiles = 2 * FN_W // LANES
    out_w = (3 * NA_W, 2 * GLA_QK_PAD, 2 * GLA_V_PAD, 2 * GLA_QK_PAD)
    out_dt = (BF16, BF16, BF16, F32)
    return pl.pallas_call(
        functools.partial(_inproj_kernel, rope=rope, tm=tm),
        grid=(bsz, t // tm),
        in_specs=[
            row(d),
            _mod_spec(d, l, mod_row),
            _layer_spec((1, d), l),
            _layer_spec((d, N_IN_PAD), l),
            _const_spec(consts["cs"].shape),
            _layer_spec((2, NA_W), l),
            pl.BlockSpec((3, tm, GLA_QK_PAD), rope_idx),
            _layer_spec((GLA_A_PAD, 2 * GLA_QK_PAD), l),
            _layer_spec((1, 2 * GLA_QK_PAD), l),
            _const_spec((2, SUB, SUB)),
        ],
        out_specs=[pl.BlockSpec((None, uv_tiles, tm, LANES), lambda b, i: (b, 0, i, 0))]
        + [row(w_) for w_ in out_w],
        out_shape=[jax.ShapeDtypeStruct((bsz, uv_tiles, t, LANES), F32)]
        + [jax.ShapeDtypeStruct((bsz, t, w_), dt) for w_, dt in zip(out_w, out_dt)],
        compiler_params=_cparams(("arbitrary", "arbitrary")),
        name="inproj_rope" if rope else "inproj_ctx",
    )(x, mod, p["gain_m"], p["w"], consts["cs"], p["qkg"], rope_tab, p["aw"], p["ab"], consts["tri"])


def _fourier_kernel(tab_ref, uv_ref, w_ref, o_ref, *, scale):
    half = uv_ref.shape[1] // 2

    def positions(parity):
        tiles = [uv_ref[j, pl.ds(parity, half, stride=2), :].astype(BF16) for j in range(uv_ref.shape[0])]
        return jnp.concatenate(tiles[:2], axis=1), jnp.concatenate(tiles[2:], axis=1)

    ue, ve = positions(0)
    uo, vo = positions(1)
    even = _dot(tab_ref[0], ue) - _dot(tab_ref[1], ve)
    odd = _dot(tab_ref[2], uo) - _dot(tab_ref[3], vo)
    w = w_ref[...]
    o_ref[0:half, :] = _dot(((even + odd) * scale).astype(BF16), w).astype(BF16)
    o_ref[half:, :] = _dot(((even - odd) * scale).astype(BF16), w).astype(BF16)


def _fourier(uv, l, tabs, fnet_w):
    bsz, tiles, t, _ = uv.shape
    scale = float((t * FN_GD) ** -0.5)
    return pl.pallas_call(
        functools.partial(_fourier_kernel, scale=scale),
        grid=(bsz,),
        in_specs=[
            _const_spec(tabs.shape),
            pl.BlockSpec((None, tiles, t, LANES), lambda b: (b, 0, 0, 0)),
            _layer_spec((FN_W, FN_W), l),
        ],
        out_specs=pl.BlockSpec((None, t, FN_W), lambda b: (b, 0, 0)),
        out_shape=jax.ShapeDtypeStruct((bsz, t, FN_W), BF16),
        compiler_params=_cparams(("arbitrary",)),
        name=f"fourier_{t}",
    )(tabs, uv, fnet_w)


def _softmax_pv(parts):
    m = functools.reduce(jnp.maximum, [jnp.max(s, axis=-1, keepdims=True) for s, _ in parts])
    return functools.reduce(jnp.add, [_dot(jnp.exp(s - m).astype(BF16), v) for s, v in parts])


def _na_heads_out(o0, o1, first):
    o = jnp.where(first, o0, o1)
    l = jnp.where(first, pltpu.roll(o0, NA_HD, 1), pltpu.roll(o1, NA_HD, 1))
    return (o / l).astype(BF16)


def _na_group_kinds():
    margin = NA_WIN - NA_KH
    return ((0, lambda i: 0), (NA_KH // 2, lambda i: i), (margin + NA_KH - NA_GROUP, lambda i: margin))


def _na_kernel(q_ref, k_ref, v_ref, qc_ref, kc_ref, vc_ref, rpb_ref, band_ref, o_ref, oc_ref,
               vsplit_ref, sc_ref, bias_ref, *, n_rows, need_ctx):
    lane = lax.broadcasted_iota(jnp.int32, (1, LANES), 1)
    first = lane < NA_HD
    zero = jnp.zeros((), BF16)
    one = jnp.ones((), BF16)
    v = v_ref[...]
    vsplit_ref[0] = jnp.where(first, v, one)
    vsplit_ref[1] = jnp.where(first, one, v)
    kc = kc_ref[...]
    vc = vc_ref[...]
    vc_split = (jnp.where(first, vc, one), jnp.where(first, one, vc))

    def both_heads(q):
        return jnp.concatenate([jnp.where(first, q, zero), jnp.where(first, zero, q)], axis=0)

    n_groups = n_rows // NA_GROUP
    gq = NA_GROUP * GRID_W
    n_loc = NA_WIN * GRID_W
    n_loc_pad = sc_ref.shape[-1] - kc_ref.shape[0]

    @pl.when(pl.program_id(1) == 0)
    def _():
        for hh in range(2):
            for kind, (q_off, _) in enumerate(_na_group_kinds()):
                for i in range(NA_GROUP):
                    start = (NA_KH - 1 - q_off - i + NA_WIN - NA_KH) * GRID_W
                    tile = rpb_ref[hh, :, start:start + n_loc]
                    bias_ref[hh, kind, i * GRID_W:(i + 1) * GRID_W, :] = jnp.where(
                        band_ref[kind * NA_GROUP + i] > 0.0, tile, NA_MASKED)

    def window(g):
        ws = jnp.clip(g * NA_GROUP - NA_KH // 2, 0, n_rows - NA_WIN)
        return (pl.ds(pl.multiple_of(g * gq, gq), gq), pl.ds(pl.multiple_of(ws * GRID_W, GRID_W), n_loc))

    def scores(g, slot):
        qs, ks = window(g)
        kind = jnp.where(g == 0, 0, jnp.where(g == n_groups - 1, 2, 1))
        q = both_heads(q_ref[qs, :])
        s_loc = _dot_nt(q, k_ref[ks, :])
        s_ctx = _dot_nt(q, kc)
        for hh in range(2):
            sc_ref[slot, hh, :, 0:n_loc] = s_loc[hh * gq:(hh + 1) * gq] + bias_ref[hh, kind]
            sc_ref[slot, hh, :, n_loc_pad:] = s_ctx[hh * gq:(hh + 1) * gq]

    def finish(g, slot):
        qs, ks = window(g)
        outs = [_softmax_pv([(sc_ref[slot, hh, :, 0:n_loc], vsplit_ref[hh, ks, :]),
                             (sc_ref[slot, hh, :, n_loc_pad:], vc_split[hh])]) for hh in range(2)]
        o_ref[qs, :] = _na_heads_out(outs[0], outs[1], first)

    scores(0, 0)

    def pair(i, carry):
        scores(2 * i + 1, 1)
        finish(2 * i, 0)
        scores(2 * i + 2, 0)
        finish(2 * i + 1, 1)
        return carry

    lax.fori_loop(0, n_groups // 2 - 1, pair, 0)
    scores(n_groups - 1, 1)
    finish(n_groups - 2, 0)
    finish(n_groups - 1, 1)

    if need_ctx:
        n_ctx = qc_ref.shape[0]
        s = _dot_nt(both_heads(qc_ref[...]), kc)
        outs = [_softmax_pv([(s[hh * n_ctx:(hh + 1) * n_ctx], vc_split[hh])]) for hh in range(2)]
        oc_ref[...] = _na_heads_out(outs[0], outs[1], first)
    else:
        oc_ref[...] = jnp.zeros(oc_ref.shape, oc_ref.dtype)


def _na(nqkv, nqkv_c, l, rpb_rows, band, *, need_ctx):
    bsz, t, _ = nqkv.shape
    n = nqkv_c.shape[1]
    n_rows = t // GRID_W
    assert n_rows % (2 * NA_GROUP) == 0 and n_rows >= NA_WIN + NA_GROUP
    pairs = NA_HEADS // 2
    n_loc = NA_WIN * GRID_W
    n_loc_pad = pl.cdiv(n_loc, LANES) * LANES
    nb = NA_W // LANES
    lat = lambda off: pl.BlockSpec((None, t, LANES), lambda j, b: (b, 0, off + j))
    ctx = lambda off: pl.BlockSpec((None, n, LANES), lambda j, b: (b, 0, off + j))
    return pl.pallas_call(
        functools.partial(_na_kernel, n_rows=n_rows, need_ctx=need_ctx),
        grid=(pairs, bsz),
        in_specs=[lat(0), lat(nb), lat(2 * nb), ctx(0), ctx(nb), ctx(2 * nb),
                  pl.BlockSpec((None, 2) + rpb_rows.shape[2:], lambda j, b: (l, j, 0, 0)),
                  _const_spec(band.shape)],
        out_specs=[pl.BlockSpec((None, t, LANES), lambda j, b: (b, 0, j)),
                   pl.BlockSpec((None, n, LANES), lambda j, b: (b, 0, j))],
        out_shape=[jax.ShapeDtypeStruct((bsz, t, NA_W), BF16),
                   jax.ShapeDtypeStruct((bsz, n, NA_W), BF16)],
        scratch_shapes=[pltpu.VMEM((2, t, LANES), BF16),
                        pltpu.VMEM((2, 2, NA_GROUP * GRID_W, n_loc_pad + n), F32),
                        pltpu.VMEM((2, len(_na_group_kinds()), NA_GROUP * GRID_W, n_loc), F32)],
        compiler_params=_cparams(("arbitrary", "arbitrary")),
        name="na_attn",
    )(nqkv, nqkv, nqkv, nqkv_c, nqkv_c, nqkv_c, rpb_rows, band)


def _gla_steps(jobs, st_ref, causal_ref, vmask_ref):
    c = GLA_CHUNK
    head = lax.broadcasted_iota(jnp.int32, (1, GLA_QK_PAD), 1) // GLA_DK_PAD
    prep = []
    for qk_ref, vg_ref, b_ref, out_ref, ci, d in jobs:
        rows = pl.ds(pl.multiple_of(ci * GLA_BLOCK, GLA_BLOCK), GLA_BLOCK)
        k = qk_ref[rows, GLA_QK_PAD:].astype(F32)
        v = vg_ref[rows, 0:GLA_V_PAD]
        b = b_ref[rows, d * GLA_QK_PAD:(d + 1) * GLA_QK_PAD]
        lo, hi = b[0:c, :], b[c:, :]
        if d == 0:
            tot1, tot2 = lo[c - 1:c, :], hi[c - 1:c, :]
            b_rel = jnp.concatenate([lo - tot1, hi], axis=0)
        else:
            tot1, tot2 = hi[0:1, :], lo[0:1, :]
            b_rel = jnp.concatenate([lo, hi - tot1], axis=0)
        k_in = k * jnp.exp(-b_rel)
        k_end = k_in * jnp.exp(tot2)
        q_in = q_st = None
        if out_ref is not None:
            q_rel = qk_ref[rows, 0:GLA_QK_PAD].astype(F32) * jnp.exp(b_rel)
            q_in = q_rel.astype(BF16)
            q_st = (q_rel * jnp.exp(tot1)).astype(BF16)
        prep.append((rows, q_in, q_st, k_in, k_end, v, jnp.exp(tot1 + tot2)))

    att = [None if q_in is None else
           _dot_nt(q_in, jnp.concatenate([jnp.where(head == h, k_in, 0.0) for h in range(GLA_HEADS)],
                                         axis=0).astype(BF16))
           for (rows, q_in, q_st, k_in, k_end, v, decay) in prep]

    kv = [[_dot_tn(v[:, h * GLA_DV_PAD:(h + 1) * GLA_DV_PAD], jnp.where(head == h, k_end, 0.0).astype(BF16))
           for h in range(GLA_HEADS)] for (rows, q_in, q_st, k_in, k_end, v, decay) in prep]

    inter = []
    for j, ((qk_ref, vg_ref, b_ref, out_ref, ci, d), (rows, q_in, q_st, k_in, k_end, v, decay)) in enumerate(
            zip(jobs, prep)):
        inter.append(None if q_st is None else _dot_nt(q_st, st_ref[d].astype(BF16)))
        for h in range(GLA_HEADS):
            blk = slice(h * GLA_DV_PAD, (h + 1) * GLA_DV_PAD)
            st_ref[d, blk, :] = st_ref[d, blk, :] * decay + kv[j][h]

    for j, ((qk_ref, vg_ref, b_ref, out_ref, ci, d), (rows, q_in, q_st, k_in, k_end, v, decay)) in enumerate(
            zip(jobs, prep)):
        if out_ref is not None:
            a = (att[j] * causal_ref[d]).astype(BF16)
            intra = []
            for pair in range(GLA_HEADS // 2):
                cols = slice(pair * 2 * GLA_DV_PAD, (pair + 1) * 2 * GLA_DV_PAD)
                v_pair = jnp.concatenate([v[:, cols]] * 2, axis=0) * vmask_ref[...]
                intra.append(_dot(a[:, pair * 2 * GLA_BLOCK:(pair + 1) * 2 * GLA_BLOCK], v_pair))
            out_ref[rows, :] = (inter[j] + jnp.concatenate(intra, axis=1)).astype(out_ref.dtype)


def _gla_kernel(qk_ref, vg_ref, b_ref, qkc_ref, vgc_ref, bc_ref, causal_ref, vmask_ref, *rest, t, n, need_ctx):
    if need_ctx:
        of_ref, ob_ref, ocf_ref, ocb_ref, st_ref = rest
    else:
        of_ref, ob_ref, st_ref = rest
    st_ref[...] = jnp.zeros(st_ref.shape, F32)
    nb_ctx = n // GLA_BLOCK
    nb_lat = t // GLA_BLOCK

    def jobs(refs, outs, n_blocks, i):
        return [refs + (outs[0], i, 0), refs + (outs[1], n_blocks - 1 - i, 1)]

    ctx_outs = (ocf_ref, ocb_ref) if need_ctx else (None, None)
    for i in range(nb_ctx):
        _gla_steps(jobs((qkc_ref, vgc_ref, bc_ref), ctx_outs, nb_ctx, i), st_ref, causal_ref, vmask_ref)

    def step(i, carry):
        _gla_steps(jobs((qk_ref, vg_ref, b_ref), (of_ref, ob_ref), nb_lat, i), st_ref, causal_ref, vmask_ref)
        return carry

    lax.fori_loop(0, nb_lat, step, 0, unroll=GLA_UNROLL)


def _gla(gqk, gvg, gb, gqk_c, gvg_c, gb_c, causal, vmask, *, need_ctx):
    bsz, t, _ = gqk.shape
    n = gqk_c.shape[1]
    assert t % GLA_BLOCK == 0 and n % GLA_BLOCK == 0
    row = lambda rows, w_: pl.BlockSpec((None, rows, w_), lambda b: (b, 0, 0))
    out_rows = (t, t, n, n) if need_ctx else (t, t)
    return pl.pallas_call(
        functools.partial(_gla_kernel, t=t, n=n, need_ctx=need_ctx),
        grid=(bsz,),
        in_specs=[row(t, 2 * GLA_QK_PAD), row(t, GLA_V_PAD), row(t, 2 * GLA_QK_PAD),
                  row(n, 2 * GLA_QK_PAD), row(n, GLA_V_PAD), row(n, 2 * GLA_QK_PAD),
                  _const_spec(causal.shape), _const_spec(vmask.shape)],
        out_specs=[row(r, GLA_V_PAD) for r in out_rows],
        out_shape=[jax.ShapeDtypeStruct((bsz, r, GLA_V_PAD), BF16) for r in out_rows],
        scratch_shapes=[pltpu.VMEM((2, GLA_V_PAD, GLA_QK_PAD), F32)],
        compiler_params=_cparams(("arbitrary",)),
        name="gla_scan",
    )(gqk, gvg, gb, gqk_c, gvg_c, gb_c, causal, vmask)


def _gla_out(of_ref, ob_ref, g_ref, gain_ref):
    heads = []
    for h in range(GLA_HEADS):
        cols = slice(h * GLA_DV_PAD, (h + 1) * GLA_DV_PAD)
        o = of_ref[:, cols].astype(F32) + ob_ref[:, cols].astype(F32)
        ms = jnp.sum(o * o, axis=-1, keepdims=True) * (1.0 / GLA_DV)
        g = g_ref[:, cols].astype(F32)
        heads.append((o * lax.rsqrt(ms + EPS) * gain_ref[:, cols] * (g * jax.nn.sigmoid(g))).astype(BF16))
    return jnp.concatenate(heads, axis=1)


def _post_kernel(x_ref, yfn_ref, yna_ref, of_ref, ob_ref, g_ref, mod_ref, gain_o_ref, gain_ref,
                 wo_ref, w1_ref, w3_ref, w2_ref, o_ref):
    y = (_dot(yfn_ref[...], wo_ref[0:FN_W, :])
         + _dot(yna_ref[...], wo_ref[FN_W:FN_W + NA_W, :])
         + _dot(_gla_out(of_ref, ob_ref, g_ref, gain_o_ref), wo_ref[FN_W + NA_W:, :]))
    x1 = x_ref[...] + mod_ref[2:3, :] * y
    h = _modulated_norm(x1, gain_ref[...], mod_ref[3:4, :], mod_ref[4:5, :]).astype(BF16)
    acc = jnp.zeros(x1.shape, F32)
    for c in range(D_FF // MXU_W):
        cols = slice(c * MXU_W, (c + 1) * MXU_W)
        u = _dot(h, w1_ref[:, cols])
        g = _dot(h, w3_ref[:, cols])
        a = (u * jax.nn.sigmoid(u) * g).astype(BF16)
        acc = acc + _dot(a, w2_ref[cols, :])
    o_ref[...] = x1 + mod_ref[5:6, :] * acc


def _post(x, yfn, yna, o_fwd, o_bwd, gvg, l, mod, mod_row, p, *, tm):
    bsz, t, d = x.shape
    assert t % tm == 0
    row = lambda w_: pl.BlockSpec((None, tm, w_), lambda b, i: (b, i, 0))
    gate = pl.BlockSpec((None, tm, GLA_V_PAD), lambda b, i: (b, i, 1))
    return pl.pallas_call(
        _post_kernel,
        grid=(bsz, t // tm),
        in_specs=[row(d), row(FN_W), row(NA_W), row(GLA_V_PAD), row(GLA_V_PAD), gate,
                  _mod_spec(d, l, mod_row), _layer_spec((1, GLA_V_PAD), l),
                  _layer_spec((1, d), l), _layer_spec((MIX_PAD, d), l), _layer_spec((d, D_FF), l),
                  _layer_spec((d, D_FF), l), _layer_spec((D_FF, d), l)],
        out_specs=row(d),
        out_shape=jax.ShapeDtypeStruct((bsz, t, d), F32),
        compiler_params=_cparams(("arbitrary", "arbitrary")),
        name=f"post_{t}",
    )(x, yfn, yna, o_fwd, o_bwd, gvg, mod, p["gain_o"], p["gain_f"], p["wo"], p["w1"], p["w3"], p["w2"])


def _dft_tables(n):
    idx = (np.arange(n)[:, None] * np.arange(n)[None, :]) % n
    ang = 2.0 * np.pi * idx / n
    return np.cos(ang), np.sin(ang)


def _shape_consts(t, n):
    c64, s64 = _dft_tables(FN_GD)
    eye = np.eye(FN_GROUPS)
    cs = np.concatenate([np.kron(eye, c64), np.kron(eye, s64)], axis=1)
    chunk_id = np.arange(SUB) // GLA_CHUNK
    same = chunk_id[:, None] == chunk_id[None, :]
    pos = np.arange(SUB)
    tri = np.stack([same & (pos[None, :] <= pos[:, None]), same & (pos[None, :] >= pos[:, None])])
    cpos = np.arange(GLA_BLOCK)
    fwd = cpos[None, :] <= cpos[:, None]
    gla_causal = np.stack([np.tile(fwd, (1, GLA_HEADS)), np.tile(fwd.T, (1, GLA_HEADS))])
    gla_vmask = ((np.arange(2 * GLA_BLOCK)[:, None] // GLA_BLOCK)
                 == (np.arange(2 * GLA_DV_PAD)[None, :] // GLA_DV_PAD))

    m = GLA_DK // 4
    inv = ROPE_BASE ** (-np.arange(m) / m)
    tok = np.arange(t)
    cos_h = np.ones((t, GLA_DK_PAD))
    sin_lo = np.zeros((t, GLA_DK_PAD))
    sin_hi = np.zeros((t, GLA_DK_PAD))
    for blk, p in enumerate((tok // GRID_W, tok % GRID_W)):
        ang = p[:, None] * inv[None, :]
        o = blk * 2 * m
        cos_h[:, o:o + m] = np.cos(ang)
        cos_h[:, o + m:o + 2 * m] = np.cos(ang)
        sin_lo[:, o:o + m] = -np.sin(ang)
        sin_hi[:, o + m:o + 2 * m] = np.sin(ang)
    rope = np.stack([np.tile(a, (1, GLA_HEADS)) for a in (cos_h, sin_lo, sin_hi)])

    def position_tables(length):
        cos, sin = _dft_tables(length)
        half = length // 2
        return np.stack([cos[:half, 0::2], sin[:half, 0::2], cos[:half, 1::2], sin[:half, 1::2]])

    to_bf16 = lambda a: jnp.asarray(a, F32).astype(BF16)
    return {
        "cs": to_bf16(cs), "tri": jnp.asarray(tri, BF16),
        "gla_causal": jnp.asarray(gla_causal, F32), "gla_vmask": jnp.asarray(gla_vmask, BF16),
        "rope": jnp.asarray(rope, F32), "na_band": jnp.asarray(_na_band_mask(), F32),
        "dft_lat": to_bf16(position_tables(t)), "dft_ctx": to_bf16(position_tables(n)),
    }


def _bias_kernel(rpb_ref, onehot_ref, ok_ref, o_ref):
    r = rpb_ref[...]
    onehot = onehot_ref[...]
    t = jnp.zeros(o_ref.shape, F32)
    for _ in range(3):
        piece = r.astype(BF16)
        t = t + _dot(piece, onehot)
        r = r - piece.astype(F32)
    o_ref[...] = jnp.where(ok_ref[...] > 0.0, t, NA_MASKED)


def _window_bias_rows(na_rpb):
    depth, heads, n_dr, n_dc = na_rpb.shape
    cq = np.arange(GRID_W)
    c0 = np.clip(cq - NA_KW // 2, 0, GRID_W - NA_KW)
    col_ok = (cq[None, :] >= c0[:, None]) & (cq[None, :] < c0[:, None] + NA_KW)
    dc = np.clip(cq[None, :] - cq[:, None], -(NA_KW - 1), NA_KW - 1) + NA_KW - 1
    n_dc_pad = n_dc + 1
    n_tab = depth * heads * n_dr
    rows_pad = -(-n_tab // 8) * 8
    onehot = (np.arange(n_dc_pad)[:, None, None] == dc[None]).reshape(n_dc_pad, GRID_W * GRID_W)
    rpb2 = jnp.pad(na_rpb.reshape(n_tab, n_dc), ((0, rows_pad - n_tab), (0, 1)))
    t = pl.pallas_call(
        _bias_kernel,
        out_shape=jax.ShapeDtypeStruct((rows_pad, GRID_W * GRID_W), F32),
        name="na_bias",
    )(rpb2, jnp.asarray(onehot, BF16), jnp.asarray(col_ok.reshape(1, -1), F32))
    t = t[:n_tab].reshape(depth * heads, n_dr, GRID_W, GRID_W)
    t = t.transpose(0, 2, 1, 3).reshape(depth * heads, GRID_W, n_dr * GRID_W)
    margin = NA_WIN - NA_KH
    t = jnp.pad(t, ((0, 0), (0, 0), (margin * GRID_W, margin * GRID_W)))
    return t.reshape(depth, heads, GRID_W, (n_dr + 2 * margin) * GRID_W)


def _na_band_mask():
    kinds = _na_group_kinds()
    band = np.zeros((len(kinds), NA_GROUP, NA_WIN), np.float32)
    for kind, (_, a_lo_of) in enumerate(kinds):
        for i in range(NA_GROUP):
            band[kind, i, a_lo_of(i):a_lo_of(i) + NA_KH] = 1.0
    return np.repeat(band, GRID_W, axis=2).reshape(len(kinds) * NA_GROUP, 1, NA_WIN * GRID_W)


def _pad_heads(w, heads, dim, pad, axis=-1):
    axis = axis % w.ndim
    shape = w.shape
    w = w.reshape(shape[:axis] + (heads, dim) + shape[axis + 1:])
    widths = [(0, 0)] * w.ndim
    widths[axis + 1] = (0, pad - dim)
    return jnp.pad(w, widths).reshape(shape[:axis] + (heads * pad,) + shape[axis + 1:])


def _params(norm_mix, norm_ffn, w_in, fnet_w, na_q_norm, na_k_norm, na_rpb, alpha_w, alpha_b, o_norm, w_out,
            ffn_w1, ffn_w3, ffn_w2):
    depth = w_in.shape[0]
    o = FN_W + 3 * NA_W
    gq = w_in[..., o:o + GLA_QK_W]
    gk = w_in[..., o + GLA_QK_W:o + 2 * GLA_QK_W]
    o2 = o + 2 * GLA_QK_W
    gv = w_in[..., o2:o2 + GLA_V_W]
    gg = w_in[..., o2 + GLA_V_W:o2 + 2 * GLA_V_W]
    ga = w_in[..., o2 + 2 * GLA_V_W:]
    w = jnp.concatenate([
        w_in[..., :o],
        jnp.pad(ga, ((0, 0), (0, 0), (0, GLA_A_PAD - 2 * GLA_RANK))),
        _pad_heads(gq, GLA_HEADS, GLA_DK, GLA_DK_PAD), _pad_heads(gk, GLA_HEADS, GLA_DK, GLA_DK_PAD),
        _pad_heads(gv, GLA_HEADS, GLA_DV, GLA_DV_PAD), _pad_heads(gg, GLA_HEADS, GLA_DV, GLA_DV_PAD),
    ], axis=-1).astype(BF16)

    qkg = jnp.stack([jnp.tile(na_q_norm, (1, NA_HEADS)) * (NA_HD ** -0.5),
                     jnp.tile(na_k_norm, (1, NA_HEADS))], axis=1)

    aw_pad = _pad_heads(alpha_w, GLA_HEADS, GLA_DK, GLA_DK_PAD)
    aw = jnp.concatenate([jnp.pad(aw_pad[:, 0], ((0, 0), (0, 0), (0, GLA_QK_PAD))),
                          jnp.pad(aw_pad[:, 1], ((0, 0), (0, 0), (GLA_QK_PAD, 0)))], axis=1)
    aw = jnp.pad(aw, ((0, 0), (0, GLA_A_PAD - 2 * GLA_RANK), (0, 0))).astype(BF16)
    ab = _pad_heads(alpha_b, GLA_HEADS, GLA_DK, GLA_DK_PAD).reshape(depth, 1, 2 * GLA_QK_PAD)

    gain_o = _pad_heads(jnp.tile(o_norm, (1, GLA_HEADS)), GLA_HEADS, GLA_DV, GLA_DV_PAD)[:, None, :]
    wo = jnp.concatenate([w_out[:, :FN_W + NA_W],
                          _pad_heads(w_out[:, FN_W + NA_W:], GLA_HEADS, GLA_DV, GLA_DV_PAD, axis=1)],
                         axis=1).astype(BF16)
    return {
        "gain_m": norm_mix[:, None, :], "gain_f": norm_ffn[:, None, :], "w": w, "qkg": qkg, "aw": aw, "ab": ab,
        "rpb_rows": _window_bias_rows(na_rpb), "gain_o": gain_o, "wo": wo, "fw": fnet_w.astype(BF16),
        "w1": ffn_w1.astype(BF16), "w3": ffn_w3.astype(BF16), "w2": ffn_w2.astype(BF16),
    }


def kernel(x, c, ctx, c_ctx, ada_w, ada_b, norm_mix, norm_ffn, w_in, fnet_w, na_q_norm, na_k_norm, na_rpb,
           gla_alpha_w, gla_alpha_b, gla_o_norm, w_out, ffn_w1, ffn_w3, ffn_w2):
    bsz, t, d = x.shape
    n = ctx.shape[1]
    depth = ada_w.shape[0]
    ctx_row = bsz
    assert bsz < MOD_ROWS and n % SUB == 0 and (bsz * n) % CTX_TM == 0
    consts = _shape_consts(t, n)
    p = _params(norm_mix, norm_ffn, w_in, fnet_w, na_q_norm, na_k_norm, na_rpb, gla_alpha_w, gla_alpha_b,
                gla_o_norm, w_out, ffn_w1, ffn_w3, ffn_w2)

    cc = jnp.concatenate([c, c_ctx[None, :], jnp.zeros((MOD_ROWS - bsz - 1, d), F32)])
    mod = _ada_mod(cc, ada_w, ada_b).reshape(depth, MOD_ROWS, 6, d)

    cx = ctx
    flat = lambda a: a.reshape(1, bsz * n, a.shape[-1])
    for l in range(depth):
        need_ctx = l < depth - 1
        uv, nqkv, gqk, gvg, gb = _inproj(x, l, mod, None, p, consts, consts["rope"], tm=1024)
        uv_c, nqkv_c, gqk_c, gvg_c, gb_c = _inproj(cx, l, mod, ctx_row, p, consts, None, tm=n)

        y_fn = _fourier(uv, l, consts["dft_lat"], p["fw"])
        y_na, y_na_c = _na(nqkv, nqkv_c, l, p["rpb_rows"], consts["na_band"], need_ctx=need_ctx)
        o_gla = _gla(gqk, gvg, gb, gqk_c, gvg_c, gb_c, consts["gla_causal"], consts["gla_vmask"],
                     need_ctx=need_ctx)
        x = _post(x, y_fn, y_na, o_gla[0], o_gla[1], gvg, l, mod, None, p, tm=512)
        if need_ctx:
            y_fn_c = _fourier(uv_c, l, consts["dft_ctx"], p["fw"])
            cx = _post(flat(cx), flat(y_fn_c), flat(y_na_c), flat(o_gla[2]), flat(o_gla[3]), flat(gvg_c),
                       l, mod, ctx_row, p, tm=CTX_TM).reshape(bsz, n, d)
    return x
```

```python
import functools

import numpy as np
import jax
import jax.numpy as jnp
from jax import lax
from jax.experimental import pallas as pl
from jax.experimental.pallas import tpu as pltpu

F32 = jnp.float32
BF16 = jnp.bfloat16

D_MODEL = 1024
GRID_W = 64
FN_W = 256
FN_GROUPS = 4
FN_GD = 64
NA_HD = 64
NA_W = 384
NA_HEADS = 6
NA_KH = 8
NA_KW = 16
NA_GROUP = 4
NA_WIN = NA_GROUP + NA_KH - 1
NA_MASKED = -1e30
GLA_HEADS = 4
GLA_DK = 48
GLA_DV = 96
GLA_QK_W = GLA_HEADS * GLA_DK
GLA_V_W = GLA_HEADS * GLA_DV
GLA_RANK = 16
GLA_GATE_NORM = 16.0
GLA_CHUNK = 64
GLA_BLOCK = 2 * GLA_CHUNK
GLA_UNROLL = 4
ROPE_BASE = 10000.0
D_FF = 2816
EPS = 1e-6
MOD_ROWS = 16

LANES = 128
MXU_W = 256
VMEM_LIMIT_BYTES = 56 * 1024 * 1024

GLA_DK_PAD = 64
GLA_DV_PAD = 128
GLA_QK_PAD = GLA_HEADS * GLA_DK_PAD
GLA_V_PAD = GLA_HEADS * GLA_DV_PAD
GLA_A_PAD = LANES
MIX_PAD = FN_W + NA_W + GLA_V_PAD

C_FX = 0
C_NQ = C_FX + FN_W
C_NK = C_NQ + NA_W
C_NV = C_NK + NA_W
C_GA = C_NV + NA_W
C_GQ = C_GA + GLA_A_PAD
C_GK = C_GQ + GLA_QK_PAD
C_GV = C_GK + GLA_QK_PAD
C_GG = C_GV + GLA_V_PAD
N_IN_PAD = C_GG + GLA_V_PAD

SUB = 256
CTX_TM = 512


def _cparams(sem):
    return pltpu.CompilerParams(dimension_semantics=sem, vmem_limit_bytes=VMEM_LIMIT_BYTES)


def _const_spec(shape):
    nd = len(shape)
    return pl.BlockSpec(tuple(shape), lambda *_: (0,) * nd, pipeline_mode=pl.Buffered(1))


def _layer_spec(shape, l):
    nd = len(shape)
    return pl.BlockSpec((None,) + tuple(shape), lambda *_: (l,) + (0,) * nd, pipeline_mode=pl.Buffered(1))


def _mod_spec(d, l, mod_row):
    if mod_row is None:
        return pl.BlockSpec((None, None, 6, d), lambda b, i: (l, b, 0, 0))
    return pl.BlockSpec((None, None, 6, d), lambda b, i: (l, mod_row, 0, 0))


def _dot(a, b):
    return jnp.dot(a, b, preferred_element_type=F32)


def _dot_nt(a, b):
    return lax.dot_general(a, b, (((1,), (1,)), ((), ())), preferred_element_type=F32)


def _ada_kernel(c_ref, w_ref, b_ref, o_ref):
    a = c_ref[...]
    a = (a * jax.nn.sigmoid(a)).astype(BF16)
    o_ref[...] = _dot(a, w_ref[...].astype(BF16)) + b_ref[...]


def _ada_mod(cc, ada_w, ada_b):
    depth, d, n = ada_w.shape
    rows = cc.shape[0]
    tn = 1024
    return pl.pallas_call(
        _ada_kernel,
        grid=(depth, n // tn),
        in_specs=[
            pl.BlockSpec((rows, d), lambda l, j: (0, 0)),
            pl.BlockSpec((None, d, tn), lambda l, j: (l, 0, j)),
            pl.BlockSpec((None, 1, tn), lambda l, j: (l, 0, j)),
        ],
        out_specs=pl.BlockSpec((None, rows, tn), lambda l, j: (l, 0, j)),
        out_shape=jax.ShapeDtypeStruct((depth, rows, n), F32),
        compiler_params=_cparams(("arbitrary", "arbitrary")),
        name="ada_mod",
    )(cc, ada_w, ada_b.reshape(depth, 1, n))


def _modulated_norm(x, gain, shift, scale):
    ms = jnp.mean(x * x, axis=-1, keepdims=True)
    return (x * lax.rsqrt(ms + EPS) * gain) * (1.0 + scale) + shift


def _log_sigmoid(x):
    return jnp.minimum(x, 0.0) - jnp.log1p(jnp.exp(-jnp.abs(x)))


def _head_mean_sq(z):
    first = lax.broadcasted_iota(jnp.int32, (1, LANES), 1) < NA_HD
    out = []
    for t in range(NA_W // LANES):
        sq = z[:, t * LANES:(t + 1) * LANES]
        sq = sq * sq
        lo = jnp.sum(jnp.where(first, sq, 0.0), axis=-1, keepdims=True)
        hi = jnp.sum(jnp.where(first, 0.0, sq), axis=-1, keepdims=True)
        out.append(jnp.where(first, lo, hi))
    return jnp.concatenate(out, axis=1) * (1.0 / NA_HD)


def _inproj_kernel(x_ref, mod_ref, gain_ref, w_ref, cs_ref, qkg_ref, rope_ref, aw_ref, ab_ref, tri_ref,
                   uv_ref, nqkv_ref, gqk_ref, gvg_ref, gb_ref, *, rope, tm):
    x = x_ref[...]
    h = _modulated_norm(x, gain_ref[...], mod_ref[0:1, :], mod_ref[1:2, :]).astype(BF16)

    zf = _dot(h, w_ref[:, C_FX:C_NQ]).astype(BF16)
    zva = _dot(h, w_ref[:, C_NV:C_GQ])
    nqkv_ref[:, 2 * NA_W:3 * NA_W] = zva[:, :NA_W].astype(BF16)
    a = zva[:, NA_W:].astype(BF16)
    zqk = _dot(h, w_ref[:, C_NQ:C_NV])
    logits = _dot(a, aw_ref[...]) + ab_ref[...]
    zg = _dot(h, w_ref[:, C_GQ:C_GV])
    uv = _dot(zf, cs_ref[...])
    for j in range(uv_ref.shape[0]):
        uv_ref[j] = uv[:, j * LANES:(j + 1) * LANES]
    gvg_ref[...] = _dot(h, w_ref[:, C_GV:N_IN_PAD]).astype(BF16)

    glog = (_log_sigmoid(logits) * (1.0 / GLA_GATE_NORM)).astype(BF16)
    for s in range(tm // SUB):
        rows = slice(s * SUB, (s + 1) * SUB)
        gb_ref[rows, 0:GLA_QK_PAD] = _dot(tri_ref[0], glog[rows, 0:GLA_QK_PAD])
        gb_ref[rows, GLA_QK_PAD:] = _dot(tri_ref[1], glog[rows, GLA_QK_PAD:])

    for i in range(2):
        z = zqk[:, i * NA_W:(i + 1) * NA_W]
        z = z * lax.rsqrt(_head_mean_sq(z) + EPS) * qkg_ref[i:i + 1, :]
        nqkv_ref[:, i * NA_W:(i + 1) * NA_W] = z.astype(BF16)

    for i in range(2):
        z = zg[:, i * GLA_QK_PAD:(i + 1) * GLA_QK_PAD]
        if rope:
            z = (z * rope_ref[0]
                 + pltpu.roll(z, GLA_QK_PAD - GLA_DK // 4, 1) * rope_ref[1]
                 + pltpu.roll(z, GLA_DK // 4, 1) * rope_ref[2])
        if i == 0:
            z = z * (GLA_DK ** -0.5)
        gqk_ref[:, i * GLA_QK_PAD:(i + 1) * GLA_QK_PAD] = z.astype(BF16)


def _inproj(x, l, mod, mod_row, p, consts, rope_tab, *, tm):
    bsz, t, d = x.shape
    assert t % tm == 0 and tm % SUB == 0
    rope = rope_tab is not None
    if not rope:
        rope_tab = jnp.zeros((3, tm, GLA_QK_PAD), F32)
    rope_idx = (lambda b, i: (0, i, 0)) if rope else (lambda b, i: (0, 0, 0))
    row = lambda w_: pl.BlockSpec((None, tm, w_), lambda b, i: (b, i, 0))
    uv_tiles = 2 * FN_W // LANES
    out_w = (3 * NA_W, 2 * GLA_QK_PAD, 2 * GLA_V_PAD, 2 * GLA_QK_PAD)
    out_dt = (BF16, BF16, BF16, F32)
    return pl.pallas_call(
        functools.partial(_inproj_kernel, rope=rope, tm=tm),
        grid=(bsz, t // tm),
        in_specs=[
            row(d),
            _mod_spec(d, l, mod_row),
            _layer_spec((1, d), l),
            _layer_spec((d, N_IN_PAD), l),
            _const_spec(consts["cs"].shape),
            _layer_spec((2, NA_W), l),
            pl.BlockSpec((3, tm, GLA_QK_PAD), rope_idx),
            _layer_spec((GLA_A_PAD, 2 * GLA_QK_PAD), l),
            _layer_spec((1, 2 * GLA_QK_PAD), l),
            _const_spec((2, SUB, SUB)),
        ],
        out_specs=[pl.BlockSpec((None, uv_tiles, tm, LANES), lambda b, i: (b, 0, i, 0))]
        + [row(w_) for w_ in out_w],
        out_shape=[jax.ShapeDtypeStruct((bsz, uv_tiles, t, LANES), F32)]
        + [jax.ShapeDtypeStruct((bsz, t, w_), dt) for w_, dt in zip(out_w, out_dt)],
        compiler_params=_cparams(("arbitrary", "arbitrary")),
        name="inproj_rope" if rope else "inproj_ctx",
    )(x, mod, p["gain_m"], p["w"], consts["cs"], p["qkg"], rope_tab, p["aw"], p["ab"], consts["tri"])


def _fourier_kernel(tab_ref, uv_ref, w_ref, o_ref, *, scale):
    half = uv_ref.shape[1] // 2

    def positions(parity):
        tiles = [uv_ref[j, pl.ds(parity, half, stride=2), :].astype(BF16) for j in range(uv_ref.shape[0])]
        return jnp.concatenate(tiles[:2], axis=1), jnp.concatenate(tiles[2:], axis=1)

    ue, ve = positions(0)
    uo, vo = positions(1)
    even = _dot(tab_ref[0], ue) - _dot(tab_ref[1], ve)
    odd = _dot(tab_ref[2], uo) - _dot(tab_ref[3], vo)
    w = w_ref[...]
    o_ref[0:half, :] = _dot(((even + odd) * scale).astype(BF16), w).astype(BF16)
    o_ref[half:, :] = _dot(((even - odd) * scale).astype(BF16), w).astype(BF16)


def _fourier(uv, l, tabs, fnet_w):
    bsz, tiles, t, _ = uv.shape
    scale = float((t * FN_GD) ** -0.5)
    return pl.pallas_call(
        functools.partial(_fourier_kernel, scale=scale),
        grid=(bsz,),
        in_specs=[
            _const_spec(tabs.shape),
            pl.BlockSpec((None, tiles, t, LANES), lambda b: (b, 0, 0, 0)),
            _layer_spec((FN_W, FN_W), l),
        ],
        out_specs=pl.BlockSpec((None, t, FN_W), lambda b: (b, 0, 0)),
        out_shape=jax.ShapeDtypeStruct((bsz, t, FN_W), BF16),
        compiler_params=_cparams(("arbitrary",)),
        name=f"fourier_{t}",
    )(tabs, uv, fnet_w)


def _softmax_pv(parts):
    m = functools.reduce(jnp.maximum, [jnp.max(s, axis=-1, keepdims=True) for s, _ in parts])
    return functools.reduce(jnp.add, [_dot(jnp.exp(s - m).astype(BF16), v) for s, v in parts])


def _na_heads_out(o0, o1, first):
    o = jnp.where(first, o0, o1)
    l = jnp.where(first, pltpu.roll(o0, NA_HD, 1), pltpu.roll(o1, NA_HD, 1))
    return (o / l).astype(BF16)


def _na_group_kinds():
    margin = NA_WIN - NA_KH
    return ((0, lambda i: 0), (NA_KH // 2, lambda i: i), (margin + NA_KH - NA_GROUP, lambda i: margin))


def _na_kernel(q_ref, k_ref, v_ref, qc_ref, kc_ref, vc_ref, rpb_ref, band_ref, o_ref, oc_ref,
               vsplit_ref, sc_ref, bias_ref, *, n_rows, need_ctx):
    lane = lax.broadcasted_iota(jnp.int32, (1, LANES), 1)
    first = lane < NA_HD
    zero = jnp.zeros((), BF16)
    one = jnp.ones((), BF16)
    v = v_ref[...]
    vsplit_ref[0] = jnp.where(first, v, one)
    vsplit_ref[1] = jnp.where(first, one, v)
    kc = kc_ref[...]
    vc = vc_ref[...]
    vc_split = (jnp.where(first, vc, one), jnp.where(first, one, vc))

    def both_heads(q):
        return jnp.concatenate([jnp.where(first, q, zero), jnp.where(first, zero, q)], axis=0)

    n_groups = n_rows // NA_GROUP
    gq = NA_GROUP * GRID_W
    n_loc = NA_WIN * GRID_W
    n_loc_pad = sc_ref.shape[-1] - kc_ref.shape[0]

    @pl.when(pl.program_id(1) == 0)
    def _():
        for hh in range(2):
            for kind, (q_off, _) in enumerate(_na_group_kinds()):
                for i in range(NA_GROUP):
                    start = (NA_KH - 1 - q_off - i + NA_WIN - NA_KH) * GRID_W
                    tile = rpb_ref[hh, :, start:start + n_loc]
                    bias_ref[hh, kind, i * GRID_W:(i + 1) * GRID_W, :] = jnp.where(
                        band_ref[kind * NA_GROUP + i] > 0.0, tile, NA_MASKED)

    def window(g):
        ws = jnp.clip(g * NA_GROUP - NA_KH // 2, 0, n_rows - NA_WIN)
        return (pl.ds(pl.multiple_of(g * gq, gq), gq), pl.ds(pl.multiple_of(ws * GRID_W, GRID_W), n_loc))

    def scores(g, slot):
        qs, ks = window(g)
        kind = jnp.where(g == 0, 0, jnp.where(g == n_groups - 1, 2, 1))
        q = both_heads(q_ref[qs, :])
        s_loc = _dot_nt(q, k_ref[ks, :])
        s_ctx = _dot_nt(q, kc)
        for hh in range(2):
            sc_ref[slot, hh, :, 0:n_loc] = s_loc[hh * gq:(hh + 1) * gq] + bias_ref[hh, kind]
            sc_ref[slot, hh, :, n_loc_pad:] = s_ctx[hh * gq:(hh + 1) * gq]

    def finish(g, slot):
        qs, ks = window(g)
        outs = [_softmax_pv([(sc_ref[slot, hh, :, 0:n_loc], vsplit_ref[hh, ks, :]),
                             (sc_ref[slot, hh, :, n_loc_pad:], vc_split[hh])]) for hh in range(2)]
        o_ref[qs, :] = _na_heads_out(outs[0], outs[1], first)

    scores(0, 0)

    def pair(i, carry):
        scores(2 * i + 1, 1)
        finish(2 * i, 0)
        scores(2 * i + 2, 0)
        finish(2 * i + 1, 1)
        return carry

    lax.fori_loop(0, n_groups // 2 - 1, pair, 0)
    scores(n_groups - 1, 1)
    finish(n_groups - 2, 0)
    finish(n_groups - 1, 1)

    if need_ctx:
        n_ctx = qc_ref.shape[0]
        s = _dot_nt(both_heads(qc_ref[...]), kc)
        outs = [_softmax_pv([(s[hh * n_ctx:(hh + 1) * n_ctx], vc_split[hh])]) for hh in range(2)]
        oc_ref[...] = _na_heads_out(outs[0], outs[1], first)
    else:
        oc_ref[...] = jnp.zeros(oc_ref.shape, oc_ref.dtype)


def _na(nqkv, nqkv_c, l, rpb_rows, band, *, need_ctx):
    bsz, t, _ = nqkv.shape
    n = nqkv_c.shape[1]
    n_rows = t // GRID_W
    assert n_rows % (2 * NA_GROUP) == 0 and n_rows >= NA_WIN + NA_GROUP
    pairs = NA_HEADS // 2
    n_loc = NA_WIN * GRID_W
    n_loc_pad = pl.cdiv(n_loc, LANES) * LANES
    nb = NA_W // LANES
    lat = lambda off: pl.BlockSpec((None, t, LANES), lambda j, b: (b, 0, off + j))
    ctx = lambda off: pl.BlockSpec((None, n, LANES), lambda j, b: (b, 0, off + j))
    return pl.pallas_call(
        functools.partial(_na_kernel, n_rows=n_rows, need_ctx=need_ctx),
        grid=(pairs, bsz),
        in_specs=[lat(0), lat(nb), lat(2 * nb), ctx(0), ctx(nb), ctx(2 * nb),
                  pl.BlockSpec((None, 2) + rpb_rows.shape[2:], lambda j, b: (l, j, 0, 0)),
                  _const_spec(band.shape)],
        out_specs=[pl.BlockSpec((None, t, LANES), lambda j, b: (b, 0, j)),
                   pl.BlockSpec((None, n, LANES), lambda j, b: (b, 0, j))],
        out_shape=[jax.ShapeDtypeStruct((bsz, t, NA_W), BF16),
                   jax.ShapeDtypeStruct((bsz, n, NA_W), BF16)],
        scratch_shapes=[pltpu.VMEM((2, t, LANES), BF16),
                        pltpu.VMEM((2, 2, NA_GROUP * GRID_W, n_loc_pad + n), F32),
                        pltpu.VMEM((2, len(_na_group_kinds()), NA_GROUP * GRID_W, n_loc), F32)],
        compiler_params=_cparams(("arbitrary", "arbitrary")),
        name="na_attn",
    )(nqkv, nqkv, nqkv, nqkv_c, nqkv_c, nqkv_c, rpb_rows, band)


def _gla_steps(jobs, st_ref, causal_ref, vmask_ref):
    c = GLA_CHUNK
    head = lax.broadcasted_iota(jnp.int32, (1, GLA_QK_PAD), 1) // GLA_DK_PAD
    prep = []
    for qk_ref, vg_ref, b_ref, out_ref, ci, d in jobs:
        rows = pl.ds(pl.multiple_of(ci * GLA_BLOCK, GLA_BLOCK), GLA_BLOCK)
        k = qk_ref[rows, GLA_QK_PAD:].astype(F32)
        v = vg_ref[rows, 0:GLA_V_PAD]
        b = b_ref[rows, d * GLA_QK_PAD:(d + 1) * GLA_QK_PAD]
        lo, hi = b[0:c, :], b[c:, :]
        if d == 0:
            tot1, tot2 = lo[c - 1:c, :], hi[c - 1:c, :]
            b_rel = jnp.concatenate([lo - tot1, hi], axis=0)
        else:
            tot1, tot2 = hi[0:1, :], lo[0:1, :]
            b_rel = jnp.concatenate([lo, hi - tot1], axis=0)
        k_in = k * jnp.exp(-b_rel)
        k_end = k_in * jnp.exp(tot2)
        q_in = q_st = None
        if out_ref is not None:
            q_rel = qk_ref[rows, 0:GLA_QK_PAD].astype(F32) * jnp.exp(b_rel)
            q_in = q_rel.astype(BF16)
            q_st = (q_rel * jnp.exp(tot1)).astype(BF16)
        decay = jnp.exp(jnp.broadcast_to(tot1 + tot2, (GLA_DV_PAD, GLA_QK_PAD)).T)
        prep.append((rows, q_in, q_st, k_in, k_end, v, decay))

    att = [None if q_in is None else
           _dot_nt(q_in, jnp.concatenate([jnp.where(head == h, k_in, 0.0) for h in range(GLA_HEADS)],
                                         axis=0).astype(BF16))
           for (rows, q_in, q_st, k_in, k_end, v, decay) in prep]

    kv = []
    for (rows, q_in, q_st, k_in, k_end, v, decay) in prep:
        k_t = k_end.astype(BF16).T
        kv.append(jnp.concatenate(
            [_dot(k_t[h * GLA_DK_PAD:(h + 1) * GLA_DK_PAD, :], v[:, h * GLA_DV_PAD:(h + 1) * GLA_DV_PAD])
             for h in range(GLA_HEADS)], axis=0))

    inter = []
    zero = jnp.zeros((GLA_DK_PAD, GLA_DV_PAD), BF16)
    for j, ((qk_ref, vg_ref, b_ref, out_ref, ci, d), (rows, q_in, q_st, k_in, k_end, v, decay)) in enumerate(
            zip(jobs, prep)):
        st = st_ref[d]
        if q_st is None:
            inter.append(None)
        else:
            sb = st.astype(BF16)
            s_bd = jnp.concatenate(
                [jnp.concatenate([sb[h * GLA_DK_PAD:(h + 1) * GLA_DK_PAD, :] if h2 == h else zero
                                  for h2 in range(GLA_HEADS)], axis=1) for h in range(GLA_HEADS)], axis=0)
            inter.append(_dot(q_st, s_bd))
        st_ref[d] = st * decay + kv[j]

    for j, ((qk_ref, vg_ref, b_ref, out_ref, ci, d), (rows, q_in, q_st, k_in, k_end, v, decay)) in enumerate(
            zip(jobs, prep)):
        if out_ref is not None:
            a = (att[j] * causal_ref[d]).astype(BF16)
            intra = []
            for pair in range(GLA_HEADS // 2):
                cols = slice(pair * 2 * GLA_DV_PAD, (pair + 1) * 2 * GLA_DV_PAD)
                v_pair = jnp.concatenate([v[:, cols]] * 2, axis=0) * vmask_ref[...]
                intra.append(_dot(a[:, pair * 2 * GLA_BLOCK:(pair + 1) * 2 * GLA_BLOCK], v_pair))
            out_ref[rows, :] = (inter[j] + jnp.concatenate(intra, axis=1)).astype(out_ref.dtype)


def _gla_kernel(qk_ref, vg_ref, b_ref, qkc_ref, vgc_ref, bc_ref, causal_ref, vmask_ref, *rest, t, n, need_ctx):
    if need_ctx:
        of_ref, ob_ref, ocf_ref, ocb_ref, st_ref = rest
    else:
        of_ref, ob_ref, st_ref = rest
    st_ref[...] = jnp.zeros(st_ref.shape, F32)
    nb_ctx = n // GLA_BLOCK
    nb_lat = t // GLA_BLOCK

    def jobs(refs, outs, n_blocks, i):
        return [refs + (outs[0], i, 0), refs + (outs[1], n_blocks - 1 - i, 1)]

    ctx_outs = (ocf_ref, ocb_ref) if need_ctx else (None, None)
    for i in range(nb_ctx):
        _gla_steps(jobs((qkc_ref, vgc_ref, bc_ref), ctx_outs, nb_ctx, i), st_ref, causal_ref, vmask_ref)

    def step(i, carry):
        _gla_steps(jobs((qk_ref, vg_ref, b_ref), (of_ref, ob_ref), nb_lat, i), st_ref, causal_ref, vmask_ref)
        return carry

    lax.fori_loop(0, nb_lat, step, 0, unroll=GLA_UNROLL)


def _gla(gqk, gvg, gb, gqk_c, gvg_c, gb_c, causal, vmask, *, need_ctx):
    bsz, t, _ = gqk.shape
    n = gqk_c.shape[1]
    assert t % GLA_BLOCK == 0 and n % GLA_BLOCK == 0
    row = lambda rows, w_: pl.BlockSpec((None, rows, w_), lambda b: (b, 0, 0))
    out_rows = (t, t, n, n) if need_ctx else (t, t)
    return pl.pallas_call(
        functools.partial(_gla_kernel, t=t, n=n, need_ctx=need_ctx),
        grid=(bsz,),
        in_specs=[row(t, 2 * GLA_QK_PAD), row(t, GLA_V_PAD), row(t, 2 * GLA_QK_PAD),
                  row(n, 2 * GLA_QK_PAD), row(n, GLA_V_PAD), row(n, 2 * GLA_QK_PAD),
                  _const_spec(causal.shape), _const_spec(vmask.shape)],
        out_specs=[row(r, GLA_V_PAD) for r in out_rows],
        out_shape=[jax.ShapeDtypeStruct((bsz, r, GLA_V_PAD), BF16) for r in out_rows],
        scratch_shapes=[pltpu.VMEM((2, GLA_QK_PAD, GLA_DV_PAD), F32)],
        compiler_params=_cparams(("arbitrary",)),
        name="gla_scan",
    )(gqk, gvg, gb, gqk_c, gvg_c, gb_c, causal, vmask)


def _gla_out(of_ref, ob_ref, g_ref, gain_ref):
    heads = []
    for h in range(GLA_HEADS):
        cols = slice(h * GLA_DV_PAD, (h + 1) * GLA_DV_PAD)
        o = of_ref[:, cols].astype(F32) + ob_ref[:, cols].astype(F32)
        ms = jnp.sum(o * o, axis=-1, keepdims=True) * (1.0 / GLA_DV)
        g = g_ref[:, cols].astype(F32)
        heads.append((o * lax.rsqrt(ms + EPS) * gain_ref[:, cols] * (g * jax.nn.sigmoid(g))).astype(BF16))
    return jnp.concatenate(heads, axis=1)


def _post_kernel(x_ref, yfn_ref, yna_ref, of_ref, ob_ref, g_ref, mod_ref, gain_o_ref, gain_ref,
                 wo_ref, w1_ref, w3_ref, w2_ref, o_ref):
    y = (_dot(yfn_ref[...], wo_ref[0:FN_W, :])
         + _dot(yna_ref[...], wo_ref[FN_W:FN_W + NA_W, :])
         + _dot(_gla_out(of_ref, ob_ref, g_ref, gain_o_ref), wo_ref[FN_W + NA_W:, :]))
    x1 = x_ref[...] + mod_ref[2:3, :] * y
    h = _modulated_norm(x1, gain_ref[...], mod_ref[3:4, :], mod_ref[4:5, :]).astype(BF16)
    acc = jnp.zeros(x1.shape, F32)
    for c in range(D_FF // MXU_W):
        cols = slice(c * MXU_W, (c + 1) * MXU_W)
        u = _dot(h, w1_ref[:, cols])
        g = _dot(h, w3_ref[:, cols])
        a = (u * jax.nn.sigmoid(u) * g).astype(BF16)
        acc = acc + _dot(a, w2_ref[cols, :])
    o_ref[...] = x1 + mod_ref[5:6, :] * acc


def _post(x, yfn, yna, o_fwd, o_bwd, gvg, l, mod, mod_row, p, *, tm):
    bsz, t, d = x.shape
    assert t % tm == 0
    row = lambda w_: pl.BlockSpec((None, tm, w_), lambda b, i: (b, i, 0))
    gate = pl.BlockSpec((None, tm, GLA_V_PAD), lambda b, i: (b, i, 1))
    return pl.pallas_call(
        _post_kernel,
        grid=(bsz, t // tm),
        in_specs=[row(d), row(FN_W), row(NA_W), row(GLA_V_PAD), row(GLA_V_PAD), gate,
                  _mod_spec(d, l, mod_row), _layer_spec((1, GLA_V_PAD), l),
                  _layer_spec((1, d), l), _layer_spec((MIX_PAD, d), l), _layer_spec((d, D_FF), l),
                  _layer_spec((d, D_FF), l), _layer_spec((D_FF, d), l)],
        out_specs=row(d),
        out_shape=jax.ShapeDtypeStruct((bsz, t, d), F32),
        compiler_params=_cparams(("arbitrary", "arbitrary")),
        name=f"post_{t}",
    )(x, yfn, yna, o_fwd, o_bwd, gvg, mod, p["gain_o"], p["gain_f"], p["wo"], p["w1"], p["w3"], p["w2"])


def _dft_tables(n):
    idx = (np.arange(n)[:, None] * np.arange(n)[None, :]) % n
    ang = 2.0 * np.pi * idx / n
    return np.cos(ang), np.sin(ang)


def _shape_consts(t, n):
    c64, s64 = _dft_tables(FN_GD)
    eye = np.eye(FN_GROUPS)
    cs = np.concatenate([np.kron(eye, c64), np.kron(eye, s64)], axis=1)
    chunk_id = np.arange(SUB) // GLA_CHUNK
    same = chunk_id[:, None] == chunk_id[None, :]
    pos = np.arange(SUB)
    tri = np.stack([same & (pos[None, :] <= pos[:, None]), same & (pos[None, :] >= pos[:, None])])
    cpos = np.arange(GLA_BLOCK)
    fwd = cpos[None, :] <= cpos[:, None]
    gla_causal = np.stack([np.tile(fwd, (1, GLA_HEADS)), np.tile(fwd.T, (1, GLA_HEADS))])
    gla_vmask = ((np.arange(2 * GLA_BLOCK)[:, None] // GLA_BLOCK)
                 == (np.arange(2 * GLA_DV_PAD)[None, :] // GLA_DV_PAD))

    m = GLA_DK // 4
    inv = ROPE_BASE ** (-np.arange(m) / m)
    tok = np.arange(t)
    cos_h = np.ones((t, GLA_DK_PAD))
    sin_lo = np.zeros((t, GLA_DK_PAD))
    sin_hi = np.zeros((t, GLA_DK_PAD))
    for blk, p in enumerate((tok // GRID_W, tok % GRID_W)):
        ang = p[:, None] * inv[None, :]
        o = blk * 2 * m
        cos_h[:, o:o + m] = np.cos(ang)
        cos_h[:, o + m:o + 2 * m] = np.cos(ang)
        sin_lo[:, o:o + m] = -np.sin(ang)
        sin_hi[:, o + m:o + 2 * m] = np.sin(ang)
    rope = np.stack([np.tile(a, (1, GLA_HEADS)) for a in (cos_h, sin_lo, sin_hi)])

    def position_tables(length):
        cos, sin = _dft_tables(length)
        half = length // 2
        return np.stack([cos[:half, 0::2], sin[:half, 0::2], cos[:half, 1::2], sin[:half, 1::2]])

    to_bf16 = lambda a: jnp.asarray(a, F32).astype(BF16)
    return {
        "cs": to_bf16(cs), "tri": jnp.asarray(tri, BF16),
        "gla_causal": jnp.asarray(gla_causal, F32), "gla_vmask": jnp.asarray(gla_vmask, BF16),
        "rope": jnp.asarray(rope, F32), "na_band": jnp.asarray(_na_band_mask(), F32),
        "dft_lat": to_bf16(position_tables(t)), "dft_ctx": to_bf16(position_tables(n)),
    }


def _bias_kernel(rpb_ref, onehot_ref, ok_ref, o_ref):
    r = rpb_ref[...]
    onehot = onehot_ref[...]
    t = jnp.zeros(o_ref.shape, F32)
    for _ in range(3):
        piece = r.astype(BF16)
        t = t + _dot(piece, onehot)
        r = r - piece.astype(F32)
    o_ref[...] = jnp.where(ok_ref[...] > 0.0, t, NA_MASKED)


def _window_bias_rows(na_rpb):
    depth, heads, n_dr, n_dc = na_rpb.shape
    cq = np.arange(GRID_W)
    c0 = np.clip(cq - NA_KW // 2, 0, GRID_W - NA_KW)
    col_ok = (cq[None, :] >= c0[:, None]) & (cq[None, :] < c0[:, None] + NA_KW)
    dc = np.clip(cq[None, :] - cq[:, None], -(NA_KW - 1), NA_KW - 1) + NA_KW - 1
    n_dc_pad = n_dc + 1
    n_tab = depth * heads * n_dr
    rows_pad = -(-n_tab // 8) * 8
    onehot = (np.arange(n_dc_pad)[:, None, None] == dc[None]).reshape(n_dc_pad, GRID_W * GRID_W)
    rpb2 = jnp.pad(na_rpb.reshape(n_tab, n_dc), ((0, rows_pad - n_tab), (0, 1)))
    t = pl.pallas_call(
        _bias_kernel,
        out_shape=jax.ShapeDtypeStruct((rows_pad, GRID_W * GRID_W), F32),
        name="na_bias",
    )(rpb2, jnp.asarray(onehot, BF16), jnp.asarray(col_ok.reshape(1, -1), F32))
    t = t[:n_tab].reshape(depth * heads, n_dr, GRID_W, GRID_W)
    t = t.transpose(0, 2, 1, 3).reshape(depth * heads, GRID_W, n_dr * GRID_W)
    margin = NA_WIN - NA_KH
    t = jnp.pad(t, ((0, 0), (0, 0), (margin * GRID_W, margin * GRID_W)))
    return t.reshape(depth, heads, GRID_W, (n_dr + 2 * margin) * GRID_W)


def _na_band_mask():
    kinds = _na_group_kinds()
    band = np.zeros((len(kinds), NA_GROUP, NA_WIN), np.float32)
    for kind, (_, a_lo_of) in enumerate(kinds):
        for i in range(NA_GROUP):
            band[kind, i, a_lo_of(i):a_lo_of(i) + NA_KH] = 1.0
    return np.repeat(band, GRID_W, axis=2).reshape(len(kinds) * NA_GROUP, 1, NA_WIN * GRID_W)


def _pad_heads(w, heads, dim, pad, axis=-1):
    axis = axis % w.ndim
    shape = w.shape
    w = w.reshape(shape[:axis] + (heads, dim) + shape[axis + 1:])
    widths = [(0, 0)] * w.ndim
    widths[axis + 1] = (0, pad - dim)
    return jnp.pad(w, widths).reshape(shape[:axis] + (heads * pad,) + shape[axis + 1:])


def _params(norm_mix, norm_ffn, w_in, fnet_w, na_q_norm, na_k_norm, na_rpb, alpha_w, alpha_b, o_norm, w_out,
            ffn_w1, ffn_w3, ffn_w2):
    depth = w_in.shape[0]
    o = FN_W + 3 * NA_W
    gq = w_in[..., o:o + GLA_QK_W]
    gk = w_in[..., o + GLA_QK_W:o + 2 * GLA_QK_W]
    o2 = o + 2 * GLA_QK_W
    gv = w_in[..., o2:o2 + GLA_V_W]
    gg = w_in[..., o2 + GLA_V_W:o2 + 2 * GLA_V_W]
    ga = w_in[..., o2 + 2 * GLA_V_W:]
    w = jnp.concatenate([
        w_in[..., :o],
        jnp.pad(ga, ((0, 0), (0, 0), (0, GLA_A_PAD - 2 * GLA_RANK))),
        _pad_heads(gq, GLA_HEADS, GLA_DK, GLA_DK_PAD), _pad_heads(gk, GLA_HEADS, GLA_DK, GLA_DK_PAD),
        _pad_heads(gv, GLA_HEADS, GLA_DV, GLA_DV_PAD), _pad_heads(gg, GLA_HEADS, GLA_DV, GLA_DV_PAD),
    ], axis=-1).astype(BF16)

    qkg = jnp.stack([jnp.tile(na_q_norm, (1, NA_HEADS)) * (NA_HD ** -0.5),
                     jnp.tile(na_k_norm, (1, NA_HEADS))], axis=1)

    aw_pad = _pad_heads(alpha_w, GLA_HEADS, GLA_DK, GLA_DK_PAD)
    aw = jnp.concatenate([jnp.pad(aw_pad[:, 0], ((0, 0), (0, 0), (0, GLA_QK_PAD))),
                          jnp.pad(aw_pad[:, 1], ((0, 0), (0, 0), (GLA_QK_PAD, 0)))], axis=1)
    aw = jnp.pad(aw, ((0, 0), (0, GLA_A_PAD - 2 * GLA_RANK), (0, 0))).astype(BF16)
    ab = _pad_heads(alpha_b, GLA_HEADS, GLA_DK, GLA_DK_PAD).reshape(depth, 1, 2 * GLA_QK_PAD)

    gain_o = _pad_heads(jnp.tile(o_norm, (1, GLA_HEADS)), GLA_HEADS, GLA_DV, GLA_DV_PAD)[:, None, :]
    wo = jnp.concatenate([w_out[:, :FN_W + NA_W],
                          _pad_heads(w_out[:, FN_W + NA_W:], GLA_HEADS, GLA_DV, GLA_DV_PAD, axis=1)],
                         axis=1).astype(BF16)
    return {
        "gain_m": norm_mix[:, None, :], "gain_f": norm_ffn[:, None, :], "w": w, "qkg": qkg, "aw": aw, "ab": ab,
        "rpb_rows": _window_bias_rows(na_rpb), "gain_o": gain_o, "wo": wo, "fw": fnet_w.astype(BF16),
        "w1": ffn_w1.astype(BF16), "w3": ffn_w3.astype(BF16), "w2": ffn_w2.astype(BF16),
    }


def kernel(x, c, ctx, c_ctx, ada_w, ada_b, norm_mix, norm_ffn, w_in, fnet_w, na_q_norm, na_k_norm, na_rpb,
           gla_alpha_w, gla_alpha_b, gla_o_norm, w_out, ffn_w1, ffn_w3, ffn_w2):
    bsz, t, d = x.shape
    n = ctx.shape[1]
    depth = ada_w.shape[0]
    ctx_row = bsz
    assert bsz < MOD_ROWS and n % SUB == 0 and (bsz * n) % CTX_TM == 0
    consts = _shape_consts(t, n)
    p = _params(norm_mix, norm_ffn, w_in, fnet_w, na_q_norm, na_k_norm, na_rpb, gla_alpha_w, gla_alpha_b,
                gla_o_norm, w_out, ffn_w1, ffn_w3, ffn_w2)

    cc = jnp.concatenate([c, c_ctx[None, :], jnp.zeros((MOD_ROWS - bsz - 1, d), F32)])
    mod = _ada_mod(cc, ada_w, ada_b).reshape(depth, MOD_ROWS, 6, d)

    cx = ctx
    flat = lambda a: a.reshape(1, bsz * n, a.shape[-1])
    for l in range(depth):
        need_ctx = l < depth - 1
        uv, nqkv, gqk, gvg, gb = _inproj(x, l, mod, None, p, consts, consts["rope"], tm=1024)
        uv_c, nqkv_c, gqk_c, gvg_c, gb_c = _inproj(cx, l, mod, ctx_row, p, consts, None, tm=n)

        y_fn = _fourier(uv, l, consts["dft_lat"], p["fw"])
        y_na, y_na_c = _na(nqkv, nqkv_c, l, p["rpb_rows"], consts["na_band"], need_ctx=need_ctx)
        o_gla = _gla(gqk, gvg, gb, gqk_c, gvg_c, gb_c, consts["gla_causal"], consts["gla_vmask"],
                     need_ctx=need_ctx)
        x = _post(x, y_fn, y_na, o_gla[0], o_gla[1], gvg, l, mod, None, p, tm=512)
        if need_ctx:
            y_fn_c = _fourier(uv_c, l, consts["dft_ctx"], p["fw"])
            cx = _post(flat(cx), flat(y_fn_c), flat(y_na_c), flat(o_gla[2]), flat(o_gla[3]), flat(gvg_c),
                       l, mod, ctx_row, p, tm=CTX_TM).reshape(bsz, n, d)
    return x
```

```python
import functools

import numpy as np
import jax
import jax.numpy as jnp
from jax import lax
from jax.experimental import pallas as pl
from jax.experimental.pallas import tpu as pltpu

F32 = jnp.float32
BF16 = jnp.bfloat16

D_MODEL = 1024
GRID_W = 64
FN_W = 256
FN_GROUPS = 4
FN_GD = 64
NA_HD = 64
NA_W = 384
NA_HEADS = 6
NA_KH = 8
NA_KW = 16
NA_GROUP = 4
NA_WIN = NA_GROUP + NA_KH
LOG2E = float(np.log2(np.e))
NA_MASKED = -1e30
GLA_HEADS = 4
GLA_DK = 48
GLA_DV = 96
GLA_QK_W = GLA_HEADS * GLA_DK
GLA_V_W = GLA_HEADS * GLA_DV
GLA_RANK = 16
GLA_GATE_NORM = 16.0
GLA_CHUNK = 64
GLA_BLOCK = 2 * GLA_CHUNK
GLA_UNROLL = 4
ROPE_BASE = 10000.0
D_FF = 2816
EPS = 1e-6
MOD_ROWS = 16

LANES = 128
MXU_W = 256
VMEM_LIMIT_BYTES = 56 * 1024 * 1024

GLA_DK_PAD = 64
GLA_DV_PAD = 128
GLA_QK_PAD = GLA_HEADS * GLA_DK_PAD
GLA_V_PAD = GLA_HEADS * GLA_DV_PAD
GLA_A_PAD = LANES
MIX_PAD = FN_W + NA_W + GLA_V_PAD

C_FX = 0
C_NQ = C_FX + FN_W
C_NK = C_NQ + NA_W
C_NV = C_NK + NA_W
C_GA = C_NV + NA_W
C_GQ = C_GA + GLA_A_PAD
C_GK = C_GQ + GLA_QK_PAD
C_GV = C_GK + GLA_QK_PAD
C_GG = C_GV + GLA_V_PAD
N_IN_PAD = C_GG + GLA_V_PAD

SUB = 256
CTX_TM = 512


def _cparams(sem):
    return pltpu.CompilerParams(dimension_semantics=sem, vmem_limit_bytes=VMEM_LIMIT_BYTES)


def _const_spec(shape):
    nd = len(shape)
    return pl.BlockSpec(tuple(shape), lambda *_: (0,) * nd, pipeline_mode=pl.Buffered(1))


def _layer_spec(shape, l):
    nd = len(shape)
    return pl.BlockSpec((None,) + tuple(shape), lambda *_: (l,) + (0,) * nd, pipeline_mode=pl.Buffered(1))


def _mod_spec(d, l, mod_row):
    if mod_row is None:
        return pl.BlockSpec((None, None, 6, d), lambda b, i: (l, b, 0, 0))
    return pl.BlockSpec((None, None, 6, d), lambda b, i: (l, mod_row, 0, 0))


def _dot(a, b):
    return jnp.dot(a, b, preferred_element_type=F32)


def _dot_nt(a, b):
    return lax.dot_general(a, b, (((1,), (1,)), ((), ())), preferred_element_type=F32)


def _ada_kernel(c_ref, w_ref, b_ref, o_ref):
    a = c_ref[...]
    a = (a * jax.nn.sigmoid(a)).astype(BF16)
    o_ref[...] = _dot(a, w_ref[...].astype(BF16)) + b_ref[...]


def _ada_mod(cc, ada_w, ada_b):
    depth, d, n = ada_w.shape
    rows = cc.shape[0]
    tn = 1024
    return pl.pallas_call(
        _ada_kernel,
        grid=(depth, n // tn),
        in_specs=[
            pl.BlockSpec((rows, d), lambda l, j: (0, 0)),
            pl.BlockSpec((None, d, tn), lambda l, j: (l, 0, j)),
            pl.BlockSpec((None, 1, tn), lambda l, j: (l, 0, j)),
        ],
        out_specs=pl.BlockSpec((None, rows, tn), lambda l, j: (l, 0, j)),
        out_shape=jax.ShapeDtypeStruct((depth, rows, n), F32),
        compiler_params=_cparams(("arbitrary", "arbitrary")),
        name="ada_mod",
    )(cc, ada_w, ada_b.reshape(depth, 1, n))


def _modulated_norm(x, gain, shift, scale):
    ms = jnp.mean(x * x, axis=-1, keepdims=True)
    return (x * lax.rsqrt(ms + EPS) * gain) * (1.0 + scale) + shift


def _log_sigmoid(x):
    return jnp.minimum(x, 0.0) - jnp.log1p(jnp.exp(-jnp.abs(x)))


def _head_mean_sq(z):
    first = lax.broadcasted_iota(jnp.int32, (1, LANES), 1) < NA_HD
    out = []
    for t in range(NA_W // LANES):
        sq = z[:, t * LANES:(t + 1) * LANES]
        sq = sq * sq
        lo = jnp.sum(jnp.where(first, sq, 0.0), axis=-1, keepdims=True)
        hi = jnp.sum(jnp.where(first, 0.0, sq), axis=-1, keepdims=True)
        out.append(jnp.where(first, lo, hi))
    return jnp.concatenate(out, axis=1) * (1.0 / NA_HD)


def _inproj_kernel(x_ref, mod_ref, gain_ref, w_ref, cs_ref, qkg_ref, rope_ref, aw_ref, ab_ref, tri_ref,
                   uv_ref, nqkv_ref, gqk_ref, gvg_ref, gb_ref, *, rope, tm):
    x = x_ref[...]
    h = _modulated_norm(x, gain_ref[...], mod_ref[0:1, :], mod_ref[1:2, :]).astype(BF16)

    zf = _dot(h, w_ref[:, C_FX:C_NQ]).astype(BF16)
    zva = _dot(h, w_ref[:, C_NV:C_GQ])
    nqkv_ref[:, 2 * NA_W:3 * NA_W] = zva[:, :NA_W].astype(BF16)
    a = zva[:, NA_W:].astype(BF16)
    zqk = _dot(h, w_ref[:, C_NQ:C_NV])
    logits = _dot(a, aw_ref[...]) + ab_ref[...]
    zg = _dot(h, w_ref[:, C_GQ:C_GV])
    uv = _dot(zf, cs_ref[...])
    for j in range(uv_ref.shape[0]):
        uv_ref[j] = uv[:, j * LANES:(j + 1) * LANES]
    gvg_ref[...] = _dot(h, w_ref[:, C_GV:N_IN_PAD]).astype(BF16)

    glog = (_log_sigmoid(logits) * (1.0 / GLA_GATE_NORM)).astype(BF16)
    for s in range(tm // SUB):
        rows = slice(s * SUB, (s + 1) * SUB)
        gb_ref[rows, 0:GLA_QK_PAD] = _dot(tri_ref[0], glog[rows, 0:GLA_QK_PAD])
        gb_ref[rows, GLA_QK_PAD:] = _dot(tri_ref[1], glog[rows, GLA_QK_PAD:])

    for i in range(2):
        z = zqk[:, i * NA_W:(i + 1) * NA_W]
        z = z * lax.rsqrt(_head_mean_sq(z) + EPS) * qkg_ref[i:i + 1, :]
        nqkv_ref[:, i * NA_W:(i + 1) * NA_W] = z.astype(BF16)

    for i in range(2):
        z = zg[:, i * GLA_QK_PAD:(i + 1) * GLA_QK_PAD]
        if rope:
            z = (z * rope_ref[0]
                 + pltpu.roll(z, GLA_QK_PAD - GLA_DK // 4, 1) * rope_ref[1]
                 + pltpu.roll(z, GLA_DK // 4, 1) * rope_ref[2])
        if i == 0:
            z = z * (GLA_DK ** -0.5)
        gqk_ref[:, i * GLA_QK_PAD:(i + 1) * GLA_QK_PAD] = z.astype(BF16)


def _inproj(x, l, mod, mod_row, p, consts, rope_tab, *, tm):
    bsz, t, d = x.shape
    assert t % tm == 0 and tm % SUB == 0
    rope = rope_tab is not None
    if not rope:
        rope_tab = jnp.zeros((3, tm, GLA_QK_PAD), F32)
    rope_idx = (lambda b, i: (0, i, 0)) if rope else (lambda b, i: (0, 0, 0))
    row = lambda w_: pl.BlockSpec((None, tm, w_), lambda b, i: (b, i, 0))
    uv_tiles = 2 * FN_W // LANES
    out_w = (3 * NA_W, 2 * GLA_QK_PAD, 2 * GLA_V_PAD, 2 * GLA_QK_PAD)
    out_dt = (BF16, BF16, BF16, F32)
    return pl.pallas_call(
        functools.partial(_inproj_kernel, rope=rope, tm=tm),
        grid=(bsz, t // tm),
        in_specs=[
            row(d),
            _mod_spec(d, l, mod_row),
            _layer_spec((1, d), l),
            _layer_spec((d, N_IN_PAD), l),
            _const_spec(consts["cs"].shape),
            _layer_spec((2, NA_W), l),
            pl.BlockSpec((3, tm, GLA_QK_PAD), rope_idx),
            _layer_spec((GLA_A_PAD, 2 * GLA_QK_PAD), l),
            _layer_spec((1, 2 * GLA_QK_PAD), l),
            _const_spec((2, SUB, SUB)),
        ],
        out_specs=[pl.BlockSpec((None, uv_tiles, tm, LANES), lambda b, i: (b, 0, i, 0))]
        + [row(w_) for w_ in out_w],
        out_shape=[jax.ShapeDtypeStruct((bsz, uv_tiles, t, LANES), F32)]
        + [jax.ShapeDtypeStruct((bsz, t, w_), dt) for w_, dt in zip(out_w, out_dt)],
        compiler_params=_cparams(("arbitrary", "arbitrary")),
        name="inproj_rope" if rope else "inproj_ctx",
    )(x, mod, p["gain_m"], p["w"], consts["cs"], p["qkg"], rope_tab, p["aw"], p["ab"], consts["tri"])


def _fourier_kernel(tab_ref, uv_ref, w_ref, o_ref, *, scale):
    half = uv_ref.shape[1] // 2

    def positions(parity):
        tiles = [uv_ref[j, pl.ds(parity, half, stride=2), :].astype(BF16) for j in range(uv_ref.shape[0])]
        return jnp.concatenate(tiles[:2], axis=1), jnp.concatenate(tiles[2:], axis=1)

    ue, ve = positions(0)
    uo, vo = positions(1)
    even = _dot(tab_ref[0], ue) - _dot(tab_ref[1], ve)
    odd = _dot(tab_ref[2], uo) - _dot(tab_ref[3], vo)
    w = w_ref[...]
    o_ref[0:half, :] = _dot(((even + odd) * scale).astype(BF16), w).astype(BF16)
    o_ref[half:, :] = _dot(((even - odd) * scale).astype(BF16), w).astype(BF16)


def _fourier(uv, l, tabs, fnet_w):
    bsz, tiles, t, _ = uv.shape
    scale = float((t * FN_GD) ** -0.5)
    return pl.pallas_call(
        functools.partial(_fourier_kernel, scale=scale),
        grid=(bsz,),
        in_specs=[
            _const_spec(tabs.shape),
            pl.BlockSpec((None, tiles, t, LANES), lambda b: (b, 0, 0, 0)),
            _layer_spec((FN_W, FN_W), l),
        ],
        out_specs=pl.BlockSpec((None, t, FN_W), lambda b: (b, 0, 0)),
        out_shape=jax.ShapeDtypeStruct((bsz, t, FN_W), BF16),
        compiler_params=_cparams(("arbitrary",)),
        name=f"fourier_{t}",
    )(tabs, uv, fnet_w)


def _na_softmax_t(s):
    return jnp.exp2(s - jnp.max(s, axis=0, keepdims=True)).astype(BF16)


def _na_heads_out_t(o0, o1):
    row = lax.broadcasted_iota(jnp.int32, o0.shape, 0)
    out_t = jnp.where(row < NA_HD, o0 / o0[NA_HD:NA_HD + 1, :], o1 / o1[0:1, :])
    return out_t.T.astype(BF16)


def _na_group_kinds():
    margin = NA_WIN - NA_KH
    return ((0, lambda i: 0), (NA_KH // 2, lambda i: i), (margin + NA_KH - NA_GROUP, lambda i: margin))


def _na_kernel(q_ref, k_ref, v_ref, qc_ref, kc_ref, vc_ref, rpb_ref, o_ref, oc_ref,
               vt_ref, sc_ref, bias_ref, *, n_rows, need_ctx):
    lane = lax.broadcasted_iota(jnp.int32, (1, LANES), 1)
    first = lane < NA_HD
    zero = jnp.zeros((), BF16)
    one = jnp.ones((), BF16)
    v = v_ref[...]
    vt_ref[0] = jnp.where(first, v, one).T
    vt_ref[1] = jnp.where(first, one, v).T
    kc = kc_ref[...]
    vc = vc_ref[...]
    vct = (jnp.where(first, vc, one).T, jnp.where(first, one, vc).T)

    def both_heads(q):
        return jnp.concatenate([jnp.where(first, q, zero), jnp.where(first, zero, q)], axis=0)

    n_groups = n_rows // NA_GROUP
    gq = NA_GROUP * GRID_W
    n_loc = NA_WIN * GRID_W
    margin = NA_WIN - NA_KH

    @pl.when(pl.program_id(1) == 0)
    def _():
        wrow = lax.broadcasted_iota(jnp.int32, (n_loc, GRID_W), 0)
        for hh in range(2):
            for kind, (q_off, a_lo_of) in enumerate(_na_group_kinds()):
                for j in range(NA_GROUP // 2):
                    halves = []
                    for i in (2 * j, 2 * j + 1):
                        start = (NA_KH - 1 - q_off - i + margin) * GRID_W
                        lo = a_lo_of(i) * GRID_W
                        band = (wrow >= lo) & (wrow < lo + NA_KH * GRID_W)
                        halves.append(jnp.where(band, rpb_ref[hh, start:start + n_loc, :], NA_MASKED))
                    bias_ref[hh, kind, :, j * LANES:(j + 1) * LANES] = jnp.concatenate(halves, axis=1)

    def window(g):
        ws = jnp.clip(g * NA_GROUP - NA_KH // 2, 0, n_rows - NA_WIN)
        return (pl.ds(pl.multiple_of(g * gq, gq), gq), pl.ds(pl.multiple_of(ws * GRID_W, gq), n_loc))

    def scores(g, slot):
        qs, ks = window(g)
        kind = jnp.where(g == 0, 0, jnp.where(g == n_groups - 1, 2, 1))
        q = both_heads(q_ref[qs, :])
        s_loc = _dot_nt(k_ref[ks, :], q)
        s_ctx = _dot_nt(kc, q)
        for hh in range(2):
            cols = slice(hh * gq, (hh + 1) * gq)
            sc_ref[slot, 0:n_loc, cols] = s_loc[:, cols] + bias_ref[hh, kind]
        sc_ref[slot, n_loc:, :] = s_ctx

    def finish(g, slot):
        qs, ks = window(g)
        p = _na_softmax_t(sc_ref[slot])
        outs = [_dot(jnp.concatenate([vt_ref[hh, :, ks], vct[hh]], axis=1), p[:, hh * gq:(hh + 1) * gq])
                for hh in range(2)]
        o_ref[qs, :] = _na_heads_out_t(outs[0], outs[1])

    scores(0, 0)

    def pair(i, carry):
        scores(2 * i + 1, 1)
        finish(2 * i, 0)
        scores(2 * i + 2, 0)
        finish(2 * i + 1, 1)
        return carry

    lax.fori_loop(0, n_groups // 2 - 1, pair, 0)
    scores(n_groups - 1, 1)
    finish(n_groups - 2, 0)
    finish(n_groups - 1, 1)

    if need_ctx:
        n_ctx = qc_ref.shape[0]
        p = _na_softmax_t(_dot_nt(kc, both_heads(qc_ref[...])))
        outs = [_dot(vct[hh], p[:, hh * n_ctx:(hh + 1) * n_ctx]) for hh in range(2)]
        oc_ref[...] = _na_heads_out_t(outs[0], outs[1])
    else:
        oc_ref[...] = jnp.zeros(oc_ref.shape, oc_ref.dtype)


def _na(nqkv, nqkv_c, l, rpb_rows, *, need_ctx):
    bsz, t, _ = nqkv.shape
    n = nqkv_c.shape[1]
    n_rows = t // GRID_W
    assert n_rows % (2 * NA_GROUP) == 0 and n_rows >= NA_WIN + NA_GROUP
    pairs = NA_HEADS // 2
    n_loc = NA_WIN * GRID_W
    gq = NA_GROUP * GRID_W
    nb = NA_W // LANES
    lat = lambda off: pl.BlockSpec((None, t, LANES), lambda j, b: (b, 0, off + j))
    ctx = lambda off: pl.BlockSpec((None, n, LANES), lambda j, b: (b, 0, off + j))
    return pl.pallas_call(
        functools.partial(_na_kernel, n_rows=n_rows, need_ctx=need_ctx),
        grid=(pairs, bsz),
        in_specs=[lat(0), lat(nb), lat(2 * nb), ctx(0), ctx(nb), ctx(2 * nb),
                  pl.BlockSpec((None, 2) + rpb_rows.shape[2:], lambda j, b: (l, j, 0, 0))],
        out_specs=[pl.BlockSpec((None, t, LANES), lambda j, b: (b, 0, j)),
                   pl.BlockSpec((None, n, LANES), lambda j, b: (b, 0, j))],
        out_shape=[jax.ShapeDtypeStruct((bsz, t, NA_W), BF16),
                   jax.ShapeDtypeStruct((bsz, n, NA_W), BF16)],
        scratch_shapes=[pltpu.VMEM((2, LANES, t), BF16),
                        pltpu.VMEM((2, n_loc + n, 2 * gq), F32),
                        pltpu.VMEM((2, len(_na_group_kinds()), n_loc, gq), F32)],
        compiler_params=_cparams(("arbitrary", "arbitrary")),
        name="na_attn",
    )(nqkv, nqkv, nqkv, nqkv_c, nqkv_c, nqkv_c, rpb_rows)


def _gla_steps(jobs, st_ref, causal_ref, vmask_ref):
    c = GLA_CHUNK
    head = lax.broadcasted_iota(jnp.int32, (1, GLA_QK_PAD), 1) // GLA_DK_PAD
    prep = []
    for qk_ref, vg_ref, b_ref, out_ref, ci, d in jobs:
        rows = pl.ds(pl.multiple_of(ci * GLA_BLOCK, GLA_BLOCK), GLA_BLOCK)
        k = qk_ref[rows, GLA_QK_PAD:].astype(F32)
        v = vg_ref[rows, 0:GLA_V_PAD]
        b = b_ref[rows, d * GLA_QK_PAD:(d + 1) * GLA_QK_PAD]
        lo, hi = b[0:c, :], b[c:, :]
        if d == 0:
            tot1, tot2 = lo[c - 1:c, :], hi[c - 1:c, :]
            b_rel = jnp.concatenate([lo - tot1, hi], axis=0)
        else:
            tot1, tot2 = hi[0:1, :], lo[0:1, :]
            b_rel = jnp.concatenate([lo, hi - tot1], axis=0)
        k_in = k * jnp.exp(-b_rel)
        k_end = k_in * jnp.exp(tot2)
        q_in = q_st = None
        if out_ref is not None:
            q_rel = qk_ref[rows, 0:GLA_QK_PAD].astype(F32) * jnp.exp(b_rel)
            q_in = q_rel.astype(BF16)
            q_st = (q_rel * jnp.exp(tot1)).astype(BF16)
        decay = jnp.exp(jnp.broadcast_to(tot1 + tot2, (GLA_DV_PAD, GLA_QK_PAD)).T)
        prep.append((rows, q_in, q_st, k_in, k_end, v, decay))

    att = [None if q_in is None else
           _dot_nt(q_in, jnp.concatenate([jnp.where(head == h, k_in, 0.0) for h in range(GLA_HEADS)],
                                         axis=0).astype(BF16))
           for (rows, q_in, q_st, k_in, k_end, v, decay) in prep]

    kv = []
    for (rows, q_in, q_st, k_in, k_end, v, decay) in prep:
        k_t = k_end.astype(BF16).T
        kv.append(jnp.concatenate(
            [_dot(k_t[h * GLA_DK_PAD:(h + 1) * GLA_DK_PAD, :], v[:, h * GLA_DV_PAD:(h + 1) * GLA_DV_PAD])
             for h in range(GLA_HEADS)], axis=0))

    inter = []
    zero = jnp.zeros((GLA_DK_PAD, GLA_DV_PAD), BF16)
    for j, ((qk_ref, vg_ref, b_ref, out_ref, ci, d), (rows, q_in, q_st, k_in, k_end, v, decay)) in enumerate(
            zip(jobs, prep)):
        st = st_ref[d]
        if q_st is None:
            inter.append(None)
        else:
            sb = st.astype(BF16)
            s_bd = jnp.concatenate(
                [jnp.concatenate([sb[h * GLA_DK_PAD:(h + 1) * GLA_DK_PAD, :] if h2 == h else zero
                                  for h2 in range(GLA_HEADS)], axis=1) for h in range(GLA_HEADS)], axis=0)
            inter.append(_dot(q_st, s_bd))
        st_ref[d] = st * decay + kv[j]

    for j, ((qk_ref, vg_ref, b_ref, out_ref, ci, d), (rows, q_in, q_st, k_in, k_end, v, decay)) in enumerate(
            zip(jobs, prep)):
        if out_ref is not None:
            a = (att[j] * causal_ref[d]).astype(BF16)
            intra = []
            for pair in range(GLA_HEADS // 2):
                cols = slice(pair * 2 * GLA_DV_PAD, (pair + 1) * 2 * GLA_DV_PAD)
                v_pair = jnp.concatenate([v[:, cols]] * 2, axis=0) * vmask_ref[...]
                intra.append(_dot(a[:, pair * 2 * GLA_BLOCK:(pair + 1) * 2 * GLA_BLOCK], v_pair))
            out_ref[rows, :] = (inter[j] + jnp.concatenate(intra, axis=1)).astype(out_ref.dtype)


def _gla_kernel(qk_ref, vg_ref, b_ref, qkc_ref, vgc_ref, bc_ref, causal_ref, vmask_ref, *rest, t, n, need_ctx):
    if need_ctx:
        of_ref, ob_ref, ocf_ref, ocb_ref, st_ref = rest
    else:
        of_ref, ob_ref, st_ref = rest
    st_ref[...] = jnp.zeros(st_ref.shape, F32)
    nb_ctx = n // GLA_BLOCK
    nb_lat = t // GLA_BLOCK

    def jobs(refs, outs, n_blocks, i):
        return [refs + (outs[0], i, 0), refs + (outs[1], n_blocks - 1 - i, 1)]

    ctx_outs = (ocf_ref, ocb_ref) if need_ctx else (None, None)
    for i in range(nb_ctx):
        _gla_steps(jobs((qkc_ref, vgc_ref, bc_ref), ctx_outs, nb_ctx, i), st_ref, causal_ref, vmask_ref)

    def step(i, carry):
        _gla_steps(jobs((qk_ref, vg_ref, b_ref), (of_ref, ob_ref), nb_lat, i), st_ref, causal_ref, vmask_ref)
        return carry

    lax.fori_loop(0, nb_lat, step, 0, unroll=GLA_UNROLL)


def _gla(gqk, gvg, gb, gqk_c, gvg_c, gb_c, causal, vmask, *, need_ctx):
    bsz, t, _ = gqk.shape
    n = gqk_c.shape[1]
    assert t % GLA_BLOCK == 0 and n % GLA_BLOCK == 0
    row = lambda rows, w_: pl.BlockSpec((None, rows, w_), lambda b: (b, 0, 0))
    out_rows = (t, t, n, n) if need_ctx else (t, t)
    return pl.pallas_call(
        functools.partial(_gla_kernel, t=t, n=n, need_ctx=need_ctx),
        grid=(bsz,),
        in_specs=[row(t, 2 * GLA_QK_PAD), row(t, GLA_V_PAD), row(t, 2 * GLA_QK_PAD),
                  row(n, 2 * GLA_QK_PAD), row(n, GLA_V_PAD), row(n, 2 * GLA_QK_PAD),
                  _const_spec(causal.shape), _const_spec(vmask.shape)],
        out_specs=[row(r, GLA_V_PAD) for r in out_rows],
        out_shape=[jax.ShapeDtypeStruct((bsz, r, GLA_V_PAD), BF16) for r in out_rows],
        scratch_shapes=[pltpu.VMEM((2, GLA_QK_PAD, GLA_DV_PAD), F32)],
        compiler_params=_cparams(("arbitrary",)),
        name="gla_scan",
    )(gqk, gvg, gb, gqk_c, gvg_c, gb_c, causal, vmask)


def _gla_out(of_ref, ob_ref, g_ref, gain_ref):
    heads = []
    for h in range(GLA_HEADS):
        cols = slice(h * GLA_DV_PAD, (h + 1) * GLA_DV_PAD)
        o = of_ref[:, cols].astype(F32) + ob_ref[:, cols].astype(F32)
        ms = jnp.sum(o * o, axis=-1, keepdims=True) * (1.0 / GLA_DV)
        g = g_ref[:, cols].astype(F32)
        heads.append((o * lax.rsqrt(ms + EPS) * gain_ref[:, cols] * (g * jax.nn.sigmoid(g))).astype(BF16))
    return jnp.concatenate(heads, axis=1)


def _post_kernel(x_ref, yfn_ref, yna_ref, of_ref, ob_ref, g_ref, mod_ref, gain_o_ref, gain_ref,
                 wo_ref, w1_ref, w3_ref, w2_ref, o_ref):
    y = (_dot(yfn_ref[...], wo_ref[0:FN_W, :])
         + _dot(yna_ref[...], wo_ref[FN_W:FN_W + NA_W, :])
         + _dot(_gla_out(of_ref, ob_ref, g_ref, gain_o_ref), wo_ref[FN_W + NA_W:, :]))
    x1 = x_ref[...] + mod_ref[2:3, :] * y
    h = _modulated_norm(x1, gain_ref[...], mod_ref[3:4, :], mod_ref[4:5, :]).astype(BF16)
    acc = jnp.zeros(x1.shape, F32)
    for c in range(D_FF // MXU_W):
        cols = slice(c * MXU_W, (c + 1) * MXU_W)
        u = _dot(h, w1_ref[:, cols])
        g = _dot(h, w3_ref[:, cols])
        a = (u * jax.nn.sigmoid(u) * g).astype(BF16)
        acc = acc + _dot(a, w2_ref[cols, :])
    o_ref[...] = x1 + mod_ref[5:6, :] * acc


def _post(x, yfn, yna, o_fwd, o_bwd, gvg, l, mod, mod_row, p, *, tm):
    bsz, t, d = x.shape
    assert t % tm == 0
    row = lambda w_: pl.BlockSpec((None, tm, w_), lambda b, i: (b, i, 0))
    gate = pl.BlockSpec((None, tm, GLA_V_PAD), lambda b, i: (b, i, 1))
    return pl.pallas_call(
        _post_kernel,
        grid=(bsz, t // tm),
        in_specs=[row(d), row(FN_W), row(NA_W), row(GLA_V_PAD), row(GLA_V_PAD), gate,
                  _mod_spec(d, l, mod_row), _layer_spec((1, GLA_V_PAD), l),
                  _layer_spec((1, d), l), _layer_spec((MIX_PAD, d), l), _layer_spec((d, D_FF), l),
                  _layer_spec((d, D_FF), l), _layer_spec((D_FF, d), l)],
        out_specs=row(d),
        out_shape=jax.ShapeDtypeStruct((bsz, t, d), F32),
        compiler_params=_cparams(("arbitrary", "arbitrary")),
        name=f"post_{t}",
    )(x, yfn, yna, o_fwd, o_bwd, gvg, mod, p["gain_o"], p["gain_f"], p["wo"], p["w1"], p["w3"], p["w2"])


def _dft_tables(n):
    idx = (np.arange(n)[:, None] * np.arange(n)[None, :]) % n
    ang = 2.0 * np.pi * idx / n
    return np.cos(ang), np.sin(ang)


def _shape_consts(t, n):
    c64, s64 = _dft_tables(FN_GD)
    eye = np.eye(FN_GROUPS)
    cs = np.concatenate([np.kron(eye, c64), np.kron(eye, s64)], axis=1)
    chunk_id = np.arange(SUB) // GLA_CHUNK
    same = chunk_id[:, None] == chunk_id[None, :]
    pos = np.arange(SUB)
    tri = np.stack([same & (pos[None, :] <= pos[:, None]), same & (pos[None, :] >= pos[:, None])])
    cpos = np.arange(GLA_BLOCK)
    fwd = cpos[None, :] <= cpos[:, None]
    gla_causal = np.stack([np.tile(fwd, (1, GLA_HEADS)), np.tile(fwd.T, (1, GLA_HEADS))])
    gla_vmask = ((np.arange(2 * GLA_BLOCK)[:, None] // GLA_BLOCK)
                 == (np.arange(2 * GLA_DV_PAD)[None, :] // GLA_DV_PAD))

    m = GLA_DK // 4
    inv = ROPE_BASE ** (-np.arange(m) / m)
    tok = np.arange(t)
    cos_h = np.ones((t, GLA_DK_PAD))
    sin_lo = np.zeros((t, GLA_DK_PAD))
    sin_hi = np.zeros((t, GLA_DK_PAD))
    for blk, p in enumerate((tok // GRID_W, tok % GRID_W)):
        ang = p[:, None] * inv[None, :]
        o = blk * 2 * m
        cos_h[:, o:o + m] = np.cos(ang)
        cos_h[:, o + m:o + 2 * m] = np.cos(ang)
        sin_lo[:, o:o + m] = -np.sin(ang)
        sin_hi[:, o + m:o + 2 * m] = np.sin(ang)
    rope = np.stack([np.tile(a, (1, GLA_HEADS)) for a in (cos_h, sin_lo, sin_hi)])

    def position_tables(length):
        cos, sin = _dft_tables(length)
        half = length // 2
        return np.stack([cos[:half, 0::2], sin[:half, 0::2], cos[:half, 1::2], sin[:half, 1::2]])

    to_bf16 = lambda a: jnp.asarray(a, F32).astype(BF16)
    return {
        "cs": to_bf16(cs), "tri": jnp.asarray(tri, BF16),
        "gla_causal": jnp.asarray(gla_causal, F32), "gla_vmask": jnp.asarray(gla_vmask, BF16),
        "rope": jnp.asarray(rope, F32),
        "dft_lat": to_bf16(position_tables(t)), "dft_ctx": to_bf16(position_tables(n)),
    }


def _bias_kernel(rpb_ref, onehot_ref, ok_ref, o_ref):
    r = rpb_ref[...]
    onehot = onehot_ref[...]
    t = jnp.zeros(o_ref.shape, F32)
    for _ in range(3):
        piece = r.astype(BF16)
        t = t + _dot(piece, onehot)
        r = r - piece.astype(F32)
    o_ref[...] = jnp.where(ok_ref[...] > 0.0, t * LOG2E, NA_MASKED)


def _window_bias_rows(na_rpb):
    depth, heads, n_dr, n_dc = na_rpb.shape
    cq = np.arange(GRID_W)
    c0 = np.clip(cq - NA_KW // 2, 0, GRID_W - NA_KW)
    col_ok = (cq[None, :] >= c0[:, None]) & (cq[None, :] < c0[:, None] + NA_KW)
    dc = np.clip(cq[None, :] - cq[:, None], -(NA_KW - 1), NA_KW - 1) + NA_KW - 1
    n_dc_pad = n_dc + 1
    n_tab = depth * heads * n_dr
    rows_pad = -(-n_tab // 8) * 8
    onehot = (np.arange(n_dc_pad)[:, None, None] == dc[None]).reshape(n_dc_pad, GRID_W * GRID_W)
    rpb2 = jnp.pad(na_rpb.reshape(n_tab, n_dc), ((0, rows_pad - n_tab), (0, 1)))
    t = pl.pallas_call(
        _bias_kernel,
        out_shape=jax.ShapeDtypeStruct((rows_pad, GRID_W * GRID_W), F32),
        name="na_bias",
    )(rpb2, jnp.asarray(onehot, BF16), jnp.asarray(col_ok.reshape(1, -1), F32))
    t = t[:n_tab].reshape(depth * heads, n_dr, GRID_W, GRID_W)
    t = t.transpose(0, 1, 3, 2).reshape(depth * heads, n_dr * GRID_W, GRID_W)
    margin = NA_WIN - NA_KH
    t = jnp.pad(t, ((0, 0), (margin * GRID_W, margin * GRID_W), (0, 0)))
    return t.reshape(depth, heads, (n_dr + 2 * margin) * GRID_W, GRID_W)


def _pad_heads(w, heads, dim, pad, axis=-1):
    axis = axis % w.ndim
    shape = w.shape
    w = w.reshape(shape[:axis] + (heads, dim) + shape[axis + 1:])
    widths = [(0, 0)] * w.ndim
    widths[axis + 1] = (0, pad - dim)
    return jnp.pad(w, widths).reshape(shape[:axis] + (heads * pad,) + shape[axis + 1:])


def _params(norm_mix, norm_ffn, w_in, fnet_w, na_q_norm, na_k_norm, na_rpb, alpha_w, alpha_b, o_norm, w_out,
            ffn_w1, ffn_w3, ffn_w2):
    depth = w_in.shape[0]
    o = FN_W + 3 * NA_W
    gq = w_in[..., o:o + GLA_QK_W]
    gk = w_in[..., o + GLA_QK_W:o + 2 * GLA_QK_W]
    o2 = o + 2 * GLA_QK_W
    gv = w_in[..., o2:o2 + GLA_V_W]
    gg = w_in[..., o2 + GLA_V_W:o2 + 2 * GLA_V_W]
    ga = w_in[..., o2 + 2 * GLA_V_W:]
    w = jnp.concatenate([
        w_in[..., :o],
        jnp.pad(ga, ((0, 0), (0, 0), (0, GLA_A_PAD - 2 * GLA_RANK))),
        _pad_heads(gq, GLA_HEADS, GLA_DK, GLA_DK_PAD), _pad_heads(gk, GLA_HEADS, GLA_DK, GLA_DK_PAD),
        _pad_heads(gv, GLA_HEADS, GLA_DV, GLA_DV_PAD), _pad_heads(gg, GLA_HEADS, GLA_DV, GLA_DV_PAD),
    ], axis=-1).astype(BF16)

    qkg = jnp.stack([jnp.tile(na_q_norm, (1, NA_HEADS)) * (NA_HD ** -0.5 * LOG2E),
                     jnp.tile(na_k_norm, (1, NA_HEADS))], axis=1)

    aw_pad = _pad_heads(alpha_w, GLA_HEADS, GLA_DK, GLA_DK_PAD)
    aw = jnp.concatenate([jnp.pad(aw_pad[:, 0], ((0, 0), (0, 0), (0, GLA_QK_PAD))),
                          jnp.pad(aw_pad[:, 1], ((0, 0), (0, 0), (GLA_QK_PAD, 0)))], axis=1)
    aw = jnp.pad(aw, ((0, 0), (0, GLA_A_PAD - 2 * GLA_RANK), (0, 0))).astype(BF16)
    ab = _pad_heads(alpha_b, GLA_HEADS, GLA_DK, GLA_DK_PAD).reshape(depth, 1, 2 * GLA_QK_PAD)

    gain_o = _pad_heads(jnp.tile(o_norm, (1, GLA_HEADS)), GLA_HEADS, GLA_DV, GLA_DV_PAD)[:, None, :]
    wo = jnp.concatenate([w_out[:, :FN_W + NA_W],
                          _pad_heads(w_out[:, FN_W + NA_W:], GLA_HEADS, GLA_DV, GLA_DV_PAD, axis=1)],
                         axis=1).astype(BF16)
    return {
        "gain_m": norm_mix[:, None, :], "gain_f": norm_ffn[:, None, :], "w": w, "qkg": qkg, "aw": aw, "ab": ab,
        "rpb_rows": _window_bias_rows(na_rpb), "gain_o": gain_o, "wo": wo, "fw": fnet_w.astype(BF16),
        "w1": ffn_w1.astype(BF16), "w3": ffn_w3.astype(BF16), "w2": ffn_w2.astype(BF16),
    }


def kernel(x, c, ctx, c_ctx, ada_w, ada_b, norm_mix, norm_ffn, w_in, fnet_w, na_q_norm, na_k_norm, na_rpb,
           gla_alpha_w, gla_alpha_b, gla_o_norm, w_out, ffn_w1, ffn_w3, ffn_w2):
    bsz, t, d = x.shape
    n = ctx.shape[1]
    depth = ada_w.shape[0]
    ctx_row = bsz
    assert bsz < MOD_ROWS and n % SUB == 0 and (bsz * n) % CTX_TM == 0
    consts = _shape_consts(t, n)
    p = _params(norm_mix, norm_ffn, w_in, fnet_w, na_q_norm, na_k_norm, na_rpb, gla_alpha_w, gla_alpha_b,
                gla_o_norm, w_out, ffn_w1, ffn_w3, ffn_w2)

    cc = jnp.concatenate([c, c_ctx[None, :], jnp.zeros((MOD_ROWS - bsz - 1, d), F32)])
    mod = _ada_mod(cc, ada_w, ada_b).reshape(depth, MOD_ROWS, 6, d)

    cx = ctx
    flat = lambda a: a.reshape(1, bsz * n, a.shape[-1])
    for l in range(depth):
        need_ctx = l < depth - 1
        uv, nqkv, gqk, gvg, gb = _inproj(x, l, mod, None, p, consts, consts["rope"], tm=1024)
        uv_c, nqkv_c, gqk_c, gvg_c, gb_c = _inproj(cx, l, mod, ctx_row, p, consts, None, tm=n)

        y_fn = _fourier(uv, l, consts["dft_lat"], p["fw"])
        y_na, y_na_c = _na(nqkv, nqkv_c, l, p["rpb_rows"], need_ctx=need_ctx)
        o_gla = _gla(gqk, gvg, gb, gqk_c, gvg_c, gb_c, consts["gla_causal"], consts["gla_vmask"],
                     need_ctx=need_ctx)
        x = _post(x, y_fn, y_na, o_gla[0], o_gla[1], gvg, l, mod, None, p, tm=512)
        if need_ctx:
            y_fn_c = _fourier(uv_c, l, consts["dft_ctx"], p["fw"])
            cx = _post(flat(cx), flat(y_fn_c), flat(y_na_c), flat(o_gla[2]), flat(o_gla[3]), flat(gvg_c),
                       l, mod, ctx_row, p, tm=CTX_TM).reshape(bsz, n, d)
    return x
```

```python
import functools

import numpy as np
import jax
import jax.numpy as jnp
from jax import lax
from jax.experimental import pallas as pl
from jax.experimental.pallas import tpu as pltpu

F32 = jnp.float32
BF16 = jnp.bfloat16

D_MODEL = 1024
GRID_W = 64
FN_W = 256
FN_GROUPS = 4
FN_GD = 64
NA_HD = 64
NA_W = 384
NA_HEADS = 6
NA_KH = 8
NA_KW = 16
NA_GROUP = 4
NA_WIN = NA_GROUP + NA_KH
LOG2E = float(np.log2(np.e))
NA_MASKED = -1e30
GLA_HEADS = 4
GLA_DK = 48
GLA_DV = 96
GLA_QK_W = GLA_HEADS * GLA_DK
GLA_V_W = GLA_HEADS * GLA_DV
GLA_RANK = 16
GLA_GATE_NORM = 16.0
GLA_CHUNK = 64
GLA_BLOCK = 2 * GLA_CHUNK
GLA_UNROLL = 4
ROPE_BASE = 10000.0
D_FF = 2816
EPS = 1e-6
MOD_ROWS = 16

LANES = 128
MXU_W = 256
VMEM_LIMIT_BYTES = 56 * 1024 * 1024

GLA_DK_PAD = 64
GLA_DV_PAD = 128
GLA_QK_PAD = GLA_HEADS * GLA_DK_PAD
GLA_V_PAD = GLA_HEADS * GLA_DV_PAD
GLA_A_PAD = LANES
MIX_PAD = FN_W + NA_W + GLA_V_PAD

C_FX = 0
C_NQ = C_FX + FN_W
C_NK = C_NQ + NA_W
C_NV = C_NK + NA_W
C_GA = C_NV + NA_W
C_GQ = C_GA + GLA_A_PAD
C_GK = C_GQ + GLA_QK_PAD
C_GV = C_GK + GLA_QK_PAD
C_GG = C_GV + GLA_V_PAD
N_IN_PAD = C_GG + GLA_V_PAD

SUB = 256
CTX_TM = 512


def _cparams(sem):
    return pltpu.CompilerParams(dimension_semantics=sem, vmem_limit_bytes=VMEM_LIMIT_BYTES)


def _const_spec(shape):
    nd = len(shape)
    return pl.BlockSpec(tuple(shape), lambda *_: (0,) * nd, pipeline_mode=pl.Buffered(1))


def _layer_spec(shape, l):
    nd = len(shape)
    return pl.BlockSpec((None,) + tuple(shape), lambda *_: (l,) + (0,) * nd, pipeline_mode=pl.Buffered(1))


def _mod_spec(d, l, mod_row):
    if mod_row is None:
        return pl.BlockSpec((None, None, 6, d), lambda b, i: (l, b, 0, 0))
    return pl.BlockSpec((None, None, 6, d), lambda b, i: (l, mod_row, 0, 0))


def _dot(a, b):
    return jnp.dot(a, b, preferred_element_type=F32)


def _dot_nt(a, b):
    return lax.dot_general(a, b, (((1,), (1,)), ((), ())), preferred_element_type=F32)


def _ada_kernel(c_ref, w_ref, b_ref, o_ref):
    a = c_ref[...]
    a = (a * jax.nn.sigmoid(a)).astype(BF16)
    o_ref[...] = _dot(a, w_ref[...].astype(BF16)) + b_ref[...]


def _ada_mod(cc, ada_w, ada_b):
    depth, d, n = ada_w.shape
    rows = cc.shape[0]
    tn = 1024
    return pl.pallas_call(
        _ada_kernel,
        grid=(depth, n // tn),
        in_specs=[
            pl.BlockSpec((rows, d), lambda l, j: (0, 0)),
            pl.BlockSpec((None, d, tn), lambda l, j: (l, 0, j)),
            pl.BlockSpec((None, 1, tn), lambda l, j: (l, 0, j)),
        ],
        out_specs=pl.BlockSpec((None, rows, tn), lambda l, j: (l, 0, j)),
        out_shape=jax.ShapeDtypeStruct((depth, rows, n), F32),
        compiler_params=_cparams(("arbitrary", "arbitrary")),
        name="ada_mod",
    )(cc, ada_w, ada_b.reshape(depth, 1, n))


def _modulated_norm(x, gain, shift, scale):
    ms = jnp.mean(x * x, axis=-1, keepdims=True)
    return (x * lax.rsqrt(ms + EPS) * gain) * (1.0 + scale) + shift


def _log_sigmoid(x):
    return jnp.minimum(x, 0.0) - jnp.log1p(jnp.exp(-jnp.abs(x)))


def _head_mean_sq(z):
    first = lax.broadcasted_iota(jnp.int32, (1, LANES), 1) < NA_HD
    out = []
    for t in range(NA_W // LANES):
        sq = z[:, t * LANES:(t + 1) * LANES]
        sq = sq * sq
        lo = jnp.sum(jnp.where(first, sq, 0.0), axis=-1, keepdims=True)
        hi = jnp.sum(jnp.where(first, 0.0, sq), axis=-1, keepdims=True)
        out.append(jnp.where(first, lo, hi))
    return jnp.concatenate(out, axis=1) * (1.0 / NA_HD)


def _inproj_kernel(x_ref, mod_ref, gain_ref, w_ref, cs_ref, qkg_ref, rope_ref, aw_ref, ab_ref, tri_ref,
                   uv_ref, nqkv_ref, gqk_ref, gvg_ref, gb_ref, *, rope, tm):
    x = x_ref[...]
    h = _modulated_norm(x, gain_ref[...], mod_ref[0:1, :], mod_ref[1:2, :]).astype(BF16)

    zf = _dot(h, w_ref[:, C_FX:C_NQ]).astype(BF16)
    zva = _dot(h, w_ref[:, C_NV:C_GQ])
    nqkv_ref[:, 2 * NA_W:3 * NA_W] = zva[:, :NA_W].astype(BF16)
    a = zva[:, NA_W:].astype(BF16)
    zqk = _dot(h, w_ref[:, C_NQ:C_NV])
    logits = _dot(a, aw_ref[...]) + ab_ref[...]
    zg = _dot(h, w_ref[:, C_GQ:C_GV])
    uv = _dot(zf, cs_ref[...])
    for j in range(uv_ref.shape[0]):
        uv_ref[j] = uv[:, j * LANES:(j + 1) * LANES]
    gvg_ref[...] = _dot(h, w_ref[:, C_GV:N_IN_PAD]).astype(BF16)

    glog = (_log_sigmoid(logits) * (1.0 / GLA_GATE_NORM)).astype(BF16)
    for s in range(tm // SUB):
        rows = slice(s * SUB, (s + 1) * SUB)
        gb_ref[rows, 0:GLA_QK_PAD] = _dot(tri_ref[0], glog[rows, 0:GLA_QK_PAD])
        gb_ref[rows, GLA_QK_PAD:] = _dot(tri_ref[1], glog[rows, GLA_QK_PAD:])

    for i in range(2):
        z = zqk[:, i * NA_W:(i + 1) * NA_W]
        z = z * lax.rsqrt(_head_mean_sq(z) + EPS) * qkg_ref[i:i + 1, :]
        nqkv_ref[:, i * NA_W:(i + 1) * NA_W] = z.astype(BF16)

    for i in range(2):
        z = zg[:, i * GLA_QK_PAD:(i + 1) * GLA_QK_PAD]
        if rope:
            z = (z * rope_ref[0]
                 + pltpu.roll(z, GLA_QK_PAD - GLA_DK // 4, 1) * rope_ref[1]
                 + pltpu.roll(z, GLA_DK // 4, 1) * rope_ref[2])
        if i == 0:
            z = z * (GLA_DK ** -0.5)
        gqk_ref[:, i * GLA_QK_PAD:(i + 1) * GLA_QK_PAD] = z.astype(BF16)


def _inproj(x, l, mod, mod_row, p, consts, rope_tab, *, tm):
    bsz, t, d = x.shape
    assert t % tm == 0 and tm % SUB == 0
    rope = rope_tab is not None
    if not rope:
        rope_tab = jnp.zeros((3, tm, GLA_QK_PAD), F32)
    rope_idx = (lambda b, i: (0, i, 0)) if rope else (lambda b, i: (0, 0, 0))
    row = lambda w_: pl.BlockSpec((None, tm, w_), lambda b, i: (b, i, 0))
    uv_tiles = 2 * FN_W // LANES
    out_w = (3 * NA_W, 2 * GLA_QK_PAD, 2 * GLA_V_PAD, 2 * GLA_QK_PAD)
    out_dt = (BF16, BF16, BF16, F32)
    return pl.pallas_call(
        functools.partial(_inproj_kernel, rope=rope, tm=tm),
        grid=(bsz, t // tm),
        in_specs=[
            row(d),
            _mod_spec(d, l, mod_row),
            _layer_spec((1, d), l),
            _layer_spec((d, N_IN_PAD), l),
            _const_spec(consts["cs"].shape),
            _layer_spec((2, NA_W), l),
            pl.BlockSpec((3, tm, GLA_QK_PAD), rope_idx),
            _layer_spec((GLA_A_PAD, 2 * GLA_QK_PAD), l),
            _layer_spec((1, 2 * GLA_QK_PAD), l),
            _const_spec((2, SUB, SUB)),
        ],
        out_specs=[pl.BlockSpec((None, uv_tiles, tm, LANES), lambda b, i: (b, 0, i, 0))]
        + [row(w_) for w_ in out_w],
        out_shape=[jax.ShapeDtypeStruct((bsz, uv_tiles, t, LANES), F32)]
        + [jax.ShapeDtypeStruct((bsz, t, w_), dt) for w_, dt in zip(out_w, out_dt)],
        compiler_params=_cparams(("arbitrary", "arbitrary")),
        name="inproj_rope" if rope else "inproj_ctx",
    )(x, mod, p["gain_m"], p["w"], consts["cs"], p["qkg"], rope_tab, p["aw"], p["ab"], consts["tri"])


def _fourier_kernel(tab_ref, uv_ref, w_ref, o_ref, *, scale):
    half = uv_ref.shape[1] // 2

    def positions(parity):
        tiles = [uv_ref[j, pl.ds(parity, half, stride=2), :].astype(BF16) for j in range(uv_ref.shape[0])]
        return jnp.concatenate(tiles[:2], axis=1), jnp.concatenate(tiles[2:], axis=1)

    ue, ve = positions(0)
    uo, vo = positions(1)
    even = _dot(tab_ref[0], ue) - _dot(tab_ref[1], ve)
    odd = _dot(tab_ref[2], uo) - _dot(tab_ref[3], vo)
    w = w_ref[...]
    o_ref[0:half, :] = _dot(((even + odd) * scale).astype(BF16), w).astype(BF16)
    o_ref[half:, :] = _dot(((even - odd) * scale).astype(BF16), w).astype(BF16)


def _fourier(uv, l, tabs, fnet_w):
    bsz, tiles, t, _ = uv.shape
    scale = float((t * FN_GD) ** -0.5)
    return pl.pallas_call(
        functools.partial(_fourier_kernel, scale=scale),
        grid=(bsz,),
        in_specs=[
            _const_spec(tabs.shape),
            pl.BlockSpec((None, tiles, t, LANES), lambda b: (b, 0, 0, 0)),
            _layer_spec((FN_W, FN_W), l),
        ],
        out_specs=pl.BlockSpec((None, t, FN_W), lambda b: (b, 0, 0)),
        out_shape=jax.ShapeDtypeStruct((bsz, t, FN_W), BF16),
        compiler_params=_cparams(("arbitrary",)),
        name=f"fourier_{t}",
    )(tabs, uv, fnet_w)


def _na_softmax_t(s):
    return jnp.exp2(s - jnp.max(s, axis=0, keepdims=True)).astype(BF16)


def _na_heads_out_t(o0, o1):
    row = lax.broadcasted_iota(jnp.int32, o0.shape, 0)
    out_t = jnp.where(row < NA_HD, o0 / o0[NA_HD:NA_HD + 1, :], o1 / o1[0:1, :])
    return out_t.T.astype(BF16)


def _na_group_kinds():
    margin = NA_WIN - NA_KH
    return ((0, lambda i: 0), (NA_KH // 2, lambda i: i), (margin + NA_KH - NA_GROUP, lambda i: margin))


def _na_kernel(q_ref, k_ref, v_ref, qc_ref, kc_ref, vc_ref, rpb_ref, o_ref, oc_ref,
               vt_ref, sc_ref, bias_ref, *, n_rows, need_ctx):
    lane = lax.broadcasted_iota(jnp.int32, (1, LANES), 1)
    first = lane < NA_HD
    zero = jnp.zeros((), BF16)
    one = jnp.ones((), BF16)
    v = v_ref[...]
    vt_ref[0] = jnp.where(first, v, one).T
    vt_ref[1] = jnp.where(first, one, v).T
    kc = kc_ref[...]
    vc = vc_ref[...]
    vct = (jnp.where(first, vc, one).T, jnp.where(first, one, vc).T)

    def both_heads(q):
        return jnp.concatenate([jnp.where(first, q, zero), jnp.where(first, zero, q)], axis=0)

    n_groups = n_rows // NA_GROUP
    gq = NA_GROUP * GRID_W
    n_loc = NA_WIN * GRID_W
    margin = NA_WIN - NA_KH

    @pl.when(pl.program_id(1) == 0)
    def _():
        wrow = lax.broadcasted_iota(jnp.int32, (n_loc, GRID_W), 0)
        for hh in range(2):
            for kind, (q_off, a_lo_of) in enumerate(_na_group_kinds()):
                for j in range(NA_GROUP // 2):
                    halves = []
                    for i in (2 * j, 2 * j + 1):
                        start = (NA_KH - 1 - q_off - i + margin) * GRID_W
                        lo = a_lo_of(i) * GRID_W
                        band = (wrow >= lo) & (wrow < lo + NA_KH * GRID_W)
                        halves.append(jnp.where(band, rpb_ref[hh, start:start + n_loc, :], NA_MASKED))
                    bias_ref[hh, kind, :, j * LANES:(j + 1) * LANES] = jnp.concatenate(halves, axis=1)

    def window(g):
        ws = jnp.clip(g * NA_GROUP - NA_KH // 2, 0, n_rows - NA_WIN)
        return (pl.ds(pl.multiple_of(g * gq, gq), gq), pl.ds(pl.multiple_of(ws * GRID_W, gq), n_loc))

    def scores(g, slot):
        qs, ks = window(g)
        kind = jnp.where(g == 0, 0, jnp.where(g == n_groups - 1, 2, 1))
        q = both_heads(q_ref[qs, :])
        s_loc = _dot_nt(k_ref[ks, :], q)
        s_ctx = _dot_nt(kc, q)
        for hh in range(2):
            cols = slice(hh * gq, (hh + 1) * gq)
            sc_ref[slot, 0:n_loc, cols] = s_loc[:, cols] + bias_ref[hh, kind]
        sc_ref[slot, n_loc:, :] = s_ctx

    def finish(g, slot):
        qs, ks = window(g)
        p = _na_softmax_t(sc_ref[slot])
        outs = [_dot(jnp.concatenate([vt_ref[hh, :, ks], vct[hh]], axis=1), p[:, hh * gq:(hh + 1) * gq])
                for hh in range(2)]
        o_ref[qs, :] = _na_heads_out_t(outs[0], outs[1])

    scores(0, 0)

    def pair(i, carry):
        scores(2 * i + 1, 1)
        finish(2 * i, 0)
        scores(2 * i + 2, 0)
        finish(2 * i + 1, 1)
        return carry

    lax.fori_loop(0, n_groups // 2 - 1, pair, 0)
    scores(n_groups - 1, 1)
    finish(n_groups - 2, 0)
    finish(n_groups - 1, 1)

    if need_ctx:
        n_ctx = qc_ref.shape[0]
        p = _na_softmax_t(_dot_nt(kc, both_heads(qc_ref[...])))
        outs = [_dot(vct[hh], p[:, hh * n_ctx:(hh + 1) * n_ctx]) for hh in range(2)]
        oc_ref[...] = _na_heads_out_t(outs[0], outs[1])
    else:
        oc_ref[...] = jnp.zeros(oc_ref.shape, oc_ref.dtype)


def _na(nqkv, nqkv_c, l, rpb_rows, *, need_ctx):
    bsz, t, _ = nqkv.shape
    n = nqkv_c.shape[1]
    n_rows = t // GRID_W
    assert n_rows % (2 * NA_GROUP) == 0 and n_rows >= NA_WIN + NA_GROUP
    pairs = NA_HEADS // 2
    n_loc = NA_WIN * GRID_W
    gq = NA_GROUP * GRID_W
    nb = NA_W // LANES
    lat = lambda off: pl.BlockSpec((None, t, LANES), lambda j, b: (b, 0, off + j))
    ctx = lambda off: pl.BlockSpec((None, n, LANES), lambda j, b: (b, 0, off + j))
    return pl.pallas_call(
        functools.partial(_na_kernel, n_rows=n_rows, need_ctx=need_ctx),
        grid=(pairs, bsz),
        in_specs=[lat(0), lat(nb), lat(2 * nb), ctx(0), ctx(nb), ctx(2 * nb),
                  pl.BlockSpec((None, 2) + rpb_rows.shape[2:], lambda j, b: (l, j, 0, 0))],
        out_specs=[pl.BlockSpec((None, t, LANES), lambda j, b: (b, 0, j)),
                   pl.BlockSpec((None, n, LANES), lambda j, b: (b, 0, j))],
        out_shape=[jax.ShapeDtypeStruct((bsz, t, NA_W), BF16),
                   jax.ShapeDtypeStruct((bsz, n, NA_W), BF16)],
        scratch_shapes=[pltpu.VMEM((2, LANES, t), BF16),
                        pltpu.VMEM((2, n_loc + n, 2 * gq), F32),
                        pltpu.VMEM((2, len(_na_group_kinds()), n_loc, gq), F32)],
        compiler_params=_cparams(("arbitrary", "arbitrary")),
        name="na_attn",
    )(nqkv, nqkv, nqkv, nqkv_c, nqkv_c, nqkv_c, rpb_rows)


def _gla_steps(jobs, st_ref, causal_ref, vmask_ref):
    c = GLA_CHUNK
    head = lax.broadcasted_iota(jnp.int32, (1, GLA_QK_PAD), 1) // GLA_DK_PAD
    prep = []
    for qk_ref, vg_ref, b_ref, out_ref, ci, d in jobs:
        rows = pl.ds(pl.multiple_of(ci * GLA_BLOCK, GLA_BLOCK), GLA_BLOCK)
        k = qk_ref[rows, GLA_QK_PAD:].astype(F32)
        v = vg_ref[rows, 0:GLA_V_PAD]
        b = b_ref[rows, d * GLA_QK_PAD:(d + 1) * GLA_QK_PAD]
        lo, hi = b[0:c, :], b[c:, :]
        if d == 0:
            tot1, tot2 = lo[c - 1:c, :], hi[c - 1:c, :]
            b_rel = jnp.concatenate([lo - tot1, hi], axis=0)
        else:
            tot1, tot2 = hi[0:1, :], lo[0:1, :]
            b_rel = jnp.concatenate([lo, hi - tot1], axis=0)
        k_in = k * jnp.exp(-b_rel)
        k_end = k_in * jnp.exp(tot2)
        q_in = q_st = None
        if out_ref is not None:
            q_rel = qk_ref[rows, 0:GLA_QK_PAD].astype(F32) * jnp.exp(b_rel)
            q_in = q_rel.astype(BF16)
            q_st = (q_rel * jnp.exp(tot1)).astype(BF16)
        decay = jnp.exp(jnp.broadcast_to(tot1 + tot2, (GLA_DV_PAD, GLA_QK_PAD)).T)
        prep.append((rows, q_in, q_st, k_in, k_end, v, decay))

    att = [None if q_in is None else
           _dot_nt(q_in, jnp.concatenate([jnp.where(head == h, k_in, 0.0) for h in range(GLA_HEADS)],
                                         axis=0).astype(BF16))
           for (rows, q_in, q_st, k_in, k_end, v, decay) in prep]

    kv = []
    for (rows, q_in, q_st, k_in, k_end, v, decay) in prep:
        k_t = k_end.astype(BF16).T
        kv.append(jnp.concatenate(
            [_dot(k_t[h * GLA_DK_PAD:(h + 1) * GLA_DK_PAD, :], v[:, h * GLA_DV_PAD:(h + 1) * GLA_DV_PAD])
             for h in range(GLA_HEADS)], axis=0))

    inter = []
    zero = jnp.zeros((GLA_DK_PAD, GLA_DV_PAD), BF16)
    for j, ((qk_ref, vg_ref, b_ref, out_ref, ci, d), (rows, q_in, q_st, k_in, k_end, v, decay)) in enumerate(
            zip(jobs, prep)):
        st = st_ref[d]
        if q_st is None:
            inter.append(None)
        else:
            sb = st.astype(BF16)
            s_bd = jnp.concatenate(
                [jnp.concatenate([sb[h * GLA_DK_PAD:(h + 1) * GLA_DK_PAD, :] if h2 == h else zero
                                  for h2 in range(GLA_HEADS)], axis=1) for h in range(GLA_HEADS)], axis=0)
            inter.append(_dot(q_st, s_bd))
        st_ref[d] = st * decay + kv[j]

    for j, ((qk_ref, vg_ref, b_ref, out_ref, ci, d), (rows, q_in, q_st, k_in, k_end, v, decay)) in enumerate(
            zip(jobs, prep)):
        if out_ref is not None:
            a = (att[j] * causal_ref[d]).astype(BF16)
            intra = []
            for pair in range(GLA_HEADS // 2):
                cols = slice(pair * 2 * GLA_DV_PAD, (pair + 1) * 2 * GLA_DV_PAD)
                v_pair = jnp.concatenate([v[:, cols]] * 2, axis=0) * vmask_ref[...]
                intra.append(_dot(a[:, pair * 2 * GLA_BLOCK:(pair + 1) * 2 * GLA_BLOCK], v_pair))
            out_ref[rows, :] = (inter[j] + jnp.concatenate(intra, axis=1)).astype(out_ref.dtype)


def _gla_kernel(qk_ref, vg_ref, b_ref, qkc_ref, vgc_ref, bc_ref, causal_ref, vmask_ref, *rest, t, n, need_ctx):
    if need_ctx:
        of_ref, ob_ref, ocf_ref, ocb_ref, st_ref = rest
    else:
        of_ref, ob_ref, st_ref = rest
    st_ref[...] = jnp.zeros(st_ref.shape, F32)
    nb_ctx = n // GLA_BLOCK
    nb_lat = t // GLA_BLOCK

    def jobs(refs, outs, n_blocks, i):
        return [refs + (outs[0], i, 0), refs + (outs[1], n_blocks - 1 - i, 1)]

    ctx_outs = (ocf_ref, ocb_ref) if need_ctx else (None, None)
    for i in range(nb_ctx):
        _gla_steps(jobs((qkc_ref, vgc_ref, bc_ref), ctx_outs, nb_ctx, i), st_ref, causal_ref, vmask_ref)

    def step(i, carry):
        _gla_steps(jobs((qk_ref, vg_ref, b_ref), (of_ref, ob_ref), nb_lat, i), st_ref, causal_ref, vmask_ref)
        return carry

    lax.fori_loop(0, nb_lat, step, 0, unroll=GLA_UNROLL)


def _gla(gqk, gvg, gb, gqk_c, gvg_c, gb_c, causal, vmask, *, need_ctx):
    bsz, t, _ = gqk.shape
    n = gqk_c.shape[1]
    assert t % GLA_BLOCK == 0 and n % GLA_BLOCK == 0
    row = lambda rows, w_: pl.BlockSpec((None, rows, w_), lambda b: (b, 0, 0))
    out_rows = (t, t, n, n) if need_ctx else (t, t)
    return pl.pallas_call(
        functools.partial(_gla_kernel, t=t, n=n, need_ctx=need_ctx),
        grid=(bsz,),
        in_specs=[row(t, 2 * GLA_QK_PAD), row(t, GLA_V_PAD), row(t, 2 * GLA_QK_PAD),
                  row(n, 2 * GLA_QK_PAD), row(n, GLA_V_PAD), row(n, 2 * GLA_QK_PAD),
                  _const_spec(causal.shape), _const_spec(vmask.shape)],
        out_specs=[row(r, GLA_V_PAD) for r in out_rows],
        out_shape=[jax.ShapeDtypeStruct((bsz, r, GLA_V_PAD), BF16) for r in out_rows],
        scratch_shapes=[pltpu.VMEM((2, GLA_QK_PAD, GLA_DV_PAD), F32)],
        compiler_params=_cparams(("arbitrary",)),
        name="gla_scan",
    )(gqk, gvg, gb, gqk_c, gvg_c, gb_c, causal, vmask)


def _gla_out(of_ref, ob_ref, g_ref, gain_ref):
    heads = []
    for h in range(GLA_HEADS):
        cols = slice(h * GLA_DV_PAD, (h + 1) * GLA_DV_PAD)
        o = of_ref[:, cols].astype(F32) + ob_ref[:, cols].astype(F32)
        ms = jnp.sum(o * o, axis=-1, keepdims=True) * (1.0 / GLA_DV)
        g = g_ref[:, cols].astype(F32)
        heads.append((o * lax.rsqrt(ms + EPS) * gain_ref[:, cols] * (g * jax.nn.sigmoid(g))).astype(BF16))
    return jnp.concatenate(heads, axis=1)


def _post_kernel(x_ref, yfn_ref, yna_ref, of_ref, ob_ref, g_ref, mod_ref, gain_o_ref, gain_ref,
                 wo_ref, w1_ref, w3_ref, w2_ref, o_ref):
    y = (_dot(yfn_ref[...], wo_ref[0:FN_W, :])
         + _dot(yna_ref[...], wo_ref[FN_W:FN_W + NA_W, :])
         + _dot(_gla_out(of_ref, ob_ref, g_ref, gain_o_ref), wo_ref[FN_W + NA_W:, :]))
    x1 = x_ref[...] + mod_ref[2:3, :] * y
    h = _modulated_norm(x1, gain_ref[...], mod_ref[3:4, :], mod_ref[4:5, :]).astype(BF16)
    acts = []
    for c in range(D_FF // MXU_W):
        cols = slice(c * MXU_W, (c + 1) * MXU_W)
        u = _dot(h, w1_ref[:, cols])
        g = _dot(h, w3_ref[:, cols])
        acts.append((u * jax.nn.sigmoid(u) * g).astype(BF16))
    acc = _dot(jnp.concatenate(acts, axis=1), w2_ref[...])
    o_ref[...] = x1 + mod_ref[5:6, :] * acc


def _post(x, yfn, yna, o_fwd, o_bwd, gvg, l, mod, mod_row, p, *, tm):
    bsz, t, d = x.shape
    assert t % tm == 0
    row = lambda w_: pl.BlockSpec((None, tm, w_), lambda b, i: (b, i, 0))
    gate = pl.BlockSpec((None, tm, GLA_V_PAD), lambda b, i: (b, i, 1))
    return pl.pallas_call(
        _post_kernel,
        grid=(bsz, t // tm),
        in_specs=[row(d), row(FN_W), row(NA_W), row(GLA_V_PAD), row(GLA_V_PAD), gate,
                  _mod_spec(d, l, mod_row), _layer_spec((1, GLA_V_PAD), l),
                  _layer_spec((1, d), l), _layer_spec((MIX_PAD, d), l), _layer_spec((d, D_FF), l),
                  _layer_spec((d, D_FF), l), _layer_spec((D_FF, d), l)],
        out_specs=row(d),
        out_shape=jax.ShapeDtypeStruct((bsz, t, d), F32),
        compiler_params=_cparams(("arbitrary", "arbitrary")),
        name=f"post_{t}",
    )(x, yfn, yna, o_fwd, o_bwd, gvg, mod, p["gain_o"], p["gain_f"], p["wo"], p["w1"], p["w3"], p["w2"])


def _dft_tables(n):
    idx = (np.arange(n)[:, None] * np.arange(n)[None, :]) % n
    ang = 2.0 * np.pi * idx / n
    return np.cos(ang), np.sin(ang)


def _shape_consts(t, n):
    c64, s64 = _dft_tables(FN_GD)
    eye = np.eye(FN_GROUPS)
    cs = np.concatenate([np.kron(eye, c64), np.kron(eye, s64)], axis=1)
    chunk_id = np.arange(SUB) // GLA_CHUNK
    same = chunk_id[:, None] == chunk_id[None, :]
    pos = np.arange(SUB)
    tri = np.stack([same & (pos[None, :] <= pos[:, None]), same & (pos[None, :] >= pos[:, None])])
    cpos = np.arange(GLA_BLOCK)
    fwd = cpos[None, :] <= cpos[:, None]
    gla_causal = np.stack([np.tile(fwd, (1, GLA_HEADS)), np.tile(fwd.T, (1, GLA_HEADS))])
    gla_vmask = ((np.arange(2 * GLA_BLOCK)[:, None] // GLA_BLOCK)
                 == (np.arange(2 * GLA_DV_PAD)[None, :] // GLA_DV_PAD))

    m = GLA_DK // 4
    inv = ROPE_BASE ** (-np.arange(m) / m)
    tok = np.arange(t)
    cos_h = np.ones((t, GLA_DK_PAD))
    sin_lo = np.zeros((t, GLA_DK_PAD))
    sin_hi = np.zeros((t, GLA_DK_PAD))
    for blk, p in enumerate((tok // GRID_W, tok % GRID_W)):
        ang = p[:, None] * inv[None, :]
        o = blk * 2 * m
        cos_h[:, o:o + m] = np.cos(ang)
        cos_h[:, o + m:o + 2 * m] = np.cos(ang)
        sin_lo[:, o:o + m] = -np.sin(ang)
        sin_hi[:, o + m:o + 2 * m] = np.sin(ang)
    rope = np.stack([np.tile(a, (1, GLA_HEADS)) for a in (cos_h, sin_lo, sin_hi)])

    def position_tables(length):
        cos, sin = _dft_tables(length)
        half = length // 2
        return np.stack([cos[:half, 0::2], sin[:half, 0::2], cos[:half, 1::2], sin[:half, 1::2]])

    to_bf16 = lambda a: jnp.asarray(a, F32).astype(BF16)
    return {
        "cs": to_bf16(cs), "tri": jnp.asarray(tri, BF16),
        "gla_causal": jnp.asarray(gla_causal, F32), "gla_vmask": jnp.asarray(gla_vmask, BF16),
        "rope": jnp.asarray(rope, F32),
        "dft_lat": to_bf16(position_tables(t)), "dft_ctx": to_bf16(position_tables(n)),
    }


def _bias_kernel(rpb_ref, onehot_ref, ok_ref, o_ref):
    r = rpb_ref[...]
    onehot = onehot_ref[...]
    t = jnp.zeros(o_ref.shape, F32)
    for _ in range(3):
        piece = r.astype(BF16)
        t = t + _dot(piece, onehot)
        r = r - piece.astype(F32)
    o_ref[...] = jnp.where(ok_ref[...] > 0.0, t * LOG2E, NA_MASKED)


def _window_bias_rows(na_rpb):
    depth, heads, n_dr, n_dc = na_rpb.shape
    cq = np.arange(GRID_W)
    c0 = np.clip(cq - NA_KW // 2, 0, GRID_W - NA_KW)
    col_ok = (cq[None, :] >= c0[:, None]) & (cq[None, :] < c0[:, None] + NA_KW)
    dc = np.clip(cq[None, :] - cq[:, None], -(NA_KW - 1), NA_KW - 1) + NA_KW - 1
    n_dc_pad = n_dc + 1
    n_tab = depth * heads * n_dr
    rows_pad = -(-n_tab // 8) * 8
    onehot = (np.arange(n_dc_pad)[:, None, None] == dc[None]).reshape(n_dc_pad, GRID_W * GRID_W)
    rpb2 = jnp.pad(na_rpb.reshape(n_tab, n_dc), ((0, rows_pad - n_tab), (0, 1)))
    t = pl.pallas_call(
        _bias_kernel,
        out_shape=jax.ShapeDtypeStruct((rows_pad, GRID_W * GRID_W), F32),
        name="na_bias",
    )(rpb2, jnp.asarray(onehot, BF16), jnp.asarray(col_ok.reshape(1, -1), F32))
    t = t[:n_tab].reshape(depth * heads, n_dr, GRID_W, GRID_W)
    t = t.transpose(0, 1, 3, 2).reshape(depth * heads, n_dr * GRID_W, GRID_W)
    margin = NA_WIN - NA_KH
    t = jnp.pad(t, ((0, 0), (margin * GRID_W, margin * GRID_W), (0, 0)))
    return t.reshape(depth, heads, (n_dr + 2 * margin) * GRID_W, GRID_W)


def _pad_heads(w, heads, dim, pad, axis=-1):
    axis = axis % w.ndim
    shape = w.shape
    w = w.reshape(shape[:axis] + (heads, dim) + shape[axis + 1:])
    widths = [(0, 0)] * w.ndim
    widths[axis + 1] = (0, pad - dim)
    return jnp.pad(w, widths).reshape(shape[:axis] + (heads * pad,) + shape[axis + 1:])


def _params(norm_mix, norm_ffn, w_in, fnet_w, na_q_norm, na_k_norm, na_rpb, alpha_w, alpha_b, o_norm, w_out,
            ffn_w1, ffn_w3, ffn_w2):
    depth = w_in.shape[0]
    o = FN_W + 3 * NA_W
    gq = w_in[..., o:o + GLA_QK_W]
    gk = w_in[..., o + GLA_QK_W:o + 2 * GLA_QK_W]
    o2 = o + 2 * GLA_QK_W
    gv = w_in[..., o2:o2 + GLA_V_W]
    gg = w_in[..., o2 + GLA_V_W:o2 + 2 * GLA_V_W]
    ga = w_in[..., o2 + 2 * GLA_V_W:]
    w = jnp.concatenate([
        w_in[..., :o],
        jnp.pad(ga, ((0, 0), (0, 0), (0, GLA_A_PAD - 2 * GLA_RANK))),
        _pad_heads(gq, GLA_HEADS, GLA_DK, GLA_DK_PAD), _pad_heads(gk, GLA_HEADS, GLA_DK, GLA_DK_PAD),
        _pad_heads(gv, GLA_HEADS, GLA_DV, GLA_DV_PAD), _pad_heads(gg, GLA_HEADS, GLA_DV, GLA_DV_PAD),
    ], axis=-1).astype(BF16)

    qkg = jnp.stack([jnp.tile(na_q_norm, (1, NA_HEADS)) * (NA_HD ** -0.5 * LOG2E),
                     jnp.tile(na_k_norm, (1, NA_HEADS))], axis=1)

    aw_pad = _pad_heads(alpha_w, GLA_HEADS, GLA_DK, GLA_DK_PAD)
    aw = jnp.concatenate([jnp.pad(aw_pad[:, 0], ((0, 0), (0, 0), (0, GLA_QK_PAD))),
                          jnp.pad(aw_pad[:, 1], ((0, 0), (0, 0), (GLA_QK_PAD, 0)))], axis=1)
    aw = jnp.pad(aw, ((0, 0), (0, GLA_A_PAD - 2 * GLA_RANK), (0, 0))).astype(BF16)
    ab = _pad_heads(alpha_b, GLA_HEADS, GLA_DK, GLA_DK_PAD).reshape(depth, 1, 2 * GLA_QK_PAD)

    gain_o = _pad_heads(jnp.tile(o_norm, (1, GLA_HEADS)), GLA_HEADS, GLA_DV, GLA_DV_PAD)[:, None, :]
    wo = jnp.concatenate([w_out[:, :FN_W + NA_W],
                          _pad_heads(w_out[:, FN_W + NA_W:], GLA_HEADS, GLA_DV, GLA_DV_PAD, axis=1)],
                         axis=1).astype(BF16)
    return {
        "gain_m": norm_mix[:, None, :], "gain_f": norm_ffn[:, None, :], "w": w, "qkg": qkg, "aw": aw, "ab": ab,
        "rpb_rows": _window_bias_rows(na_rpb), "gain_o": gain_o, "wo": wo, "fw": fnet_w.astype(BF16),
        "w1": ffn_w1.astype(BF16), "w3": ffn_w3.astype(BF16), "w2": ffn_w2.astype(BF16),
    }


def kernel(x, c, ctx, c_ctx, ada_w, ada_b, norm_mix, norm_ffn, w_in, fnet_w, na_q_norm, na_k_norm, na_rpb,
           gla_alpha_w, gla_alpha_b, gla_o_norm, w_out, ffn_w1, ffn_w3, ffn_w2):
    bsz, t, d = x.shape
    n = ctx.shape[1]
    depth = ada_w.shape[0]
    ctx_row = bsz
    assert bsz < MOD_ROWS and n % SUB == 0 and (bsz * n) % CTX_TM == 0
    consts = _shape_consts(t, n)
    p = _params(norm_mix, norm_ffn, w_in, fnet_w, na_q_norm, na_k_norm, na_rpb, gla_alpha_w, gla_alpha_b,
                gla_o_norm, w_out, ffn_w1, ffn_w3, ffn_w2)

    cc = jnp.concatenate([c, c_ctx[None, :], jnp.zeros((MOD_ROWS - bsz - 1, d), F32)])
    mod = _ada_mod(cc, ada_w, ada_b).reshape(depth, MOD_ROWS, 6, d)

    cx = ctx
    flat = lambda a: a.reshape(1, bsz * n, a.shape[-1])
    for l in range(depth):
        need_ctx = l < depth - 1
        uv, nqkv, gqk, gvg, gb = _inproj(x, l, mod, None, p, consts, consts["rope"], tm=1024)
        uv_c, nqkv_c, gqk_c, gvg_c, gb_c = _inproj(cx, l, mod, ctx_row, p, consts, None, tm=n)

        y_fn = _fourier(uv, l, consts["dft_lat"], p["fw"])
        y_na, y_na_c = _na(nqkv, nqkv_c, l, p["rpb_rows"], need_ctx=need_ctx)
        o_gla = _gla(gqk, gvg, gb, gqk_c, gvg_c, gb_c, consts["gla_causal"], consts["gla_vmask"],
                     need_ctx=need_ctx)
        x = _post(x, y_fn, y_na, o_gla[0], o_gla[1], gvg, l, mod, None, p, tm=512)
        if need_ctx:
            y_fn_c = _fourier(uv_c, l, consts["dft_ctx"], p["fw"])
            cx = _post(flat(cx), flat(y_fn_c), flat(y_na_c), flat(o_gla[2]), flat(o_gla[3]), flat(gvg_c),
                       l, mod, ctx_row, p, tm=CTX_TM).reshape(bsz, n, d)
    return x
```

```python
import functools

import numpy as np
import jax
import jax.numpy as jnp
from jax import lax
from jax.experimental import pallas as pl
from jax.experimental.pallas import tpu as pltpu

F32 = jnp.float32
BF16 = jnp.bfloat16

D_MODEL = 1024
GRID_W = 64
FN_W = 256
FN_GROUPS = 4
FN_GD = 64
NA_HD = 64
NA_W = 384
NA_HEADS = 6
NA_KH = 8
NA_KW = 16
NA_GROUP = 4
NA_WIN = NA_GROUP + NA_KH
LOG2E = float(np.log2(np.e))
NA_MASKED = -1e30
GLA_HEADS = 4
GLA_DK = 48
GLA_DV = 96
GLA_QK_W = GLA_HEADS * GLA_DK
GLA_V_W = GLA_HEADS * GLA_DV
GLA_RANK = 16
GLA_GATE_NORM = 16.0
GLA_CHUNK = 64
GLA_BLOCK = 2 * GLA_CHUNK
GLA_UNROLL = 4
ROPE_BASE = 10000.0
D_FF = 2816
EPS = 1e-6
MOD_ROWS = 16

LANES = 128
MXU_W = 256
VMEM_LIMIT_BYTES = 56 * 1024 * 1024

GLA_DK_PAD = 64
GLA_DV_PAD = 128
GLA_QK_PAD = GLA_HEADS * GLA_DK_PAD
GLA_V_PAD = GLA_HEADS * GLA_DV_PAD
GLA_A_PAD = LANES
MIX_PAD = FN_W + NA_W + GLA_V_PAD

C_FX = 0
C_NQ = C_FX + FN_W
C_NK = C_NQ + NA_W
C_NV = C_NK + NA_W
C_GA = C_NV + NA_W
C_GQ = C_GA + GLA_A_PAD
C_GK = C_GQ + GLA_QK_PAD
C_GV = C_GK + GLA_QK_PAD
C_GG = C_GV + GLA_V_PAD
N_IN_PAD = C_GG + GLA_V_PAD

SUB = 256

INPROJ_TM = 1024
POST_TM = 512
CTX_TM = 512
ADA_TN = 1024


def _cparams(sem):
    return pltpu.CompilerParams(dimension_semantics=sem, vmem_limit_bytes=VMEM_LIMIT_BYTES)


def _const_spec(shape):
    nd = len(shape)
    return pl.BlockSpec(tuple(shape), lambda *_: (0,) * nd, pipeline_mode=pl.Buffered(1))


def _layer_spec(shape, l):
    nd = len(shape)
    return pl.BlockSpec((None,) + tuple(shape), lambda *_: (l,) + (0,) * nd, pipeline_mode=pl.Buffered(1))


def _mod_spec(d, l, mod_row):
    if mod_row is None:
        return pl.BlockSpec((None, None, 6, d), lambda b, i: (l, b, 0, 0))
    return pl.BlockSpec((None, None, 6, d), lambda b, i: (l, mod_row, 0, 0))


def _dot(a, b):
    return jnp.dot(a, b, preferred_element_type=F32)


def _dot_nt(a, b):
    return lax.dot_general(a, b, (((1,), (1,)), ((), ())), preferred_element_type=F32)


def _ada_kernel(c_ref, w_ref, b_ref, o_ref):
    a = c_ref[...]
    a = (a * jax.nn.sigmoid(a)).astype(BF16)
    o_ref[...] = _dot(a, w_ref[...].astype(BF16)) + b_ref[...]


def _ada_mod(cc, ada_w, ada_b):
    depth, d, n = ada_w.shape
    rows = cc.shape[0]
    tn = ADA_TN
    return pl.pallas_call(
        _ada_kernel,
        grid=(depth, n // tn),
        in_specs=[
            pl.BlockSpec((rows, d), lambda l, j: (0, 0)),
            pl.BlockSpec((None, d, tn), lambda l, j: (l, 0, j)),
            pl.BlockSpec((None, 1, tn), lambda l, j: (l, 0, j)),
        ],
        out_specs=pl.BlockSpec((None, rows, tn), lambda l, j: (l, 0, j)),
        out_shape=jax.ShapeDtypeStruct((depth, rows, n), F32),
        compiler_params=_cparams(("arbitrary", "arbitrary")),
        name="ada_mod",
    )(cc, ada_w, ada_b.reshape(depth, 1, n))


def _modulated_norm(x, gain, shift, scale):
    ms = jnp.mean(x * x, axis=-1, keepdims=True)
    return (x * lax.rsqrt(ms + EPS) * gain) * (1.0 + scale) + shift


def _log_sigmoid(x):
    return jnp.minimum(x, 0.0) - jnp.log1p(jnp.exp(-jnp.abs(x)))


def _head_mean_sq(z):
    first = lax.broadcasted_iota(jnp.int32, (1, LANES), 1) < NA_HD
    out = []
    for t in range(NA_W // LANES):
        sq = z[:, t * LANES:(t + 1) * LANES]
        sq = sq * sq
        lo = jnp.sum(jnp.where(first, sq, 0.0), axis=-1, keepdims=True)
        hi = jnp.sum(jnp.where(first, 0.0, sq), axis=-1, keepdims=True)
        out.append(jnp.where(first, lo, hi))
    return jnp.concatenate(out, axis=1) * (1.0 / NA_HD)


def _inproj_kernel(x_ref, mod_ref, gain_ref, w_ref, cs_ref, qkg_ref, rope_ref, aw_ref, ab_ref, tri_ref,
                   uv_ref, nqkv_ref, gqk_ref, gvg_ref, gb_ref, *, rope, tm):
    x = x_ref[...]
    h = _modulated_norm(x, gain_ref[...], mod_ref[0:1, :], mod_ref[1:2, :]).astype(BF16)

    zf = _dot(h, w_ref[:, C_FX:C_NQ]).astype(BF16)
    zva = _dot(h, w_ref[:, C_NV:C_GQ])
    nqkv_ref[:, 2 * NA_W:3 * NA_W] = zva[:, :NA_W].astype(BF16)
    a = zva[:, NA_W:].astype(BF16)
    zqk = _dot(h, w_ref[:, C_NQ:C_NV])
    logits = _dot(a, aw_ref[...]) + ab_ref[...]
    zg = _dot(h, w_ref[:, C_GQ:C_GV])
    uv = _dot(zf, cs_ref[...])
    for j in range(uv_ref.shape[0]):
        uv_ref[j] = uv[:, j * LANES:(j + 1) * LANES]
    gvg_ref[...] = _dot(h, w_ref[:, C_GV:N_IN_PAD]).astype(BF16)

    glog = (_log_sigmoid(logits) * (1.0 / GLA_GATE_NORM)).astype(BF16)
    for s in range(tm // SUB):
        rows = slice(s * SUB, (s + 1) * SUB)
        gb_ref[rows, 0:GLA_QK_PAD] = _dot(tri_ref[0], glog[rows, 0:GLA_QK_PAD])
        gb_ref[rows, GLA_QK_PAD:] = _dot(tri_ref[1], glog[rows, GLA_QK_PAD:])

    for i in range(2):
        z = zqk[:, i * NA_W:(i + 1) * NA_W]
        z = z * lax.rsqrt(_head_mean_sq(z) + EPS) * qkg_ref[i:i + 1, :]
        nqkv_ref[:, i * NA_W:(i + 1) * NA_W] = z.astype(BF16)

    for i in range(2):
        z = zg[:, i * GLA_QK_PAD:(i + 1) * GLA_QK_PAD]
        if rope:
            z = (z * rope_ref[0]
                 + pltpu.roll(z, GLA_QK_PAD - GLA_DK // 4, 1) * rope_ref[1]
                 + pltpu.roll(z, GLA_DK // 4, 1) * rope_ref[2])
        if i == 0:
            z = z * (GLA_DK ** -0.5)
        gqk_ref[:, i * GLA_QK_PAD:(i + 1) * GLA_QK_PAD] = z.astype(BF16)


def _inproj(x, l, mod, mod_row, p, consts, rope_tab, *, tm):
    bsz, t, d = x.shape
    assert t % tm == 0 and tm % SUB == 0
    rope = rope_tab is not None
    if not rope:
        rope_tab = jnp.zeros((3, tm, GLA_QK_PAD), F32)
    rope_idx = (lambda b, i: (0, i, 0)) if rope else (lambda b, i: (0, 0, 0))
    row = lambda w_: pl.BlockSpec((None, tm, w_), lambda b, i: (b, i, 0))
    uv_tiles = 2 * FN_W // LANES
    out_w = (3 * NA_W, 2 * GLA_QK_PAD, 2 * GLA_V_PAD, 2 * GLA_QK_PAD)
    out_dt = (BF16, BF16, BF16, F32)
    return pl.pallas_call(
        functools.partial(_inproj_kernel, rope=rope, tm=tm),
        grid=(bsz, t // tm),
        in_specs=[
            row(d),
            _mod_spec(d, l, mod_row),
            _layer_spec((1, d), l),
            _layer_spec((d, N_IN_PAD), l),
            _const_spec(consts["cs"].shape),
            _layer_spec((2, NA_W), l),
            pl.BlockSpec((3, tm, GLA_QK_PAD), rope_idx),
            _layer_spec((GLA_A_PAD, 2 * GLA_QK_PAD), l),
            _layer_spec((1, 2 * GLA_QK_PAD), l),
            _const_spec((2, SUB, SUB)),
        ],
        out_specs=[pl.BlockSpec((None, uv_tiles, tm, LANES), lambda b, i: (b, 0, i, 0))]
        + [row(w_) for w_ in out_w],
        out_shape=[jax.ShapeDtypeStruct((bsz, uv_tiles, t, LANES), F32)]
        + [jax.ShapeDtypeStruct((bsz, t, w_), dt) for w_, dt in zip(out_w, out_dt)],
        compiler_params=_cparams(("arbitrary", "arbitrary")),
        name="inproj_rope" if rope else "inproj_ctx",
    )(x, mod, p["gain_m"], p["w"], consts["cs"], p["qkg"], rope_tab, p["aw"], p["ab"], consts["tri"])


def _fourier_kernel(tab_ref, uv_ref, w_ref, o_ref, *, scale):
    half = uv_ref.shape[1] // 2

    def positions(parity):
        tiles = [uv_ref[j, pl.ds(parity, half, stride=2), :].astype(BF16) for j in range(uv_ref.shape[0])]
        return jnp.concatenate(tiles[:2], axis=1), jnp.concatenate(tiles[2:], axis=1)

    ue, ve = positions(0)
    uo, vo = positions(1)
    even = _dot(tab_ref[0], ue) - _dot(tab_ref[1], ve)
    odd = _dot(tab_ref[2], uo) - _dot(tab_ref[3], vo)
    w = w_ref[...]
    o_ref[0:half, :] = _dot(((even + odd) * scale).astype(BF16), w).astype(BF16)
    o_ref[half:, :] = _dot(((even - odd) * scale).astype(BF16), w).astype(BF16)


def _fourier(uv, l, tabs, fnet_w):
    bsz, tiles, t, _ = uv.shape
    scale = float((t * FN_GD) ** -0.5)
    return pl.pallas_call(
        functools.partial(_fourier_kernel, scale=scale),
        grid=(bsz,),
        in_specs=[
            _const_spec(tabs.shape),
            pl.BlockSpec((None, tiles, t, LANES), lambda b: (b, 0, 0, 0)),
            _layer_spec((FN_W, FN_W), l),
        ],
        out_specs=pl.BlockSpec((None, t, FN_W), lambda b: (b, 0, 0)),
        out_shape=jax.ShapeDtypeStruct((bsz, t, FN_W), BF16),
        compiler_params=_cparams(("arbitrary",)),
        name=f"fourier_{t}",
    )(tabs, uv, fnet_w)


def _na_softmax_t(s):
    return jnp.exp2(s - jnp.max(s, axis=0, keepdims=True)).astype(BF16)


def _na_heads_out_t(o0, o1):
    row = lax.broadcasted_iota(jnp.int32, o0.shape, 0)
    out_t = jnp.where(row < NA_HD, o0 / o0[NA_HD:NA_HD + 1, :], o1 / o1[0:1, :])
    return out_t.T.astype(BF16)


def _na_group_kinds():
    margin = NA_WIN - NA_KH
    return ((0, lambda i: 0), (NA_KH // 2, lambda i: i), (margin + NA_KH - NA_GROUP, lambda i: margin))


def _na_kernel(q_ref, k_ref, v_ref, qc_ref, kc_ref, vc_ref, rpb_ref, o_ref, oc_ref,
               vt_ref, sc_ref, bias_ref, *, n_rows, need_ctx):
    lane = lax.broadcasted_iota(jnp.int32, (1, LANES), 1)
    first = lane < NA_HD
    zero = jnp.zeros((), BF16)
    one = jnp.ones((), BF16)
    v = v_ref[...]
    vt_ref[0] = jnp.where(first, v, one).T
    vt_ref[1] = jnp.where(first, one, v).T
    kc = kc_ref[...]
    vc = vc_ref[...]
    vct = (jnp.where(first, vc, one).T, jnp.where(first, one, vc).T)

    def both_heads(q):
        return jnp.concatenate([jnp.where(first, q, zero), jnp.where(first, zero, q)], axis=0)

    n_groups = n_rows // NA_GROUP
    gq = NA_GROUP * GRID_W
    n_loc = NA_WIN * GRID_W
    margin = NA_WIN - NA_KH

    @pl.when(pl.program_id(1) == 0)
    def _():
        wrow = lax.broadcasted_iota(jnp.int32, (n_loc, GRID_W), 0)
        for hh in range(2):
            for kind, (q_off, a_lo_of) in enumerate(_na_group_kinds()):
                for j in range(NA_GROUP // 2):
                    halves = []
                    for i in (2 * j, 2 * j + 1):
                        start = (NA_KH - 1 - q_off - i + margin) * GRID_W
                        lo = a_lo_of(i) * GRID_W
                        band = (wrow >= lo) & (wrow < lo + NA_KH * GRID_W)
                        halves.append(jnp.where(band, rpb_ref[hh, start:start + n_loc, :], NA_MASKED))
                    bias_ref[hh, kind, :, j * LANES:(j + 1) * LANES] = jnp.concatenate(halves, axis=1)

    def window(g):
        ws = jnp.clip(g * NA_GROUP - NA_KH // 2, 0, n_rows - NA_WIN)
        return (pl.ds(pl.multiple_of(g * gq, gq), gq), pl.ds(pl.multiple_of(ws * GRID_W, gq), n_loc))

    def scores(g, slot):
        qs, ks = window(g)
        kind = jnp.where(g == 0, 0, jnp.where(g == n_groups - 1, 2, 1))
        q = both_heads(q_ref[qs, :])
        s_loc = _dot_nt(k_ref[ks, :], q)
        s_ctx = _dot_nt(kc, q)
        for hh in range(2):
            cols = slice(hh * gq, (hh + 1) * gq)
            sc_ref[slot, 0:n_loc, cols] = s_loc[:, cols] + bias_ref[hh, kind]
        sc_ref[slot, n_loc:, :] = s_ctx

    def finish(g, slot):
        qs, ks = window(g)
        p = _na_softmax_t(sc_ref[slot])
        outs = [_dot(jnp.concatenate([vt_ref[hh, :, ks], vct[hh]], axis=1), p[:, hh * gq:(hh + 1) * gq])
                for hh in range(2)]
        o_ref[qs, :] = _na_heads_out_t(outs[0], outs[1])

    scores(0, 0)

    def pair(i, carry):
        scores(2 * i + 1, 1)
        finish(2 * i, 0)
        scores(2 * i + 2, 0)
        finish(2 * i + 1, 1)
        return carry

    lax.fori_loop(0, n_groups // 2 - 1, pair, 0)
    scores(n_groups - 1, 1)
    finish(n_groups - 2, 0)
    finish(n_groups - 1, 1)

    if need_ctx:
        n_ctx = qc_ref.shape[0]
        p = _na_softmax_t(_dot_nt(kc, both_heads(qc_ref[...])))
        outs = [_dot(vct[hh], p[:, hh * n_ctx:(hh + 1) * n_ctx]) for hh in range(2)]
        oc_ref[...] = _na_heads_out_t(outs[0], outs[1])
    else:
        oc_ref[...] = jnp.zeros(oc_ref.shape, oc_ref.dtype)


def _na(nqkv, nqkv_c, l, rpb_rows, *, need_ctx):
    bsz, t, _ = nqkv.shape
    n = nqkv_c.shape[1]
    n_rows = t // GRID_W
    assert n_rows % (2 * NA_GROUP) == 0 and n_rows >= NA_WIN + NA_GROUP
    pairs = NA_HEADS // 2
    n_loc = NA_WIN * GRID_W
    gq = NA_GROUP * GRID_W
    nb = NA_W // LANES
    lat = lambda off: pl.BlockSpec((None, t, LANES), lambda j, b: (b, 0, off + j))
    ctx = lambda off: pl.BlockSpec((None, n, LANES), lambda j, b: (b, 0, off + j))
    return pl.pallas_call(
        functools.partial(_na_kernel, n_rows=n_rows, need_ctx=need_ctx),
        grid=(pairs, bsz),
        in_specs=[lat(0), lat(nb), lat(2 * nb), ctx(0), ctx(nb), ctx(2 * nb),
                  pl.BlockSpec((None, 2) + rpb_rows.shape[2:], lambda j, b: (l, j, 0, 0))],
        out_specs=[pl.BlockSpec((None, t, LANES), lambda j, b: (b, 0, j)),
                   pl.BlockSpec((None, n, LANES), lambda j, b: (b, 0, j))],
        out_shape=[jax.ShapeDtypeStruct((bsz, t, NA_W), BF16),
                   jax.ShapeDtypeStruct((bsz, n, NA_W), BF16)],
        scratch_shapes=[pltpu.VMEM((2, LANES, t), BF16),
                        pltpu.VMEM((2, n_loc + n, 2 * gq), F32),
                        pltpu.VMEM((2, len(_na_group_kinds()), n_loc, gq), F32)],
        compiler_params=_cparams(("arbitrary", "arbitrary")),
        name="na_attn",
    )(nqkv, nqkv, nqkv, nqkv_c, nqkv_c, nqkv_c, rpb_rows)


def _gla_steps(jobs, st_ref, causal_ref, vmask_ref):
    c = GLA_CHUNK
    head = lax.broadcasted_iota(jnp.int32, (1, GLA_QK_PAD), 1) // GLA_DK_PAD
    prep = []
    for qk_ref, vg_ref, b_ref, out_ref, ci, d in jobs:
        rows = pl.ds(pl.multiple_of(ci * GLA_BLOCK, GLA_BLOCK), GLA_BLOCK)
        k = qk_ref[rows, GLA_QK_PAD:].astype(F32)
        v = vg_ref[rows, 0:GLA_V_PAD]
        b = b_ref[rows, d * GLA_QK_PAD:(d + 1) * GLA_QK_PAD]
        lo, hi = b[0:c, :], b[c:, :]
        if d == 0:
            tot1, tot2 = lo[c - 1:c, :], hi[c - 1:c, :]
            b_rel = jnp.concatenate([lo - tot1, hi], axis=0)
        else:
            tot1, tot2 = hi[0:1, :], lo[0:1, :]
            b_rel = jnp.concatenate([lo, hi - tot1], axis=0)
        k_in = k * jnp.exp(-b_rel)
        k_end = k_in * jnp.exp(tot2)
        q_in = q_st = None
        if out_ref is not None:
            q_rel = qk_ref[rows, 0:GLA_QK_PAD].astype(F32) * jnp.exp(b_rel)
            q_in = q_rel.astype(BF16)
            q_st = (q_rel * jnp.exp(tot1)).astype(BF16)
        decay = jnp.exp(jnp.broadcast_to(tot1 + tot2, (GLA_DV_PAD, GLA_QK_PAD)).T)
        prep.append((rows, q_in, q_st, k_in, k_end, v, decay))

    att = [None if q_in is None else
           _dot_nt(q_in, jnp.concatenate([jnp.where(head == h, k_in, 0.0) for h in range(GLA_HEADS)],
                                         axis=0).astype(BF16))
           for (rows, q_in, q_st, k_in, k_end, v, decay) in prep]

    kv = []
    for (rows, q_in, q_st, k_in, k_end, v, decay) in prep:
        k_t = k_end.astype(BF16).T
        kv.append(jnp.concatenate(
            [_dot(k_t[h * GLA_DK_PAD:(h + 1) * GLA_DK_PAD, :], v[:, h * GLA_DV_PAD:(h + 1) * GLA_DV_PAD])
             for h in range(GLA_HEADS)], axis=0))

    inter = []
    zero = jnp.zeros((GLA_DK_PAD, GLA_DV_PAD), BF16)
    for j, ((qk_ref, vg_ref, b_ref, out_ref, ci, d), (rows, q_in, q_st, k_in, k_end, v, decay)) in enumerate(
            zip(jobs, prep)):
        st = st_ref[d]
        if q_st is None:
            inter.append(None)
        else:
            sb = st.astype(BF16)
            s_bd = jnp.concatenate(
                [jnp.concatenate([sb[h * GLA_DK_PAD:(h + 1) * GLA_DK_PAD, :] if h2 == h else zero
                                  for h2 in range(GLA_HEADS)], axis=1) for h in range(GLA_HEADS)], axis=0)
            inter.append(_dot(q_st, s_bd))
        st_ref[d] = st * decay + kv[j]

    for j, ((qk_ref, vg_ref, b_ref, out_ref, ci, d), (rows, q_in, q_st, k_in, k_end, v, decay)) in enumerate(
            zip(jobs, prep)):
        if out_ref is not None:
            a = (att[j] * causal_ref[d]).astype(BF16)
            intra = []
            for pair in range(GLA_HEADS // 2):
                cols = slice(pair * 2 * GLA_DV_PAD, (pair + 1) * 2 * GLA_DV_PAD)
                v_pair = jnp.concatenate([v[:, cols]] * 2, axis=0) * vmask_ref[...]
                intra.append(_dot(a[:, pair * 2 * GLA_BLOCK:(pair + 1) * 2 * GLA_BLOCK], v_pair))
            out_ref[rows, :] = (inter[j] + jnp.concatenate(intra, axis=1)).astype(out_ref.dtype)


def _gla_kernel(qk_ref, vg_ref, b_ref, qkc_ref, vgc_ref, bc_ref, causal_ref, vmask_ref, *rest, t, n, need_ctx):
    if need_ctx:
        of_ref, ob_ref, ocf_ref, ocb_ref, st_ref = rest
    else:
        of_ref, ob_ref, st_ref = rest
    st_ref[...] = jnp.zeros(st_ref.shape, F32)
    nb_ctx = n // GLA_BLOCK
    nb_lat = t // GLA_BLOCK

    def jobs(refs, outs, n_blocks, i):
        return [refs + (outs[0], i, 0), refs + (outs[1], n_blocks - 1 - i, 1)]

    ctx_outs = (ocf_ref, ocb_ref) if need_ctx else (None, None)
    for i in range(nb_ctx):
        _gla_steps(jobs((qkc_ref, vgc_ref, bc_ref), ctx_outs, nb_ctx, i), st_ref, causal_ref, vmask_ref)

    def step(i, carry):
        _gla_steps(jobs((qk_ref, vg_ref, b_ref), (of_ref, ob_ref), nb_lat, i), st_ref, causal_ref, vmask_ref)
        return carry

    lax.fori_loop(0, nb_lat, step, 0, unroll=GLA_UNROLL)


def _gla(gqk, gvg, gb, gqk_c, gvg_c, gb_c, causal, vmask, *, need_ctx):
    bsz, t, _ = gqk.shape
    n = gqk_c.shape[1]
    assert t % GLA_BLOCK == 0 and n % GLA_BLOCK == 0
    row = lambda rows, w_: pl.BlockSpec((None, rows, w_), lambda b: (b, 0, 0))
    out_rows = (t, t, n, n) if need_ctx else (t, t)
    return pl.pallas_call(
        functools.partial(_gla_kernel, t=t, n=n, need_ctx=need_ctx),
        grid=(bsz,),
        in_specs=[row(t, 2 * GLA_QK_PAD), row(t, GLA_V_PAD), row(t, 2 * GLA_QK_PAD),
                  row(n, 2 * GLA_QK_PAD), row(n, GLA_V_PAD), row(n, 2 * GLA_QK_PAD),
                  _const_spec(causal.shape), _const_spec(vmask.shape)],
        out_specs=[row(r, GLA_V_PAD) for r in out_rows],
        out_shape=[jax.ShapeDtypeStruct((bsz, r, GLA_V_PAD), BF16) for r in out_rows],
        scratch_shapes=[pltpu.VMEM((2, GLA_QK_PAD, GLA_DV_PAD), F32)],
        compiler_params=_cparams(("arbitrary",)),
        name="gla_scan",
    )(gqk, gvg, gb, gqk_c, gvg_c, gb_c, causal, vmask)


def _gla_out(of_ref, ob_ref, g_ref, gain_ref):
    heads = []
    for h in range(GLA_HEADS):
        cols = slice(h * GLA_DV_PAD, (h + 1) * GLA_DV_PAD)
        o = of_ref[:, cols].astype(F32) + ob_ref[:, cols].astype(F32)
        ms = jnp.sum(o * o, axis=-1, keepdims=True) * (1.0 / GLA_DV)
        g = g_ref[:, cols].astype(F32)
        heads.append((o * lax.rsqrt(ms + EPS) * gain_ref[:, cols] * (g * jax.nn.sigmoid(g))).astype(BF16))
    return jnp.concatenate(heads, axis=1)


def _post_kernel(x_ref, yfn_ref, yna_ref, of_ref, ob_ref, g_ref, mod_ref, gain_o_ref, gain_ref,
                 wo_ref, w1_ref, w3_ref, w2_ref, o_ref):
    y = (_dot(yfn_ref[...], wo_ref[0:FN_W, :])
         + _dot(yna_ref[...], wo_ref[FN_W:FN_W + NA_W, :])
         + _dot(_gla_out(of_ref, ob_ref, g_ref, gain_o_ref), wo_ref[FN_W + NA_W:, :]))
    x1 = x_ref[...] + mod_ref[2:3, :] * y
    h = _modulated_norm(x1, gain_ref[...], mod_ref[3:4, :], mod_ref[4:5, :]).astype(BF16)
    acts = []
    for c in range(D_FF // MXU_W):
        cols = slice(c * MXU_W, (c + 1) * MXU_W)
        u = _dot(h, w1_ref[:, cols])
        g = _dot(h, w3_ref[:, cols])
        acts.append((u * jax.nn.sigmoid(u) * g).astype(BF16))
    acc = _dot(jnp.concatenate(acts, axis=1), w2_ref[...])
    o_ref[...] = x1 + mod_ref[5:6, :] * acc


def _post(x, yfn, yna, o_fwd, o_bwd, gvg, l, mod, mod_row, p, *, tm):
    bsz, t, d = x.shape
    assert t % tm == 0
    row = lambda w_: pl.BlockSpec((None, tm, w_), lambda b, i: (b, i, 0))
    gate = pl.BlockSpec((None, tm, GLA_V_PAD), lambda b, i: (b, i, 1))
    return pl.pallas_call(
        _post_kernel,
        grid=(bsz, t // tm),
        in_specs=[row(d), row(FN_W), row(NA_W), row(GLA_V_PAD), row(GLA_V_PAD), gate,
                  _mod_spec(d, l, mod_row), _layer_spec((1, GLA_V_PAD), l),
                  _layer_spec((1, d), l), _layer_spec((MIX_PAD, d), l), _layer_spec((d, D_FF), l),
                  _layer_spec((d, D_FF), l), _layer_spec((D_FF, d), l)],
        out_specs=row(d),
        out_shape=jax.ShapeDtypeStruct((bsz, t, d), F32),
        compiler_params=_cparams(("arbitrary", "arbitrary")),
        name=f"post_{t}",
    )(x, yfn, yna, o_fwd, o_bwd, gvg, mod, p["gain_o"], p["gain_f"], p["wo"], p["w1"], p["w3"], p["w2"])


def _dft_tables(n):
    idx = (np.arange(n)[:, None] * np.arange(n)[None, :]) % n
    ang = 2.0 * np.pi * idx / n
    return np.cos(ang), np.sin(ang)


def _shape_consts(t, n):
    c64, s64 = _dft_tables(FN_GD)
    eye = np.eye(FN_GROUPS)
    cs = np.concatenate([np.kron(eye, c64), np.kron(eye, s64)], axis=1)
    chunk_id = np.arange(SUB) // GLA_CHUNK
    same = chunk_id[:, None] == chunk_id[None, :]
    pos = np.arange(SUB)
    tri = np.stack([same & (pos[None, :] <= pos[:, None]), same & (pos[None, :] >= pos[:, None])])
    cpos = np.arange(GLA_BLOCK)
    fwd = cpos[None, :] <= cpos[:, None]
    gla_causal = np.stack([np.tile(fwd, (1, GLA_HEADS)), np.tile(fwd.T, (1, GLA_HEADS))])
    gla_vmask = ((np.arange(2 * GLA_BLOCK)[:, None] // GLA_BLOCK)
                 == (np.arange(2 * GLA_DV_PAD)[None, :] // GLA_DV_PAD))

    m = GLA_DK // 4
    inv = ROPE_BASE ** (-np.arange(m) / m)
    tok = np.arange(t)
    cos_h = np.ones((t, GLA_DK_PAD))
    sin_lo = np.zeros((t, GLA_DK_PAD))
    sin_hi = np.zeros((t, GLA_DK_PAD))
    for blk, p in enumerate((tok // GRID_W, tok % GRID_W)):
        ang = p[:, None] * inv[None, :]
        o = blk * 2 * m
        cos_h[:, o:o + m] = np.cos(ang)
        cos_h[:, o + m:o + 2 * m] = np.cos(ang)
        sin_lo[:, o:o + m] = -np.sin(ang)
        sin_hi[:, o + m:o + 2 * m] = np.sin(ang)
    rope = np.stack([np.tile(a, (1, GLA_HEADS)) for a in (cos_h, sin_lo, sin_hi)])

    def position_tables(length):
        cos, sin = _dft_tables(length)
        half = length // 2
        return np.stack([cos[:half, 0::2], sin[:half, 0::2], cos[:half, 1::2], sin[:half, 1::2]])

    to_bf16 = lambda a: jnp.asarray(a, F32).astype(BF16)
    return {
        "cs": to_bf16(cs), "tri": jnp.asarray(tri, BF16),
        "gla_causal": jnp.asarray(gla_causal, F32), "gla_vmask": jnp.asarray(gla_vmask, BF16),
        "rope": jnp.asarray(rope, F32),
        "dft_lat": to_bf16(position_tables(t)), "dft_ctx": to_bf16(position_tables(n)),
    }


def _bias_kernel(rpb_ref, onehot_ref, ok_ref, o_ref):
    r = rpb_ref[...]
    onehot = onehot_ref[...]
    t = jnp.zeros(o_ref.shape, F32)
    for _ in range(3):
        piece = r.astype(BF16)
        t = t + _dot(piece, onehot)
        r = r - piece.astype(F32)
    o_ref[...] = jnp.where(ok_ref[...] > 0.0, t * LOG2E, NA_MASKED)


def _window_bias_rows(na_rpb):
    depth, heads, n_dr, n_dc = na_rpb.shape
    cq = np.arange(GRID_W)
    c0 = np.clip(cq - NA_KW // 2, 0, GRID_W - NA_KW)
    col_ok = (cq[None, :] >= c0[:, None]) & (cq[None, :] < c0[:, None] + NA_KW)
    dc = np.clip(cq[None, :] - cq[:, None], -(NA_KW - 1), NA_KW - 1) + NA_KW - 1
    n_dc_pad = n_dc + 1
    n_tab = depth * heads * n_dr
    rows_pad = -(-n_tab // 8) * 8
    onehot = (np.arange(n_dc_pad)[:, None, None] == dc[None]).reshape(n_dc_pad, GRID_W * GRID_W)
    rpb2 = jnp.pad(na_rpb.reshape(n_tab, n_dc), ((0, rows_pad - n_tab), (0, 1)))
    t = pl.pallas_call(
        _bias_kernel,
        out_shape=jax.ShapeDtypeStruct((rows_pad, GRID_W * GRID_W), F32),
        name="na_bias",
    )(rpb2, jnp.asarray(onehot, BF16), jnp.asarray(col_ok.reshape(1, -1), F32))
    t = t[:n_tab].reshape(depth * heads, n_dr, GRID_W, GRID_W)
    t = t.transpose(0, 1, 3, 2).reshape(depth * heads, n_dr * GRID_W, GRID_W)
    margin = NA_WIN - NA_KH
    t = jnp.pad(t, ((0, 0), (margin * GRID_W, margin * GRID_W), (0, 0)))
    return t.reshape(depth, heads, (n_dr + 2 * margin) * GRID_W, GRID_W)


def _pad_heads(w, heads, dim, pad, axis=-1):
    axis = axis % w.ndim
    shape = w.shape
    w = w.reshape(shape[:axis] + (heads, dim) + shape[axis + 1:])
    widths = [(0, 0)] * w.ndim
    widths[axis + 1] = (0, pad - dim)
    return jnp.pad(w, widths).reshape(shape[:axis] + (heads * pad,) + shape[axis + 1:])


def _params(norm_mix, norm_ffn, w_in, fnet_w, na_q_norm, na_k_norm, na_rpb, alpha_w, alpha_b, o_norm, w_out,
            ffn_w1, ffn_w3, ffn_w2):
    depth = w_in.shape[0]
    o = FN_W + 3 * NA_W
    gq = w_in[..., o:o + GLA_QK_W]
    gk = w_in[..., o + GLA_QK_W:o + 2 * GLA_QK_W]
    o2 = o + 2 * GLA_QK_W
    gv = w_in[..., o2:o2 + GLA_V_W]
    gg = w_in[..., o2 + GLA_V_W:o2 + 2 * GLA_V_W]
    ga = w_in[..., o2 + 2 * GLA_V_W:]
    w = jnp.concatenate([
        w_in[..., :o],
        jnp.pad(ga, ((0, 0), (0, 0), (0, GLA_A_PAD - 2 * GLA_RANK))),
        _pad_heads(gq, GLA_HEADS, GLA_DK, GLA_DK_PAD), _pad_heads(gk, GLA_HEADS, GLA_DK, GLA_DK_PAD),
        _pad_heads(gv, GLA_HEADS, GLA_DV, GLA_DV_PAD), _pad_heads(gg, GLA_HEADS, GLA_DV, GLA_DV_PAD),
    ], axis=-1).astype(BF16)

    qkg = jnp.stack([jnp.tile(na_q_norm, (1, NA_HEADS)) * (NA_HD ** -0.5 * LOG2E),
                     jnp.tile(na_k_norm, (1, NA_HEADS))], axis=1)

    aw_pad = _pad_heads(alpha_w, GLA_HEADS, GLA_DK, GLA_DK_PAD)
    aw = jnp.concatenate([jnp.pad(aw_pad[:, 0], ((0, 0), (0, 0), (0, GLA_QK_PAD))),
                          jnp.pad(aw_pad[:, 1], ((0, 0), (0, 0), (GLA_QK_PAD, 0)))], axis=1)
    aw = jnp.pad(aw, ((0, 0), (0, GLA_A_PAD - 2 * GLA_RANK), (0, 0))).astype(BF16)
    ab = _pad_heads(alpha_b, GLA_HEADS, GLA_DK, GLA_DK_PAD).reshape(depth, 1, 2 * GLA_QK_PAD)

    gain_o = _pad_heads(jnp.tile(o_norm, (1, GLA_HEADS)), GLA_HEADS, GLA_DV, GLA_DV_PAD)[:, None, :]
    wo = jnp.concatenate([w_out[:, :FN_W + NA_W],
                          _pad_heads(w_out[:, FN_W + NA_W:], GLA_HEADS, GLA_DV, GLA_DV_PAD, axis=1)],
                         axis=1).astype(BF16)
    return {
        "gain_m": norm_mix[:, None, :], "gain_f": norm_ffn[:, None, :], "w": w, "qkg": qkg, "aw": aw, "ab": ab,
        "rpb_rows": _window_bias_rows(na_rpb), "gain_o": gain_o, "wo": wo, "fw": fnet_w.astype(BF16),
        "w1": ffn_w1.astype(BF16), "w3": ffn_w3.astype(BF16), "w2": ffn_w2.astype(BF16),
    }


def kernel(x, c, ctx, c_ctx, ada_w, ada_b, norm_mix, norm_ffn, w_in, fnet_w, na_q_norm, na_k_norm, na_rpb,
           gla_alpha_w, gla_alpha_b, gla_o_norm, w_out, ffn_w1, ffn_w3, ffn_w2):
    bsz, t, d = x.shape
    n = ctx.shape[1]
    depth = ada_w.shape[0]
    ctx_row = bsz
    assert bsz < MOD_ROWS and n % SUB == 0 and (bsz * n) % CTX_TM == 0
    consts = _shape_consts(t, n)
    p = _params(norm_mix, norm_ffn, w_in, fnet_w, na_q_norm, na_k_norm, na_rpb, gla_alpha_w, gla_alpha_b,
                gla_o_norm, w_out, ffn_w1, ffn_w3, ffn_w2)

    cc = jnp.concatenate([c, c_ctx[None, :], jnp.zeros((MOD_ROWS - bsz - 1, d), F32)])
    mod = _ada_mod(cc, ada_w, ada_b).reshape(depth, MOD_ROWS, 6, d)

    cx = ctx
    flat = lambda a: a.reshape(1, bsz * n, a.shape[-1])
    for l in range(depth):
        need_ctx = l < depth - 1
        uv, nqkv, gqk, gvg, gb = _inproj(x, l, mod, None, p, consts, consts["rope"], tm=INPROJ_TM)
        uv_c, nqkv_c, gqk_c, gvg_c, gb_c = _inproj(cx, l, mod, ctx_row, p, consts, None, tm=n)

        y_fn = _fourier(uv, l, consts["dft_lat"], p["fw"])
        y_na, y_na_c = _na(nqkv, nqkv_c, l, p["rpb_rows"], need_ctx=need_ctx)
        o_gla = _gla(gqk, gvg, gb, gqk_c, gvg_c, gb_c, consts["gla_causal"], consts["gla_vmask"],
                     need_ctx=need_ctx)
        x = _post(x, y_fn, y_na, o_gla[0], o_gla[1], gvg, l, mod, None, p, tm=POST_TM)
        if need_ctx:
            y_fn_c = _fourier(uv_c, l, consts["dft_ctx"], p["fw"])
            cx = _post(flat(cx), flat(y_fn_c), flat(y_na_c), flat(o_gla[2]), flat(o_gla[3]), flat(gvg_c),
                       l, mod, ctx_row, p, tm=CTX_TM).reshape(bsz, n, d)
    return x
```

```python
import functools

import numpy as np
import jax
import jax.numpy as jnp
from jax import lax
from jax.experimental import pallas as pl
from jax.experimental.pallas import tpu as pltpu

F32 = jnp.float32
BF16 = jnp.bfloat16

D_MODEL = 1024
GRID_W = 64
FN_W = 256
FN_GROUPS = 4
FN_GD = 64
NA_HD = 64
NA_W = 384
NA_HEADS = 6
NA_KH = 8
NA_KW = 16
NA_GROUP = 4
NA_WIN = NA_GROUP + NA_KH
LOG2E = float(np.log2(np.e))
NA_MASKED = -1e30
GLA_HEADS = 4
GLA_DK = 48
GLA_DV = 96
GLA_QK_W = GLA_HEADS * GLA_DK
GLA_V_W = GLA_HEADS * GLA_DV
GLA_RANK = 16
GLA_GATE_NORM = 16.0
GLA_CHUNK = 64
GLA_BLOCK = 2 * GLA_CHUNK
GLA_UNROLL = 4
ROPE_BASE = 10000.0
D_FF = 2816
EPS = 1e-6
MOD_ROWS = 16

LANES = 128
MXU_W = 256
VMEM_LIMIT_BYTES = 56 * 1024 * 1024

GLA_DK_PAD = 64
GLA_DV_PAD = 128
GLA_QK_PAD = GLA_HEADS * GLA_DK_PAD
GLA_V_PAD = GLA_HEADS * GLA_DV_PAD
GLA_A_PAD = LANES
MIX_PAD = FN_W + NA_W + GLA_V_PAD

C_FX = 0
C_NQ = C_FX + FN_W
C_NK = C_NQ + NA_W
C_NV = C_NK + NA_W
C_GA = C_NV + NA_W
C_GQ = C_GA + GLA_A_PAD
C_GK = C_GQ + GLA_QK_PAD
C_GV = C_GK + GLA_QK_PAD
C_GG = C_GV + GLA_V_PAD
N_IN_PAD = C_GG + GLA_V_PAD

SUB = 256

INPROJ_TM = 1024
POST_TM = 1024
POST_SUB = 512
ADA_TN = 1024


def _cparams(sem):
    return pltpu.CompilerParams(dimension_semantics=sem, vmem_limit_bytes=VMEM_LIMIT_BYTES)


def _const_spec(shape):
    nd = len(shape)
    return pl.BlockSpec(tuple(shape), lambda *_: (0,) * nd, pipeline_mode=pl.Buffered(1))


def _layer_spec(shape, l):
    nd = len(shape)
    return pl.BlockSpec((None,) + tuple(shape), lambda *_: (l,) + (0,) * nd, pipeline_mode=pl.Buffered(1))


def _mod_spec(d, l, mod_row):
    if mod_row is None:
        return pl.BlockSpec((None, None, 6, d), lambda b, i: (l, b, 0, 0))
    return pl.BlockSpec((None, None, 6, d), lambda b, i: (l, mod_row, 0, 0))


def _dot(a, b):
    return jnp.dot(a, b, preferred_element_type=F32)


def _dot_nt(a, b):
    return lax.dot_general(a, b, (((1,), (1,)), ((), ())), preferred_element_type=F32)


def _ada_kernel(c_ref, w_ref, b_ref, o_ref):
    a = c_ref[...]
    a = (a * jax.nn.sigmoid(a)).astype(BF16)
    o_ref[...] = _dot(a, w_ref[...].astype(BF16)) + b_ref[...]


def _ada_mod(cc, ada_w, ada_b):
    depth, d, n = ada_w.shape
    rows = cc.shape[0]
    tn = ADA_TN
    return pl.pallas_call(
        _ada_kernel,
        grid=(depth, n // tn),
        in_specs=[
            pl.BlockSpec((rows, d), lambda l, j: (0, 0)),
            pl.BlockSpec((None, d, tn), lambda l, j: (l, 0, j)),
            pl.BlockSpec((None, 1, tn), lambda l, j: (l, 0, j)),
        ],
        out_specs=pl.BlockSpec((None, rows, tn), lambda l, j: (l, 0, j)),
        out_shape=jax.ShapeDtypeStruct((depth, rows, n), F32),
        compiler_params=_cparams(("arbitrary", "arbitrary")),
        name="ada_mod",
    )(cc, ada_w, ada_b.reshape(depth, 1, n))


def _modulated_norm(x, gain, shift, scale):
    ms = jnp.mean(x * x, axis=-1, keepdims=True)
    return (x * lax.rsqrt(ms + EPS) * gain) * (1.0 + scale) + shift


def _log_sigmoid(x):
    return jnp.minimum(x, 0.0) - jnp.log1p(jnp.exp(-jnp.abs(x)))


def _head_mean_sq(z):
    first = lax.broadcasted_iota(jnp.int32, (1, LANES), 1) < NA_HD
    out = []
    for t in range(NA_W // LANES):
        sq = z[:, t * LANES:(t + 1) * LANES]
        sq = sq * sq
        lo = jnp.sum(jnp.where(first, sq, 0.0), axis=-1, keepdims=True)
        hi = jnp.sum(jnp.where(first, 0.0, sq), axis=-1, keepdims=True)
        out.append(jnp.where(first, lo, hi))
    return jnp.concatenate(out, axis=1) * (1.0 / NA_HD)


def _inproj_kernel(x_ref, mod_ref, gain_ref, w_ref, cs_ref, qkg_ref, rope_ref, aw_ref, ab_ref, tri_ref,
                   uv_ref, nqkv_ref, gqk_ref, gvg_ref, gb_ref, *, rope, tm):
    x = x_ref[...]
    h = _modulated_norm(x, gain_ref[...], mod_ref[0:1, :], mod_ref[1:2, :]).astype(BF16)

    zf = _dot(h, w_ref[:, C_FX:C_NQ]).astype(BF16)
    zva = _dot(h, w_ref[:, C_NV:C_GQ])
    nqkv_ref[:, 2 * NA_W:3 * NA_W] = zva[:, :NA_W].astype(BF16)
    a = zva[:, NA_W:].astype(BF16)
    zqk = _dot(h, w_ref[:, C_NQ:C_NV])
    logits = _dot(a, aw_ref[...]) + ab_ref[...]
    zg = _dot(h, w_ref[:, C_GQ:C_GV])
    uv = _dot(zf, cs_ref[...])
    for j in range(uv_ref.shape[0]):
        uv_ref[j] = uv[:, j * LANES:(j + 1) * LANES]
    gvg_ref[...] = _dot(h, w_ref[:, C_GV:N_IN_PAD]).astype(BF16)

    glog = (_log_sigmoid(logits) * (1.0 / GLA_GATE_NORM)).astype(BF16)
    for s in range(tm // SUB):
        rows = slice(s * SUB, (s + 1) * SUB)
        gb_ref[rows, 0:GLA_QK_PAD] = _dot(tri_ref[0], glog[rows, 0:GLA_QK_PAD])
        gb_ref[rows, GLA_QK_PAD:] = _dot(tri_ref[1], glog[rows, GLA_QK_PAD:])

    for i in range(2):
        z = zqk[:, i * NA_W:(i + 1) * NA_W]
        z = z * lax.rsqrt(_head_mean_sq(z) + EPS) * qkg_ref[i:i + 1, :]
        nqkv_ref[:, i * NA_W:(i + 1) * NA_W] = z.astype(BF16)

    for i in range(2):
        z = zg[:, i * GLA_QK_PAD:(i + 1) * GLA_QK_PAD]
        if rope:
            z = (z * rope_ref[0]
                 + pltpu.roll(z, GLA_QK_PAD - GLA_DK // 4, 1) * rope_ref[1]
                 + pltpu.roll(z, GLA_DK // 4, 1) * rope_ref[2])
        if i == 0:
            z = z * (GLA_DK ** -0.5)
        gqk_ref[:, i * GLA_QK_PAD:(i + 1) * GLA_QK_PAD] = z.astype(BF16)


def _inproj(x, l, mod, mod_row, p, consts, rope_tab, *, tm):
    bsz, t, d = x.shape
    assert t % tm == 0 and tm % SUB == 0
    rope = rope_tab is not None
    if not rope:
        rope_tab = jnp.zeros((3, tm, GLA_QK_PAD), F32)
    rope_idx = (lambda b, i: (0, i, 0)) if rope else (lambda b, i: (0, 0, 0))
    row = lambda w_: pl.BlockSpec((None, tm, w_), lambda b, i: (b, i, 0))
    uv_tiles = 2 * FN_W // LANES
    out_w = (3 * NA_W, 2 * GLA_QK_PAD, 2 * GLA_V_PAD, 2 * GLA_QK_PAD)
    out_dt = (BF16, BF16, BF16, F32)
    return pl.pallas_call(
        functools.partial(_inproj_kernel, rope=rope, tm=tm),
        grid=(bsz, t // tm),
        in_specs=[
            row(d),
            _mod_spec(d, l, mod_row),
            _layer_spec((1, d), l),
            _layer_spec((d, N_IN_PAD), l),
            _const_spec(consts["cs"].shape),
            _layer_spec((2, NA_W), l),
            pl.BlockSpec((3, tm, GLA_QK_PAD), rope_idx),
            _layer_spec((GLA_A_PAD, 2 * GLA_QK_PAD), l),
            _layer_spec((1, 2 * GLA_QK_PAD), l),
            _const_spec((2, SUB, SUB)),
        ],
        out_specs=[pl.BlockSpec((None, uv_tiles, tm, LANES), lambda b, i: (b, 0, i, 0))]
        + [row(w_) for w_ in out_w],
        out_shape=[jax.ShapeDtypeStruct((bsz, uv_tiles, t, LANES), F32)]
        + [jax.ShapeDtypeStruct((bsz, t, w_), dt) for w_, dt in zip(out_w, out_dt)],
        compiler_params=_cparams(("arbitrary", "arbitrary")),
        name="inproj_rope" if rope else "inproj_ctx",
    )(x, mod, p["gain_m"], p["w"], consts["cs"], p["qkg"], rope_tab, p["aw"], p["ab"], consts["tri"])


def _fourier_kernel(tab_ref, uv_ref, w_ref, o_ref, *, scale):
    half = uv_ref.shape[1] // 2

    def positions(parity):
        tiles = [uv_ref[j, pl.ds(parity, half, stride=2), :].astype(BF16) for j in range(uv_ref.shape[0])]
        return jnp.concatenate(tiles[:2], axis=1), jnp.concatenate(tiles[2:], axis=1)

    ue, ve = positions(0)
    uo, vo = positions(1)
    even = _dot(tab_ref[0], ue) - _dot(tab_ref[1], ve)
    odd = _dot(tab_ref[2], uo) - _dot(tab_ref[3], vo)
    w = w_ref[...]
    o_ref[0:half, :] = _dot(((even + odd) * scale).astype(BF16), w).astype(BF16)
    o_ref[half:, :] = _dot(((even - odd) * scale).astype(BF16), w).astype(BF16)


def _fourier(uv, l, tabs, fnet_w, bsz):
    tiles = uv.shape[1]
    t = 2 * tabs.shape[1]
    assert uv.shape[0] * uv.shape[2] == bsz * t
    uv_idx = (lambda b: (b, 0, 0, 0)) if uv.shape[0] == bsz else (lambda b: (0, 0, b, 0))
    scale = float((t * FN_GD) ** -0.5)
    return pl.pallas_call(
        functools.partial(_fourier_kernel, scale=scale),
        grid=(bsz,),
        in_specs=[
            _const_spec(tabs.shape),
            pl.BlockSpec((None, tiles, t, LANES), uv_idx),
            _layer_spec((FN_W, FN_W), l),
        ],
        out_specs=pl.BlockSpec((None, t, FN_W), lambda b: (b, 0, 0)),
        out_shape=jax.ShapeDtypeStruct((bsz, t, FN_W), BF16),
        compiler_params=_cparams(("arbitrary",)),
        name=f"fourier_{t}",
    )(tabs, uv, fnet_w)


def _na_softmax_t(s):
    return jnp.exp2(s - jnp.max(s, axis=0, keepdims=True)).astype(BF16)


def _na_heads_out_t(o0, o1):
    row = lax.broadcasted_iota(jnp.int32, o0.shape, 0)
    out_t = jnp.where(row < NA_HD, o0 / o0[NA_HD:NA_HD + 1, :], o1 / o1[0:1, :])
    return out_t.T.astype(BF16)


def _na_group_kinds():
    margin = NA_WIN - NA_KH
    return ((0, lambda i: 0), (NA_KH // 2, lambda i: i), (margin + NA_KH - NA_GROUP, lambda i: margin))


def _na_kernel(q_ref, k_ref, v_ref, qc_ref, kc_ref, vc_ref, rpb_ref, o_ref, oc_ref,
               vt_ref, sc_ref, bias_ref, *, n_rows, need_ctx):
    lane = lax.broadcasted_iota(jnp.int32, (1, LANES), 1)
    first = lane < NA_HD
    zero = jnp.zeros((), BF16)
    one = jnp.ones((), BF16)
    v = v_ref[...]
    vt_ref[0] = jnp.where(first, v, one).T
    vt_ref[1] = jnp.where(first, one, v).T
    kc = kc_ref[...]
    vc = vc_ref[...]
    vct = (jnp.where(first, vc, one).T, jnp.where(first, one, vc).T)

    def both_heads(q):
        return jnp.concatenate([jnp.where(first, q, zero), jnp.where(first, zero, q)], axis=0)

    n_groups = n_rows // NA_GROUP
    gq = NA_GROUP * GRID_W
    n_loc = NA_WIN * GRID_W
    margin = NA_WIN - NA_KH

    @pl.when(pl.program_id(1) == 0)
    def _():
        wrow = lax.broadcasted_iota(jnp.int32, (n_loc, GRID_W), 0)
        for hh in range(2):
            for kind, (q_off, a_lo_of) in enumerate(_na_group_kinds()):
                for j in range(NA_GROUP // 2):
                    halves = []
                    for i in (2 * j, 2 * j + 1):
                        start = (NA_KH - 1 - q_off - i + margin) * GRID_W
                        lo = a_lo_of(i) * GRID_W
                        band = (wrow >= lo) & (wrow < lo + NA_KH * GRID_W)
                        halves.append(jnp.where(band, rpb_ref[hh, start:start + n_loc, :], NA_MASKED))
                    bias_ref[hh, kind, :, j * LANES:(j + 1) * LANES] = jnp.concatenate(halves, axis=1)

    def window(g):
        ws = jnp.clip(g * NA_GROUP - NA_KH // 2, 0, n_rows - NA_WIN)
        return (pl.ds(pl.multiple_of(g * gq, gq), gq), pl.ds(pl.multiple_of(ws * GRID_W, gq), n_loc))

    def scores(g, slot):
        qs, ks = window(g)
        kind = jnp.where(g == 0, 0, jnp.where(g == n_groups - 1, 2, 1))
        q = both_heads(q_ref[qs, :])
        s_loc = _dot_nt(k_ref[ks, :], q)
        s_ctx = _dot_nt(kc, q)
        for hh in range(2):
            cols = slice(hh * gq, (hh + 1) * gq)
            sc_ref[slot, 0:n_loc, cols] = s_loc[:, cols] + bias_ref[hh, kind]
        sc_ref[slot, n_loc:, :] = s_ctx

    def finish(g, slot):
        qs, ks = window(g)
        p = _na_softmax_t(sc_ref[slot])
        outs = [_dot(jnp.concatenate([vt_ref[hh, :, ks], vct[hh]], axis=1), p[:, hh * gq:(hh + 1) * gq])
                for hh in range(2)]
        o_ref[qs, :] = _na_heads_out_t(outs[0], outs[1])

    scores(0, 0)

    def pair(i, carry):
        scores(2 * i + 1, 1)
        finish(2 * i, 0)
        scores(2 * i + 2, 0)
        finish(2 * i + 1, 1)
        return carry

    lax.fori_loop(0, n_groups // 2 - 1, pair, 0)
    scores(n_groups - 1, 1)
    finish(n_groups - 2, 0)
    finish(n_groups - 1, 1)

    if need_ctx:
        n_ctx = qc_ref.shape[0]
        p = _na_softmax_t(_dot_nt(kc, both_heads(qc_ref[...])))
        outs = [_dot(vct[hh], p[:, hh * n_ctx:(hh + 1) * n_ctx]) for hh in range(2)]
        oc_ref[...] = _na_heads_out_t(outs[0], outs[1])
    else:
        oc_ref[...] = jnp.zeros(oc_ref.shape, oc_ref.dtype)


def _na(nqkv, nqkv_c, l, rpb_rows, *, need_ctx):
    bsz, t, _ = nqkv.shape
    n = nqkv_c.shape[1]
    n_rows = t // GRID_W
    assert n_rows % (2 * NA_GROUP) == 0 and n_rows >= NA_WIN + NA_GROUP
    pairs = NA_HEADS // 2
    n_loc = NA_WIN * GRID_W
    gq = NA_GROUP * GRID_W
    nb = NA_W // LANES
    lat = lambda off: pl.BlockSpec((None, t, LANES), lambda j, b: (b, 0, off + j))
    ctx = lambda off: pl.BlockSpec((None, n, LANES), lambda j, b: (b, 0, off + j))
    return pl.pallas_call(
        functools.partial(_na_kernel, n_rows=n_rows, need_ctx=need_ctx),
        grid=(pairs, bsz),
        in_specs=[lat(0), lat(nb), lat(2 * nb), ctx(0), ctx(nb), ctx(2 * nb),
                  pl.BlockSpec((None, 2) + rpb_rows.shape[2:], lambda j, b: (l, j, 0, 0))],
        out_specs=[pl.BlockSpec((None, t, LANES), lambda j, b: (b, 0, j)),
                   pl.BlockSpec((None, n, LANES), lambda j, b: (b, 0, j))],
        out_shape=[jax.ShapeDtypeStruct((bsz, t, NA_W), BF16),
                   jax.ShapeDtypeStruct((bsz, n, NA_W), BF16)],
        scratch_shapes=[pltpu.VMEM((2, LANES, t), BF16),
                        pltpu.VMEM((2, n_loc + n, 2 * gq), F32),
                        pltpu.VMEM((2, len(_na_group_kinds()), n_loc, gq), F32)],
        compiler_params=_cparams(("arbitrary", "arbitrary")),
        name="na_attn",
    )(nqkv, nqkv, nqkv, nqkv_c, nqkv_c, nqkv_c, rpb_rows)


def _gla_steps(jobs, st_ref, causal_ref, vmask_ref):
    c = GLA_CHUNK
    head = lax.broadcasted_iota(jnp.int32, (1, GLA_QK_PAD), 1) // GLA_DK_PAD
    prep = []
    for qk_ref, vg_ref, b_ref, out_ref, ci, d in jobs:
        rows = pl.ds(pl.multiple_of(ci * GLA_BLOCK, GLA_BLOCK), GLA_BLOCK)
        k = qk_ref[rows, GLA_QK_PAD:].astype(F32)
        v = vg_ref[rows, 0:GLA_V_PAD]
        b = b_ref[rows, d * GLA_QK_PAD:(d + 1) * GLA_QK_PAD]
        lo, hi = b[0:c, :], b[c:, :]
        if d == 0:
            tot1, tot2 = lo[c - 1:c, :], hi[c - 1:c, :]
            b_rel = jnp.concatenate([lo - tot1, hi], axis=0)
        else:
            tot1, tot2 = hi[0:1, :], lo[0:1, :]
            b_rel = jnp.concatenate([lo, hi - tot1], axis=0)
        k_in = k * jnp.exp(-b_rel)
        k_end = k_in * jnp.exp(tot2)
        q_in = q_st = None
        if out_ref is not None:
            q_rel = qk_ref[rows, 0:GLA_QK_PAD].astype(F32) * jnp.exp(b_rel)
            q_in = q_rel.astype(BF16)
            q_st = (q_rel * jnp.exp(tot1)).astype(BF16)
        decay = jnp.exp(jnp.broadcast_to(tot1 + tot2, (GLA_DV_PAD, GLA_QK_PAD)).T)
        prep.append((rows, q_in, q_st, k_in, k_end, v, decay))

    att = [None if q_in is None else
           _dot_nt(q_in, jnp.concatenate([jnp.where(head == h, k_in, 0.0) for h in range(GLA_HEADS)],
                                         axis=0).astype(BF16))
           for (rows, q_in, q_st, k_in, k_end, v, decay) in prep]

    kv = []
    for (rows, q_in, q_st, k_in, k_end, v, decay) in prep:
        k_t = k_end.astype(BF16).T
        kv.append(jnp.concatenate(
            [_dot(k_t[h * GLA_DK_PAD:(h + 1) * GLA_DK_PAD, :], v[:, h * GLA_DV_PAD:(h + 1) * GLA_DV_PAD])
             for h in range(GLA_HEADS)], axis=0))

    inter = []
    zero = jnp.zeros((GLA_DK_PAD, GLA_DV_PAD), BF16)
    for j, ((qk_ref, vg_ref, b_ref, out_ref, ci, d), (rows, q_in, q_st, k_in, k_end, v, decay)) in enumerate(
            zip(jobs, prep)):
        st = st_ref[d]
        if q_st is None:
            inter.append(None)
        else:
            sb = st.astype(BF16)
            s_bd = jnp.concatenate(
                [jnp.concatenate([sb[h * GLA_DK_PAD:(h + 1) * GLA_DK_PAD, :] if h2 == h else zero
                                  for h2 in range(GLA_HEADS)], axis=1) for h in range(GLA_HEADS)], axis=0)
            inter.append(_dot(q_st, s_bd))
        st_ref[d] = st * decay + kv[j]

    for j, ((qk_ref, vg_ref, b_ref, out_ref, ci, d), (rows, q_in, q_st, k_in, k_end, v, decay)) in enumerate(
            zip(jobs, prep)):
        if out_ref is not None:
            a = (att[j] * causal_ref[d]).astype(BF16)
            intra = []
            for pair in range(GLA_HEADS // 2):
                cols = slice(pair * 2 * GLA_DV_PAD, (pair + 1) * 2 * GLA_DV_PAD)
                v_pair = jnp.concatenate([v[:, cols]] * 2, axis=0) * vmask_ref[...]
                intra.append(_dot(a[:, pair * 2 * GLA_BLOCK:(pair + 1) * 2 * GLA_BLOCK], v_pair))
            out_ref[rows, :] = (inter[j] + jnp.concatenate(intra, axis=1)).astype(out_ref.dtype)


def _gla_kernel(qk_ref, vg_ref, b_ref, qkc_ref, vgc_ref, bc_ref, causal_ref, vmask_ref, *rest, t, n, need_ctx):
    if need_ctx:
        of_ref, ob_ref, ocf_ref, ocb_ref, st_ref = rest
    else:
        of_ref, ob_ref, st_ref = rest
    st_ref[...] = jnp.zeros(st_ref.shape, F32)
    nb_ctx = n // GLA_BLOCK
    nb_lat = t // GLA_BLOCK

    def jobs(refs, outs, n_blocks, i):
        return [refs + (outs[0], i, 0), refs + (outs[1], n_blocks - 1 - i, 1)]

    ctx_outs = (ocf_ref, ocb_ref) if need_ctx else (None, None)
    for i in range(nb_ctx):
        _gla_steps(jobs((qkc_ref, vgc_ref, bc_ref), ctx_outs, nb_ctx, i), st_ref, causal_ref, vmask_ref)

    def step(i, carry):
        _gla_steps(jobs((qk_ref, vg_ref, b_ref), (of_ref, ob_ref), nb_lat, i), st_ref, causal_ref, vmask_ref)
        return carry

    lax.fori_loop(0, nb_lat, step, 0, unroll=GLA_UNROLL)


def _gla(gqk, gvg, gb, gqk_c, gvg_c, gb_c, causal, vmask, *, need_ctx):
    bsz, t, _ = gqk.shape
    n = gqk_c.shape[1]
    assert t % GLA_BLOCK == 0 and n % GLA_BLOCK == 0
    row = lambda rows, w_: pl.BlockSpec((None, rows, w_), lambda b: (b, 0, 0))
    out_rows = (t, t, n, n) if need_ctx else (t, t)
    return pl.pallas_call(
        functools.partial(_gla_kernel, t=t, n=n, need_ctx=need_ctx),
        grid=(bsz,),
        in_specs=[row(t, 2 * GLA_QK_PAD), row(t, GLA_V_PAD), row(t, 2 * GLA_QK_PAD),
                  row(n, 2 * GLA_QK_PAD), row(n, GLA_V_PAD), row(n, 2 * GLA_QK_PAD),
                  _const_spec(causal.shape), _const_spec(vmask.shape)],
        out_specs=[row(r, GLA_V_PAD) for r in out_rows],
        out_shape=[jax.ShapeDtypeStruct((bsz, r, GLA_V_PAD), BF16) for r in out_rows],
        scratch_shapes=[pltpu.VMEM((2, GLA_QK_PAD, GLA_DV_PAD), F32)],
        compiler_params=_cparams(("arbitrary",)),
        name="gla_scan",
    )(gqk, gvg, gb, gqk_c, gvg_c, gb_c, causal, vmask)


def _gla_out(of_ref, ob_ref, g_ref, gain_ref, rows):
    heads = []
    for h in range(GLA_HEADS):
        cols = slice(h * GLA_DV_PAD, (h + 1) * GLA_DV_PAD)
        o = of_ref[rows, cols].astype(F32) + ob_ref[rows, cols].astype(F32)
        ms = jnp.sum(o * o, axis=-1, keepdims=True) * (1.0 / GLA_DV)
        g = g_ref[rows, cols].astype(F32)
        heads.append((o * lax.rsqrt(ms + EPS) * gain_ref[:, cols] * (g * jax.nn.sigmoid(g))).astype(BF16))
    return jnp.concatenate(heads, axis=1)


def _post_kernel(x_ref, yfn_ref, yna_ref, of_ref, ob_ref, g_ref, mod_ref, gain_o_ref, gain_ref,
                 wo_ref, w1_ref, w3_ref, w2_ref, o_ref):
    tm = x_ref.shape[0]
    sub = min(tm, POST_SUB)
    for r0 in range(0, tm, sub):
        rows = slice(r0, r0 + sub)
        y = (_dot(yfn_ref[rows, :], wo_ref[0:FN_W, :])
             + _dot(yna_ref[rows, :], wo_ref[FN_W:FN_W + NA_W, :])
             + _dot(_gla_out(of_ref, ob_ref, g_ref, gain_o_ref, rows), wo_ref[FN_W + NA_W:, :]))
        x1 = x_ref[rows, :] + mod_ref[2:3, :] * y
        h = _modulated_norm(x1, gain_ref[...], mod_ref[3:4, :], mod_ref[4:5, :]).astype(BF16)
        acts = []
        for c in range(D_FF // MXU_W):
            cols = slice(c * MXU_W, (c + 1) * MXU_W)
            u = _dot(h, w1_ref[:, cols])
            g = _dot(h, w3_ref[:, cols])
            acts.append((u * jax.nn.sigmoid(u) * g).astype(BF16))
        acc = _dot(jnp.concatenate(acts, axis=1), w2_ref[...])
        o_ref[rows, :] = x1 + mod_ref[5:6, :] * acc


def _post(x, yfn, yna, o_fwd, o_bwd, gvg, l, mod, mod_row, p, *, tm):
    bsz, t, d = x.shape
    assert t % tm == 0
    row = lambda w_: pl.BlockSpec((None, tm, w_), lambda b, i: (b, i, 0))
    gate = pl.BlockSpec((None, tm, GLA_V_PAD), lambda b, i: (b, i, 1))
    return pl.pallas_call(
        _post_kernel,
        grid=(bsz, t // tm),
        in_specs=[row(d), row(FN_W), row(NA_W), row(GLA_V_PAD), row(GLA_V_PAD), gate,
                  _mod_spec(d, l, mod_row), _layer_spec((1, GLA_V_PAD), l),
                  _layer_spec((1, d), l), _layer_spec((MIX_PAD, d), l), _layer_spec((d, D_FF), l),
                  _layer_spec((d, D_FF), l), _layer_spec((D_FF, d), l)],
        out_specs=row(d),
        out_shape=jax.ShapeDtypeStruct((bsz, t, d), F32),
        compiler_params=_cparams(("arbitrary", "arbitrary")),
        name=f"post_{t}",
    )(x, yfn, yna, o_fwd, o_bwd, gvg, mod, p["gain_o"], p["gain_f"], p["wo"], p["w1"], p["w3"], p["w2"])


def _dft_tables(n):
    idx = (np.arange(n)[:, None] * np.arange(n)[None, :]) % n
    ang = 2.0 * np.pi * idx / n
    return np.cos(ang), np.sin(ang)


def _shape_consts(t, n):
    c64, s64 = _dft_tables(FN_GD)
    eye = np.eye(FN_GROUPS)
    cs = np.concatenate([np.kron(eye, c64), np.kron(eye, s64)], axis=1)
    chunk_id = np.arange(SUB) // GLA_CHUNK
    same = chunk_id[:, None] == chunk_id[None, :]
    pos = np.arange(SUB)
    tri = np.stack([same & (pos[None, :] <= pos[:, None]), same & (pos[None, :] >= pos[:, None])])
    cpos = np.arange(GLA_BLOCK)
    fwd = cpos[None, :] <= cpos[:, None]
    gla_causal = np.stack([np.tile(fwd, (1, GLA_HEADS)), np.tile(fwd.T, (1, GLA_HEADS))])
    gla_vmask = ((np.arange(2 * GLA_BLOCK)[:, None] // GLA_BLOCK)
                 == (np.arange(2 * GLA_DV_PAD)[None, :] // GLA_DV_PAD))

    m = GLA_DK // 4
    inv = ROPE_BASE ** (-np.arange(m) / m)
    tok = np.arange(t)
    cos_h = np.ones((t, GLA_DK_PAD))
    sin_lo = np.zeros((t, GLA_DK_PAD))
    sin_hi = np.zeros((t, GLA_DK_PAD))
    for blk, p in enumerate((tok // GRID_W, tok % GRID_W)):
        ang = p[:, None] * inv[None, :]
        o = blk * 2 * m
        cos_h[:, o:o + m] = np.cos(ang)
        cos_h[:, o + m:o + 2 * m] = np.cos(ang)
        sin_lo[:, o:o + m] = -np.sin(ang)
        sin_hi[:, o + m:o + 2 * m] = np.sin(ang)
    rope = np.stack([np.tile(a, (1, GLA_HEADS)) for a in (cos_h, sin_lo, sin_hi)])

    def position_tables(length):
        cos, sin = _dft_tables(length)
        half = length // 2
        return np.stack([cos[:half, 0::2], sin[:half, 0::2], cos[:half, 1::2], sin[:half, 1::2]])

    to_bf16 = lambda a: jnp.asarray(a, F32).astype(BF16)
    return {
        "cs": to_bf16(cs), "tri": jnp.asarray(tri, BF16),
        "gla_causal": jnp.asarray(gla_causal, F32), "gla_vmask": jnp.asarray(gla_vmask, BF16),
        "rope": jnp.asarray(rope, F32),
        "dft_lat": to_bf16(position_tables(t)), "dft_ctx": to_bf16(position_tables(n)),
    }


def _bias_kernel(rpb_ref, onehot_ref, ok_ref, o_ref):
    r = rpb_ref[...]
    onehot = onehot_ref[...]
    t = jnp.zeros(o_ref.shape, F32)
    for _ in range(3):
        piece = r.astype(BF16)
        t = t + _dot(piece, onehot)
        r = r - piece.astype(F32)
    o_ref[...] = jnp.where(ok_ref[...] > 0.0, t * LOG2E, NA_MASKED)


def _window_bias_rows(na_rpb):
    depth, heads, n_dr, n_dc = na_rpb.shape
    cq = np.arange(GRID_W)
    c0 = np.clip(cq - NA_KW // 2, 0, GRID_W - NA_KW)
    col_ok = (cq[None, :] >= c0[:, None]) & (cq[None, :] < c0[:, None] + NA_KW)
    dc = np.clip(cq[None, :] - cq[:, None], -(NA_KW - 1), NA_KW - 1) + NA_KW - 1
    n_dc_pad = n_dc + 1
    n_tab = depth * heads * n_dr
    rows_pad = -(-n_tab // 8) * 8
    onehot = (np.arange(n_dc_pad)[:, None, None] == dc[None]).reshape(n_dc_pad, GRID_W * GRID_W)
    rpb2 = jnp.pad(na_rpb.reshape(n_tab, n_dc), ((0, rows_pad - n_tab), (0, 1)))
    t = pl.pallas_call(
        _bias_kernel,
        out_shape=jax.ShapeDtypeStruct((rows_pad, GRID_W * GRID_W), F32),
        name="na_bias",
    )(rpb2, jnp.asarray(onehot, BF16), jnp.asarray(col_ok.reshape(1, -1), F32))
    t = t[:n_tab].reshape(depth * heads, n_dr, GRID_W, GRID_W)
    t = t.transpose(0, 1, 3, 2).reshape(depth * heads, n_dr * GRID_W, GRID_W)
    margin = NA_WIN - NA_KH
    t = jnp.pad(t, ((0, 0), (margin * GRID_W, margin * GRID_W), (0, 0)))
    return t.reshape(depth, heads, (n_dr + 2 * margin) * GRID_W, GRID_W)


def _pad_heads(w, heads, dim, pad, axis=-1):
    axis = axis % w.ndim
    shape = w.shape
    w = w.reshape(shape[:axis] + (heads, dim) + shape[axis + 1:])
    widths = [(0, 0)] * w.ndim
    widths[axis + 1] = (0, pad - dim)
    return jnp.pad(w, widths).reshape(shape[:axis] + (heads * pad,) + shape[axis + 1:])


def _params(norm_mix, norm_ffn, w_in, fnet_w, na_q_norm, na_k_norm, na_rpb, alpha_w, alpha_b, o_norm, w_out,
            ffn_w1, ffn_w3, ffn_w2):
    depth = w_in.shape[0]
    o = FN_W + 3 * NA_W
    gq = w_in[..., o:o + GLA_QK_W]
    gk = w_in[..., o + GLA_QK_W:o + 2 * GLA_QK_W]
    o2 = o + 2 * GLA_QK_W
    gv = w_in[..., o2:o2 + GLA_V_W]
    gg = w_in[..., o2 + GLA_V_W:o2 + 2 * GLA_V_W]
    ga = w_in[..., o2 + 2 * GLA_V_W:]
    w = jnp.concatenate([
        w_in[..., :o],
        jnp.pad(ga, ((0, 0), (0, 0), (0, GLA_A_PAD - 2 * GLA_RANK))),
        _pad_heads(gq, GLA_HEADS, GLA_DK, GLA_DK_PAD), _pad_heads(gk, GLA_HEADS, GLA_DK, GLA_DK_PAD),
        _pad_heads(gv, GLA_HEADS, GLA_DV, GLA_DV_PAD), _pad_heads(gg, GLA_HEADS, GLA_DV, GLA_DV_PAD),
    ], axis=-1).astype(BF16)

    qkg = jnp.stack([jnp.tile(na_q_norm, (1, NA_HEADS)) * (NA_HD ** -0.5 * LOG2E),
                     jnp.tile(na_k_norm, (1, NA_HEADS))], axis=1)

    aw_pad = _pad_heads(alpha_w, GLA_HEADS, GLA_DK, GLA_DK_PAD)
    aw = jnp.concatenate([jnp.pad(aw_pad[:, 0], ((0, 0), (0, 0), (0, GLA_QK_PAD))),
                          jnp.pad(aw_pad[:, 1], ((0, 0), (0, 0), (GLA_QK_PAD, 0)))], axis=1)
    aw = jnp.pad(aw, ((0, 0), (0, GLA_A_PAD - 2 * GLA_RANK), (0, 0))).astype(BF16)
    ab = _pad_heads(alpha_b, GLA_HEADS, GLA_DK, GLA_DK_PAD).reshape(depth, 1, 2 * GLA_QK_PAD)

    gain_o = _pad_heads(jnp.tile(o_norm, (1, GLA_HEADS)), GLA_HEADS, GLA_DV, GLA_DV_PAD)[:, None, :]
    wo = jnp.concatenate([w_out[:, :FN_W + NA_W],
                          _pad_heads(w_out[:, FN_W + NA_W:], GLA_HEADS, GLA_DV, GLA_DV_PAD, axis=1)],
                         axis=1).astype(BF16)
    return {
        "gain_m": norm_mix[:, None, :], "gain_f": norm_ffn[:, None, :], "w": w, "qkg": qkg, "aw": aw, "ab": ab,
        "rpb_rows": _window_bias_rows(na_rpb), "gain_o": gain_o, "wo": wo, "fw": fnet_w.astype(BF16),
        "w1": ffn_w1.astype(BF16), "w3": ffn_w3.astype(BF16), "w2": ffn_w2.astype(BF16),
    }


def kernel(x, c, ctx, c_ctx, ada_w, ada_b, norm_mix, norm_ffn, w_in, fnet_w, na_q_norm, na_k_norm, na_rpb,
           gla_alpha_w, gla_alpha_b, gla_o_norm, w_out, ffn_w1, ffn_w3, ffn_w2):
    bsz, t, d = x.shape
    n = ctx.shape[1]
    depth = ada_w.shape[0]
    ctx_row = bsz
    assert bsz < MOD_ROWS and n % SUB == 0
    ctx_tm = min(bsz * n, INPROJ_TM)
    consts = _shape_consts(t, n)
    p = _params(norm_mix, norm_ffn, w_in, fnet_w, na_q_norm, na_k_norm, na_rpb, gla_alpha_w, gla_alpha_b,
                gla_o_norm, w_out, ffn_w1, ffn_w3, ffn_w2)

    cc = jnp.concatenate([c, c_ctx[None, :], jnp.zeros((MOD_ROWS - bsz - 1, d), F32)])
    mod = _ada_mod(cc, ada_w, ada_b).reshape(depth, MOD_ROWS, 6, d)

    cx = ctx.reshape(1, bsz * n, d)
    flat = lambda a: a.reshape(1, bsz * n, a.shape[-1])
    per_batch = lambda a: a.reshape(bsz, n, a.shape[-1])
    for l in range(depth):
        need_ctx = l < depth - 1
        uv, nqkv, gqk, gvg, gb = _inproj(x, l, mod, None, p, consts, consts["rope"], tm=INPROJ_TM)
        uv_c, *rest_c = _inproj(cx, l, mod, ctx_row, p, consts, None, tm=ctx_tm)
        nqkv_c, gqk_c, gvg_c, gb_c = [per_batch(a) for a in rest_c]

        y_fn = _fourier(uv, l, consts["dft_lat"], p["fw"], bsz)
        y_na, y_na_c = _na(nqkv, nqkv_c, l, p["rpb_rows"], need_ctx=need_ctx)
        o_gla = _gla(gqk, gvg, gb, gqk_c, gvg_c, gb_c, consts["gla_causal"], consts["gla_vmask"],
                     need_ctx=need_ctx)
        x = _post(x, y_fn, y_na, o_gla[0], o_gla[1], gvg, l, mod, None, p, tm=POST_TM)
        if need_ctx:
            y_fn_c = _fourier(uv_c, l, consts["dft_ctx"], p["fw"], bsz)
            cx = _post(cx, flat(y_fn_c), flat(y_na_c), flat(o_gla[2]), flat(o_gla[3]), flat(gvg_c),
                       l, mod, ctx_row, p, tm=ctx_tm)
    return x
```

```python
import functools

import numpy as np
import jax
import jax.numpy as jnp
from jax import lax
from jax.experimental import pallas as pl
from jax.experimental.pallas import tpu as pltpu

F32 = jnp.float32
BF16 = jnp.bfloat16

D_MODEL = 1024
GRID_W = 64
FN_W = 256
FN_GROUPS = 4
FN_GD = 64
NA_HD = 64
NA_W = 384
NA_HEADS = 6
NA_KH = 8
NA_KW = 16
NA_GROUP = 4
NA_WIN = NA_GROUP + NA_KH
LOG2E = float(np.log2(np.e))
NA_MASKED = -1e30
GLA_HEADS = 4
GLA_DK = 48
GLA_DV = 96
GLA_QK_W = GLA_HEADS * GLA_DK
GLA_V_W = GLA_HEADS * GLA_DV
GLA_RANK = 16
GLA_GATE_NORM = 16.0
GLA_CHUNK = 64
GLA_BLOCK = 2 * GLA_CHUNK
GLA_UNROLL = 4
ROPE_BASE = 10000.0
D_FF = 2816
EPS = 1e-6
MOD_ROWS = 16

LANES = 128
MXU_W = 256
VMEM_LIMIT_BYTES = 56 * 1024 * 1024

GLA_DK_PAD = 64
GLA_DV_PAD = 128
GLA_QK_PAD = GLA_HEADS * GLA_DK_PAD
GLA_V_PAD = GLA_HEADS * GLA_DV_PAD
GLA_A_PAD = LANES
MIX_PAD = FN_W + NA_W + GLA_V_PAD

C_FX = 0
C_NQ = C_FX + FN_W
C_NK = C_NQ + NA_W
C_NV = C_NK + NA_W
C_GA = C_NV + NA_W
C_GQ = C_GA + GLA_A_PAD
C_GK = C_GQ + GLA_QK_PAD
C_GV = C_GK + GLA_QK_PAD
C_GG = C_GV + GLA_V_PAD
N_IN_PAD = C_GG + GLA_V_PAD

SUB = 256

INPROJ_TM = 1024
POST_TM = 512
POST_SUB = 256
CTX_TM = 512
ADA_TN = 1024


def _cparams(sem):
    return pltpu.CompilerParams(dimension_semantics=sem, vmem_limit_bytes=VMEM_LIMIT_BYTES)


def _const_spec(shape):
    nd = len(shape)
    return pl.BlockSpec(tuple(shape), lambda *_: (0,) * nd, pipeline_mode=pl.Buffered(1))


def _layer_spec(shape, l):
    nd = len(shape)
    return pl.BlockSpec((None,) + tuple(shape), lambda *_: (l,) + (0,) * nd, pipeline_mode=pl.Buffered(1))


def _mod_spec(d, l, mod_row):
    if mod_row is None:
        return pl.BlockSpec((None, None, 6, d), lambda b, i: (l, b, 0, 0))
    return pl.BlockSpec((None, None, 6, d), lambda b, i: (l, mod_row, 0, 0))


def _dot(a, b):
    return jnp.dot(a, b, preferred_element_type=F32)


def _dot_nt(a, b):
    return lax.dot_general(a, b, (((1,), (1,)), ((), ())), preferred_element_type=F32)


def _ada_kernel(c_ref, w_ref, b_ref, o_ref):
    a = c_ref[...]
    a = (a * jax.nn.sigmoid(a)).astype(BF16)
    o_ref[...] = _dot(a, w_ref[...].astype(BF16)) + b_ref[...]


def _ada_mod(cc, ada_w, ada_b):
    depth, d, n = ada_w.shape
    rows = cc.shape[0]
    tn = ADA_TN
    return pl.pallas_call(
        _ada_kernel,
        grid=(depth, n // tn),
        in_specs=[
            pl.BlockSpec((rows, d), lambda l, j: (0, 0)),
            pl.BlockSpec((None, d, tn), lambda l, j: (l, 0, j)),
            pl.BlockSpec((None, 1, tn), lambda l, j: (l, 0, j)),
        ],
        out_specs=pl.BlockSpec((None, rows, tn), lambda l, j: (l, 0, j)),
        out_shape=jax.ShapeDtypeStruct((depth, rows, n), F32),
        compiler_params=_cparams(("arbitrary", "arbitrary")),
        name="ada_mod",
    )(cc, ada_w, ada_b.reshape(depth, 1, n))


def _modulated_norm(x, gain, shift, scale):
    ms = jnp.mean(x * x, axis=-1, keepdims=True)
    return (x * lax.rsqrt(ms + EPS) * gain) * (1.0 + scale) + shift


def _log_sigmoid(x):
    return jnp.minimum(x, 0.0) - jnp.log1p(jnp.exp(-jnp.abs(x)))


def _head_mean_sq(z):
    first = lax.broadcasted_iota(jnp.int32, (1, LANES), 1) < NA_HD
    out = []
    for t in range(NA_W // LANES):
        sq = z[:, t * LANES:(t + 1) * LANES]
        sq = sq * sq
        lo = jnp.sum(jnp.where(first, sq, 0.0), axis=-1, keepdims=True)
        hi = jnp.sum(jnp.where(first, 0.0, sq), axis=-1, keepdims=True)
        out.append(jnp.where(first, lo, hi))
    return jnp.concatenate(out, axis=1) * (1.0 / NA_HD)


def _inproj_kernel(x_ref, mod_ref, gain_ref, w_ref, cs_ref, qkg_ref, rope_ref, aw_ref, ab_ref, tri_ref,
                   uv_ref, nqkv_ref, gqk_ref, gvg_ref, gb_ref, *, rope, tm):
    x = x_ref[...]
    h = _modulated_norm(x, gain_ref[...], mod_ref[0:1, :], mod_ref[1:2, :]).astype(BF16)

    zf = _dot(h, w_ref[:, C_FX:C_NQ]).astype(BF16)
    zva = _dot(h, w_ref[:, C_NV:C_GQ])
    nqkv_ref[:, 2 * NA_W:3 * NA_W] = zva[:, :NA_W].astype(BF16)
    a = zva[:, NA_W:].astype(BF16)
    zqk = _dot(h, w_ref[:, C_NQ:C_NV])
    logits = _dot(a, aw_ref[...]) + ab_ref[...]
    zg = _dot(h, w_ref[:, C_GQ:C_GV])
    uv = _dot(zf, cs_ref[...])
    for j in range(uv_ref.shape[0]):
        uv_ref[j] = uv[:, j * LANES:(j + 1) * LANES]
    gvg_ref[...] = _dot(h, w_ref[:, C_GV:N_IN_PAD]).astype(BF16)

    glog = (_log_sigmoid(logits) * (1.0 / GLA_GATE_NORM)).astype(BF16)
    for s in range(tm // SUB):
        rows = slice(s * SUB, (s + 1) * SUB)
        gb_ref[rows, 0:GLA_QK_PAD] = _dot(tri_ref[0], glog[rows, 0:GLA_QK_PAD])
        gb_ref[rows, GLA_QK_PAD:] = _dot(tri_ref[1], glog[rows, GLA_QK_PAD:])

    for i in range(2):
        z = zqk[:, i * NA_W:(i + 1) * NA_W]
        z = z * lax.rsqrt(_head_mean_sq(z) + EPS) * qkg_ref[i:i + 1, :]
        nqkv_ref[:, i * NA_W:(i + 1) * NA_W] = z.astype(BF16)

    for i in range(2):
        z = zg[:, i * GLA_QK_PAD:(i + 1) * GLA_QK_PAD]
        if rope:
            z = (z * rope_ref[0]
                 + pltpu.roll(z, GLA_QK_PAD - GLA_DK // 4, 1) * rope_ref[1]
                 + pltpu.roll(z, GLA_DK // 4, 1) * rope_ref[2])
        if i == 0:
            z = z * (GLA_DK ** -0.5)
        gqk_ref[:, i * GLA_QK_PAD:(i + 1) * GLA_QK_PAD] = z.astype(BF16)


def _inproj(x, l, mod, mod_row, p, consts, rope_tab, *, tm):
    bsz, t, d = x.shape
    assert t % tm == 0 and tm % SUB == 0
    rope = rope_tab is not None
    if not rope:
        rope_tab = jnp.zeros((3, tm, GLA_QK_PAD), F32)
    rope_idx = (lambda b, i: (0, i, 0)) if rope else (lambda b, i: (0, 0, 0))
    row = lambda w_: pl.BlockSpec((None, tm, w_), lambda b, i: (b, i, 0))
    uv_tiles = 2 * FN_W // LANES
    out_w = (3 * NA_W, 2 * GLA_QK_PAD, 2 * GLA_V_PAD, 2 * GLA_QK_PAD)
    out_dt = (BF16, BF16, BF16, F32)
    return pl.pallas_call(
        functools.partial(_inproj_kernel, rope=rope, tm=tm),
        grid=(bsz, t // tm),
        in_specs=[
            row(d),
            _mod_spec(d, l, mod_row),
            _layer_spec((1, d), l),
            _layer_spec((d, N_IN_PAD), l),
            _const_spec(consts["cs"].shape),
            _layer_spec((2, NA_W), l),
            pl.BlockSpec((3, tm, GLA_QK_PAD), rope_idx),
            _layer_spec((GLA_A_PAD, 2 * GLA_QK_PAD), l),
            _layer_spec((1, 2 * GLA_QK_PAD), l),
            _const_spec((2, SUB, SUB)),
        ],
        out_specs=[pl.BlockSpec((None, uv_tiles, tm, LANES), lambda b, i: (b, 0, i, 0))]
        + [row(w_) for w_ in out_w],
        out_shape=[jax.ShapeDtypeStruct((bsz, uv_tiles, t, LANES), F32)]
        + [jax.ShapeDtypeStruct((bsz, t, w_), dt) for w_, dt in zip(out_w, out_dt)],
        compiler_params=_cparams(("arbitrary", "arbitrary")),
        name="inproj_rope" if rope else "inproj_ctx",
    )(x, mod, p["gain_m"], p["w"], consts["cs"], p["qkg"], rope_tab, p["aw"], p["ab"], consts["tri"])


def _fourier_kernel(tab_ref, uv_ref, w_ref, o_ref, *, scale):
    half = uv_ref.shape[1] // 2

    def positions(parity):
        tiles = [uv_ref[j, pl.ds(parity, half, stride=2), :].astype(BF16) for j in range(uv_ref.shape[0])]
        return jnp.concatenate(tiles[:2], axis=1), jnp.concatenate(tiles[2:], axis=1)

    ue, ve = positions(0)
    uo, vo = positions(1)
    even = _dot(tab_ref[0], ue) - _dot(tab_ref[1], ve)
    odd = _dot(tab_ref[2], uo) - _dot(tab_ref[3], vo)
    w = w_ref[...]
    o_ref[0:half, :] = _dot(((even + odd) * scale).astype(BF16), w).astype(BF16)
    o_ref[half:, :] = _dot(((even - odd) * scale).astype(BF16), w).astype(BF16)


def _fourier(uv, l, tabs, fnet_w):
    bsz, tiles, t, _ = uv.shape
    scale = float((t * FN_GD) ** -0.5)
    return pl.pallas_call(
        functools.partial(_fourier_kernel, scale=scale),
        grid=(bsz,),
        in_specs=[
            _const_spec(tabs.shape),
            pl.BlockSpec((None, tiles, t, LANES), lambda b: (b, 0, 0, 0)),
            _layer_spec((FN_W, FN_W), l),
        ],
        out_specs=pl.BlockSpec((None, t, FN_W), lambda b: (b, 0, 0)),
        out_shape=jax.ShapeDtypeStruct((bsz, t, FN_W), BF16),
        compiler_params=_cparams(("arbitrary",)),
        name=f"fourier_{t}",
    )(tabs, uv, fnet_w)


def _na_softmax_t(s):
    return jnp.exp2(s - jnp.max(s, axis=0, keepdims=True)).astype(BF16)


def _na_heads_out_t(o0, o1):
    row = lax.broadcasted_iota(jnp.int32, o0.shape, 0)
    out_t = jnp.where(row < NA_HD, o0 / o0[NA_HD:NA_HD + 1, :], o1 / o1[0:1, :])
    return out_t.T.astype(BF16)


def _na_group_kinds():
    margin = NA_WIN - NA_KH
    return ((0, lambda i: 0), (NA_KH // 2, lambda i: i), (margin + NA_KH - NA_GROUP, lambda i: margin))


def _na_kernel(q_ref, k_ref, v_ref, qc_ref, kc_ref, vc_ref, rpb_ref, o_ref, oc_ref,
               vt_ref, sc_ref, bias_ref, *, n_rows, need_ctx):
    lane = lax.broadcasted_iota(jnp.int32, (1, LANES), 1)
    first = lane < NA_HD
    zero = jnp.zeros((), BF16)
    one = jnp.ones((), BF16)
    v = v_ref[...]
    vt_ref[0] = jnp.where(first, v, one).T
    vt_ref[1] = jnp.where(first, one, v).T
    kc = kc_ref[...]
    vc = vc_ref[...]
    vct = (jnp.where(first, vc, one).T, jnp.where(first, one, vc).T)

    def both_heads(q):
        return jnp.concatenate([jnp.where(first, q, zero), jnp.where(first, zero, q)], axis=0)

    n_groups = n_rows // NA_GROUP
    gq = NA_GROUP * GRID_W
    n_loc = NA_WIN * GRID_W
    margin = NA_WIN - NA_KH

    @pl.when(pl.program_id(1) == 0)
    def _():
        wrow = lax.broadcasted_iota(jnp.int32, (n_loc, GRID_W), 0)
        for hh in range(2):
            for kind, (q_off, a_lo_of) in enumerate(_na_group_kinds()):
                for j in range(NA_GROUP // 2):
                    halves = []
                    for i in (2 * j, 2 * j + 1):
                        start = (NA_KH - 1 - q_off - i + margin) * GRID_W
                        lo = a_lo_of(i) * GRID_W
                        band = (wrow >= lo) & (wrow < lo + NA_KH * GRID_W)
                        halves.append(jnp.where(band, rpb_ref[hh, start:start + n_loc, :], NA_MASKED))
                    bias_ref[hh, kind, :, j * LANES:(j + 1) * LANES] = jnp.concatenate(halves, axis=1)

    def window(g):
        ws = jnp.clip(g * NA_GROUP - NA_KH // 2, 0, n_rows - NA_WIN)
        return (pl.ds(pl.multiple_of(g * gq, gq), gq), pl.ds(pl.multiple_of(ws * GRID_W, gq), n_loc))

    def scores(g, slot):
        qs, ks = window(g)
        kind = jnp.where(g == 0, 0, jnp.where(g == n_groups - 1, 2, 1))
        q = both_heads(q_ref[qs, :])
        s_loc = _dot_nt(k_ref[ks, :], q)
        s_ctx = _dot_nt(kc, q)
        for hh in range(2):
            cols = slice(hh * gq, (hh + 1) * gq)
            sc_ref[slot, 0:n_loc, cols] = s_loc[:, cols] + bias_ref[hh, kind]
        sc_ref[slot, n_loc:, :] = s_ctx

    def finish(g, slot):
        qs, ks = window(g)
        p = _na_softmax_t(sc_ref[slot])
        outs = [_dot(jnp.concatenate([vt_ref[hh, :, ks], vct[hh]], axis=1), p[:, hh * gq:(hh + 1) * gq])
                for hh in range(2)]
        o_ref[qs, :] = _na_heads_out_t(outs[0], outs[1])

    scores(0, 0)

    def pair(i, carry):
        scores(2 * i + 1, 1)
        finish(2 * i, 0)
        scores(2 * i + 2, 0)
        finish(2 * i + 1, 1)
        return carry

    lax.fori_loop(0, n_groups // 2 - 1, pair, 0)
    scores(n_groups - 1, 1)
    finish(n_groups - 2, 0)
    finish(n_groups - 1, 1)

    if need_ctx:
        n_ctx = qc_ref.shape[0]
        p = _na_softmax_t(_dot_nt(kc, both_heads(qc_ref[...])))
        outs = [_dot(vct[hh], p[:, hh * n_ctx:(hh + 1) * n_ctx]) for hh in range(2)]
        oc_ref[...] = _na_heads_out_t(outs[0], outs[1])
    else:
        oc_ref[...] = jnp.zeros(oc_ref.shape, oc_ref.dtype)


def _na(nqkv, nqkv_c, l, rpb_rows, *, need_ctx):
    bsz, t, _ = nqkv.shape
    n = nqkv_c.shape[1]
    n_rows = t // GRID_W
    assert n_rows % (2 * NA_GROUP) == 0 and n_rows >= NA_WIN + NA_GROUP
    pairs = NA_HEADS // 2
    n_loc = NA_WIN * GRID_W
    gq = NA_GROUP * GRID_W
    nb = NA_W // LANES
    lat = lambda off: pl.BlockSpec((None, t, LANES), lambda j, b: (b, 0, off + j))
    ctx = lambda off: pl.BlockSpec((None, n, LANES), lambda j, b: (b, 0, off + j))
    return pl.pallas_call(
        functools.partial(_na_kernel, n_rows=n_rows, need_ctx=need_ctx),
        grid=(pairs, bsz),
        in_specs=[lat(0), lat(nb), lat(2 * nb), ctx(0), ctx(nb), ctx(2 * nb),
                  pl.BlockSpec((None, 2) + rpb_rows.shape[2:], lambda j, b: (l, j, 0, 0))],
        out_specs=[pl.BlockSpec((None, t, LANES), lambda j, b: (b, 0, j)),
                   pl.BlockSpec((None, n, LANES), lambda j, b: (b, 0, j))],
        out_shape=[jax.ShapeDtypeStruct((bsz, t, NA_W), BF16),
                   jax.ShapeDtypeStruct((bsz, n, NA_W), BF16)],
        scratch_shapes=[pltpu.VMEM((2, LANES, t), BF16),
                        pltpu.VMEM((2, n_loc + n, 2 * gq), F32),
                        pltpu.VMEM((2, len(_na_group_kinds()), n_loc, gq), F32)],
        compiler_params=_cparams(("arbitrary", "arbitrary")),
        name="na_attn",
    )(nqkv, nqkv, nqkv, nqkv_c, nqkv_c, nqkv_c, rpb_rows)


def _gla_steps(jobs, st_ref, causal_ref, vmask_ref):
    c = GLA_CHUNK
    head = lax.broadcasted_iota(jnp.int32, (1, GLA_QK_PAD), 1) // GLA_DK_PAD
    prep = []
    for qk_ref, vg_ref, b_ref, out_ref, ci, d in jobs:
        rows = pl.ds(pl.multiple_of(ci * GLA_BLOCK, GLA_BLOCK), GLA_BLOCK)
        k = qk_ref[rows, GLA_QK_PAD:].astype(F32)
        v = vg_ref[rows, 0:GLA_V_PAD]
        b = b_ref[rows, d * GLA_QK_PAD:(d + 1) * GLA_QK_PAD]
        lo, hi = b[0:c, :], b[c:, :]
        if d == 0:
            tot1, tot2 = lo[c - 1:c, :], hi[c - 1:c, :]
            b_rel = jnp.concatenate([lo - tot1, hi], axis=0)
        else:
            tot1, tot2 = hi[0:1, :], lo[0:1, :]
            b_rel = jnp.concatenate([lo, hi - tot1], axis=0)
        k_in = k * jnp.exp(-b_rel)
        k_end = k_in * jnp.exp(tot2)
        q_in = q_st = None
        if out_ref is not None:
            q_rel = qk_ref[rows, 0:GLA_QK_PAD].astype(F32) * jnp.exp(b_rel)
            q_in = q_rel.astype(BF16)
            q_st = (q_rel * jnp.exp(tot1)).astype(BF16)
        decay = jnp.exp(jnp.broadcast_to(tot1 + tot2, (GLA_DV_PAD, GLA_QK_PAD)).T)
        prep.append((rows, q_in, q_st, k_in, k_end, v, decay))

    att = [None if q_in is None else
           _dot_nt(q_in, jnp.concatenate([jnp.where(head == h, k_in, 0.0) for h in range(GLA_HEADS)],
                                         axis=0).astype(BF16))
           for (rows, q_in, q_st, k_in, k_end, v, decay) in prep]

    kv = []
    for (rows, q_in, q_st, k_in, k_end, v, decay) in prep:
        k_t = k_end.astype(BF16).T
        kv.append(jnp.concatenate(
            [_dot(k_t[h * GLA_DK_PAD:(h + 1) * GLA_DK_PAD, :], v[:, h * GLA_DV_PAD:(h + 1) * GLA_DV_PAD])
             for h in range(GLA_HEADS)], axis=0))

    inter = []
    zero = jnp.zeros((GLA_DK_PAD, GLA_DV_PAD), BF16)
    for j, ((qk_ref, vg_ref, b_ref, out_ref, ci, d), (rows, q_in, q_st, k_in, k_end, v, decay)) in enumerate(
            zip(jobs, prep)):
        st = st_ref[d]
        if q_st is None:
            inter.append(None)
        else:
            sb = st.astype(BF16)
            s_bd = jnp.concatenate(
                [jnp.concatenate([sb[h * GLA_DK_PAD:(h + 1) * GLA_DK_PAD, :] if h2 == h else zero
                                  for h2 in range(GLA_HEADS)], axis=1) for h in range(GLA_HEADS)], axis=0)
            inter.append(_dot(q_st, s_bd))
        st_ref[d] = st * decay + kv[j]

    for j, ((qk_ref, vg_ref, b_ref, out_ref, ci, d), (rows, q_in, q_st, k_in, k_end, v, decay)) in enumerate(
            zip(jobs, prep)):
        if out_ref is not None:
            a = (att[j] * causal_ref[d]).astype(BF16)
            intra = []
            for pair in range(GLA_HEADS // 2):
                cols = slice(pair * 2 * GLA_DV_PAD, (pair + 1) * 2 * GLA_DV_PAD)
                v_pair = jnp.concatenate([v[:, cols]] * 2, axis=0) * vmask_ref[...]
                intra.append(_dot(a[:, pair * 2 * GLA_BLOCK:(pair + 1) * 2 * GLA_BLOCK], v_pair))
            out_ref[rows, :] = (inter[j] + jnp.concatenate(intra, axis=1)).astype(out_ref.dtype)


def _gla_kernel(qk_ref, vg_ref, b_ref, qkc_ref, vgc_ref, bc_ref, causal_ref, vmask_ref, *rest, t, n, need_ctx):
    if need_ctx:
        of_ref, ob_ref, ocf_ref, ocb_ref, st_ref = rest
    else:
        of_ref, ob_ref, st_ref = rest
    st_ref[...] = jnp.zeros(st_ref.shape, F32)
    nb_ctx = n // GLA_BLOCK
    nb_lat = t // GLA_BLOCK

    def jobs(refs, outs, n_blocks, i):
        return [refs + (outs[0], i, 0), refs + (outs[1], n_blocks - 1 - i, 1)]

    ctx_outs = (ocf_ref, ocb_ref) if need_ctx else (None, None)
    for i in range(nb_ctx):
        _gla_steps(jobs((qkc_ref, vgc_ref, bc_ref), ctx_outs, nb_ctx, i), st_ref, causal_ref, vmask_ref)

    def step(i, carry):
        _gla_steps(jobs((qk_ref, vg_ref, b_ref), (of_ref, ob_ref), nb_lat, i), st_ref, causal_ref, vmask_ref)
        return carry

    lax.fori_loop(0, nb_lat, step, 0, unroll=GLA_UNROLL)


def _gla(gqk, gvg, gb, gqk_c, gvg_c, gb_c, causal, vmask, *, need_ctx):
    bsz, t, _ = gqk.shape
    n = gqk_c.shape[1]
    assert t % GLA_BLOCK == 0 and n % GLA_BLOCK == 0
    row = lambda rows, w_: pl.BlockSpec((None, rows, w_), lambda b: (b, 0, 0))
    out_rows = (t, t, n, n) if need_ctx else (t, t)
    return pl.pallas_call(
        functools.partial(_gla_kernel, t=t, n=n, need_ctx=need_ctx),
        grid=(bsz,),
        in_specs=[row(t, 2 * GLA_QK_PAD), row(t, GLA_V_PAD), row(t, 2 * GLA_QK_PAD),
                  row(n, 2 * GLA_QK_PAD), row(n, GLA_V_PAD), row(n, 2 * GLA_QK_PAD),
                  _const_spec(causal.shape), _const_spec(vmask.shape)],
        out_specs=[row(r, GLA_V_PAD) for r in out_rows],
        out_shape=[jax.ShapeDtypeStruct((bsz, r, GLA_V_PAD), BF16) for r in out_rows],
        scratch_shapes=[pltpu.VMEM((2, GLA_QK_PAD, GLA_DV_PAD), F32)],
        compiler_params=_cparams(("arbitrary",)),
        name="gla_scan",
    )(gqk, gvg, gb, gqk_c, gvg_c, gb_c, causal, vmask)


def _gla_out(of_ref, ob_ref, g_ref, gain_ref, rows):
    heads = []
    for h in range(GLA_HEADS):
        cols = slice(h * GLA_DV_PAD, (h + 1) * GLA_DV_PAD)
        o = of_ref[rows, cols].astype(F32) + ob_ref[rows, cols].astype(F32)
        ms = jnp.sum(o * o, axis=-1, keepdims=True) * (1.0 / GLA_DV)
        g = g_ref[rows, cols].astype(F32)
        heads.append((o * lax.rsqrt(ms + EPS) * gain_ref[:, cols] * (g * jax.nn.sigmoid(g))).astype(BF16))
    return jnp.concatenate(heads, axis=1)


def _post_kernel(x_ref, yfn_ref, yna_ref, of_ref, ob_ref, g_ref, mod_ref, gain_o_ref, gain_ref,
                 wo_ref, w1_ref, w3_ref, w2_ref, o_ref):
    tm = x_ref.shape[0]
    tiles = [slice(r0, r0 + POST_SUB) for r0 in range(0, tm, POST_SUB)]
    ys = [(_dot(yfn_ref[rows, :], wo_ref[0:FN_W, :])
           + _dot(yna_ref[rows, :], wo_ref[FN_W:FN_W + NA_W, :])
           + _dot(_gla_out(of_ref, ob_ref, g_ref, gain_o_ref, rows), wo_ref[FN_W + NA_W:, :]))
          for rows in tiles]
    for rows, y in zip(tiles, ys):
        x1 = x_ref[rows, :] + mod_ref[2:3, :] * y
        h = _modulated_norm(x1, gain_ref[...], mod_ref[3:4, :], mod_ref[4:5, :]).astype(BF16)
        acts = []
        for c in range(D_FF // MXU_W):
            cols = slice(c * MXU_W, (c + 1) * MXU_W)
            u = _dot(h, w1_ref[:, cols])
            g = _dot(h, w3_ref[:, cols])
            acts.append((u * jax.nn.sigmoid(u) * g).astype(BF16))
        acc = _dot(jnp.concatenate(acts, axis=1), w2_ref[...])
        o_ref[rows, :] = x1 + mod_ref[5:6, :] * acc


def _post(x, yfn, yna, o_fwd, o_bwd, gvg, l, mod, mod_row, p, *, tm):
    bsz, t, d = x.shape
    assert t % tm == 0 and tm % POST_SUB == 0
    row = lambda w_: pl.BlockSpec((None, tm, w_), lambda b, i: (b, i, 0))
    gate =pl.BlockSpec((None, tm, GLA_V_PAD), lambda b, i: (b, i, 1))
    return pl.pallas_call(
        _post_kernel,
        grid=(bsz, t // tm),
        in_specs=[row(d), row(FN_W), row(NA_W), row(GLA_V_PAD), row(GLA_V_PAD), gate,
                  _mod_spec(d, l, mod_row), _layer_spec((1, GLA_V_PAD), l),
                  _layer_spec((1, d), l), _layer_spec((MIX_PAD, d), l), _layer_spec((d, D_FF), l),
                  _layer_spec((d, D_FF), l), _layer_spec((D_FF, d), l)],
        out_specs=row(d),
        out_shape=jax.ShapeDtypeStruct((bsz, t, d), F32),
        compiler_params=_cparams(("arbitrary", "arbitrary")),
        name=f"post_{t}",
    )(x, yfn, yna, o_fwd, o_bwd, gvg, mod, p["gain_o"], p["gain_f"], p["wo"], p["w1"], p["w3"], p["w2"])


def _dft_tables(n):
    idx = (np.arange(n)[:, None] * np.arange(n)[None, :]) % n
    ang = 2.0 * np.pi * idx / n
    return np.cos(ang), np.sin(ang)


def _shape_consts(t, n):
    c64, s64 = _dft_tables(FN_GD)
    eye = np.eye(FN_GROUPS)
    cs = np.concatenate([np.kron(eye, c64), np.kron(eye, s64)], axis=1)
    chunk_id = np.arange(SUB) // GLA_CHUNK
    same = chunk_id[:, None] == chunk_id[None, :]
    pos = np.arange(SUB)
    tri = np.stack([same & (pos[None, :] <= pos[:, None]), same & (pos[None, :] >= pos[:, None])])
    cpos = np.arange(GLA_BLOCK)
    fwd = cpos[None, :] <= cpos[:, None]
    gla_causal = np.stack([np.tile(fwd, (1, GLA_HEADS)), np.tile(fwd.T, (1, GLA_HEADS))])
    gla_vmask = ((np.arange(2 * GLA_BLOCK)[:, None] // GLA_BLOCK)
                 == (np.arange(2 * GLA_DV_PAD)[None, :] // GLA_DV_PAD))

    m = GLA_DK // 4
    inv = ROPE_BASE ** (-np.arange(m) / m)
    tok = np.arange(t)
    cos_h = np.ones((t, GLA_DK_PAD))
    sin_lo = np.zeros((t, GLA_DK_PAD))
    sin_hi = np.zeros((t, GLA_DK_PAD))
    for blk, p in enumerate((tok // GRID_W, tok % GRID_W)):
        ang = p[:, None] * inv[None, :]
        o = blk * 2 * m
        cos_h[:, o:o + m] = np.cos(ang)
        cos_h[:, o + m:o + 2 * m] = np.cos(ang)
        sin_lo[:, o:o + m] = -np.sin(ang)
        sin_hi[:, o + m:o + 2 * m] = np.sin(ang)
    rope = np.stack([np.tile(a, (1, GLA_HEADS)) for a in (cos_h, sin_lo, sin_hi)])

    def position_tables(length):
        cos, sin = _dft_tables(length)
        half = length // 2
        return np.stack([cos[:half, 0::2], sin[:half, 0::2], cos[:half, 1::2], sin[:half, 1::2]])

    to_bf16 = lambda a: jnp.asarray(a, F32).astype(BF16)
    return {
        "cs": to_bf16(cs), "tri": jnp.asarray(tri, BF16),
        "gla_causal": jnp.asarray(gla_causal, F32), "gla_vmask": jnp.asarray(gla_vmask, BF16),
        "rope": jnp.asarray(rope, F32),
        "dft_lat": to_bf16(position_tables(t)), "dft_ctx": to_bf16(position_tables(n)),
    }


def _bias_kernel(rpb_ref, onehot_ref, ok_ref, o_ref):
    r = rpb_ref[...]
    onehot = onehot_ref[...]
    t = jnp.zeros(o_ref.shape, F32)
    for _ in range(3):
        piece = r.astype(BF16)
        t = t + _dot(piece, onehot)
        r = r - piece.astype(F32)
    o_ref[...] = jnp.where(ok_ref[...] > 0.0, t * LOG2E, NA_MASKED)


def _window_bias_rows(na_rpb):
    depth, heads, n_dr, n_dc = na_rpb.shape
    cq = np.arange(GRID_W)
    c0 = np.clip(cq - NA_KW // 2, 0, GRID_W - NA_KW)
    col_ok = (cq[None, :] >= c0[:, None]) & (cq[None, :] < c0[:, None] + NA_KW)
    dc = np.clip(cq[None, :] - cq[:, None], -(NA_KW - 1), NA_KW - 1) + NA_KW - 1
    n_dc_pad = n_dc + 1
    n_tab = depth * heads * n_dr
    rows_pad = -(-n_tab // 8) * 8
    onehot = (np.arange(n_dc_pad)[:, None, None] == dc[None]).reshape(n_dc_pad, GRID_W * GRID_W)
    rpb2 = jnp.pad(na_rpb.reshape(n_tab, n_dc), ((0, rows_pad - n_tab), (0, 1)))
    t = pl.pallas_call(
        _bias_kernel,
        out_shape=jax.ShapeDtypeStruct((rows_pad, GRID_W * GRID_W), F32),
        name="na_bias",
    )(rpb2, jnp.asarray(onehot, BF16), jnp.asarray(col_ok.reshape(1, -1), F32))
    t = t[:n_tab].reshape(depth * heads, n_dr, GRID_W, GRID_W)
    t = t.transpose(0, 1, 3, 2).reshape(depth * heads, n_dr * GRID_W, GRID_W)
    margin = NA_WIN - NA_KH
    t = jnp.pad(t, ((0, 0), (margin * GRID_W, margin * GRID_W), (0, 0)))
    return t.reshape(depth, heads, (n_dr + 2 * margin) * GRID_W, GRID_W)


def _pad_heads(w, heads, dim, pad, axis=-1):
    axis = axis % w.ndim
    shape = w.shape
    w = w.reshape(shape[:axis] + (heads, dim) + shape[axis + 1:])
    widths = [(0, 0)] * w.ndim
    widths[axis + 1] = (0, pad - dim)
    return jnp.pad(w, widths).reshape(shape[:axis] + (heads * pad,) + shape[axis + 1:])


def _params(norm_mix, norm_ffn, w_in, fnet_w, na_q_norm, na_k_norm, na_rpb, alpha_w, alpha_b, o_norm, w_out,
            ffn_w1, ffn_w3, ffn_w2):
    depth = w_in.shape[0]
    o = FN_W + 3 * NA_W
    gq = w_in[..., o:o + GLA_QK_W]
    gk = w_in[..., o + GLA_QK_W:o + 2 * GLA_QK_W]
    o2 = o + 2 * GLA_QK_W
    gv = w_in[..., o2:o2 + GLA_V_W]
    gg = w_in[..., o2 + GLA_V_W:o2 + 2 * GLA_V_W]
    ga = w_in[..., o2 + 2 * GLA_V_W:]
    w = jnp.concatenate([
        w_in[..., :o],
        jnp.pad(ga, ((0, 0), (0, 0), (0, GLA_A_PAD - 2 * GLA_RANK))),
        _pad_heads(gq, GLA_HEADS, GLA_DK, GLA_DK_PAD), _pad_heads(gk, GLA_HEADS, GLA_DK, GLA_DK_PAD),
        _pad_heads(gv, GLA_HEADS, GLA_DV, GLA_DV_PAD), _pad_heads(gg, GLA_HEADS, GLA_DV, GLA_DV_PAD),
    ], axis=-1).astype(BF16)

    qkg = jnp.stack([jnp.tile(na_q_norm, (1, NA_HEADS)) * (NA_HD ** -0.5 * LOG2E),
                     jnp.tile(na_k_norm, (1, NA_HEADS))], axis=1)

    aw_pad = _pad_heads(alpha_w, GLA_HEADS, GLA_DK, GLA_DK_PAD)
    aw = jnp.concatenate([jnp.pad(aw_pad[:, 0], ((0, 0), (0, 0), (0, GLA_QK_PAD))),
                          jnp.pad(aw_pad[:, 1], ((0, 0), (0, 0), (GLA_QK_PAD, 0)))], axis=1)
    aw = jnp.pad(aw, ((0, 0), (0, GLA_A_PAD - 2 * GLA_RANK), (0, 0))).astype(BF16)
    ab = _pad_heads(alpha_b, GLA_HEADS, GLA_DK, GLA_DK_PAD).reshape(depth, 1, 2 * GLA_QK_PAD)

    gain_o = _pad_heads(jnp.tile(o_norm, (1, GLA_HEADS)), GLA_HEADS, GLA_DV, GLA_DV_PAD)[:, None, :]
    wo = jnp.concatenate([w_out[:, :FN_W + NA_W],
                          _pad_heads(w_out[:, FN_W + NA_W:], GLA_HEADS, GLA_DV, GLA_DV_PAD, axis=1)],
                         axis=1).astype(BF16)
    return {
        "gain_m": norm_mix[:, None, :], "gain_f": norm_ffn[:, None, :], "w": w, "qkg": qkg, "aw": aw, "ab": ab,
        "rpb_rows": _window_bias_rows(na_rpb), "gain_o": gain_o, "wo": wo, "fw": fnet_w.astype(BF16),
        "w1": ffn_w1.astype(BF16), "w3": ffn_w3.astype(BF16), "w2": ffn_w2.astype(BF16),
    }


def kernel(x, c, ctx, c_ctx, ada_w, ada_b, norm_mix, norm_ffn, w_in, fnet_w, na_q_norm, na_k_norm, na_rpb,
           gla_alpha_w, gla_alpha_b, gla_o_norm, w_out, ffn_w1, ffn_w3, ffn_w2):
    bsz, t, d = x.shape
    n = ctx.shape[1]
    depth = ada_w.shape[0]
    ctx_row = bsz
    assert bsz < MOD_ROWS and n % SUB == 0 and (bsz * n) % CTX_TM == 0
    consts = _shape_consts(t, n)
    p = _params(norm_mix, norm_ffn, w_in, fnet_w, na_q_norm, na_k_norm, na_rpb, gla_alpha_w, gla_alpha_b,
                gla_o_norm, w_out, ffn_w1, ffn_w3, ffn_w2)

    cc = jnp.concatenate([c, c_ctx[None, :], jnp.zeros((MOD_ROWS - bsz - 1, d), F32)])
    mod = _ada_mod(cc, ada_w, ada_b).reshape(depth, MOD_ROWS, 6, d)

    cx = ctx
    flat = lambda a: a.reshape(1, bsz * n, a.shape[-1])
    for l in range(depth):
        need_ctx = l < depth - 1
        uv, nqkv, gqk, gvg, gb = _inproj(x, l, mod, None, p, consts, consts["rope"], tm=INPROJ_TM)
        uv_c, nqkv_c, gqk_c, gvg_c, gb_c = _inproj(cx, l, mod, ctx_row, p, consts, None, tm=n)

        y_fn = _fourier(uv, l, consts["dft_lat"], p["fw"])
        y_na, y_na_c = _na(nqkv, nqkv_c, l, p["rpb_rows"], need_ctx=need_ctx)
        o_gla = _gla(gqk, gvg, gb, gqk_c, gvg_c, gb_c, consts["gla_causal"], consts["gla_vmask"],
                     need_ctx=need_ctx)
        x = _post(x, y_fn, y_na, o_gla[0], o_gla[1], gvg, l, mod, None, p, tm=POST_TM)
        if need_ctx:
            y_fn_c = _fourier(uv_c, l, consts["dft_ctx"], p["fw"])
            cx = _post(flat(cx), flat(y_fn_c), flat(y_na_c), flat(o_gla[2]), flat(o_gla[3]), flat(gvg_c),
                       l, mod, ctx_row, p, tm=CTX_TM).reshape(bsz, n, d)
    return x
```

```python
import functools

import numpy as np
import jax
import jax.numpy as jnp
from jax import lax
from jax.experimental import pallas as pl
from jax.experimental.pallas import tpu as pltpu

F32 = jnp.float32
BF16 = jnp.bfloat16

D_MODEL = 1024
GRID_W = 64
FN_W = 256
FN_GROUPS = 4
FN_GD = 64
NA_HD = 64
NA_W = 384
NA_HEADS = 6
NA_KH = 8
NA_KW = 16
NA_GROUP = 4
NA_WIN = NA_GROUP + NA_KH
LOG2E = float(np.log2(np.e))
NA_MASKED = -1e30
GLA_HEADS = 4
GLA_DK = 48
GLA_DV = 96
GLA_QK_W = GLA_HEADS * GLA_DK
GLA_V_W = GLA_HEADS * GLA_DV
GLA_RANK = 16
GLA_GATE_NORM = 16.0
GLA_CHUNK = 64
GLA_BLOCK = 2 * GLA_CHUNK
GLA_UNROLL = 4
ROPE_BASE = 10000.0
D_FF = 2816
EPS = 1e-6
MOD_ROWS = 16

LANES = 128
MXU_W = 256
VMEM_LIMIT_BYTES = 56 * 1024 * 1024

GLA_DK_PAD = 64
GLA_DV_PAD = 128
GLA_QK_PAD = GLA_HEADS * GLA_DK_PAD
GLA_V_PAD = GLA_HEADS * GLA_DV_PAD
GLA_A_PAD = LANES
MIX_PAD = FN_W + NA_W + GLA_V_PAD

C_FX = 0
C_NQ = C_FX + FN_W
C_NK = C_NQ + NA_W
C_NV = C_NK + NA_W
C_GA = C_NV + NA_W
C_GQ = C_GA + GLA_A_PAD
C_GK = C_GQ + GLA_QK_PAD
C_GV = C_GK + GLA_QK_PAD
C_GG = C_GV + GLA_V_PAD
N_IN_PAD = C_GG + GLA_V_PAD

SUB = 256

INPROJ_TM = 1024
POST_TM = 1024
POST_SUB = 256
CTX_TM = 512
ADA_TN = 1024


def _cparams(sem):
    return pltpu.CompilerParams(dimension_semantics=sem, vmem_limit_bytes=VMEM_LIMIT_BYTES)


def _const_spec(shape):
    nd = len(shape)
    return pl.BlockSpec(tuple(shape), lambda *_: (0,) * nd, pipeline_mode=pl.Buffered(1))


def _layer_spec(shape, l):
    nd = len(shape)
    return pl.BlockSpec((None,) + tuple(shape), lambda *_: (l,) + (0,) * nd, pipeline_mode=pl.Buffered(1))


def _mod_spec(d, l, mod_row):
    if mod_row is None:
        return pl.BlockSpec((None, None, 6, d), lambda b, i: (l, b, 0, 0))
    return pl.BlockSpec((None, None, 6, d), lambda b, i: (l, mod_row, 0, 0))


def _dot(a, b):
    return jnp.dot(a, b, preferred_element_type=F32)


def _dot_nt(a, b):
    return lax.dot_general(a, b, (((1,), (1,)), ((), ())), preferred_element_type=F32)


def _ada_kernel(c_ref, w_ref, b_ref, o_ref):
    a = c_ref[...]
    a = (a * jax.nn.sigmoid(a)).astype(BF16)
    o_ref[...] = _dot(a, w_ref[...].astype(BF16)) + b_ref[...]


def _ada_mod(cc, ada_w, ada_b):
    depth, d, n = ada_w.shape
    rows = cc.shape[0]
    tn = ADA_TN
    return pl.pallas_call(
        _ada_kernel,
        grid=(depth, n // tn),
        in_specs=[
            pl.BlockSpec((rows, d), lambda l, j: (0, 0)),
            pl.BlockSpec((None, d, tn), lambda l, j: (l, 0, j)),
            pl.BlockSpec((None, 1, tn), lambda l, j: (l, 0, j)),
        ],
        out_specs=pl.BlockSpec((None, rows, tn), lambda l, j: (l, 0, j)),
        out_shape=jax.ShapeDtypeStruct((depth, rows, n), F32),
        compiler_params=_cparams(("arbitrary", "arbitrary")),
        name="ada_mod",
    )(cc, ada_w, ada_b.reshape(depth, 1, n))


def _modulated_norm(x, gain, shift, scale):
    ms = jnp.mean(x * x, axis=-1, keepdims=True)
    return (x * lax.rsqrt(ms + EPS) * gain) * (1.0 + scale) + shift


def _log_sigmoid(x):
    return jnp.minimum(x, 0.0) - jnp.log1p(jnp.exp(-jnp.abs(x)))


def _head_mean_sq(z):
    first = lax.broadcasted_iota(jnp.int32, (1, LANES), 1) < NA_HD
    out = []
    for t in range(NA_W // LANES):
        sq = z[:, t * LANES:(t + 1) * LANES]
        sq = sq * sq
        lo = jnp.sum(jnp.where(first, sq, 0.0), axis=-1, keepdims=True)
        hi = jnp.sum(jnp.where(first, 0.0, sq), axis=-1, keepdims=True)
        out.append(jnp.where(first, lo, hi))
    return jnp.concatenate(out, axis=1) * (1.0 / NA_HD)


def _inproj_kernel(x_ref, mod_ref, gain_ref, w_ref, cs_ref, qkg_ref, rope_ref, aw_ref, ab_ref, tri_ref,
                   uv_ref, nqkv_ref, gqk_ref, gvg_ref, gb_ref, *, rope, tm):
    x = x_ref[...]
    h = _modulated_norm(x, gain_ref[...], mod_ref[0:1, :], mod_ref[1:2, :]).astype(BF16)

    zf = _dot(h, w_ref[:, C_FX:C_NQ]).astype(BF16)
    zva = _dot(h, w_ref[:, C_NV:C_GQ])
    nqkv_ref[:, 2 * NA_W:3 * NA_W] = zva[:, :NA_W].astype(BF16)
    a = zva[:, NA_W:].astype(BF16)
    zqk = _dot(h, w_ref[:, C_NQ:C_NV])
    logits = _dot(a, aw_ref[...]) + ab_ref[...]
    zg = _dot(h, w_ref[:, C_GQ:C_GV])
    uv = _dot(zf, cs_ref[...])
    for j in range(uv_ref.shape[0]):
        uv_ref[j] = uv[:, j * LANES:(j + 1) * LANES]
    gvg_ref[...] = _dot(h, w_ref[:, C_GV:N_IN_PAD]).astype(BF16)

    glog = (_log_sigmoid(logits) * (1.0 / GLA_GATE_NORM)).astype(BF16)
    for s in range(tm // SUB):
        rows = slice(s * SUB, (s + 1) * SUB)
        gb_ref[rows, 0:GLA_QK_PAD] = _dot(tri_ref[0], glog[rows, 0:GLA_QK_PAD])
        gb_ref[rows, GLA_QK_PAD:] = _dot(tri_ref[1], glog[rows, GLA_QK_PAD:])

    for i in range(2):
        z = zqk[:, i * NA_W:(i + 1) * NA_W]
        z = z * lax.rsqrt(_head_mean_sq(z) + EPS) * qkg_ref[i:i + 1, :]
        nqkv_ref[:, i * NA_W:(i + 1) * NA_W] = z.astype(BF16)

    for i in range(2):
        z = zg[:, i * GLA_QK_PAD:(i + 1) * GLA_QK_PAD]
        if rope:
            z = (z * rope_ref[0]
                 + pltpu.roll(z, GLA_QK_PAD - GLA_DK // 4, 1) * rope_ref[1]
                 + pltpu.roll(z, GLA_DK // 4, 1) * rope_ref[2])
        if i == 0:
            z = z * (GLA_DK ** -0.5)
        gqk_ref[:, i * GLA_QK_PAD:(i + 1) * GLA_QK_PAD] = z.astype(BF16)


def _inproj(x, l, mod, mod_row, p, consts, rope_tab, *, tm):
    bsz, t, d = x.shape
    assert t % tm == 0 and tm % SUB == 0
    rope = rope_tab is not None
    if not rope:
        rope_tab = jnp.zeros((3, tm, GLA_QK_PAD), F32)
    rope_idx = (lambda b, i: (0, i, 0)) if rope else (lambda b, i: (0, 0, 0))
    row = lambda w_: pl.BlockSpec((None, tm, w_), lambda b, i: (b, i, 0))
    uv_tiles = 2 * FN_W // LANES
    out_w = (3 * NA_W, 2 * GLA_QK_PAD, 2 * GLA_V_PAD, 2 * GLA_QK_PAD)
    out_dt = (BF16, BF16, BF16, F32)
    return pl.pallas_call(
        functools.partial(_inproj_kernel, rope=rope, tm=tm),
        grid=(bsz, t // tm),
        in_specs=[
            row(d),
            _mod_spec(d, l, mod_row),
            _layer_spec((1, d), l),
            _layer_spec((d, N_IN_PAD), l),
            _const_spec(consts["cs"].shape),
            _layer_spec((2, NA_W), l),
            pl.BlockSpec((3, tm, GLA_QK_PAD), rope_idx),
            _layer_spec((GLA_A_PAD, 2 * GLA_QK_PAD), l),
            _layer_spec((1, 2 * GLA_QK_PAD), l),
            _const_spec((2, SUB, SUB)),
        ],
        out_specs=[pl.BlockSpec((None, uv_tiles, tm, LANES), lambda b, i: (b, 0, i, 0))]
        + [row(w_) for w_ in out_w],
        out_shape=[jax.ShapeDtypeStruct((bsz, uv_tiles, t, LANES), F32)]
        + [jax.ShapeDtypeStruct((bsz, t, w_), dt) for w_, dt in zip(out_w, out_dt)],
        compiler_params=_cparams(("arbitrary", "arbitrary")),
        name="inproj_rope" if rope else "inproj_ctx",
    )(x, mod, p["gain_m"], p["w"], consts["cs"], p["qkg"], rope_tab, p["aw"], p["ab"], consts["tri"])


def _fourier_kernel(tab_ref, uv_ref, w_ref, o_ref, *, scale):
    half = uv_ref.shape[1] // 2

    def positions(parity):
        tiles = [uv_ref[j, pl.ds(parity, half, stride=2), :].astype(BF16) for j in range(uv_ref.shape[0])]
        return jnp.concatenate(tiles[:2], axis=1), jnp.concatenate(tiles[2:], axis=1)

    ue, ve = positions(0)
    uo, vo = positions(1)
    even = _dot(tab_ref[0], ue) - _dot(tab_ref[1], ve)
    odd = _dot(tab_ref[2], uo) - _dot(tab_ref[3], vo)
    w = w_ref[...]
    o_ref[0:half, :] = _dot(((even + odd) * scale).astype(BF16), w).astype(BF16)
    o_ref[half:, :] = _dot(((even - odd) * scale).astype(BF16), w).astype(BF16)


def _fourier(uv, l, tabs, fnet_w):
    bsz, tiles, t, _ = uv.shape
    scale = float((t * FN_GD) ** -0.5)
    return pl.pallas_call(
        functools.partial(_fourier_kernel, scale=scale),
        grid=(bsz,),
        in_specs=[
            _const_spec(tabs.shape),
            pl.BlockSpec((None, tiles, t, LANES), lambda b: (b, 0, 0, 0)),
            _layer_spec((FN_W, FN_W), l),
        ],
        out_specs=pl.BlockSpec((None, t, FN_W), lambda b: (b, 0, 0)),
        out_shape=jax.ShapeDtypeStruct((bsz, t, FN_W), BF16),
        compiler_params=_cparams(("arbitrary",)),
        name=f"fourier_{t}",
    )(tabs, uv, fnet_w)


def _na_softmax_t(s):
    return jnp.exp2(s - jnp.max(s, axis=0, keepdims=True)).astype(BF16)


def _na_heads_out_t(o0, o1):
    row = lax.broadcasted_iota(jnp.int32, o0.shape, 0)
    out_t = jnp.where(row < NA_HD, o0 / o0[NA_HD:NA_HD + 1, :], o1 / o1[0:1, :])
    return out_t.T.astype(BF16)


def _na_group_kinds():
    margin = NA_WIN - NA_KH
    return ((0, lambda i: 0), (NA_KH // 2, lambda i: i), (margin + NA_KH - NA_GROUP, lambda i: margin))


def _na_kernel(q_ref, k_ref, v_ref, qc_ref, kc_ref, vc_ref, rpb_ref, o_ref, oc_ref,
               vt_ref, sc_ref, bias_ref, *, n_rows, need_ctx):
    lane = lax.broadcasted_iota(jnp.int32, (1, LANES), 1)
    first = lane < NA_HD
    zero = jnp.zeros((), BF16)
    one = jnp.ones((), BF16)
    v = v_ref[...]
    vt_ref[0] = jnp.where(first, v, one).T
    vt_ref[1] = jnp.where(first, one, v).T
    kc = kc_ref[...]
    vc = vc_ref[...]
    vct = (jnp.where(first, vc, one).T, jnp.where(first, one, vc).T)

    def both_heads(q):
        return jnp.concatenate([jnp.where(first, q, zero), jnp.where(first, zero, q)], axis=0)

    n_groups = n_rows // NA_GROUP
    gq = NA_GROUP * GRID_W
    n_loc = NA_WIN * GRID_W
    margin = NA_WIN - NA_KH

    @pl.when(pl.program_id(1) == 0)
    def _():
        wrow = lax.broadcasted_iota(jnp.int32, (n_loc, GRID_W), 0)
        for hh in range(2):
            for kind, (q_off, a_lo_of) in enumerate(_na_group_kinds()):
                for j in range(NA_GROUP // 2):
                    halves = []
                    for i in (2 * j, 2 * j + 1):
                        start = (NA_KH - 1 - q_off - i + margin) * GRID_W
                        lo = a_lo_of(i) * GRID_W
                        band = (wrow >= lo) & (wrow < lo + NA_KH * GRID_W)
                        halves.append(jnp.where(band, rpb_ref[hh, start:start + n_loc, :], NA_MASKED))
                    bias_ref[hh, kind, :, j * LANES:(j + 1) * LANES] = jnp.concatenate(halves, axis=1)

    def window(g):
        ws = jnp.clip(g * NA_GROUP - NA_KH // 2, 0, n_rows - NA_WIN)
        return (pl.ds(pl.multiple_of(g * gq, gq), gq), pl.ds(pl.multiple_of(ws * GRID_W, gq), n_loc))

    def scores(g, slot):
        qs, ks = window(g)
        kind = jnp.where(g == 0, 0, jnp.where(g == n_groups - 1, 2, 1))
        q = both_heads(q_ref[qs, :])
        s_loc = _dot_nt(k_ref[ks, :], q)
        s_ctx = _dot_nt(kc, q)
        for hh in range(2):
            cols = slice(hh * gq, (hh + 1) * gq)
            sc_ref[slot, 0:n_loc, cols] = s_loc[:, cols] + bias_ref[hh, kind]
        sc_ref[slot, n_loc:, :] = s_ctx

    def finish(g, slot):
        qs, ks = window(g)
        p = _na_softmax_t(sc_ref[slot])
        outs = [_dot(jnp.concatenate([vt_ref[hh, :, ks], vct[hh]], axis=1), p[:, hh * gq:(hh + 1) * gq])
                for hh in range(2)]
        o_ref[qs, :] = _na_heads_out_t(outs[0], outs[1])

    scores(0, 0)

    def pair(i, carry):
        scores(2 * i + 1, 1)
        finish(2 * i, 0)
        scores(2 * i + 2, 0)
        finish(2 * i + 1, 1)
        return carry

    lax.fori_loop(0, n_groups // 2 - 1, pair, 0)
    scores(n_groups - 1, 1)
    finish(n_groups - 2, 0)
    finish(n_groups - 1, 1)

    if need_ctx:
        n_ctx = qc_ref.shape[0]
        p = _na_softmax_t(_dot_nt(kc, both_heads(qc_ref[...])))
        outs = [_dot(vct[hh], p[:, hh * n_ctx:(hh + 1) * n_ctx]) for hh in range(2)]
        oc_ref[...] = _na_heads_out_t(outs[0], outs[1])
    else:
        oc_ref[...] = jnp.zeros(oc_ref.shape, oc_ref.dtype)


def _na(nqkv, nqkv_c, l, rpb_rows, *, need_ctx):
    bsz, t, _ = nqkv.shape
    n = nqkv_c.shape[1]
    n_rows = t // GRID_W
    assert n_rows % (2 * NA_GROUP) == 0 and n_rows >= NA_WIN + NA_GROUP
    pairs = NA_HEADS // 2
    n_loc = NA_WIN * GRID_W
    gq = NA_GROUP * GRID_W
    nb = NA_W // LANES
    lat = lambda off: pl.BlockSpec((None, t, LANES), lambda j, b: (b, 0, off + j))
    ctx = lambda off: pl.BlockSpec((None, n, LANES), lambda j, b: (b, 0, off + j))
    return pl.pallas_call(
        functools.partial(_na_kernel, n_rows=n_rows, need_ctx=need_ctx),
        grid=(pairs, bsz),
        in_specs=[lat(0), lat(nb), lat(2 * nb), ctx(0), ctx(nb), ctx(2 * nb),
                  pl.BlockSpec((None, 2) + rpb_rows.shape[2:], lambda j, b: (l, j, 0, 0))],
        out_specs=[pl.BlockSpec((None, t, LANES), lambda j, b: (b, 0, j)),
                   pl.BlockSpec((None, n, LANES), lambda j, b: (b, 0, j))],
        out_shape=[jax.ShapeDtypeStruct((bsz, t, NA_W), BF16),
                   jax.ShapeDtypeStruct((bsz, n, NA_W), BF16)],
        scratch_shapes=[pltpu.VMEM((2, LANES, t), BF16),
                        pltpu.VMEM((2, n_loc + n, 2 * gq), F32),
                        pltpu.VMEM((2, len(_na_group_kinds()), n_loc, gq), F32)],
        compiler_params=_cparams(("arbitrary", "arbitrary")),
        name="na_attn",
    )(nqkv, nqkv, nqkv, nqkv_c, nqkv_c, nqkv_c, rpb_rows)


def _gla_steps(jobs, st_ref, causal_ref, vmask_ref):
    c = GLA_CHUNK
    head = lax.broadcasted_iota(jnp.int32, (1, GLA_QK_PAD), 1) // GLA_DK_PAD
    prep = []
    for qk_ref, vg_ref, b_ref, out_ref, ci, d in jobs:
        rows = pl.ds(pl.multiple_of(ci * GLA_BLOCK, GLA_BLOCK), GLA_BLOCK)
        k = qk_ref[rows, GLA_QK_PAD:].astype(F32)
        v = vg_ref[rows, 0:GLA_V_PAD]
        b = b_ref[rows, d * GLA_QK_PAD:(d + 1) * GLA_QK_PAD]
        lo, hi = b[0:c, :], b[c:, :]
        if d == 0:
            tot1, tot2 = lo[c - 1:c, :], hi[c - 1:c, :]
            b_rel = jnp.concatenate([lo - tot1, hi], axis=0)
        else:
            tot1, tot2 = hi[0:1, :], lo[0:1, :]
            b_rel = jnp.concatenate([lo, hi - tot1], axis=0)
        k_in = k * jnp.exp(-b_rel)
        k_end = k_in * jnp.exp(tot2)
        q_in = q_st = None
        if out_ref is not None:
            q_rel = qk_ref[rows, 0:GLA_QK_PAD].astype(F32) * jnp.exp(b_rel)
            q_in = q_rel.astype(BF16)
            q_st = (q_rel * jnp.exp(tot1)).astype(BF16)
        decay = jnp.exp(jnp.broadcast_to(tot1 + tot2, (GLA_DV_PAD, GLA_QK_PAD)).T)
        prep.append((rows, q_in, q_st, k_in, k_end, v, decay))

    att = [None if q_in is None else
           _dot_nt(q_in, jnp.concatenate([jnp.where(head == h, k_in, 0.0) for h in range(GLA_HEADS)],
                                         axis=0).astype(BF16))
           for (rows, q_in, q_st, k_in, k_end, v, decay) in prep]

    kv = []
    for (rows, q_in, q_st, k_in, k_end, v, decay) in prep:
        k_t = k_end.astype(BF16).T
        kv.append(jnp.concatenate(
            [_dot(k_t[h * GLA_DK_PAD:(h + 1) * GLA_DK_PAD, :], v[:, h * GLA_DV_PAD:(h + 1) * GLA_DV_PAD])
             for h in range(GLA_HEADS)], axis=0))

    inter = []
    zero = jnp.zeros((GLA_DK_PAD, GLA_DV_PAD), BF16)
    for j, ((qk_ref, vg_ref, b_ref, out_ref, ci, d), (rows, q_in, q_st, k_in, k_end, v, decay)) in enumerate(
            zip(jobs, prep)):
        st = st_ref[d]
        if q_st is None:
            inter.append(None)
        else:
            sb = st.astype(BF16)
            s_bd = jnp.concatenate(
                [jnp.concatenate([sb[h * GLA_DK_PAD:(h + 1) * GLA_DK_PAD, :] if h2 == h else zero
                                  for h2 in range(GLA_HEADS)], axis=1) for h in range(GLA_HEADS)], axis=0)
            inter.append(_dot(q_st, s_bd))
        st_ref[d] = st * decay + kv[j]

    for j, ((qk_ref, vg_ref, b_ref, out_ref, ci, d), (rows, q_in, q_st, k_in, k_end, v, decay)) in enumerate(
            zip(jobs, prep)):
        if out_ref is not None:
            a = (att[j] * causal_ref[d]).astype(BF16)
            intra = []
            for pair in range(GLA_HEADS // 2):
                cols = slice(pair * 2 * GLA_DV_PAD, (pair + 1) * 2 * GLA_DV_PAD)
                v_pair = jnp.concatenate([v[:, cols]] * 2, axis=0) * vmask_ref[...]
                intra.append(_dot(a[:, pair * 2 * GLA_BLOCK:(pair + 1) * 2 * GLA_BLOCK], v_pair))
            out_ref[rows, :] = (inter[j] + jnp.concatenate(intra, axis=1)).astype(out_ref.dtype)


def _gla_kernel(qk_ref, vg_ref, b_ref, qkc_ref, vgc_ref, bc_ref, causal_ref, vmask_ref, *rest, t, n, need_ctx):
    if need_ctx:
        of_ref, ob_ref, ocf_ref, ocb_ref, st_ref = rest
    else:
        of_ref, ob_ref, st_ref = rest
    st_ref[...] = jnp.zeros(st_ref.shape, F32)
    nb_ctx = n // GLA_BLOCK
    nb_lat = t // GLA_BLOCK

    def jobs(refs, outs, n_blocks, i):
        return [refs + (outs[0], i, 0), refs + (outs[1], n_blocks - 1 - i, 1)]

    ctx_outs = (ocf_ref, ocb_ref) if need_ctx else (None, None)
    for i in range(nb_ctx):
        _gla_steps(jobs((qkc_ref, vgc_ref, bc_ref), ctx_outs, nb_ctx, i), st_ref, causal_ref, vmask_ref)

    def step(i, carry):
        _gla_steps(jobs((qk_ref, vg_ref, b_ref), (of_ref, ob_ref), nb_lat, i), st_ref, causal_ref, vmask_ref)
        return carry

    lax.fori_loop(0, nb_lat, step, 0, unroll=GLA_UNROLL)


def _gla(gqk, gvg, gb, gqk_c, gvg_c, gb_c, causal, vmask, *, need_ctx):
    bsz, t, _ = gqk.shape
    n = gqk_c.shape[1]
    assert t % GLA_BLOCK == 0 and n % GLA_BLOCK == 0
    row = lambda rows, w_: pl.BlockSpec((None, rows, w_), lambda b: (b, 0, 0))
    out_rows = (t, t, n, n) if need_ctx else (t, t)
    return pl.pallas_call(
        functools.partial(_gla_kernel, t=t, n=n, need_ctx=need_ctx),
        grid=(bsz,),
        in_specs=[row(t, 2 * GLA_QK_PAD), row(t, GLA_V_PAD), row(t, 2 * GLA_QK_PAD),
                  row(n, 2 * GLA_QK_PAD), row(n, GLA_V_PAD), row(n, 2 * GLA_QK_PAD),
                  _const_spec(causal.shape), _const_spec(vmask.shape)],
        out_specs=[row(r, GLA_V_PAD) for r in out_rows],
        out_shape=[jax.ShapeDtypeStruct((bsz, r, GLA_V_PAD), BF16) for r in out_rows],
        scratch_shapes=[pltpu.VMEM((2, GLA_QK_PAD, GLA_DV_PAD), F32)],
        compiler_params=_cparams(("arbitrary",)),
        name="gla_scan",
    )(gqk, gvg, gb, gqk_c, gvg_c, gb_c, causal, vmask)


def _gla_out(of_ref, ob_ref, g_ref, gain_ref, rows):
    heads = []
    for h in range(GLA_HEADS):
        cols = slice(h * GLA_DV_PAD, (h + 1) * GLA_DV_PAD)
        o = of_ref[rows, cols].astype(F32) + ob_ref[rows, cols].astype(F32)
        ms = jnp.sum(o * o, axis=-1, keepdims=True) * (1.0 / GLA_DV)
        g = g_ref[rows, cols].astype(F32)
        heads.append((o * lax.rsqrt(ms + EPS) * gain_ref[:, cols] * (g * jax.nn.sigmoid(g))).astype(BF16))
    return jnp.concatenate(heads, axis=1)


def _post_kernel(x_ref, yfn_ref, yna_ref, of_ref, ob_ref, g_ref, mod_ref, gain_o_ref, gain_ref,
                 wo_ref, w1_ref, w3_ref, w2_ref, o_ref):
    tm = x_ref.shape[0]
    tiles = [slice(r0, r0 + POST_SUB) for r0 in range(0, tm, POST_SUB)]
    ys = [(_dot(yfn_ref[rows, :], wo_ref[0:FN_W, :])
           + _dot(yna_ref[rows, :], wo_ref[FN_W:FN_W + NA_W, :])
           + _dot(_gla_out(of_ref, ob_ref, g_ref, gain_o_ref, rows), wo_ref[FN_W + NA_W:, :]))
          for rows in tiles]
    for rows, y in zip(tiles, ys):
        x1 = x_ref[rows, :] + mod_ref[2:3, :] * y
        h = _modulated_norm(x1, gain_ref[...], mod_ref[3:4, :], mod_ref[4:5, :]).astype(BF16)
        acts = []
        for c in range(D_FF // MXU_W):
            cols = slice(c * MXU_W, (c + 1) * MXU_W)
            u = _dot(h, w1_ref[:, cols])
            g = _dot(h, w3_ref[:, cols])
            acts.append((u * jax.nn.sigmoid(u) * g).astype(BF16))
        acc = _dot(jnp.concatenate(acts, axis=1), w2_ref[...])
        o_ref[rows, :] = x1 + mod_ref[5:6, :] * acc


def _post(x, yfn, yna, o_fwd, o_bwd, gvg, l, mod, mod_row, p, *, tm):
    bsz, t, d = x.shape
    assert t % tm == 0 and tm % POST_SUB == 0
    row = lambda w_: pl.BlockSpec((None, tm, w_), lambda b, i: (b, i, 0))
    gate =pl.BlockSpec((None, tm, GLA_V_PAD), lambda b, i: (b, i, 1))
    return pl.pallas_call(
        _post_kernel,
        grid=(bsz, t // tm),
        in_specs=[row(d), row(FN_W), row(NA_W), row(GLA_V_PAD), row(GLA_V_PAD), gate,
                  _mod_spec(d, l, mod_row), _layer_spec((1, GLA_V_PAD), l),
                  _layer_spec((1, d), l), _layer_spec((MIX_PAD, d), l), _layer_spec((d, D_FF), l),
                  _layer_spec((d, D_FF), l), _layer_spec((D_FF, d), l)],
        out_specs=row(d),
        out_shape=jax.ShapeDtypeStruct((bsz, t, d), F32),
        compiler_params=_cparams(("arbitrary", "arbitrary")),
        name=f"post_{t}",
    )(x, yfn, yna, o_fwd, o_bwd, gvg, mod, p["gain_o"], p["gain_f"], p["wo"], p["w1"], p["w3"], p["w2"])


def _dft_tables(n):
    idx = (np.arange(n)[:, None] * np.arange(n)[None, :]) % n
    ang = 2.0 * np.pi * idx / n
    return np.cos(ang), np.sin(ang)


def _shape_consts(t, n):
    c64, s64 = _dft_tables(FN_GD)
    eye = np.eye(FN_GROUPS)
    cs = np.concatenate([np.kron(eye, c64), np.kron(eye, s64)], axis=1)
    chunk_id = np.arange(SUB) // GLA_CHUNK
    same = chunk_id[:, None] == chunk_id[None, :]
    pos = np.arange(SUB)
    tri = np.stack([same & (pos[None, :] <= pos[:, None]), same & (pos[None, :] >= pos[:, None])])
    cpos = np.arange(GLA_BLOCK)
    fwd = cpos[None, :] <= cpos[:, None]
    gla_causal = np.stack([np.tile(fwd, (1, GLA_HEADS)), np.tile(fwd.T, (1, GLA_HEADS))])
    gla_vmask = ((np.arange(2 * GLA_BLOCK)[:, None] // GLA_BLOCK)
                 == (np.arange(2 * GLA_DV_PAD)[None, :] // GLA_DV_PAD))

    m = GLA_DK // 4
    inv = ROPE_BASE ** (-np.arange(m) / m)
    tok = np.arange(t)
    cos_h = np.ones((t, GLA_DK_PAD))
    sin_lo = np.zeros((t, GLA_DK_PAD))
    sin_hi = np.zeros((t, GLA_DK_PAD))
    for blk, p in enumerate((tok // GRID_W, tok % GRID_W)):
        ang = p[:, None] * inv[None, :]
        o = blk * 2 * m
        cos_h[:, o:o + m] = np.cos(ang)
        cos_h[:, o + m:o + 2 * m] = np.cos(ang)
        sin_lo[:, o:o + m] = -np.sin(ang)
        sin_hi[:, o + m:o + 2 * m] = np.sin(ang)
    rope = np.stack([np.tile(a, (1, GLA_HEADS)) for a in (cos_h, sin_lo, sin_hi)])

    def position_tables(length):
        cos, sin = _dft_tables(length)
        half = length // 2
        return np.stack([cos[:half, 0::2], sin[:half, 0::2], cos[:half, 1::2], sin[:half, 1::2]])

    to_bf16 = lambda a: jnp.asarray(a, F32).astype(BF16)
    return {
        "cs": to_bf16(cs), "tri": jnp.asarray(tri, BF16),
        "gla_causal": jnp.asarray(gla_causal, F32), "gla_vmask": jnp.asarray(gla_vmask, BF16),
        "rope": jnp.asarray(rope, F32),
        "dft_lat": to_bf16(position_tables(t)), "dft_ctx": to_bf16(position_tables(n)),
    }


def _bias_kernel(rpb_ref, onehot_ref, ok_ref, o_ref):
    r = rpb_ref[...]
    onehot = onehot_ref[...]
    t = jnp.zeros(o_ref.shape, F32)
    for _ in range(3):
        piece = r.astype(BF16)
        t = t + _dot(piece, onehot)
        r = r - piece.astype(F32)
    o_ref[...] = jnp.where(ok_ref[...] > 0.0, t * LOG2E, NA_MASKED)


def _window_bias_rows(na_rpb):
    depth, heads, n_dr, n_dc = na_rpb.shape
    cq = np.arange(GRID_W)
    c0 = np.clip(cq - NA_KW // 2, 0, GRID_W - NA_KW)
    col_ok = (cq[None, :] >= c0[:, None]) & (cq[None, :] < c0[:, None] + NA_KW)
    dc = np.clip(cq[None, :] - cq[:, None], -(NA_KW - 1), NA_KW - 1) + NA_KW - 1
    n_dc_pad = n_dc + 1
    n_tab = depth * heads * n_dr
    rows_pad = -(-n_tab // 8) * 8
    onehot = (np.arange(n_dc_pad)[:, None, None] == dc[None]).reshape(n_dc_pad, GRID_W * GRID_W)
    rpb2 = jnp.pad(na_rpb.reshape(n_tab, n_dc), ((0, rows_pad - n_tab), (0, 1)))
    t = pl.pallas_call(
        _bias_kernel,
        out_shape=jax.ShapeDtypeStruct((rows_pad, GRID_W * GRID_W), F32),
        name="na_bias",
    )(rpb2, jnp.asarray(onehot, BF16), jnp.asarray(col_ok.reshape(1, -1), F32))
    t = t[:n_tab].reshape(depth * heads, n_dr, GRID_W, GRID_W)
    t = t.transpose(0, 1, 3, 2).reshape(depth * heads, n_dr * GRID_W, GRID_W)
    margin = NA_WIN - NA_KH
    t = jnp.pad(t, ((0, 0), (margin * GRID_W, margin * GRID_W), (0, 0)))
    return t.reshape(depth, heads, (n_dr + 2 * margin) * GRID_W, GRID_W)


def _pad_heads(w, heads, dim, pad, axis=-1):
    axis = axis % w.ndim
    shape = w.shape
    w = w.reshape(shape[:axis] + (heads, dim) + shape[axis + 1:])
    widths = [(0, 0)] * w.ndim
    widths[axis + 1] = (0, pad - dim)
    return jnp.pad(w, widths).reshape(shape[:axis] + (heads * pad,) + shape[axis + 1:])


def _params(norm_mix, norm_ffn, w_in, fnet_w, na_q_norm, na_k_norm, na_rpb, alpha_w, alpha_b, o_norm, w_out,
            ffn_w1, ffn_w3, ffn_w2):
    depth = w_in.shape[0]
    o = FN_W + 3 * NA_W
    gq = w_in[..., o:o + GLA_QK_W]
    gk = w_in[..., o + GLA_QK_W:o + 2 * GLA_QK_W]
    o2 = o + 2 * GLA_QK_W
    gv = w_in[..., o2:o2 + GLA_V_W]
    gg = w_in[..., o2 + GLA_V_W:o2 + 2 * GLA_V_W]
    ga = w_in[..., o2 + 2 * GLA_V_W:]
    w = jnp.concatenate([
        w_in[..., :o],
        jnp.pad(ga, ((0, 0), (0, 0), (0, GLA_A_PAD - 2 * GLA_RANK))),
        _pad_heads(gq, GLA_HEADS, GLA_DK, GLA_DK_PAD), _pad_heads(gk, GLA_HEADS, GLA_DK, GLA_DK_PAD),
        _pad_heads(gv, GLA_HEADS, GLA_DV, GLA_DV_PAD), _pad_heads(gg, GLA_HEADS, GLA_DV, GLA_DV_PAD),
    ], axis=-1).astype(BF16)

    qkg = jnp.stack([jnp.tile(na_q_norm, (1, NA_HEADS)) * (NA_HD ** -0.5 * LOG2E),
                     jnp.tile(na_k_norm, (1, NA_HEADS))], axis=1)

    aw_pad = _pad_heads(alpha_w, GLA_HEADS, GLA_DK, GLA_DK_PAD)
    aw = jnp.concatenate([jnp.pad(aw_pad[:, 0], ((0, 0), (0, 0), (0, GLA_QK_PAD))),
                          jnp.pad(aw_pad[:, 1], ((0, 0), (0, 0), (GLA_QK_PAD, 0)))], axis=1)
    aw = jnp.pad(aw, ((0, 0), (0, GLA_A_PAD - 2 * GLA_RANK), (0, 0))).astype(BF16)
    ab = _pad_heads(alpha_b, GLA_HEADS, GLA_DK, GLA_DK_PAD).reshape(depth, 1, 2 * GLA_QK_PAD)

    gain_o = _pad_heads(jnp.tile(o_norm, (1, GLA_HEADS)), GLA_HEADS, GLA_DV, GLA_DV_PAD)[:, None, :]
    wo = jnp.concatenate([w_out[:, :FN_W + NA_W],
                          _pad_heads(w_out[:, FN_W + NA_W:], GLA_HEADS, GLA_DV, GLA_DV_PAD, axis=1)],
                         axis=1).astype(BF16)
    return {
        "gain_m": norm_mix[:, None, :], "gain_f": norm_ffn[:, None, :], "w": w, "qkg": qkg, "aw": aw, "ab": ab,
        "rpb_rows": _window_bias_rows(na_rpb), "gain_o": gain_o, "wo": wo, "fw": fnet_w.astype(BF16),
        "w1": ffn_w1.astype(BF16), "w3": ffn_w3.astype(BF16), "w2": ffn_w2.astype(BF16),
    }


def kernel(x, c, ctx, c_ctx, ada_w, ada_b, norm_mix, norm_ffn, w_in, fnet_w, na_q_norm, na_k_norm, na_rpb,
           gla_alpha_w, gla_alpha_b, gla_o_norm, w_out, ffn_w1, ffn_w3, ffn_w2):
    bsz, t, d = x.shape
    n = ctx.shape[1]
    depth = ada_w.shape[0]
    ctx_row = bsz
    assert bsz < MOD_ROWS and n % SUB == 0 and (bsz * n) % CTX_TM == 0
    consts = _shape_consts(t, n)
    p = _params(norm_mix, norm_ffn, w_in, fnet_w, na_q_norm, na_k_norm, na_rpb, gla_alpha_w, gla_alpha_b,
                gla_o_norm, w_out, ffn_w1, ffn_w3, ffn_w2)

    cc = jnp.concatenate([c, c_ctx[None, :], jnp.zeros((MOD_ROWS - bsz - 1, d), F32)])
    mod = _ada_mod(cc, ada_w, ada_b).reshape(depth, MOD_ROWS, 6, d)

    cx = ctx
    flat = lambda a: a.reshape(1, bsz * n, a.shape[-1])
    for l in range(depth):
        need_ctx = l < depth - 1
        uv, nqkv, gqk, gvg, gb = _inproj(x, l, mod, None, p, consts, consts["rope"], tm=INPROJ_TM)
        uv_c, nqkv_c, gqk_c, gvg_c, gb_c = _inproj(cx, l, mod, ctx_row, p, consts, None, tm=n)

        y_fn = _fourier(uv, l, consts["dft_lat"], p["fw"])
        y_na, y_na_c = _na(nqkv, nqkv_c, l, p["rpb_rows"], need_ctx=need_ctx)
        o_gla = _gla(gqk, gvg, gb, gqk_c, gvg_c, gb_c, consts["gla_causal"], consts["gla_vmask"],
                     need_ctx=need_ctx)
        x = _post(x, y_fn, y_na, o_gla[0], o_gla[1], gvg, l, mod, None, p, tm=POST_TM)
        if need_ctx:
            y_fn_c = _fourier(uv_c, l, consts["dft_ctx"], p["fw"])
            cx = _post(flat(cx), flat(y_fn_c), flat(y_na_c), flat(o_gla[2]), flat(o_gla[3]), flat(gvg_c),
                       l, mod, ctx_row, p, tm=CTX_TM).reshape(bsz, n, d)
    return x
```

```python
import functools

import numpy as np
import jax
import jax.numpy as jnp
from jax import lax
from jax.experimental import pallas as pl
from jax.experimental.pallas import tpu as pltpu

F32 = jnp.float32
BF16 = jnp.bfloat16

D_MODEL = 1024
GRID_W = 64
FN_W = 256
FN_GROUPS = 4
FN_GD = 64
NA_HD = 64
NA_W = 384
NA_HEADS = 6
NA_KH = 8
NA_KW = 16
NA_GROUP = 4
NA_WIN = NA_GROUP + NA_KH
NA_BATCH = 2
LOG2E = float(np.log2(np.e))
NA_MASKED = -1e30
GLA_HEADS = 4
GLA_DK = 48
GLA_DV = 96
GLA_QK_W = GLA_HEADS * GLA_DK
GLA_V_W = GLA_HEADS * GLA_DV
GLA_RANK = 16
GLA_GATE_NORM = 16.0
GLA_CHUNK = 64
GLA_BLOCK = 2 * GLA_CHUNK
GLA_UNROLL = 4
ROPE_BASE = 10000.0
D_FF = 2816
EPS = 1e-6
MOD_ROWS = 16

LANES = 128
MXU_W = 256
VMEM_LIMIT_BYTES = 56 * 1024 * 1024

GLA_DK_PAD = 64
GLA_DV_PAD = 128
GLA_QK_PAD = GLA_HEADS * GLA_DK_PAD
GLA_V_PAD = GLA_HEADS * GLA_DV_PAD
GLA_A_PAD = LANES
MIX_PAD = FN_W + NA_W + GLA_V_PAD

C_FX = 0
C_NQ = C_FX + FN_W
C_NK = C_NQ + NA_W
C_NV = C_NK + NA_W
C_GA = C_NV + NA_W
C_GQ = C_GA + GLA_A_PAD
C_GK = C_GQ + GLA_QK_PAD
C_GV = C_GK + GLA_QK_PAD
C_GG = C_GV + GLA_V_PAD
N_IN_PAD = C_GG + GLA_V_PAD

SUB = 256

INPROJ_TM = 1024
POST_TM = 1024
POST_SUB = 256
CTX_TM = 512
ADA_TN = 1024


def _cparams(sem):
    return pltpu.CompilerParams(dimension_semantics=sem, vmem_limit_bytes=VMEM_LIMIT_BYTES)


def _const_spec(shape):
    nd = len(shape)
    return pl.BlockSpec(tuple(shape), lambda *_: (0,) * nd, pipeline_mode=pl.Buffered(1))


def _layer_spec(shape, l):
    nd = len(shape)
    return pl.BlockSpec((None,) + tuple(shape), lambda *_: (l,) + (0,) * nd, pipeline_mode=pl.Buffered(1))


def _mod_spec(d, l, mod_row):
    if mod_row is None:
        return pl.BlockSpec((None, None, 6, d), lambda b, i: (l, b, 0, 0))
    return pl.BlockSpec((None, None, 6, d), lambda b, i: (l, mod_row, 0, 0))


def _dot(a, b):
    return jnp.dot(a, b, preferred_element_type=F32)


def _dot_nt(a, b):
    return lax.dot_general(a, b, (((1,), (1,)), ((), ())), preferred_element_type=F32)


def _ada_kernel(c_ref, w_ref, b_ref, o_ref):
    a = c_ref[...]
    a = (a * jax.nn.sigmoid(a)).astype(BF16)
    o_ref[...] = _dot(a, w_ref[...].astype(BF16)) + b_ref[...]


def _ada_mod(cc, ada_w, ada_b):
    depth, d, n = ada_w.shape
    rows = cc.shape[0]
    tn = ADA_TN
    return pl.pallas_call(
        _ada_kernel,
        grid=(depth, n // tn),
        in_specs=[
            pl.BlockSpec((rows, d), lambda l, j: (0, 0)),
            pl.BlockSpec((None, d, tn), lambda l, j: (l, 0, j)),
            pl.BlockSpec((None, 1, tn), lambda l, j: (l, 0, j)),
        ],
        out_specs=pl.BlockSpec((None, rows, tn), lambda l, j: (l, 0, j)),
        out_shape=jax.ShapeDtypeStruct((depth, rows, n), F32),
        compiler_params=_cparams(("arbitrary", "arbitrary")),
        name="ada_mod",
    )(cc, ada_w, ada_b.reshape(depth, 1, n))


def _modulated_norm(x, gain, shift, scale):
    ms = jnp.mean(x * x, axis=-1, keepdims=True)
    return (x * lax.rsqrt(ms + EPS) * gain) * (1.0 + scale) + shift


def _log_sigmoid(x):
    return jnp.minimum(x, 0.0) - jnp.log1p(jnp.exp(-jnp.abs(x)))


def _head_mean_sq(z):
    first = lax.broadcasted_iota(jnp.int32, (1, LANES), 1) < NA_HD
    out = []
    for t in range(NA_W // LANES):
        sq = z[:, t * LANES:(t + 1) * LANES]
        sq = sq * sq
        lo = jnp.sum(jnp.where(first, sq, 0.0), axis=-1, keepdims=True)
        hi = jnp.sum(jnp.where(first, 0.0, sq), axis=-1, keepdims=True)
        out.append(jnp.where(first, lo, hi))
    return jnp.concatenate(out, axis=1) * (1.0 / NA_HD)


def _inproj_kernel(x_ref, mod_ref, gain_ref, w_ref, cs_ref, qkg_ref, rope_ref, aw_ref, ab_ref, tri_ref,
                   uv_ref, nqkv_ref, gqk_ref, gvg_ref, gb_ref, *, rope, tm):
    x = x_ref[...]
    h = _modulated_norm(x, gain_ref[...], mod_ref[0:1, :], mod_ref[1:2, :]).astype(BF16)

    zf = _dot(h, w_ref[:, C_FX:C_NQ]).astype(BF16)
    zva = _dot(h, w_ref[:, C_NV:C_GQ])
    nqkv_ref[:, 2 * NA_W:3 * NA_W] = zva[:, :NA_W].astype(BF16)
    a = zva[:, NA_W:].astype(BF16)
    zqk = _dot(h, w_ref[:, C_NQ:C_NV])
    logits = _dot(a, aw_ref[...]) + ab_ref[...]
    zg = _dot(h, w_ref[:, C_GQ:C_GV])
    uv = _dot(zf, cs_ref[...])
    for j in range(uv_ref.shape[0]):
        uv_ref[j] = uv[:, j * LANES:(j + 1) * LANES]
    gvg_ref[...] = _dot(h, w_ref[:, C_GV:N_IN_PAD]).astype(BF16)

    glog = (_log_sigmoid(logits) * (1.0 / GLA_GATE_NORM)).astype(BF16)
    for s in range(tm // SUB):
        rows = slice(s * SUB, (s + 1) * SUB)
        gb_ref[rows, 0:GLA_QK_PAD] = _dot(tri_ref[0], glog[rows, 0:GLA_QK_PAD])
        gb_ref[rows, GLA_QK_PAD:] = _dot(tri_ref[1], glog[rows, GLA_QK_PAD:])

    for i in range(2):
        z = zqk[:, i * NA_W:(i + 1) * NA_W]
        z = z * lax.rsqrt(_head_mean_sq(z) + EPS) * qkg_ref[i:i + 1, :]
        nqkv_ref[:, i * NA_W:(i + 1) * NA_W] = z.astype(BF16)

    for i in range(2):
        z = zg[:, i * GLA_QK_PAD:(i + 1) * GLA_QK_PAD]
        if rope:
            z = (z * rope_ref[0]
                 + pltpu.roll(z, GLA_QK_PAD - GLA_DK // 4, 1) * rope_ref[1]
                 + pltpu.roll(z, GLA_DK // 4, 1) * rope_ref[2])
        if i == 0:
            z = z * (GLA_DK ** -0.5)
        gqk_ref[:, i * GLA_QK_PAD:(i + 1) * GLA_QK_PAD] = z.astype(BF16)


def _inproj(x, l, mod, mod_row, p, consts, rope_tab, *, tm):
    bsz, t, d = x.shape
    assert t % tm == 0 and tm % SUB == 0
    rope = rope_tab is not None
    if not rope:
        rope_tab = jnp.zeros((3, tm, GLA_QK_PAD), F32)
    rope_idx = (lambda b, i: (0, i, 0)) if rope else (lambda b, i: (0, 0, 0))
    row = lambda w_: pl.BlockSpec((None, tm, w_), lambda b, i: (b, i, 0))
    uv_tiles = 2 * FN_W // LANES
    out_w = (3 * NA_W, 2 * GLA_QK_PAD, 2 * GLA_V_PAD, 2 * GLA_QK_PAD)
    out_dt = (BF16, BF16, BF16, F32)
    return pl.pallas_call(
        functools.partial(_inproj_kernel, rope=rope, tm=tm),
        grid=(bsz, t // tm),
        in_specs=[
            row(d),
            _mod_spec(d, l, mod_row),
            _layer_spec((1, d), l),
            _layer_spec((d, N_IN_PAD), l),
            _const_spec(consts["cs"].shape),
            _layer_spec((2, NA_W), l),
            pl.BlockSpec((3, tm, GLA_QK_PAD), rope_idx),
            _layer_spec((GLA_A_PAD, 2 * GLA_QK_PAD), l),
            _layer_spec((1, 2 * GLA_QK_PAD), l),
            _const_spec((2, SUB, SUB)),
        ],
        out_specs=[pl.BlockSpec((None, uv_tiles, tm, LANES), lambda b, i: (b, 0, i, 0))]
        + [row(w_) for w_ in out_w],
        out_shape=[jax.ShapeDtypeStruct((bsz, uv_tiles, t, LANES), F32)]
        + [jax.ShapeDtypeStruct((bsz, t, w_), dt) for w_, dt in zip(out_w, out_dt)],
        compiler_params=_cparams(("arbitrary", "arbitrary")),
        name="inproj_rope" if rope else "inproj_ctx",
    )(x, mod, p["gain_m"], p["w"], consts["cs"], p["qkg"], rope_tab, p["aw"], p["ab"], consts["tri"])


def _fourier_kernel(tab_ref, uv_ref, w_ref, o_ref, *, scale):
    half = uv_ref.shape[1] // 2

    def positions(parity):
        tiles = [uv_ref[j, pl.ds(parity, half, stride=2), :].astype(BF16) for j in range(uv_ref.shape[0])]
        return jnp.concatenate(tiles[:2], axis=1), jnp.concatenate(tiles[2:], axis=1)

    ue, ve = positions(0)
    uo, vo = positions(1)
    even = _dot(tab_ref[0], ue) - _dot(tab_ref[1], ve)
    odd = _dot(tab_ref[2], uo) - _dot(tab_ref[3], vo)
    w = w_ref[...]
    o_ref[0:half, :] = _dot(((even + odd) * scale).astype(BF16), w).astype(BF16)
    o_ref[half:, :] = _dot(((even - odd) * scale).astype(BF16), w).astype(BF16)


def _fourier(uv, l, tabs, fnet_w):
    bsz, tiles, t, _ = uv.shape
    scale = float((t * FN_GD) ** -0.5)
    return pl.pallas_call(
        functools.partial(_fourier_kernel, scale=scale),
        grid=(bsz,),
        in_specs=[
            _const_spec(tabs.shape),
            pl.BlockSpec((None, tiles, t, LANES), lambda b: (b, 0, 0, 0)),
            _layer_spec((FN_W, FN_W), l),
        ],
        out_specs=pl.BlockSpec((None, t, FN_W), lambda b: (b, 0, 0)),
        out_shape=jax.ShapeDtypeStruct((bsz, t, FN_W), BF16),
        compiler_params=_cparams(("arbitrary",)),
        name=f"fourier_{t}",
    )(tabs, uv, fnet_w)


def _na_softmax_t(s):
    return jnp.exp2(s - jnp.max(s, axis=0, keepdims=True)).astype(BF16)


def _na_heads_out_t(o0, o1):
    row = lax.broadcasted_iota(jnp.int32, o0.shape, 0)
    out_t = jnp.where(row < NA_HD, o0 / o0[NA_HD:NA_HD + 1, :], o1 / o1[0:1, :])
    return out_t.T.astype(BF16)


def _na_group_kinds():
    margin = NA_WIN - NA_KH
    return ((0, lambda i: 0), (NA_KH // 2, lambda i: i), (margin + NA_KH - NA_GROUP, lambda i: margin))


def _na_kernel(q_ref, k_ref, v_ref, qc_ref, kc_ref, vc_ref, rpb_ref, o_ref, oc_ref,
               vt_ref, sc_ref, bias_ref, *, n_rows, need_ctx):
    lane = lax.broadcasted_iota(jnp.int32, (1, LANES), 1)
    first = lane < NA_HD
    zero = jnp.zeros((), BF16)
    one = jnp.ones((), BF16)
    ne = q_ref.shape[0]
    kcs, vcts = [], []
    for e in range(ne):
        v = v_ref[e]
        vt_ref[e, 0] = jnp.where(first, v, one).T
        vt_ref[e, 1] = jnp.where(first, one, v).T
        kcs.append(kc_ref[e])
        vc = vc_ref[e]
        vcts.append((jnp.where(first, vc, one).T, jnp.where(first, one, vc).T))

    def both_heads(q):
        return jnp.concatenate([jnp.where(first, q, zero), jnp.where(first, zero, q)], axis=0)

    n_groups = n_rows // NA_GROUP
    gq = NA_GROUP * GRID_W
    n_loc = NA_WIN * GRID_W
    margin = NA_WIN - NA_KH

    @pl.when(pl.program_id(1) == 0)
    def _():
        wrow = lax.broadcasted_iota(jnp.int32, (n_loc, GRID_W), 0)
        for hh in range(2):
            for kind, (q_off, a_lo_of) in enumerate(_na_group_kinds()):
                for j in range(NA_GROUP // 2):
                    halves = []
                    for i in (2 * j, 2 * j + 1):
                        start = (NA_KH - 1 - q_off - i + margin) * GRID_W
                        lo = a_lo_of(i) * GRID_W
                        band = (wrow >= lo) & (wrow < lo + NA_KH * GRID_W)
                        halves.append(jnp.where(band, rpb_ref[hh, start:start + n_loc, :], NA_MASKED))
                    bias_ref[hh, kind, :, j * LANES:(j + 1) * LANES] = jnp.concatenate(halves, axis=1)

    def window(g):
        ws = jnp.clip(g * NA_GROUP - NA_KH // 2, 0, n_rows - NA_WIN)
        return (pl.ds(pl.multiple_of(g * gq, gq), gq), pl.ds(pl.multiple_of(ws * GRID_W, gq), n_loc))

    def scores(g, slot):
        qs, ks = window(g)
        kind = jnp.where(g == 0, 0, jnp.where(g == n_groups - 1, 2, 1))
        for e in range(ne):
            q = both_heads(q_ref[e, qs, :])
            s_loc = _dot_nt(k_ref[e, ks, :], q)
            s_ctx = _dot_nt(kcs[e], q)
            for hh in range(2):
                cols = slice(hh * gq, (hh + 1) * gq)
                sc_ref[e, slot, 0:n_loc, cols] = s_loc[:, cols] + bias_ref[hh, kind]
            sc_ref[e, slot, n_loc:, :] = s_ctx

    def finish(g, slot):
        qs, ks = window(g)
        for e in range(ne):
            p = _na_softmax_t(sc_ref[e, slot])
            outs = [_dot(jnp.concatenate([vt_ref[e, hh, :, ks], vcts[e][hh]], axis=1),
                         p[:, hh * gq:(hh + 1) * gq]) for hh in range(2)]
            o_ref[e, qs, :] = _na_heads_out_t(outs[0], outs[1])

    scores(0, 0)

    def pair(i, carry):
        scores(2 * i + 1, 1)
        finish(2 * i, 0)
        scores(2 * i + 2, 0)
        finish(2 * i + 1, 1)
        return carry

    lax.fori_loop(0, n_groups // 2 - 1, pair, 0)
    scores(n_groups - 1, 1)
    finish(n_groups - 2, 0)
    finish(n_groups - 1, 1)

    if need_ctx:
        n_ctx = qc_ref.shape[1]
        for e in range(ne):
            p = _na_softmax_t(_dot_nt(kcs[e], both_heads(qc_ref[e])))
            outs = [_dot(vcts[e][hh], p[:, hh * n_ctx:(hh + 1) * n_ctx]) for hh in range(2)]
            oc_ref[e] = _na_heads_out_t(outs[0], outs[1])
    else:
        oc_ref[...] = jnp.zeros(oc_ref.shape, oc_ref.dtype)


def _na(nqkv, nqkv_c, l, rpb_rows, *, need_ctx):
    bsz, t, _ = nqkv.shape
    n = nqkv_c.shape[1]
    n_rows = t // GRID_W
    assert n_rows % (2 * NA_GROUP) == 0 and n_rows >= NA_WIN + NA_GROUP
    ne = NA_BATCH if bsz % NA_BATCH == 0 else 1
    pairs = NA_HEADS // 2
    n_loc = NA_WIN * GRID_W
    gq = NA_GROUP * GRID_W
    nb = NA_W // LANES
    lat = lambda off: pl.BlockSpec((ne, t, LANES), lambda j, b: (b, 0, off + j))
    ctx = lambda off: pl.BlockSpec((ne, n, LANES), lambda j, b: (b, 0, off + j))
    return pl.pallas_call(
        functools.partial(_na_kernel, n_rows=n_rows, need_ctx=need_ctx),
        grid=(pairs, bsz // ne),
        in_specs=[lat(0), lat(nb), lat(2 * nb), ctx(0), ctx(nb), ctx(2 * nb),
                  pl.BlockSpec((None, 2) + rpb_rows.shape[2:], lambda j, b: (l, j, 0, 0))],
        out_specs=[pl.BlockSpec((ne, t, LANES), lambda j, b: (b, 0, j)),
                   pl.BlockSpec((ne, n, LANES), lambda j, b: (b, 0, j))],
        out_shape=[jax.ShapeDtypeStruct((bsz, t, NA_W), BF16),
                   jax.ShapeDtypeStruct((bsz, n, NA_W), BF16)],
        scratch_shapes=[pltpu.VMEM((ne, 2, LANES, t), BF16),
                        pltpu.VMEM((ne, 2, n_loc + n, 2 * gq), F32),
                        pltpu.VMEM((2, len(_na_group_kinds()), n_loc, gq), F32)],
        compiler_params=_cparams(("arbitrary", "arbitrary")),
        name="na_attn",
    )(nqkv, nqkv, nqkv, nqkv_c, nqkv_c, nqkv_c, rpb_rows)


def _gla_steps(jobs, st_ref, causal_ref, vmask_ref):
    c = GLA_CHUNK
    head = lax.broadcasted_iota(jnp.int32, (1, GLA_QK_PAD), 1) // GLA_DK_PAD
    prep = []
    for qk_ref, vg_ref, b_ref, out_ref, ci, d in jobs:
        rows = pl.ds(pl.multiple_of(ci * GLA_BLOCK, GLA_BLOCK), GLA_BLOCK)
        k = qk_ref[rows, GLA_QK_PAD:].astype(F32)
        v = vg_ref[rows, 0:GLA_V_PAD]
        b = b_ref[rows, d * GLA_QK_PAD:(d + 1) * GLA_QK_PAD]
        lo, hi = b[0:c, :], b[c:, :]
        if d == 0:
            tot1, tot2 = lo[c - 1:c, :], hi[c - 1:c, :]
            b_rel = jnp.concatenate([lo - tot1, hi], axis=0)
        else:
            tot1, tot2 = hi[0:1, :], lo[0:1, :]
            b_rel = jnp.concatenate([lo, hi - tot1], axis=0)
        k_in = k * jnp.exp(-b_rel)
        k_end = k_in * jnp.exp(tot2)
        q_in = q_st = None
        if out_ref is not None:
            q_rel = qk_ref[rows, 0:GLA_QK_PAD].astype(F32) * jnp.exp(b_rel)
            q_in = q_rel.astype(BF16)
            q_st = (q_rel * jnp.exp(tot1)).astype(BF16)
        decay = jnp.exp(jnp.broadcast_to(tot1 + tot2, (GLA_DV_PAD, GLA_QK_PAD)).T)
        prep.append((rows, q_in, q_st, k_in, k_end, v, decay))

    att = [None if q_in is None else
           _dot_nt(q_in, jnp.concatenate([jnp.where(head == h, k_in, 0.0) for h in range(GLA_HEADS)],
                                         axis=0).astype(BF16))
           for (rows, q_in, q_st, k_in, k_end, v, decay) in prep]

    kv = []
    for (rows, q_in, q_st, k_in, k_end, v, decay) in prep:
        k_t = k_end.astype(BF16).T
        kv.append(jnp.concatenate(
            [_dot(k_t[h * GLA_DK_PAD:(h + 1) * GLA_DK_PAD, :], v[:, h * GLA_DV_PAD:(h + 1) * GLA_DV_PAD])
             for h in range(GLA_HEADS)], axis=0))

    inter = []
    zero = jnp.zeros((GLA_DK_PAD, GLA_DV_PAD), BF16)
    for j, ((qk_ref, vg_ref, b_ref, out_ref, ci, d), (rows, q_in, q_st, k_in, k_end, v, decay)) in enumerate(
            zip(jobs, prep)):
        st = st_ref[d]
        if q_st is None:
            inter.append(None)
        else:
            sb = st.astype(BF16)
            s_bd = jnp.concatenate(
                [jnp.concatenate([sb[h * GLA_DK_PAD:(h + 1) * GLA_DK_PAD, :] if h2 == h else zero
                                  for h2 in range(GLA_HEADS)], axis=1) for h in range(GLA_HEADS)], axis=0)
            inter.append(_dot(q_st, s_bd))
        st_ref[d] = st * decay + kv[j]

    for j, ((qk_ref, vg_ref, b_ref, out_ref, ci, d), (rows, q_in, q_st, k_in, k_end, v, decay)) in enumerate(
            zip(jobs, prep)):
        if out_ref is not None:
            a = (att[j] * causal_ref[d]).astype(BF16)
            intra = []
            for pair in range(GLA_HEADS // 2):
                cols = slice(pair * 2 * GLA_DV_PAD, (pair + 1) * 2 * GLA_DV_PAD)
                v_pair = jnp.concatenate([v[:, cols]] * 2, axis=0) * vmask_ref[...]
                intra.append(_dot(a[:, pair * 2 * GLA_BLOCK:(pair + 1) * 2 * GLA_BLOCK], v_pair))
            out_ref[rows, :] = (inter[j] + jnp.concatenate(intra, axis=1)).astype(out_ref.dtype)


def _gla_kernel(qk_ref, vg_ref, b_ref, qkc_ref, vgc_ref, bc_ref, causal_ref, vmask_ref, *rest, t, n, need_ctx):
    if need_ctx:
        of_ref, ob_ref, ocf_ref, ocb_ref, st_ref = rest
    else:
        of_ref, ob_ref, st_ref = rest
    st_ref[...] = jnp.zeros(st_ref.shape, F32)
    nb_ctx = n // GLA_BLOCK
    nb_lat = t // GLA_BLOCK

    def jobs(refs, outs, n_blocks, i):
        return [refs + (outs[0], i, 0), refs + (outs[1], n_blocks - 1 - i, 1)]

    ctx_outs = (ocf_ref, ocb_ref) if need_ctx else (None, None)
    for i in range(nb_ctx):
        _gla_steps(jobs((qkc_ref, vgc_ref, bc_ref), ctx_outs, nb_ctx, i), st_ref, causal_ref, vmask_ref)

    def step(i, carry):
        _gla_steps(jobs((qk_ref, vg_ref, b_ref), (of_ref, ob_ref), nb_lat, i), st_ref, causal_ref, vmask_ref)
        return carry

    lax.fori_loop(0, nb_lat, step, 0, unroll=GLA_UNROLL)


def _gla(gqk, gvg, gb, gqk_c, gvg_c, gb_c, causal, vmask, *, need_ctx):
    bsz, t, _ = gqk.shape
    n = gqk_c.shape[1]
    assert t % GLA_BLOCK == 0 and n % GLA_BLOCK == 0
    row = lambda rows, w_: pl.BlockSpec((None, rows, w_), lambda b: (b, 0, 0))
    out_rows = (t, t, n, n) if need_ctx else (t, t)
    return pl.pallas_call(
        functools.partial(_gla_kernel, t=t, n=n, need_ctx=need_ctx),
        grid=(bsz,),
        in_specs=[row(t, 2 * GLA_QK_PAD), row(t, GLA_V_PAD), row(t, 2 * GLA_QK_PAD),
                  row(n, 2 * GLA_QK_PAD), row(n, GLA_V_PAD), row(n, 2 * GLA_QK_PAD),
                  _const_spec(causal.shape), _const_spec(vmask.shape)],
        out_specs=[row(r, GLA_V_PAD) for r in out_rows],
        out_shape=[jax.ShapeDtypeStruct((bsz, r, GLA_V_PAD), BF16) for r in out_rows],
        scratch_shapes=[pltpu.VMEM((2, GLA_QK_PAD, GLA_DV_PAD), F32)],
        compiler_params=_cparams(("arbitrary",)),
        name="gla_scan",
    )(gqk, gvg, gb, gqk_c, gvg_c, gb_c, causal, vmask)


def _gla_out(of_ref, ob_ref, g_ref, gain_ref, rows):
    heads = []
    for h in range(GLA_HEADS):
        cols = slice(h * GLA_DV_PAD, (h + 1) * GLA_DV_PAD)
        o = of_ref[rows, cols].astype(F32) + ob_ref[rows, cols].astype(F32)
        ms = jnp.sum(o * o, axis=-1, keepdims=True) * (1.0 / GLA_DV)
        g = g_ref[rows, cols].astype(F32)
        heads.append((o * lax.rsqrt(ms + EPS) * gain_ref[:, cols] * (g * jax.nn.sigmoid(g))).astype(BF16))
    return jnp.concatenate(heads, axis=1)


def _post_kernel(x_ref, yfn_ref, yna_ref, of_ref, ob_ref, g_ref, mod_ref, gain_o_ref, gain_ref,
                 wo_ref, w1_ref, w3_ref, w2_ref, o_ref):
    tm = x_ref.shape[0]
    tiles = [slice(r0, r0 + POST_SUB) for r0 in range(0, tm, POST_SUB)]
    ys = [(_dot(yfn_ref[rows, :], wo_ref[0:FN_W, :])
           + _dot(yna_ref[rows, :], wo_ref[FN_W:FN_W + NA_W, :])
           + _dot(_gla_out(of_ref, ob_ref, g_ref, gain_o_ref, rows), wo_ref[FN_W + NA_W:, :]))
          for rows in tiles]
    for rows, y in zip(tiles, ys):
        x1 = x_ref[rows, :] + mod_ref[2:3, :] * y
        h = _modulated_norm(x1, gain_ref[...], mod_ref[3:4, :], mod_ref[4:5, :]).astype(BF16)
        acts = []
        for c in range(D_FF // MXU_W):
            cols = slice(c * MXU_W, (c + 1) * MXU_W)
            u = _dot(h, w1_ref[:, cols])
            g = _dot(h, w3_ref[:, cols])
            acts.append((u * jax.nn.sigmoid(u) * g).astype(BF16))
        acc = _dot(jnp.concatenate(acts, axis=1), w2_ref[...])
        o_ref[rows, :] = x1 + mod_ref[5:6, :] * acc


def _post(x, yfn, yna, o_fwd, o_bwd, gvg, l, mod, mod_row, p, *, tm):
    bsz, t, d = x.shape
    assert t % tm == 0 and tm % POST_SUB == 0
    row = lambda w_: pl.BlockSpec((None, tm, w_), lambda b, i: (b, i, 0))
    gate =pl.BlockSpec((None, tm, GLA_V_PAD), lambda b, i: (b, i, 1))
    return pl.pallas_call(
        _post_kernel,
        grid=(bsz, t // tm),
        in_specs=[row(d), row(FN_W), row(NA_W), row(GLA_V_PAD), row(GLA_V_PAD), gate,
                  _mod_spec(d, l, mod_row), _layer_spec((1, GLA_V_PAD), l),
                  _layer_spec((1, d), l), _layer_spec((MIX_PAD, d), l), _layer_spec((d, D_FF), l),
                  _layer_spec((d, D_FF), l), _layer_spec((D_FF, d), l)],
        out_specs=row(d),
        out_shape=jax.ShapeDtypeStruct((bsz, t, d), F32),
        compiler_params=_cparams(("arbitrary", "arbitrary")),
        name=f"post_{t}",
    )(x, yfn, yna, o_fwd, o_bwd, gvg, mod, p["gain_o"], p["gain_f"], p["wo"], p["w1"], p["w3"], p["w2"])


def _dft_tables(n):
    idx = (np.arange(n)[:, None] * np.arange(n)[None, :]) % n
    ang = 2.0 * np.pi * idx / n
    return np.cos(ang), np.sin(ang)


def _shape_consts(t, n):
    c64, s64 = _dft_tables(FN_GD)
    eye = np.eye(FN_GROUPS)
    cs = np.concatenate([np.kron(eye, c64), np.kron(eye, s64)], axis=1)
    chunk_id = np.arange(SUB) // GLA_CHUNK
    same = chunk_id[:, None] == chunk_id[None, :]
    pos = np.arange(SUB)
    tri = np.stack([same & (pos[None, :] <= pos[:, None]), same & (pos[None, :] >= pos[:, None])])
    cpos = np.arange(GLA_BLOCK)
    fwd = cpos[None, :] <= cpos[:, None]
    gla_causal = np.stack([np.tile(fwd, (1, GLA_HEADS)), np.tile(fwd.T, (1, GLA_HEADS))])
    gla_vmask = ((np.arange(2 * GLA_BLOCK)[:, None] // GLA_BLOCK)
                 == (np.arange(2 * GLA_DV_PAD)[None, :] // GLA_DV_PAD))

    m = GLA_DK // 4
    inv = ROPE_BASE ** (-np.arange(m) / m)
    tok = np.arange(t)
    cos_h = np.ones((t, GLA_DK_PAD))
    sin_lo = np.zeros((t, GLA_DK_PAD))
    sin_hi = np.zeros((t, GLA_DK_PAD))
    for blk, p in enumerate((tok // GRID_W, tok % GRID_W)):
        ang = p[:, None] * inv[None, :]
        o = blk * 2 * m
        cos_h[:, o:o + m] = np.cos(ang)
        cos_h[:, o + m:o + 2 * m] = np.cos(ang)
        sin_lo[:, o:o + m] = -np.sin(ang)
        sin_hi[:, o + m:o + 2 * m] = np.sin(ang)
    rope = np.stack([np.tile(a, (1, GLA_HEADS)) for a in (cos_h, sin_lo, sin_hi)])

    def position_tables(length):
        cos, sin = _dft_tables(length)
        half = length // 2
        return np.stack([cos[:half, 0::2], sin[:half, 0::2], cos[:half, 1::2], sin[:half, 1::2]])

    to_bf16 = lambda a: jnp.asarray(a, F32).astype(BF16)
    return {
        "cs": to_bf16(cs), "tri": jnp.asarray(tri, BF16),
        "gla_causal": jnp.asarray(gla_causal, F32), "gla_vmask": jnp.asarray(gla_vmask, BF16),
        "rope": jnp.asarray(rope, F32),
        "dft_lat": to_bf16(position_tables(t)), "dft_ctx": to_bf16(position_tables(n)),
    }


def _bias_kernel(rpb_ref, onehot_ref, ok_ref, o_ref):
    r = rpb_ref[...]
    onehot = onehot_ref[...]
    t = jnp.zeros(o_ref.shape, F32)
    for _ in range(3):
        piece = r.astype(BF16)
        t = t + _dot(piece, onehot)
        r = r - piece.astype(F32)
    o_ref[...] = jnp.where(ok_ref[...] > 0.0, t * LOG2E, NA_MASKED)


def _window_bias_rows(na_rpb):
    depth, heads, n_dr, n_dc = na_rpb.shape
    cq = np.arange(GRID_W)
    c0 = np.clip(cq - NA_KW // 2, 0, GRID_W - NA_KW)
    col_ok = (cq[None, :] >= c0[:, None]) & (cq[None, :] < c0[:, None] + NA_KW)
    dc = np.clip(cq[None, :] - cq[:, None], -(NA_KW - 1), NA_KW - 1) + NA_KW - 1
    n_dc_pad = n_dc + 1
    n_tab = depth * heads * n_dr
    rows_pad = -(-n_tab // 8) * 8
    onehot = (np.arange(n_dc_pad)[:, None, None] == dc[None]).reshape(n_dc_pad, GRID_W * GRID_W)
    rpb2 = jnp.pad(na_rpb.reshape(n_tab, n_dc), ((0, rows_pad - n_tab), (0, 1)))
    t = pl.pallas_call(
        _bias_kernel,
        out_shape=jax.ShapeDtypeStruct((rows_pad, GRID_W * GRID_W), F32),
        name="na_bias",
    )(rpb2, jnp.asarray(onehot, BF16), jnp.asarray(col_ok.reshape(1, -1), F32))
    t = t[:n_tab].reshape(depth * heads, n_dr, GRID_W, GRID_W)
    t = t.transpose(0, 1, 3, 2).reshape(depth * heads, n_dr * GRID_W, GRID_W)
    margin = NA_WIN - NA_KH
    t = jnp.pad(t, ((0, 0), (margin * GRID_W, margin * GRID_W), (0, 0)))
    return t.reshape(depth, heads, (n_dr + 2 * margin) * GRID_W, GRID_W)


def _pad_heads(w, heads, dim, pad, axis=-1):
    axis = axis % w.ndim
    shape = w.shape
    w = w.reshape(shape[:axis] + (heads, dim) + shape[axis + 1:])
    widths = [(0, 0)] * w.ndim
    widths[axis + 1] = (0, pad - dim)
    return jnp.pad(w, widths).reshape(shape[:axis] + (heads * pad,) + shape[axis + 1:])


def _params(norm_mix, norm_ffn, w_in, fnet_w, na_q_norm, na_k_norm, na_rpb, alpha_w, alpha_b, o_norm, w_out,
            ffn_w1, ffn_w3, ffn_w2):
    depth = w_in.shape[0]
    o = FN_W + 3 * NA_W
    gq = w_in[..., o:o + GLA_QK_W]
    gk = w_in[..., o + GLA_QK_W:o + 2 * GLA_QK_W]
    o2 = o + 2 * GLA_QK_W
    gv = w_in[..., o2:o2 + GLA_V_W]
    gg = w_in[..., o2 + GLA_V_W:o2 + 2 * GLA_V_W]
    ga = w_in[..., o2 + 2 * GLA_V_W:]
    w = jnp.concatenate([
        w_in[..., :o],
        jnp.pad(ga, ((0, 0), (0, 0), (0, GLA_A_PAD - 2 * GLA_RANK))),
        _pad_heads(gq, GLA_HEADS, GLA_DK, GLA_DK_PAD), _pad_heads(gk, GLA_HEADS, GLA_DK, GLA_DK_PAD),
        _pad_heads(gv, GLA_HEADS, GLA_DV, GLA_DV_PAD), _pad_heads(gg, GLA_HEADS, GLA_DV, GLA_DV_PAD),
    ], axis=-1).astype(BF16)

    qkg = jnp.stack([jnp.tile(na_q_norm, (1, NA_HEADS)) * (NA_HD ** -0.5 * LOG2E),
                     jnp.tile(na_k_norm, (1, NA_HEADS))], axis=1)

    aw_pad = _pad_heads(alpha_w, GLA_HEADS, GLA_DK, GLA_DK_PAD)
    aw = jnp.concatenate([jnp.pad(aw_pad[:, 0], ((0, 0), (0, 0), (0, GLA_QK_PAD))),
                          jnp.pad(aw_pad[:, 1], ((0, 0), (0, 0), (GLA_QK_PAD, 0)))], axis=1)
    aw = jnp.pad(aw, ((0, 0), (0, GLA_A_PAD - 2 * GLA_RANK), (0, 0))).astype(BF16)
    ab = _pad_heads(alpha_b, GLA_HEADS, GLA_DK, GLA_DK_PAD).reshape(depth, 1, 2 * GLA_QK_PAD)

    gain_o = _pad_heads(jnp.tile(o_norm, (1, GLA_HEADS)), GLA_HEADS, GLA_DV, GLA_DV_PAD)[:, None, :]
    wo = jnp.concatenate([w_out[:, :FN_W + NA_W],
                          _pad_heads(w_out[:, FN_W + NA_W:], GLA_HEADS, GLA_DV, GLA_DV_PAD, axis=1)],
                         axis=1).astype(BF16)
    return {
        "gain_m": norm_mix[:, None, :], "gain_f": norm_ffn[:, None, :], "w": w, "qkg": qkg, "aw": aw, "ab": ab,
        "rpb_rows": _window_bias_rows(na_rpb), "gain_o": gain_o, "wo": wo, "fw": fnet_w.astype(BF16),
        "w1": ffn_w1.astype(BF16), "w3": ffn_w3.astype(BF16), "w2": ffn_w2.astype(BF16),
    }


def kernel(x, c, ctx, c_ctx, ada_w, ada_b, norm_mix, norm_ffn, w_in, fnet_w, na_q_norm, na_k_norm, na_rpb,
           gla_alpha_w, gla_alpha_b, gla_o_norm, w_out, ffn_w1, ffn_w3, ffn_w2):
    bsz, t, d = x.shape
    n = ctx.shape[1]
    depth = ada_w.shape[0]
    ctx_row = bsz
    assert bsz < MOD_ROWS and n % SUB == 0 and (bsz * n) % CTX_TM == 0
    consts = _shape_consts(t, n)
    p = _params(norm_mix, norm_ffn, w_in, fnet_w, na_q_norm, na_k_norm, na_rpb, gla_alpha_w, gla_alpha_b,
                gla_o_norm, w_out, ffn_w1, ffn_w3, ffn_w2)

    cc = jnp.concatenate([c, c_ctx[None, :], jnp.zeros((MOD_ROWS - bsz - 1, d), F32)])
    mod = _ada_mod(cc, ada_w, ada_b).reshape(depth, MOD_ROWS, 6, d)

    cx = ctx
    flat = lambda a: a.reshape(1, bsz * n, a.shape[-1])
    for l in range(depth):
        need_ctx = l < depth - 1
        uv, nqkv, gqk, gvg, gb = _inproj(x, l, mod, None, p, consts, consts["rope"], tm=INPROJ_TM)
        uv_c, nqkv_c, gqk_c, gvg_c, gb_c = _inproj(cx, l, mod, ctx_row, p, consts, None, tm=n)

        y_fn = _fourier(uv, l, consts["dft_lat"], p["fw"])
        y_na, y_na_c = _na(nqkv, nqkv_c, l, p["rpb_rows"], need_ctx=need_ctx)
        o_gla = _gla(gqk, gvg, gb, gqk_c, gvg_c, gb_c, consts["gla_causal"], consts["gla_vmask"],
                     need_ctx=need_ctx)
        x = _post(x, y_fn, y_na, o_gla[0], o_gla[1], gvg, l, mod, None, p, tm=POST_TM)
        if need_ctx:
            y_fn_c = _fourier(uv_c, l, consts["dft_ctx"], p["fw"])
            cx = _post(flat(cx), flat(y_fn_c), flat(y_na_c), flat(o_gla[2]), flat(o_gla[3]), flat(gvg_c),
                       l, mod, ctx_row, p, tm=CTX_TM).reshape(bsz, n, d)
    return x
```

```python
import functools

import numpy as np
import jax
import jax.numpy as jnp
from jax import lax
from jax.experimental import pallas as pl
from jax.experimental.pallas import tpu as pltpu

F32 = jnp.float32
BF16 = jnp.bfloat16

D_MODEL = 1024
GRID_W = 64
FN_W = 256
FN_GROUPS = 4
FN_GD = 64
NA_HD = 64
NA_W = 384
NA_HEADS = 6
NA_KH = 8
NA_KW = 16
NA_GROUP = 4
NA_WIN = NA_GROUP + NA_KH
NA_BATCH = 4
LOG2E = float(np.log2(np.e))
NA_MASKED = -1e30
GLA_HEADS = 4
GLA_DK = 48
GLA_DV = 96
GLA_QK_W = GLA_HEADS * GLA_DK
GLA_V_W = GLA_HEADS * GLA_DV
GLA_RANK = 16
GLA_GATE_NORM = 16.0
GLA_CHUNK = 64
GLA_BLOCK = 2 * GLA_CHUNK
GLA_UNROLL = 4
ROPE_BASE = 10000.0
D_FF = 2816
EPS = 1e-6
MOD_ROWS = 16

LANES = 128
MXU_W = 256
VMEM_LIMIT_BYTES = 56 * 1024 * 1024

GLA_DK_PAD = 64
GLA_DV_PAD = 128
GLA_QK_PAD = GLA_HEADS * GLA_DK_PAD
GLA_V_PAD = GLA_HEADS * GLA_DV_PAD
GLA_A_PAD = LANES
MIX_PAD = FN_W + NA_W + GLA_V_PAD

C_FX = 0
C_NQ = C_FX + FN_W
C_NK = C_NQ + NA_W
C_NV = C_NK + NA_W
C_GA = C_NV + NA_W
C_GQ = C_GA + GLA_A_PAD
C_GK = C_GQ + GLA_QK_PAD
C_GV = C_GK + GLA_QK_PAD
C_GG = C_GV + GLA_V_PAD
N_IN_PAD = C_GG + GLA_V_PAD

SUB = 256

INPROJ_TM = 1024
POST_TM = 1024
POST_SUB = 256
CTX_TM = 512
ADA_TN = 1024


def _cparams(sem):
    return pltpu.CompilerParams(dimension_semantics=sem, vmem_limit_bytes=VMEM_LIMIT_BYTES)


def _const_spec(shape):
    nd = len(shape)
    return pl.BlockSpec(tuple(shape), lambda *_: (0,) * nd, pipeline_mode=pl.Buffered(1))


def _layer_spec(shape, l):
    nd = len(shape)
    return pl.BlockSpec((None,) + tuple(shape), lambda *_: (l,) + (0,) * nd, pipeline_mode=pl.Buffered(1))


def _mod_spec(d, l, mod_row):
    if mod_row is None:
        return pl.BlockSpec((None, None, 6, d), lambda b, i: (l, b, 0, 0))
    return pl.BlockSpec((None, None, 6, d), lambda b, i: (l, mod_row, 0, 0))


def _dot(a, b):
    return jnp.dot(a, b, preferred_element_type=F32)


def _dot_nt(a, b):
    return lax.dot_general(a, b, (((1,), (1,)), ((), ())), preferred_element_type=F32)


def _ada_kernel(c_ref, w_ref, b_ref, o_ref):
    a = c_ref[...]
    a = (a * jax.nn.sigmoid(a)).astype(BF16)
    o_ref[...] = _dot(a, w_ref[...].astype(BF16)) + b_ref[...]


def _ada_mod(cc, ada_w, ada_b):
    depth, d, n = ada_w.shape
    rows = cc.shape[0]
    tn = ADA_TN
    return pl.pallas_call(
        _ada_kernel,
        grid=(depth, n // tn),
        in_specs=[
            pl.BlockSpec((rows, d), lambda l, j: (0, 0)),
            pl.BlockSpec((None, d, tn), lambda l, j: (l, 0, j)),
            pl.BlockSpec((None, 1, tn), lambda l, j: (l, 0, j)),
        ],
        out_specs=pl.BlockSpec((None, rows, tn), lambda l, j: (l, 0, j)),
        out_shape=jax.ShapeDtypeStruct((depth, rows, n), F32),
        compiler_params=_cparams(("arbitrary", "arbitrary")),
        name="ada_mod",
    )(cc, ada_w, ada_b.reshape(depth, 1, n))


def _modulated_norm(x, gain, shift, scale):
    ms = jnp.mean(x * x, axis=-1, keepdims=True)
    return (x * lax.rsqrt(ms + EPS) * gain) * (1.0 + scale) + shift


def _log_sigmoid(x):
    return jnp.minimum(x, 0.0) - jnp.log1p(jnp.exp(-jnp.abs(x)))


def _head_mean_sq(z):
    first = lax.broadcasted_iota(jnp.int32, (1, LANES), 1) < NA_HD
    out = []
    for t in range(NA_W // LANES):
        sq = z[:, t * LANES:(t + 1) * LANES]
        sq = sq * sq
        lo = jnp.sum(jnp.where(first, sq, 0.0), axis=-1, keepdims=True)
        hi = jnp.sum(jnp.where(first, 0.0, sq), axis=-1, keepdims=True)
        out.append(jnp.where(first, lo, hi))
    return jnp.concatenate(out, axis=1) * (1.0 / NA_HD)


def _inproj_kernel(x_ref, mod_ref, gain_ref, w_ref, cs_ref, qkg_ref, rope_ref, aw_ref, ab_ref, tri_ref,
                   uv_ref, nqkv_ref, gqk_ref, gvg_ref, gb_ref, *, rope, tm):
    x = x_ref[...]
    h = _modulated_norm(x, gain_ref[...], mod_ref[0:1, :], mod_ref[1:2, :]).astype(BF16)

    zf = _dot(h, w_ref[:, C_FX:C_NQ]).astype(BF16)
    zva = _dot(h, w_ref[:, C_NV:C_GQ])
    nqkv_ref[:, 2 * NA_W:3 * NA_W] = zva[:, :NA_W].astype(BF16)
    a = zva[:, NA_W:].astype(BF16)
    zqk = _dot(h, w_ref[:, C_NQ:C_NV])
    logits = _dot(a, aw_ref[...]) + ab_ref[...]
    zg = _dot(h, w_ref[:, C_GQ:C_GV])
    uv = _dot(zf, cs_ref[...])
    for j in range(uv_ref.shape[0]):
        uv_ref[j] = uv[:, j * LANES:(j + 1) * LANES]
    gvg_ref[...] = _dot(h, w_ref[:, C_GV:N_IN_PAD]).astype(BF16)

    glog = (_log_sigmoid(logits) * (1.0 / GLA_GATE_NORM)).astype(BF16)
    for s in range(tm // SUB):
        rows = slice(s * SUB, (s + 1) * SUB)
        gb_ref[rows, 0:GLA_QK_PAD] = _dot(tri_ref[0], glog[rows, 0:GLA_QK_PAD])
        gb_ref[rows, GLA_QK_PAD:] = _dot(tri_ref[1], glog[rows, GLA_QK_PAD:])

    for i in range(2):
        z = zqk[:, i * NA_W:(i + 1) * NA_W]
        z = z * lax.rsqrt(_head_mean_sq(z) + EPS) * qkg_ref[i:i + 1, :]
        nqkv_ref[:, i * NA_W:(i + 1) * NA_W] = z.astype(BF16)

    for i in range(2):
        z = zg[:, i * GLA_QK_PAD:(i + 1) * GLA_QK_PAD]
        if rope:
            z = (z * rope_ref[0]
                 + pltpu.roll(z, GLA_QK_PAD - GLA_DK // 4, 1) * rope_ref[1]
                 + pltpu.roll(z, GLA_DK // 4, 1) * rope_ref[2])
        if i == 0:
            z = z * (GLA_DK ** -0.5)
        gqk_ref[:, i * GLA_QK_PAD:(i + 1) * GLA_QK_PAD] = z.astype(BF16)


def _inproj(x, l, mod, mod_row, p, consts, rope_tab, *, tm):
    bsz, t, d = x.shape
    assert t % tm == 0 and tm % SUB == 0
    rope = rope_tab is not None
    if not rope:
        rope_tab = jnp.zeros((3, tm, GLA_QK_PAD), F32)
    rope_idx = (lambda b, i: (0, i, 0)) if rope else (lambda b, i: (0, 0, 0))
    row = lambda w_: pl.BlockSpec((None, tm, w_), lambda b, i: (b, i, 0))
    uv_tiles = 2 * FN_W // LANES
    out_w = (3 * NA_W, 2 * GLA_QK_PAD, 2 * GLA_V_PAD, 2 * GLA_QK_PAD)
    out_dt = (BF16, BF16, BF16, F32)
    return pl.pallas_call(
        functools.partial(_inproj_kernel, rope=rope, tm=tm),
        grid=(bsz, t // tm),
        in_specs=[
            row(d),
            _mod_spec(d, l, mod_row),
            _layer_spec((1, d), l),
            _layer_spec((d, N_IN_PAD), l),
            _const_spec(consts["cs"].shape),
            _layer_spec((2, NA_W), l),
            pl.BlockSpec((3, tm, GLA_QK_PAD), rope_idx),
            _layer_spec((GLA_A_PAD, 2 * GLA_QK_PAD), l),
            _layer_spec((1, 2 * GLA_QK_PAD), l),
            _const_spec((2, SUB, SUB)),
        ],
        out_specs=[pl.BlockSpec((None, uv_tiles, tm, LANES), lambda b, i: (b, 0, i, 0))]
        + [row(w_) for w_ in out_w],
        out_shape=[jax.ShapeDtypeStruct((bsz, uv_tiles, t, LANES), F32)]
        + [jax.ShapeDtypeStruct((bsz, t, w_), dt) for w_, dt in zip(out_w, out_dt)],
        compiler_params=_cparams(("arbitrary", "arbitrary")),
        name="inproj_rope" if rope else "inproj_ctx",
    )(x, mod, p["gain_m"], p["w"], consts["cs"], p["qkg"], rope_tab, p["aw"], p["ab"], consts["tri"])


def _fourier_kernel(tab_ref, uv_ref, w_ref, o_ref, *, scale):
    half = uv_ref.shape[1] // 2

    def positions(parity):
        tiles = [uv_ref[j, pl.ds(parity, half, stride=2), :].astype(BF16) for j in range(uv_ref.shape[0])]
        return jnp.concatenate(tiles[:2], axis=1), jnp.concatenate(tiles[2:], axis=1)

    ue, ve = positions(0)
    uo, vo = positions(1)
    even = _dot(tab_ref[0], ue) - _dot(tab_ref[1], ve)
    odd = _dot(tab_ref[2], uo) - _dot(tab_ref[3], vo)
    w = w_ref[...]
    o_ref[0:half, :] = _dot(((even + odd) * scale).astype(BF16), w).astype(BF16)
    o_ref[half:, :] = _dot(((even - odd) * scale).astype(BF16), w).astype(BF16)


def _fourier(uv, l, tabs, fnet_w):
    bsz, tiles, t, _ = uv.shape
    scale = float((t * FN_GD) ** -0.5)
    return pl.pallas_call(
        functools.partial(_fourier_kernel, scale=scale),
        grid=(bsz,),
        in_specs=[
            _const_spec(tabs.shape),
            pl.BlockSpec((None, tiles, t, LANES), lambda b: (b, 0, 0, 0)),
            _layer_spec((FN_W, FN_W), l),
        ],
        out_specs=pl.BlockSpec((None, t, FN_W), lambda b: (b, 0, 0)),
        out_shape=jax.ShapeDtypeStruct((bsz, t, FN_W), BF16),
        compiler_params=_cparams(("arbitrary",)),
        name=f"fourier_{t}",
    )(tabs, uv, fnet_w)


def _na_softmax_t(s):
    return jnp.exp2(s - jnp.max(s, axis=0, keepdims=True)).astype(BF16)


def _na_heads_out_t(o0, o1):
    row = lax.broadcasted_iota(jnp.int32, o0.shape, 0)
    out_t = jnp.where(row < NA_HD, o0 / o0[NA_HD:NA_HD + 1, :], o1 / o1[0:1, :])
    return out_t.T.astype(BF16)


def _na_group_kinds():
    margin = NA_WIN - NA_KH
    return ((0, lambda i: 0), (NA_KH // 2, lambda i: i), (margin + NA_KH - NA_GROUP, lambda i: margin))


def _na_kernel(q_ref, k_ref, v_ref, qc_ref, kc_ref, vc_ref, rpb_ref, o_ref, oc_ref,
               vt_ref, sc_ref, bias_ref, *, n_rows, need_ctx):
    lane = lax.broadcasted_iota(jnp.int32, (1, LANES), 1)
    first = lane < NA_HD
    zero = jnp.zeros((), BF16)
    one = jnp.ones((), BF16)
    ne = q_ref.shape[0]
    kcs, vcts = [], []
    for e in range(ne):
        v = v_ref[e]
        vt_ref[e, 0] = jnp.where(first, v, one).T
        vt_ref[e, 1] = jnp.where(first, one, v).T
        kcs.append(kc_ref[e])
        vc = vc_ref[e]
        vcts.append((jnp.where(first, vc, one).T, jnp.where(first, one, vc).T))

    def both_heads(q):
        return jnp.concatenate([jnp.where(first, q, zero), jnp.where(first, zero, q)], axis=0)

    n_groups = n_rows // NA_GROUP
    gq = NA_GROUP * GRID_W
    n_loc = NA_WIN * GRID_W
    margin = NA_WIN - NA_KH

    @pl.when(pl.program_id(1) == 0)
    def _():
        wrow = lax.broadcasted_iota(jnp.int32, (n_loc, GRID_W), 0)
        for hh in range(2):
            for kind, (q_off, a_lo_of) in enumerate(_na_group_kinds()):
                for j in range(NA_GROUP // 2):
                    halves = []
                    for i in (2 * j, 2 * j + 1):
                        start = (NA_KH - 1 - q_off - i + margin) * GRID_W
                        lo = a_lo_of(i) * GRID_W
                        band = (wrow >= lo) & (wrow < lo + NA_KH * GRID_W)
                        halves.append(jnp.where(band, rpb_ref[hh, start:start + n_loc, :], NA_MASKED))
                    bias_ref[hh, kind, :, j * LANES:(j + 1) * LANES] = jnp.concatenate(halves, axis=1)

    def window(g):
        ws = jnp.clip(g * NA_GROUP - NA_KH // 2, 0, n_rows - NA_WIN)
        return (pl.ds(pl.multiple_of(g * gq, gq), gq), pl.ds(pl.multiple_of(ws * GRID_W, gq), n_loc))

    def scores(g, slot):
        qs, ks = window(g)
        kind = jnp.where(g == 0, 0, jnp.where(g == n_groups - 1, 2, 1))
        for e in range(ne):
            q = both_heads(q_ref[e, qs, :])
            s_loc = _dot_nt(k_ref[e, ks, :], q)
            s_ctx = _dot_nt(kcs[e], q)
            for hh in range(2):
                cols = slice(hh * gq, (hh + 1) * gq)
                sc_ref[e, slot, 0:n_loc, cols] = s_loc[:, cols] + bias_ref[hh, kind]
            sc_ref[e, slot, n_loc:, :] = s_ctx

    def finish(g, slot):
        qs, ks = window(g)
        for e in range(ne):
            p = _na_softmax_t(sc_ref[e, slot])
            outs = [_dot(jnp.concatenate([vt_ref[e, hh, :, ks], vcts[e][hh]], axis=1),
                         p[:, hh * gq:(hh + 1) * gq]) for hh in range(2)]
            o_ref[e, qs, :] = _na_heads_out_t(outs[0], outs[1])

    scores(0, 0)

    def pair(i, carry):
        scores(2 * i + 1, 1)
        finish(2 * i, 0)
        scores(2 * i + 2, 0)
        finish(2 * i + 1, 1)
        return carry

    lax.fori_loop(0, n_groups // 2 - 1, pair, 0)
    scores(n_groups - 1, 1)
    finish(n_groups - 2, 0)
    finish(n_groups - 1, 1)

    if need_ctx:
        n_ctx = qc_ref.shape[1]
        for e in range(ne):
            p = _na_softmax_t(_dot_nt(kcs[e], both_heads(qc_ref[e])))
            outs = [_dot(vcts[e][hh], p[:, hh * n_ctx:(hh + 1) * n_ctx]) for hh in range(2)]
            oc_ref[e] = _na_heads_out_t(outs[0], outs[1])
    else:
        oc_ref[...] = jnp.zeros(oc_ref.shape, oc_ref.dtype)


def _na(nqkv, nqkv_c, l, rpb_rows, *, need_ctx):
    bsz, t, _ = nqkv.shape
    n = nqkv_c.shape[1]
    n_rows = t // GRID_W
    assert n_rows % (2 * NA_GROUP) == 0 and n_rows >= NA_WIN + NA_GROUP
    ne = NA_BATCH if bsz % NA_BATCH == 0 else 1
    pairs = NA_HEADS // 2
    n_loc = NA_WIN * GRID_W
    gq = NA_GROUP * GRID_W
    nb = NA_W // LANES
    lat = lambda off: pl.BlockSpec((ne, t, LANES), lambda j, b: (b, 0, off + j))
    ctx = lambda off: pl.BlockSpec((ne, n, LANES), lambda j, b: (b, 0, off + j))
    return pl.pallas_call(
        functools.partial(_na_kernel, n_rows=n_rows, need_ctx=need_ctx),
        grid=(pairs, bsz // ne),
        in_specs=[lat(0), lat(nb), lat(2 * nb), ctx(0), ctx(nb), ctx(2 * nb),
                  pl.BlockSpec((None, 2) + rpb_rows.shape[2:], lambda j, b: (l, j, 0, 0))],
        out_specs=[pl.BlockSpec((ne, t, LANES), lambda j, b: (b, 0, j)),
                   pl.BlockSpec((ne, n, LANES), lambda j, b: (b, 0, j))],
        out_shape=[jax.ShapeDtypeStruct((bsz, t, NA_W), BF16),
                   jax.ShapeDtypeStruct((bsz, n, NA_W), BF16)],
        scratch_shapes=[pltpu.VMEM((ne, 2, LANES, t), BF16),
                        pltpu.VMEM((ne, 2, n_loc + n, 2 * gq), F32),
                        pltpu.VMEM((2, len(_na_group_kinds()), n_loc, gq), F32)],
        compiler_params=_cparams(("arbitrary", "arbitrary")),
        name="na_attn",
    )(nqkv, nqkv, nqkv, nqkv_c, nqkv_c, nqkv_c, rpb_rows)


def _gla_steps(jobs, st_ref, causal_ref, vmask_ref):
    c = GLA_CHUNK
    head = lax.broadcasted_iota(jnp.int32, (1, GLA_QK_PAD), 1) // GLA_DK_PAD
    prep = []
    for qk_ref, vg_ref, b_ref, out_ref, ci, d in jobs:
        rows = pl.ds(pl.multiple_of(ci * GLA_BLOCK, GLA_BLOCK), GLA_BLOCK)
        k = qk_ref[rows, GLA_QK_PAD:].astype(F32)
        v = vg_ref[rows, 0:GLA_V_PAD]
        b = b_ref[rows, d * GLA_QK_PAD:(d + 1) * GLA_QK_PAD]
        lo, hi = b[0:c, :], b[c:, :]
        if d == 0:
            tot1, tot2 = lo[c - 1:c, :], hi[c - 1:c, :]
            b_rel = jnp.concatenate([lo - tot1, hi], axis=0)
        else:
            tot1, tot2 = hi[0:1, :], lo[0:1, :]
            b_rel = jnp.concatenate([lo, hi - tot1], axis=0)
        k_in = k * jnp.exp(-b_rel)
        k_end = k_in * jnp.exp(tot2)
        q_in = q_st = None
        if out_ref is not None:
            q_rel = qk_ref[rows, 0:GLA_QK_PAD].astype(F32) * jnp.exp(b_rel)
            q_in = q_rel.astype(BF16)
            q_st = (q_rel * jnp.exp(tot1)).astype(BF16)
        decay = jnp.exp(jnp.broadcast_to(tot1 + tot2, (GLA_DV_PAD, GLA_QK_PAD)).T)
        prep.append((rows, q_in, q_st, k_in, k_end, v, decay))

    att = [None if q_in is None else
           _dot_nt(q_in, jnp.concatenate([jnp.where(head == h, k_in, 0.0) for h in range(GLA_HEADS)],
                                         axis=0).astype(BF16))
           for (rows, q_in, q_st, k_in, k_end, v, decay) in prep]

    kv = []
    for (rows, q_in, q_st, k_in, k_end, v, decay) in prep:
        k_t = k_end.astype(BF16).T
        kv.append(jnp.concatenate(
            [_dot(k_t[h * GLA_DK_PAD:(h + 1) * GLA_DK_PAD, :], v[:, h * GLA_DV_PAD:(h + 1) * GLA_DV_PAD])
             for h in range(GLA_HEADS)], axis=0))

    inter = []
    zero = jnp.zeros((GLA_DK_PAD, GLA_DV_PAD), BF16)
    for j, ((qk_ref, vg_ref, b_ref, out_ref, ci, d), (rows, q_in, q_st, k_in, k_end, v, decay)) in enumerate(
            zip(jobs, prep)):
        st = st_ref[d]
        if q_st is None:
            inter.append(None)
        else:
            sb = st.astype(BF16)
            s_bd = jnp.concatenate(
                [jnp.concatenate([sb[h * GLA_DK_PAD:(h + 1) * GLA_DK_PAD, :] if h2 == h else zero
                                  for h2 in range(GLA_HEADS)], axis=1) for h in range(GLA_HEADS)], axis=0)
            inter.append(_dot(q_st, s_bd))
        st_ref[d] = st * decay + kv[j]

    for j, ((qk_ref, vg_ref, b_ref, out_ref, ci, d), (rows, q_in, q_st, k_in, k_end, v, decay)) in enumerate(
            zip(jobs, prep)):
        if out_ref is not None:
            a = (att[j] * causal_ref[d]).astype(BF16)
            intra = []
            for pair in range(GLA_HEADS // 2):
                cols = slice(pair * 2 * GLA_DV_PAD, (pair + 1) * 2 * GLA_DV_PAD)
                v_pair = jnp.concatenate([v[:, cols]] * 2, axis=0) * vmask_ref[...]
                intra.append(_dot(a[:, pair * 2 * GLA_BLOCK:(pair + 1) * 2 * GLA_BLOCK], v_pair))
            out_ref[rows, :] = (inter[j] + jnp.concatenate(intra, axis=1)).astype(out_ref.dtype)


def _gla_kernel(qk_ref, vg_ref, b_ref, qkc_ref, vgc_ref, bc_ref, causal_ref, vmask_ref, *rest, t, n, need_ctx):
    if need_ctx:
        of_ref, ob_ref, ocf_ref, ocb_ref, st_ref = rest
    else:
        of_ref, ob_ref, st_ref = rest
    st_ref[...] = jnp.zeros(st_ref.shape, F32)
    nb_ctx = n // GLA_BLOCK
    nb_lat = t // GLA_BLOCK

    def jobs(refs, outs, n_blocks, i):
        return [refs + (outs[0], i, 0), refs + (outs[1], n_blocks - 1 - i, 1)]

    ctx_outs = (ocf_ref, ocb_ref) if need_ctx else (None, None)
    for i in range(nb_ctx):
        _gla_steps(jobs((qkc_ref, vgc_ref, bc_ref), ctx_outs, nb_ctx, i), st_ref, causal_ref, vmask_ref)

    def step(i, carry):
        _gla_steps(jobs((qk_ref, vg_ref, b_ref), (of_ref, ob_ref), nb_lat, i), st_ref, causal_ref, vmask_ref)
        return carry

    lax.fori_loop(0, nb_lat, step, 0, unroll=GLA_UNROLL)


def _gla(gqk, gvg, gb, gqk_c, gvg_c, gb_c, causal, vmask, *, need_ctx):
    bsz, t, _ = gqk.shape
    n = gqk_c.shape[1]
    assert t % GLA_BLOCK == 0 and n % GLA_BLOCK == 0
    row = lambda rows, w_: pl.BlockSpec((None, rows, w_), lambda b: (b, 0, 0))
    out_rows = (t, t, n, n) if need_ctx else (t, t)
    return pl.pallas_call(
        functools.partial(_gla_kernel, t=t, n=n, need_ctx=need_ctx),
        grid=(bsz,),
        in_specs=[row(t, 2 * GLA_QK_PAD), row(t, GLA_V_PAD), row(t, 2 * GLA_QK_PAD),
                  row(n, 2 * GLA_QK_PAD), row(n, GLA_V_PAD), row(n, 2 * GLA_QK_PAD),
                  _const_spec(causal.shape), _const_spec(vmask.shape)],
        out_specs=[row(r, GLA_V_PAD) for r in out_rows],
        out_shape=[jax.ShapeDtypeStruct((bsz, r, GLA_V_PAD), BF16) for r in out_rows],
        scratch_shapes=[pltpu.VMEM((2, GLA_QK_PAD, GLA_DV_PAD), F32)],
        compiler_params=_cparams(("arbitrary",)),
        name="gla_scan",
    )(gqk, gvg, gb, gqk_c, gvg_c, gb_c, causal, vmask)


def _gla_out(of_ref, ob_ref, g_ref, gain_ref, rows):
    heads = []
    for h in range(GLA_HEADS):
        cols = slice(h * GLA_DV_PAD, (h + 1) * GLA_DV_PAD)
        o = of_ref[rows, cols].astype(F32) + ob_ref[rows, cols].astype(F32)
        ms = jnp.sum(o * o, axis=-1, keepdims=True) * (1.0 / GLA_DV)
        g = g_ref[rows, cols].astype(F32)
        heads.append((o * lax.rsqrt(ms + EPS) * gain_ref[:, cols] * (g * jax.nn.sigmoid(g))).astype(BF16))
    return jnp.concatenate(heads, axis=1)


def _post_kernel(x_ref, yfn_ref, yna_ref, of_ref, ob_ref, g_ref, mod_ref, gain_o_ref, gain_ref,
                 wo_ref, w1_ref, w3_ref, w2_ref, o_ref):
    tm = x_ref.shape[0]
    tiles = [slice(r0, r0 + POST_SUB) for r0 in range(0, tm, POST_SUB)]
    ys = [(_dot(yfn_ref[rows, :], wo_ref[0:FN_W, :])
           + _dot(yna_ref[rows, :], wo_ref[FN_W:FN_W + NA_W, :])
           + _dot(_gla_out(of_ref, ob_ref, g_ref, gain_o_ref, rows), wo_ref[FN_W + NA_W:, :]))
          for rows in tiles]
    for rows, y in zip(tiles, ys):
        x1 = x_ref[rows, :] + mod_ref[2:3, :] * y
        h = _modulated_norm(x1, gain_ref[...], mod_ref[3:4, :], mod_ref[4:5, :]).astype(BF16)
        acts = []
        for c in range(D_FF // MXU_W):
            cols = slice(c * MXU_W, (c + 1) * MXU_W)
            u = _dot(h, w1_ref[:, cols])
            g = _dot(h, w3_ref[:, cols])
            acts.append((u * jax.nn.sigmoid(u) * g).astype(BF16))
        acc = _dot(jnp.concatenate(acts, axis=1), w2_ref[...])
        o_ref[rows, :] = x1 + mod_ref[5:6, :] * acc


def _post(x, yfn, yna, o_fwd, o_bwd, gvg, l, mod, mod_row, p, *, tm):
    bsz, t, d = x.shape
    assert t % tm == 0 and tm % POST_SUB == 0
    row = lambda w_: pl.BlockSpec((None, tm, w_), lambda b, i: (b, i, 0))
    gate =pl.BlockSpec((None, tm, GLA_V_PAD), lambda b, i: (b, i, 1))
    return pl.pallas_call(
        _post_kernel,
        grid=(bsz, t // tm),
        in_specs=[row(d), row(FN_W), row(NA_W), row(GLA_V_PAD), row(GLA_V_PAD), gate,
                  _mod_spec(d, l, mod_row), _layer_spec((1, GLA_V_PAD), l),
                  _layer_spec((1, d), l), _layer_spec((MIX_PAD, d), l), _layer_spec((d, D_FF), l),
                  _layer_spec((d, D_FF), l), _layer_spec((D_FF, d), l)],
        out_specs=row(d),
        out_shape=jax.ShapeDtypeStruct((bsz, t, d), F32),
        compiler_params=_cparams(("arbitrary", "arbitrary")),
        name=f"post_{t}",
    )(x, yfn, yna, o_fwd, o_bwd, gvg, mod, p["gain_o"], p["gain_f"], p["wo"], p["w1"], p["w3"], p["w2"])


def _dft_tables(n):
    idx = (np.arange(n)[:, None] * np.arange(n)[None, :]) % n
    ang = 2.0 * np.pi * idx / n
    return np.cos(ang), np.sin(ang)


def _shape_consts(t, n):
    c64, s64 = _dft_tables(FN_GD)
    eye = np.eye(FN_GROUPS)
    cs = np.concatenate([np.kron(eye, c64), np.kron(eye, s64)], axis=1)
    chunk_id = np.arange(SUB) // GLA_CHUNK
    same = chunk_id[:, None] == chunk_id[None, :]
    pos = np.arange(SUB)
    tri = np.stack([same & (pos[None, :] <= pos[:, None]), same & (pos[None, :] >= pos[:, None])])
    cpos = np.arange(GLA_BLOCK)
    fwd = cpos[None, :] <= cpos[:, None]
    gla_causal = np.stack([np.tile(fwd, (1, GLA_HEADS)), np.tile(fwd.T, (1, GLA_HEADS))])
    gla_vmask = ((np.arange(2 * GLA_BLOCK)[:, None] // GLA_BLOCK)
                 == (np.arange(2 * GLA_DV_PAD)[None, :] // GLA_DV_PAD))

    m = GLA_DK // 4
    inv = ROPE_BASE ** (-np.arange(m) / m)
    tok = np.arange(t)
    cos_h = np.ones((t, GLA_DK_PAD))
    sin_lo = np.zeros((t, GLA_DK_PAD))
    sin_hi = np.zeros((t, GLA_DK_PAD))
    for blk, p in enumerate((tok // GRID_W, tok % GRID_W)):
        ang = p[:, None] * inv[None, :]
        o = blk * 2 * m
        cos_h[:, o:o + m] = np.cos(ang)
        cos_h[:, o + m:o + 2 * m] = np.cos(ang)
        sin_lo[:, o:o + m] = -np.sin(ang)
        sin_hi[:, o + m:o + 2 * m] = np.sin(ang)
    rope = np.stack([np.tile(a, (1, GLA_HEADS)) for a in (cos_h, sin_lo, sin_hi)])

    def position_tables(length):
        cos, sin = _dft_tables(length)
        half = length // 2
        return np.stack([cos[:half, 0::2], sin[:half, 0::2], cos[:half, 1::2], sin[:half, 1::2]])

    to_bf16 = lambda a: jnp.asarray(a, F32).astype(BF16)
    return {
        "cs": to_bf16(cs), "tri": jnp.asarray(tri, BF16),
        "gla_causal": jnp.asarray(gla_causal, F32), "gla_vmask": jnp.asarray(gla_vmask, BF16),
        "rope": jnp.asarray(rope, F32),
        "dft_lat": to_bf16(position_tables(t)), "dft_ctx": to_bf16(position_tables(n)),
    }


def _bias_kernel(rpb_ref, onehot_ref, ok_ref, o_ref):
    r = rpb_ref[...]
    onehot = onehot_ref[...]
    t = jnp.zeros(o_ref.shape, F32)
    for _ in range(3):
        piece = r.astype(BF16)
        t = t + _dot(piece, onehot)
        r = r - piece.astype(F32)
    o_ref[...] = jnp.where(ok_ref[...] > 0.0, t * LOG2E, NA_MASKED)


def _window_bias_rows(na_rpb):
    depth, heads, n_dr, n_dc = na_rpb.shape
    cq = np.arange(GRID_W)
    c0 = np.clip(cq - NA_KW // 2, 0, GRID_W - NA_KW)
    col_ok = (cq[None, :] >= c0[:, None]) & (cq[None, :] < c0[:, None] + NA_KW)
    dc = np.clip(cq[None, :] - cq[:, None], -(NA_KW - 1), NA_KW - 1) + NA_KW - 1
    n_dc_pad = n_dc + 1
    n_tab = depth * heads * n_dr
    rows_pad = -(-n_tab // 8) * 8
    onehot = (np.arange(n_dc_pad)[:, None, None] == dc[None]).reshape(n_dc_pad, GRID_W * GRID_W)
    rpb2 = jnp.pad(na_rpb.reshape(n_tab, n_dc), ((0, rows_pad - n_tab), (0, 1)))
    t = pl.pallas_call(
        _bias_kernel,
        out_shape=jax.ShapeDtypeStruct((rows_pad, GRID_W * GRID_W), F32),
        name="na_bias",
    )(rpb2, jnp.asarray(onehot, BF16), jnp.asarray(col_ok.reshape(1, -1), F32))
    t = t[:n_tab].reshape(depth * heads, n_dr, GRID_W, GRID_W)
    t = t.transpose(0, 1, 3, 2).reshape(depth * heads, n_dr * GRID_W, GRID_W)
    margin = NA_WIN - NA_KH
    t = jnp.pad(t, ((0, 0), (margin * GRID_W, margin * GRID_W), (0, 0)))
    return t.reshape(depth, heads, (n_dr + 2 * margin) * GRID_W, GRID_W)


def _pad_heads(w, heads, dim, pad, axis=-1):
    axis = axis % w.ndim
    shape = w.shape
    w = w.reshape(shape[:axis] + (heads, dim) + shape[axis + 1:])
    widths = [(0, 0)] * w.ndim
    widths[axis + 1] = (0, pad - dim)
    return jnp.pad(w, widths).reshape(shape[:axis] + (heads * pad,) + shape[axis + 1:])


def _params(norm_mix, norm_ffn, w_in, fnet_w, na_q_norm, na_k_norm, na_rpb, alpha_w, alpha_b, o_norm, w_out,
            ffn_w1, ffn_w3, ffn_w2):
    depth = w_in.shape[0]
    o = FN_W + 3 * NA_W
    gq = w_in[..., o:o + GLA_QK_W]
    gk = w_in[..., o + GLA_QK_W:o + 2 * GLA_QK_W]
    o2 = o + 2 * GLA_QK_W
    gv = w_in[..., o2:o2 + GLA_V_W]
    gg = w_in[..., o2 + GLA_V_W:o2 + 2 * GLA_V_W]
    ga = w_in[..., o2 + 2 * GLA_V_W:]
    w = jnp.concatenate([
        w_in[..., :o],
        jnp.pad(ga, ((0, 0), (0, 0), (0, GLA_A_PAD - 2 * GLA_RANK))),
        _pad_heads(gq, GLA_HEADS, GLA_DK, GLA_DK_PAD), _pad_heads(gk, GLA_HEADS, GLA_DK, GLA_DK_PAD),
        _pad_heads(gv, GLA_HEADS, GLA_DV, GLA_DV_PAD), _pad_heads(gg, GLA_HEADS, GLA_DV, GLA_DV_PAD),
    ], axis=-1).astype(BF16)

    qkg = jnp.stack([jnp.tile(na_q_norm, (1, NA_HEADS)) * (NA_HD ** -0.5 * LOG2E),
                     jnp.tile(na_k_norm, (1, NA_HEADS))], axis=1)

    aw_pad = _pad_heads(alpha_w, GLA_HEADS, GLA_DK, GLA_DK_PAD)
    aw = jnp.concatenate([jnp.pad(aw_pad[:, 0], ((0, 0), (0, 0), (0, GLA_QK_PAD))),
                          jnp.pad(aw_pad[:, 1], ((0, 0), (0, 0), (GLA_QK_PAD, 0)))], axis=1)
    aw = jnp.pad(aw, ((0, 0), (0, GLA_A_PAD - 2 * GLA_RANK), (0, 0))).astype(BF16)
    ab = _pad_heads(alpha_b, GLA_HEADS, GLA_DK, GLA_DK_PAD).reshape(depth, 1, 2 * GLA_QK_PAD)

    gain_o = _pad_heads(jnp.tile(o_norm, (1, GLA_HEADS)), GLA_HEADS, GLA_DV, GLA_DV_PAD)[:, None, :]
    wo = jnp.concatenate([w_out[:, :FN_W + NA_W],
                          _pad_heads(w_out[:, FN_W + NA_W:], GLA_HEADS, GLA_DV, GLA_DV_PAD, axis=1)],
                         axis=1).astype(BF16)
    return {
        "gain_m": norm_mix[:, None, :], "gain_f": norm_ffn[:, None, :], "w": w, "qkg": qkg, "aw": aw, "ab": ab,
        "rpb_rows": _window_bias_rows(na_rpb), "gain_o": gain_o, "wo": wo, "fw": fnet_w.astype(BF16),
        "w1": ffn_w1.astype(BF16), "w3": ffn_w3.astype(BF16), "w2": ffn_w2.astype(BF16),
    }


def kernel(x, c, ctx, c_ctx, ada_w, ada_b, norm_mix, norm_ffn, w_in, fnet_w, na_q_norm, na_k_norm, na_rpb,
           gla_alpha_w, gla_alpha_b, gla_o_norm, w_out, ffn_w1, ffn_w3, ffn_w2):
    bsz, t, d = x.shape
    n = ctx.shape[1]
    depth = ada_w.shape[0]
    ctx_row = bsz
    assert bsz < MOD_ROWS and n % SUB == 0 and (bsz * n) % CTX_TM == 0
    consts = _shape_consts(t, n)
    p = _params(norm_mix, norm_ffn, w_in, fnet_w, na_q_norm, na_k_norm, na_rpb, gla_alpha_w, gla_alpha_b,
                gla_o_norm, w_out, ffn_w1, ffn_w3, ffn_w2)

    cc = jnp.concatenate([c, c_ctx[None, :], jnp.zeros((MOD_ROWS - bsz - 1, d), F32)])
    mod = _ada_mod(cc, ada_w, ada_b).reshape(depth, MOD_ROWS, 6, d)

    cx = ctx
    flat = lambda a: a.reshape(1, bsz * n, a.shape[-1])
    for l in range(depth):
        need_ctx = l < depth - 1
        uv, nqkv, gqk, gvg, gb = _inproj(x, l, mod, None, p, consts, consts["rope"], tm=INPROJ_TM)
        uv_c, nqkv_c, gqk_c, gvg_c, gb_c = _inproj(cx, l, mod, ctx_row, p, consts, None, tm=n)

        y_fn = _fourier(uv, l, consts["dft_lat"], p["fw"])
        y_na, y_na_c = _na(nqkv, nqkv_c, l, p["rpb_rows"], need_ctx=need_ctx)
        o_gla = _gla(gqk, gvg, gb, gqk_c, gvg_c, gb_c, consts["gla_causal"], consts["gla_vmask"],
                     need_ctx=need_ctx)
        x = _post(x, y_fn, y_na, o_gla[0], o_gla[1], gvg, l, mod, None, p, tm=POST_TM)
        if need_ctx:
            y_fn_c = _fourier(uv_c, l, consts["dft_ctx"], p["fw"])
            cx = _post(flat(cx), flat(y_fn_c), flat(y_na_c), flat(o_gla[2]), flat(o_gla[3]), flat(gvg_c),
                       l, mod, ctx_row, p, tm=CTX_TM).reshape(bsz, n, d)
    return x
```

```python
import functools

import numpy as np
import jax
import jax.numpy as jnp
from jax import lax
from jax.experimental import pallas as pl
from jax.experimental.pallas import tpu as pltpu

F32 = jnp.float32
BF16 = jnp.bfloat16

D_MODEL = 1024
GRID_W = 64
FN_W = 256
FN_GROUPS = 4
FN_GD = 64
NA_HD = 64
NA_W = 384
NA_HEADS = 6
NA_KH = 8
NA_KW = 16
NA_GROUP = 4
NA_WIN = NA_GROUP + NA_KH
NA_BATCH = 4
LOG2E = float(np.log2(np.e))
NA_MASKED = -1e30
GLA_HEADS = 4
GLA_DK = 48
GLA_DV = 96
GLA_QK_W = GLA_HEADS * GLA_DK
GLA_V_W = GLA_HEADS * GLA_DV
GLA_RANK = 16
GLA_GATE_NORM = 16.0
GLA_CHUNK = 64
GLA_BLOCK = 2 * GLA_CHUNK
GLA_UNROLL = 16
ROPE_BASE = 10000.0
D_FF = 2816
EPS = 1e-6
MOD_ROWS = 16

LANES = 128
MXU_W = 256
VMEM_LIMIT_BYTES = 56 * 1024 * 1024

GLA_DK_PAD = 64
GLA_DV_PAD = 128
GLA_QK_PAD = GLA_HEADS * GLA_DK_PAD
GLA_V_PAD = GLA_HEADS * GLA_DV_PAD
GLA_A_PAD = LANES
MIX_PAD = FN_W + NA_W + GLA_V_PAD

C_FX = 0
C_NQ = C_FX + FN_W
C_NK = C_NQ + NA_W
C_NV = C_NK + NA_W
C_GA = C_NV + NA_W
C_GQ = C_GA + GLA_A_PAD
C_GK = C_GQ + GLA_QK_PAD
C_GV = C_GK + GLA_QK_PAD
C_GG = C_GV + GLA_V_PAD
N_IN_PAD = C_GG + GLA_V_PAD

SUB = 256

INPROJ_TM = 1024
POST_TM = 1024
POST_SUB = 256
CTX_TM = 512
ADA_TN = 1024


def _cparams(sem):
    return pltpu.CompilerParams(dimension_semantics=sem, vmem_limit_bytes=VMEM_LIMIT_BYTES)


def _const_spec(shape):
    nd = len(shape)
    return pl.BlockSpec(tuple(shape), lambda *_: (0,) * nd, pipeline_mode=pl.Buffered(1))


def _layer_spec(shape, l):
    nd = len(shape)
    return pl.BlockSpec((None,) + tuple(shape), lambda *_: (l,) + (0,) * nd, pipeline_mode=pl.Buffered(1))


def _mod_spec(d, l, mod_row):
    if mod_row is None:
        return pl.BlockSpec((None, None, 6, d), lambda b, i: (l, b, 0, 0))
    return pl.BlockSpec((None, None, 6, d), lambda b, i: (l, mod_row, 0, 0))


def _dot(a, b):
    return jnp.dot(a, b, preferred_element_type=F32)


def _dot_nt(a, b):
    return lax.dot_general(a, b, (((1,), (1,)), ((), ())), preferred_element_type=F32)


def _ada_kernel(c_ref, w_ref, b_ref, o_ref):
    a = c_ref[...]
    a = (a * jax.nn.sigmoid(a)).astype(BF16)
    o_ref[...] = _dot(a, w_ref[...].astype(BF16)) + b_ref[...]


def _ada_mod(cc, ada_w, ada_b):
    depth, d, n = ada_w.shape
    rows = cc.shape[0]
    tn = ADA_TN
    return pl.pallas_call(
        _ada_kernel,
        grid=(depth, n // tn),
        in_specs=[
            pl.BlockSpec((rows, d), lambda l, j: (0, 0)),
            pl.BlockSpec((None, d, tn), lambda l, j: (l, 0, j)),
            pl.BlockSpec((None, 1, tn), lambda l, j: (l, 0, j)),
        ],
        out_specs=pl.BlockSpec((None, rows, tn), lambda l, j: (l, 0, j)),
        out_shape=jax.ShapeDtypeStruct((depth, rows, n), F32),
        compiler_params=_cparams(("arbitrary", "arbitrary")),
        name="ada_mod",
    )(cc, ada_w, ada_b.reshape(depth, 1, n))


def _modulated_norm(x, gain, shift, scale):
    ms = jnp.mean(x * x, axis=-1, keepdims=True)
    return (x * lax.rsqrt(ms + EPS) * gain) * (1.0 + scale) + shift


def _log_sigmoid(x):
    return jnp.minimum(x, 0.0) - jnp.log1p(jnp.exp(-jnp.abs(x)))


def _head_mean_sq(z):
    first = lax.broadcasted_iota(jnp.int32, (1, LANES), 1) < NA_HD
    out = []
    for t in range(NA_W // LANES):
        sq = z[:, t * LANES:(t + 1) * LANES]
        sq = sq * sq
        lo = jnp.sum(jnp.where(first, sq, 0.0), axis=-1, keepdims=True)
        hi = jnp.sum(jnp.where(first, 0.0, sq), axis=-1, keepdims=True)
        out.append(jnp.where(first, lo, hi))
    return jnp.concatenate(out, axis=1) * (1.0 / NA_HD)


def _inproj_kernel(x_ref, mod_ref, gain_ref, w_ref, cs_ref, qkg_ref, rope_ref, aw_ref, ab_ref, tri_ref,
                   uv_ref, nqkv_ref, gqk_ref, gvg_ref, gb_ref, *, rope, tm):
    x = x_ref[...]
    h = _modulated_norm(x, gain_ref[...], mod_ref[0:1, :], mod_ref[1:2, :]).astype(BF16)

    zf = _dot(h, w_ref[:, C_FX:C_NQ]).astype(BF16)
    zva = _dot(h, w_ref[:, C_NV:C_GQ])
    nqkv_ref[:, 2 * NA_W:3 * NA_W] = zva[:, :NA_W].astype(BF16)
    a = zva[:, NA_W:].astype(BF16)
    zqk = _dot(h, w_ref[:, C_NQ:C_NV])
    logits = _dot(a, aw_ref[...]) + ab_ref[...]
    zg = _dot(h, w_ref[:, C_GQ:C_GV])
    uv = _dot(zf, cs_ref[...])
    for j in range(uv_ref.shape[0]):
        uv_ref[j] = uv[:, j * LANES:(j + 1) * LANES]
    gvg_ref[...] = _dot(h, w_ref[:, C_GV:N_IN_PAD]).astype(BF16)

    glog = (_log_sigmoid(logits) * (1.0 / GLA_GATE_NORM)).astype(BF16)
    for s in range(tm // SUB):
        rows = slice(s * SUB, (s + 1) * SUB)
        gb_ref[rows, 0:GLA_QK_PAD] = _dot(tri_ref[0], glog[rows, 0:GLA_QK_PAD])
        gb_ref[rows, GLA_QK_PAD:] = _dot(tri_ref[1], glog[rows, GLA_QK_PAD:])

    for i in range(2):
        z = zqk[:, i * NA_W:(i + 1) * NA_W]
        z = z * lax.rsqrt(_head_mean_sq(z) + EPS) * qkg_ref[i:i + 1, :]
        nqkv_ref[:, i * NA_W:(i + 1) * NA_W] = z.astype(BF16)

    for i in range(2):
        z = zg[:, i * GLA_QK_PAD:(i + 1) * GLA_QK_PAD]
        if rope:
            z = (z * rope_ref[0]
                 + pltpu.roll(z, GLA_QK_PAD - GLA_DK // 4, 1) * rope_ref[1]
                 + pltpu.roll(z, GLA_DK // 4, 1) * rope_ref[2])
        if i == 0:
            z = z * (GLA_DK ** -0.5)
        gqk_ref[:, i * GLA_QK_PAD:(i + 1) * GLA_QK_PAD] = z.astype(BF16)


def _inproj(x, l, mod, mod_row, p, consts, rope_tab, *, tm):
    bsz, t, d = x.shape
    assert t % tm == 0 and tm % SUB == 0
    rope = rope_tab is not None
    if not rope:
        rope_tab = jnp.zeros((3, tm, GLA_QK_PAD), F32)
    rope_idx = (lambda b, i: (0, i, 0)) if rope else (lambda b, i: (0, 0, 0))
    row = lambda w_: pl.BlockSpec((None, tm, w_), lambda b, i: (b, i, 0))
    uv_tiles = 2 * FN_W // LANES
    out_w = (3 * NA_W, 2 * GLA_QK_PAD, 2 * GLA_V_PAD, 2 * GLA_QK_PAD)
    out_dt = (BF16, BF16, BF16, F32)
    return pl.pallas_call(
        functools.partial(_inproj_kernel, rope=rope, tm=tm),
        grid=(bsz, t // tm),
        in_specs=[
            row(d),
            _mod_spec(d, l, mod_row),
            _layer_spec((1, d), l),
            _layer_spec((d, N_IN_PAD), l),
            _const_spec(consts["cs"].shape),
            _layer_spec((2, NA_W), l),
            pl.BlockSpec((3, tm, GLA_QK_PAD), rope_idx),
            _layer_spec((GLA_A_PAD, 2 * GLA_QK_PAD), l),
            _layer_spec((1, 2 * GLA_QK_PAD), l),
            _const_spec((2, SUB, SUB)),
        ],
        out_specs=[pl.BlockSpec((None, uv_tiles, tm, LANES), lambda b, i: (b, 0, i, 0))]
        + [row(w_) for w_ in out_w],
        out_shape=[jax.ShapeDtypeStruct((bsz, uv_tiles, t, LANES), F32)]
        + [jax.ShapeDtypeStruct((bsz, t, w_), dt) for w_, dt in zip(out_w, out_dt)],
        compiler_params=_cparams(("arbitrary", "arbitrary")),
        name="inproj_rope" if rope else "inproj_ctx",
    )(x, mod, p["gain_m"], p["w"], consts["cs"], p["qkg"], rope_tab, p["aw"], p["ab"], consts["tri"])


def _fourier_kernel(tab_ref, uv_ref, w_ref, o_ref, *, scale):
    half = uv_ref.shape[1] // 2

    def positions(parity):
        tiles = [uv_ref[j, pl.ds(parity, half, stride=2), :].astype(BF16) for j in range(uv_ref.shape[0])]
        return jnp.concatenate(tiles[:2], axis=1), jnp.concatenate(tiles[2:], axis=1)

    ue, ve = positions(0)
    uo, vo = positions(1)
    even = _dot(tab_ref[0], ue) - _dot(tab_ref[1], ve)
    odd = _dot(tab_ref[2], uo) - _dot(tab_ref[3], vo)
    w = w_ref[...]
    o_ref[0:half, :] = _dot(((even + odd) * scale).astype(BF16), w).astype(BF16)
    o_ref[half:, :] = _dot(((even - odd) * scale).astype(BF16), w).astype(BF16)


def _fourier(uv, l, tabs, fnet_w):
    bsz, tiles, t, _ = uv.shape
    scale = float((t * FN_GD) ** -0.5)
    return pl.pallas_call(
        functools.partial(_fourier_kernel, scale=scale),
        grid=(bsz,),
        in_specs=[
            _const_spec(tabs.shape),
            pl.BlockSpec((None, tiles, t, LANES), lambda b: (b, 0, 0, 0)),
            _layer_spec((FN_W, FN_W), l),
        ],
        out_specs=pl.BlockSpec((None, t, FN_W), lambda b: (b, 0, 0)),
        out_shape=jax.ShapeDtypeStruct((bsz, t, FN_W), BF16),
        compiler_params=_cparams(("arbitrary",)),
        name=f"fourier_{t}",
    )(tabs, uv, fnet_w)


def _na_softmax_t(s):
    return jnp.exp2(s - jnp.max(s, axis=0, keepdims=True)).astype(BF16)


def _na_heads_out_t(o0, o1):
    row = lax.broadcasted_iota(jnp.int32, o0.shape, 0)
    out_t = jnp.where(row < NA_HD, o0 / o0[NA_HD:NA_HD + 1, :], o1 / o1[0:1, :])
    return out_t.T.astype(BF16)


def _na_group_kinds():
    margin = NA_WIN - NA_KH
    return ((0, lambda i: 0), (NA_KH // 2, lambda i: i), (margin + NA_KH - NA_GROUP, lambda i: margin))


def _na_kernel(q_ref, k_ref, v_ref, qc_ref, kc_ref, vc_ref, rpb_ref, o_ref, oc_ref,
               vt_ref, sc_ref, bias_ref, *, n_rows, need_ctx):
    lane = lax.broadcasted_iota(jnp.int32, (1, LANES), 1)
    first = lane < NA_HD
    zero = jnp.zeros((), BF16)
    one = jnp.ones((), BF16)
    ne = q_ref.shape[0]
    kcs, vcts = [], []
    for e in range(ne):
        v = v_ref[e]
        vt_ref[e, 0] = jnp.where(first, v, one).T
        vt_ref[e, 1] = jnp.where(first, one, v).T
        kcs.append(kc_ref[e])
        vc = vc_ref[e]
        vcts.append((jnp.where(first, vc, one).T, jnp.where(first, one, vc).T))

    def both_heads(q):
        return jnp.concatenate([jnp.where(first, q, zero), jnp.where(first, zero, q)], axis=0)

    n_groups = n_rows // NA_GROUP
    gq = NA_GROUP * GRID_W
    n_loc = NA_WIN * GRID_W
    margin = NA_WIN - NA_KH

    @pl.when(pl.program_id(1) == 0)
    def _():
        wrow = lax.broadcasted_iota(jnp.int32, (n_loc, GRID_W), 0)
        for hh in range(2):
            for kind, (q_off, a_lo_of) in enumerate(_na_group_kinds()):
                for j in range(NA_GROUP // 2):
                    halves = []
                    for i in (2 * j, 2 * j + 1):
                        start = (NA_KH - 1 - q_off - i + margin) * GRID_W
                        lo = a_lo_of(i) * GRID_W
                        band = (wrow >= lo) & (wrow < lo + NA_KH * GRID_W)
                        halves.append(jnp.where(band, rpb_ref[hh, start:start + n_loc, :], NA_MASKED))
                    bias_ref[hh, kind, :, j * LANES:(j + 1) * LANES] = jnp.concatenate(halves, axis=1)

    def window(g):
        ws = jnp.clip(g * NA_GROUP - NA_KH // 2, 0, n_rows - NA_WIN)
        return (pl.ds(pl.multiple_of(g * gq, gq), gq), pl.ds(pl.multiple_of(ws * GRID_W, gq), n_loc))

    def scores(g, slot):
        qs, ks = window(g)
        kind = jnp.where(g == 0, 0, jnp.where(g == n_groups - 1, 2, 1))
        for e in range(ne):
            q = both_heads(q_ref[e, qs, :])
            s_loc = _dot_nt(k_ref[e, ks, :], q)
            s_ctx = _dot_nt(kcs[e], q)
            for hh in range(2):
                cols = slice(hh * gq, (hh + 1) * gq)
                sc_ref[e, slot, 0:n_loc, cols] = s_loc[:, cols] + bias_ref[hh, kind]
            sc_ref[e, slot, n_loc:, :] = s_ctx

    def finish(g, slot):
        qs, ks = window(g)
        for e in range(ne):
            p = _na_softmax_t(sc_ref[e, slot])
            outs = [_dot(jnp.concatenate([vt_ref[e, hh, :, ks], vcts[e][hh]], axis=1),
                         p[:, hh * gq:(hh + 1) * gq]) for hh in range(2)]
            o_ref[e, qs, :] = _na_heads_out_t(outs[0], outs[1])

    scores(0, 0)

    def pair(i, carry):
        scores(2 * i + 1, 1)
        finish(2 * i, 0)
        scores(2 * i + 2, 0)
        finish(2 * i + 1, 1)
        return carry

    lax.fori_loop(0, n_groups // 2 - 1, pair, 0)
    scores(n_groups - 1, 1)
    finish(n_groups - 2, 0)
    finish(n_groups - 1, 1)

    if need_ctx:
        n_ctx = qc_ref.shape[1]
        for e in range(ne):
            p = _na_softmax_t(_dot_nt(kcs[e], both_heads(qc_ref[e])))
            outs = [_dot(vcts[e][hh], p[:, hh * n_ctx:(hh + 1) * n_ctx]) for hh in range(2)]
            oc_ref[e] = _na_heads_out_t(outs[0], outs[1])
    else:
        oc_ref[...] = jnp.zeros(oc_ref.shape, oc_ref.dtype)


def _na(nqkv, nqkv_c, l, rpb_rows, *, need_ctx):
    bsz, t, _ = nqkv.shape
    n = nqkv_c.shape[1]
    n_rows = t // GRID_W
    assert n_rows % (2 * NA_GROUP) == 0 and n_rows >= NA_WIN + NA_GROUP
    ne = NA_BATCH if bsz % NA_BATCH == 0 else 1
    pairs = NA_HEADS // 2
    n_loc = NA_WIN * GRID_W
    gq = NA_GROUP * GRID_W
    nb = NA_W // LANES
    lat = lambda off: pl.BlockSpec((ne, t, LANES), lambda j, b: (b, 0, off + j))
    ctx = lambda off: pl.BlockSpec((ne, n, LANES), lambda j, b: (b, 0, off + j))
    return pl.pallas_call(
        functools.partial(_na_kernel, n_rows=n_rows, need_ctx=need_ctx),
        grid=(pairs, bsz // ne),
        in_specs=[lat(0), lat(nb), lat(2 * nb), ctx(0), ctx(nb), ctx(2 * nb),
                  pl.BlockSpec((None, 2) + rpb_rows.shape[2:], lambda j, b: (l, j, 0, 0))],
        out_specs=[pl.BlockSpec((ne, t, LANES), lambda j, b: (b, 0, j)),
                   pl.BlockSpec((ne, n, LANES), lambda j, b: (b, 0, j))],
        out_shape=[jax.ShapeDtypeStruct((bsz, t, NA_W), BF16),
                   jax.ShapeDtypeStruct((bsz, n, NA_W), BF16)],
        scratch_shapes=[pltpu.VMEM((ne, 2, LANES, t), BF16),
                        pltpu.VMEM((ne, 2, n_loc + n, 2 * gq), F32),
                        pltpu.VMEM((2, len(_na_group_kinds()), n_loc, gq), F32)],
        compiler_params=_cparams(("arbitrary", "arbitrary")),
        name="na_attn",
    )(nqkv, nqkv, nqkv, nqkv_c, nqkv_c, nqkv_c, rpb_rows)


def _gla_steps(jobs, st_ref, causal_ref, vmask_ref):
    c = GLA_CHUNK
    head = lax.broadcasted_iota(jnp.int32, (1, GLA_QK_PAD), 1) // GLA_DK_PAD
    prep = []
    for qk_ref, vg_ref, b_ref, out_ref, ci, d in jobs:
        rows = pl.ds(pl.multiple_of(ci * GLA_BLOCK, GLA_BLOCK), GLA_BLOCK)
        k = qk_ref[rows, GLA_QK_PAD:].astype(F32)
        v = vg_ref[rows, 0:GLA_V_PAD]
        b = b_ref[rows, d * GLA_QK_PAD:(d + 1) * GLA_QK_PAD]
        lo, hi = b[0:c, :], b[c:, :]
        if d == 0:
            tot1, tot2 = lo[c - 1:c, :], hi[c - 1:c, :]
            b_rel = jnp.concatenate([lo - tot1, hi], axis=0)
        else:
            tot1, tot2 = hi[0:1, :], lo[0:1, :]
            b_rel = jnp.concatenate([lo, hi - tot1], axis=0)
        k_in = k * jnp.exp(-b_rel)
        k_end = k_in * jnp.exp(tot2)
        q_in = q_st = None
        if out_ref is not None:
            q_rel = qk_ref[rows, 0:GLA_QK_PAD].astype(F32) * jnp.exp(b_rel)
            q_in = q_rel.astype(BF16)
            q_st = (q_rel * jnp.exp(tot1)).astype(BF16)
        decay = jnp.exp(jnp.broadcast_to(tot1 + tot2, (GLA_DV_PAD, GLA_QK_PAD)).T)
        prep.append((rows, q_in, q_st, k_in, k_end, v, decay))

    att = [None if q_in is None else
           _dot_nt(q_in, jnp.concatenate([jnp.where(head == h, k_in, 0.0) for h in range(GLA_HEADS)],
                                         axis=0).astype(BF16))
           for (rows, q_in, q_st, k_in, k_end, v, decay) in prep]

    kv = []
    for (rows, q_in, q_st, k_in, k_end, v, decay) in prep:
        k_t = k_end.astype(BF16).T
        kv.append(jnp.concatenate(
            [_dot(k_t[h * GLA_DK_PAD:(h + 1) * GLA_DK_PAD, :], v[:, h * GLA_DV_PAD:(h + 1) * GLA_DV_PAD])
             for h in range(GLA_HEADS)], axis=0))

    inter = []
    zero = jnp.zeros((GLA_DK_PAD, GLA_DV_PAD), BF16)
    for j, ((qk_ref, vg_ref, b_ref, out_ref, ci, d), (rows, q_in, q_st, k_in, k_end, v, decay)) in enumerate(
            zip(jobs, prep)):
        st = st_ref[d]
        if q_st is None:
            inter.append(None)
        else:
            sb = st.astype(BF16)
            s_bd = jnp.concatenate(
                [jnp.concatenate([sb[h * GLA_DK_PAD:(h + 1) * GLA_DK_PAD, :] if h2 == h else zero
                                  for h2 in range(GLA_HEADS)], axis=1) for h in range(GLA_HEADS)], axis=0)
            inter.append(_dot(q_st, s_bd))
        st_ref[d] = st * decay + kv[j]

    for j, ((qk_ref, vg_ref, b_ref, out_ref, ci, d), (rows, q_in, q_st, k_in, k_end, v, decay)) in enumerate(
            zip(jobs, prep)):
        if out_ref is not None:
            a = (att[j] * causal_ref[d]).astype(BF16)
            intra = []
            for pair in range(GLA_HEADS // 2):
                cols = slice(pair * 2 * GLA_DV_PAD, (pair + 1) * 2 * GLA_DV_PAD)
                v_pair = jnp.concatenate([v[:, cols]] * 2, axis=0) * vmask_ref[...]
                intra.append(_dot(a[:, pair * 2 * GLA_BLOCK:(pair + 1) * 2 * GLA_BLOCK], v_pair))
            out_ref[rows, :] = (inter[j] + jnp.concatenate(intra, axis=1)).astype(out_ref.dtype)


def _gla_kernel(qk_ref, vg_ref, b_ref, qkc_ref, vgc_ref, bc_ref, causal_ref, vmask_ref, *rest, t, n, need_ctx):
    if need_ctx:
        of_ref, ob_ref, ocf_ref, ocb_ref, st_ref = rest
    else:
        of_ref, ob_ref, st_ref = rest
    st_ref[...] = jnp.zeros(st_ref.shape, F32)
    nb_ctx = n // GLA_BLOCK
    nb_lat = t // GLA_BLOCK

    def jobs(refs, outs, n_blocks, i):
        return [refs + (outs[0], i, 0), refs + (outs[1], n_blocks - 1 - i, 1)]

    ctx_outs = (ocf_ref, ocb_ref) if need_ctx else (None, None)
    for i in range(nb_ctx):
        _gla_steps(jobs((qkc_ref, vgc_ref, bc_ref), ctx_outs, nb_ctx, i), st_ref, causal_ref, vmask_ref)

    def step(i, carry):
        _gla_steps(jobs((qk_ref, vg_ref, b_ref), (of_ref, ob_ref), nb_lat, i), st_ref, causal_ref, vmask_ref)
        return carry

    lax.fori_loop(0, nb_lat, step, 0, unroll=GLA_UNROLL)


def _gla(gqk, gvg, gb, gqk_c, gvg_c, gb_c, causal, vmask, *, need_ctx):
    bsz, t, _ = gqk.shape
    n = gqk_c.shape[1]
    assert t % GLA_BLOCK == 0 and n % GLA_BLOCK == 0
    row = lambda rows, w_: pl.BlockSpec((None, rows, w_), lambda b: (b, 0, 0))
    out_rows = (t, t, n, n) if need_ctx else (t, t)
    return pl.pallas_call(
        functools.partial(_gla_kernel, t=t, n=n, need_ctx=need_ctx),
        grid=(bsz,),
        in_specs=[row(t, 2 * GLA_QK_PAD), row(t, GLA_V_PAD), row(t, 2 * GLA_QK_PAD),
                  row(n, 2 * GLA_QK_PAD), row(n, GLA_V_PAD), row(n, 2 * GLA_QK_PAD),
                  _const_spec(causal.shape), _const_spec(vmask.shape)],
        out_specs=[row(r, GLA_V_PAD) for r in out_rows],
        out_shape=[jax.ShapeDtypeStruct((bsz, r, GLA_V_PAD), BF16) for r in out_rows],
        scratch_shapes=[pltpu.VMEM((2, GLA_QK_PAD, GLA_DV_PAD), F32)],
        compiler_params=_cparams(("arbitrary",)),
        name="gla_scan",
    )(gqk, gvg, gb, gqk_c, gvg_c, gb_c, causal, vmask)


def _gla_out(of_ref, ob_ref, g_ref, gain_ref, rows):
    heads = []
    for h in range(GLA_HEADS):
        cols = slice(h * GLA_DV_PAD, (h + 1) * GLA_DV_PAD)
        o = of_ref[rows, cols].astype(F32) + ob_ref[rows, cols].astype(F32)
        ms = jnp.sum(o * o, axis=-1, keepdims=True) * (1.0 / GLA_DV)
        g = g_ref[rows, cols].astype(F32)
        heads.append((o * lax.rsqrt(ms + EPS) * gain_ref[:, cols] * (g * jax.nn.sigmoid(g))).astype(BF16))
    return jnp.concatenate(heads, axis=1)


def _post_kernel(x_ref, yfn_ref, yna_ref, of_ref, ob_ref, g_ref, mod_ref, gain_o_ref, gain_ref,
                 wo_ref, w1_ref, w3_ref, w2_ref, o_ref):
    tm = x_ref.shape[0]
    tiles = [slice(r0, r0 + POST_SUB) for r0 in range(0, tm, POST_SUB)]
    ys = [(_dot(yfn_ref[rows, :], wo_ref[0:FN_W, :])
           + _dot(yna_ref[rows, :], wo_ref[FN_W:FN_W + NA_W, :])
           + _dot(_gla_out(of_ref, ob_ref, g_ref, gain_o_ref, rows), wo_ref[FN_W + NA_W:, :]))
          for rows in tiles]
    for rows, y in zip(tiles, ys):
        x1 = x_ref[rows, :] + mod_ref[2:3, :] * y
        h = _modulated_norm(x1, gain_ref[...], mod_ref[3:4, :], mod_ref[4:5, :]).astype(BF16)
        acts = []
        for c in range(D_FF // MXU_W):
            cols = slice(c * MXU_W, (c + 1) * MXU_W)
            u = _dot(h, w1_ref[:, cols])
            g = _dot(h, w3_ref[:, cols])
            acts.append((u * jax.nn.sigmoid(u) * g).astype(BF16))
        acc = _dot(jnp.concatenate(acts, axis=1), w2_ref[...])
        o_ref[rows, :] = x1 + mod_ref[5:6, :] * acc


def _post(x, yfn, yna, o_fwd, o_bwd, gvg, l, mod, mod_row, p, *, tm):
    bsz, t, d = x.shape
    assert t % tm == 0 and tm % POST_SUB == 0
    row = lambda w_: pl.BlockSpec((None, tm, w_), lambda b, i: (b, i, 0))
    gate =pl.BlockSpec((None, tm, GLA_V_PAD), lambda b, i: (b, i, 1))
    return pl.pallas_call(
        _post_kernel,
        grid=(bsz, t // tm),
        in_specs=[row(d), row(FN_W), row(NA_W), row(GLA_V_PAD), row(GLA_V_PAD), gate,
                  _mod_spec(d, l, mod_row), _layer_spec((1, GLA_V_PAD), l),
                  _layer_spec((1, d), l), _layer_spec((MIX_PAD, d), l), _layer_spec((d, D_FF), l),
                  _layer_spec((d, D_FF), l), _layer_spec((D_FF, d), l)],
        out_specs=row(d),
        out_shape=jax.ShapeDtypeStruct((bsz, t, d), F32),
        compiler_params=_cparams(("arbitrary", "arbitrary")),
        name=f"post_{t}",
    )(x, yfn, yna, o_fwd, o_bwd, gvg, mod, p["gain_o"], p["gain_f"], p["wo"], p["w1"], p["w3"], p["w2"])


def _dft_tables(n):
    idx = (np.arange(n)[:, None] * np.arange(n)[None, :]) % n
    ang = 2.0 * np.pi * idx / n
    return np.cos(ang), np.sin(ang)


def _shape_consts(t, n):
    c64, s64 = _dft_tables(FN_GD)
    eye = np.eye(FN_GROUPS)
    cs = np.concatenate([np.kron(eye, c64), np.kron(eye, s64)], axis=1)
    chunk_id = np.arange(SUB) // GLA_CHUNK
    same = chunk_id[:, None] == chunk_id[None, :]
    pos = np.arange(SUB)
    tri = np.stack([same & (pos[None, :] <= pos[:, None]), same & (pos[None, :] >= pos[:, None])])
    cpos = np.arange(GLA_BLOCK)
    fwd = cpos[None, :] <= cpos[:, None]
    gla_causal = np.stack([np.tile(fwd, (1, GLA_HEADS)), np.tile(fwd.T, (1, GLA_HEADS))])
    gla_vmask = ((np.arange(2 * GLA_BLOCK)[:, None] // GLA_BLOCK)
                 == (np.arange(2 * GLA_DV_PAD)[None, :] // GLA_DV_PAD))

    m = GLA_DK // 4
    inv = ROPE_BASE ** (-np.arange(m) / m)
    tok = np.arange(t)
    cos_h = np.ones((t, GLA_DK_PAD))
    sin_lo = np.zeros((t, GLA_DK_PAD))
    sin_hi = np.zeros((t, GLA_DK_PAD))
    for blk, p in enumerate((tok // GRID_W, tok % GRID_W)):
        ang = p[:, None] * inv[None, :]
        o = blk * 2 * m
        cos_h[:, o:o + m] = np.cos(ang)
        cos_h[:, o + m:o + 2 * m] = np.cos(ang)
        sin_lo[:, o:o + m] = -np.sin(ang)
        sin_hi[:, o + m:o + 2 * m] = np.sin(ang)
    rope = np.stack([np.tile(a, (1, GLA_HEADS)) for a in (cos_h, sin_lo, sin_hi)])

    def position_tables(length):
        cos, sin = _dft_tables(length)
        half = length // 2
        return np.stack([cos[:half, 0::2], sin[:half, 0::2], cos[:half, 1::2], sin[:half, 1::2]])

    to_bf16 = lambda a: jnp.asarray(a, F32).astype(BF16)
    return {
        "cs": to_bf16(cs), "tri": jnp.asarray(tri, BF16),
        "gla_causal": jnp.asarray(gla_causal, F32), "gla_vmask": jnp.asarray(gla_vmask, BF16),
        "rope": jnp.asarray(rope, F32),
        "dft_lat": to_bf16(position_tables(t)), "dft_ctx": to_bf16(position_tables(n)),
    }


def _bias_kernel(rpb_ref, onehot_ref, ok_ref, o_ref):
    r = rpb_ref[...]
    onehot = onehot_ref[...]
    t = jnp.zeros(o_ref.shape, F32)
    for _ in range(3):
        piece = r.astype(BF16)
        t = t + _dot(piece, onehot)
        r = r - piece.astype(F32)
    o_ref[...] = jnp.where(ok_ref[...] > 0.0, t * LOG2E, NA_MASKED)


def _window_bias_rows(na_rpb):
    depth, heads, n_dr, n_dc = na_rpb.shape
    cq = np.arange(GRID_W)
    c0 = np.clip(cq - NA_KW // 2, 0, GRID_W - NA_KW)
    col_ok = (cq[None, :] >= c0[:, None]) & (cq[None, :] < c0[:, None] + NA_KW)
    dc = np.clip(cq[None, :] - cq[:, None], -(NA_KW - 1), NA_KW - 1) + NA_KW - 1
    n_dc_pad = n_dc + 1
    n_tab = depth * heads * n_dr
    rows_pad = -(-n_tab // 8) * 8
    onehot = (np.arange(n_dc_pad)[:, None, None] == dc[None]).reshape(n_dc_pad, GRID_W * GRID_W)
    rpb2 = jnp.pad(na_rpb.reshape(n_tab, n_dc), ((0, rows_pad - n_tab), (0, 1)))
    t = pl.pallas_call(
        _bias_kernel,
        out_shape=jax.ShapeDtypeStruct((rows_pad, GRID_W * GRID_W), F32),
        name="na_bias",
    )(rpb2, jnp.asarray(onehot, BF16), jnp.asarray(col_ok.reshape(1, -1), F32))
    t = t[:n_tab].reshape(depth * heads, n_dr, GRID_W, GRID_W)
    t = t.transpose(0, 1, 3, 2).reshape(depth * heads, n_dr * GRID_W, GRID_W)
    margin = NA_WIN - NA_KH
    t = jnp.pad(t, ((0, 0), (margin * GRID_W, margin * GRID_W), (0, 0)))
    return t.reshape(depth, heads, (n_dr + 2 * margin) * GRID_W, GRID_W)


def _pad_heads(w, heads, dim, pad, axis=-1):
    axis = axis % w.ndim
    shape = w.shape
    w = w.reshape(shape[:axis] + (heads, dim) + shape[axis + 1:])
    widths = [(0, 0)] * w.ndim
    widths[axis + 1] = (0, pad - dim)
    return jnp.pad(w, widths).reshape(shape[:axis] + (heads * pad,) + shape[axis + 1:])


def _params(norm_mix, norm_ffn, w_in, fnet_w, na_q_norm, na_k_norm, na_rpb, alpha_w, alpha_b, o_norm, w_out,
            ffn_w1, ffn_w3, ffn_w2):
    depth = w_in.shape[0]
    o = FN_W + 3 * NA_W
    gq = w_in[..., o:o + GLA_QK_W]
    gk = w_in[..., o + GLA_QK_W:o + 2 * GLA_QK_W]
    o2 = o + 2 * GLA_QK_W
    gv = w_in[..., o2:o2 + GLA_V_W]
    gg = w_in[..., o2 + GLA_V_W:o2 + 2 * GLA_V_W]
    ga = w_in[..., o2 + 2 * GLA_V_W:]
    w = jnp.concatenate([
        w_in[..., :o],
        jnp.pad(ga, ((0, 0), (0, 0), (0, GLA_A_PAD - 2 * GLA_RANK))),
        _pad_heads(gq, GLA_HEADS, GLA_DK, GLA_DK_PAD), _pad_heads(gk, GLA_HEADS, GLA_DK, GLA_DK_PAD),
        _pad_heads(gv, GLA_HEADS, GLA_DV, GLA_DV_PAD), _pad_heads(gg, GLA_HEADS, GLA_DV, GLA_DV_PAD),
    ], axis=-1).astype(BF16)

    qkg = jnp.stack([jnp.tile(na_q_norm, (1, NA_HEADS)) * (NA_HD ** -0.5 * LOG2E),
                     jnp.tile(na_k_norm, (1, NA_HEADS))], axis=1)

    aw_pad = _pad_heads(alpha_w, GLA_HEADS, GLA_DK, GLA_DK_PAD)
    aw = jnp.concatenate([jnp.pad(aw_pad[:, 0], ((0, 0), (0, 0), (0, GLA_QK_PAD))),
                          jnp.pad(aw_pad[:, 1], ((0, 0), (0, 0), (GLA_QK_PAD, 0)))], axis=1)
    aw = jnp.pad(aw, ((0, 0), (0, GLA_A_PAD - 2 * GLA_RANK), (0, 0))).astype(BF16)
    ab = _pad_heads(alpha_b, GLA_HEADS, GLA_DK, GLA_DK_PAD).reshape(depth, 1, 2 * GLA_QK_PAD)

    gain_o = _pad_heads(jnp.tile(o_norm, (1, GLA_HEADS)), GLA_HEADS, GLA_DV, GLA_DV_PAD)[:, None, :]
    wo = jnp.concatenate([w_out[:, :FN_W + NA_W],
                          _pad_heads(w_out[:, FN_W + NA_W:], GLA_HEADS, GLA_DV, GLA_DV_PAD, axis=1)],
                         axis=1).astype(BF16)
    return {
        "gain_m": norm_mix[:, None, :], "gain_f": norm_ffn[:, None, :], "w": w, "qkg": qkg, "aw": aw, "ab": ab,
        "rpb_rows": _window_bias_rows(na_rpb), "gain_o": gain_o, "wo": wo, "fw": fnet_w.astype(BF16),
        "w1": ffn_w1.astype(BF16), "w3": ffn_w3.astype(BF16), "w2": ffn_w2.astype(BF16),
    }


def kernel(x, c, ctx, c_ctx, ada_w, ada_b, norm_mix, norm_ffn, w_in, fnet_w, na_q_norm, na_k_norm, na_rpb,
           gla_alpha_w, gla_alpha_b, gla_o_norm, w_out, ffn_w1, ffn_w3, ffn_w2):
    bsz, t, d = x.shape
    n = ctx.shape[1]
    depth = ada_w.shape[0]
    ctx_row = bsz
    assert bsz < MOD_ROWS and n % SUB == 0 and (bsz * n) % CTX_TM == 0
    consts = _shape_consts(t, n)
    p = _params(norm_mix, norm_ffn, w_in, fnet_w, na_q_norm, na_k_norm, na_rpb, gla_alpha_w, gla_alpha_b,
                gla_o_norm, w_out, ffn_w1, ffn_w3, ffn_w2)

    cc = jnp.concatenate([c, c_ctx[None, :], jnp.zeros((MOD_ROWS - bsz - 1, d), F32)])
    mod = _ada_mod(cc, ada_w, ada_b).reshape(depth, MOD_ROWS, 6, d)

    cx = ctx
    flat = lambda a: a.reshape(1, bsz * n, a.shape[-1])
    for l in range(depth):
        need_ctx = l < depth - 1
        uv, nqkv, gqk, gvg, gb = _inproj(x, l, mod, None, p, consts, consts["rope"], tm=INPROJ_TM)
        uv_c, nqkv_c, gqk_c, gvg_c, gb_c = _inproj(cx, l, mod, ctx_row, p, consts, None, tm=n)

        y_fn = _fourier(uv, l, consts["dft_lat"], p["fw"])
        y_na, y_na_c = _na(nqkv, nqkv_c, l, p["rpb_rows"], need_ctx=need_ctx)
        o_gla = _gla(gqk, gvg, gb, gqk_c, gvg_c, gb_c, consts["gla_causal"], consts["gla_vmask"],
                     need_ctx=need_ctx)
        x = _post(x, y_fn, y_na, o_gla[0], o_gla[1], gvg, l, mod, None, p, tm=POST_TM)
        if need_ctx:
            y_fn_c = _fourier(uv_c, l, consts["dft_ctx"], p["fw"])
            cx = _post(flat(cx), flat(y_fn_c), flat(y_na_c), flat(o_gla[2]), flat(o_gla[3]), flat(gvg_c),
                       l, mod, ctx_row, p, tm=CTX_TM).reshape(bsz, n, d)
    return x
```

```python
import functools

import numpy as np
import jax
import jax.numpy as jnp
from jax import lax
from jax.experimental import pallas as pl
from jax.experimental.pallas import tpu as pltpu

F32 = jnp.float32
BF16 = jnp.bfloat16

GRID_W = 64
FN_W = 256
FN_GROUPS = 4
FN_GD = 64
NA_HD = 64
NA_W = 384
NA_HEADS = 6
NA_KH = 8
NA_KW = 16
NA_GROUP = 4
NA_WIN = NA_GROUP + NA_KH
NA_BATCH = 4
LOG2E = float(np.log2(np.e))
NA_MASKED = -1e30
GLA_HEADS = 4
GLA_DK = 48
GLA_DV = 96
GLA_QK_W = GLA_HEADS * GLA_DK
GLA_V_W = GLA_HEADS * GLA_DV
GLA_RANK = 16
GLA_GATE_NORM = 16.0
GLA_CHUNK = 64
GLA_BLOCK = 2 * GLA_CHUNK
GLA_UNROLL = 16
ROPE_BASE = 10000.0
D_FF = 2816
EPS = 1e-6
MOD_ROWS = 16

LANES = 128
MXU_W = 256
VMEM_LIMIT_BYTES = 56 * 1024 * 1024

GLA_DK_PAD = 64
GLA_DV_PAD = 128
GLA_QK_PAD = GLA_HEADS * GLA_DK_PAD
GLA_V_PAD = GLA_HEADS * GLA_DV_PAD
GLA_A_PAD = LANES
MIX_PAD = FN_W + NA_W + GLA_V_PAD

C_FX = 0
C_NQ = C_FX + FN_W
C_NK = C_NQ + NA_W
C_NV = C_NK + NA_W
C_GA = C_NV + NA_W
C_GQ = C_GA + GLA_A_PAD
C_GK = C_GQ + GLA_QK_PAD
C_GV = C_GK + GLA_QK_PAD
C_GG = C_GV + GLA_V_PAD
N_IN_PAD = C_GG + GLA_V_PAD

SUB = 256

INPROJ_TM = 1024
POST_TM = 1024
POST_SUB = 256
CTX_TM = 1024
ADA_TN = 2048


def _cparams(sem):
    return pltpu.CompilerParams(dimension_semantics=sem, vmem_limit_bytes=VMEM_LIMIT_BYTES)


def _const_spec(shape):
    nd = len(shape)
    return pl.BlockSpec(tuple(shape), lambda *_: (0,) * nd, pipeline_mode=pl.Buffered(1))


def _layer_spec(shape, l):
    nd = len(shape)
    return pl.BlockSpec((None,) + tuple(shape), lambda *_: (l,) + (0,) * nd, pipeline_mode=pl.Buffered(1))


def _mod_spec(d, l, mod_row):
    if mod_row is None:
        return pl.BlockSpec((None, None, 6, d), lambda b, i: (l, b, 0, 0))
    return pl.BlockSpec((None, None, 6, d), lambda b, i: (l, mod_row, 0, 0))


def _dot(a, b):
    return jnp.dot(a, b, preferred_element_type=F32)


def _dot_nt(a, b):
    return lax.dot_general(a, b, (((1,), (1,)), ((), ())), preferred_element_type=F32)


def _ada_kernel(c_ref, w_ref, b_ref, o_ref):
    a = c_ref[...]
    a = (a * jax.nn.sigmoid(a)).astype(BF16)
    o_ref[...] = _dot(a, w_ref[...].astype(BF16)) + b_ref[...]


def _ada_mod(cc, ada_w, ada_b):
    depth, d, n = ada_w.shape
    rows = cc.shape[0]
    tn = ADA_TN
    return pl.pallas_call(
        _ada_kernel,
        grid=(depth, n // tn),
        in_specs=[
            pl.BlockSpec((rows, d), lambda l, j: (0, 0)),
            pl.BlockSpec((None, d, tn), lambda l, j: (l, 0, j)),
            pl.BlockSpec((None, 1, tn), lambda l, j: (l, 0, j)),
        ],
        out_specs=pl.BlockSpec((None, rows, tn), lambda l, j: (l, 0, j)),
        out_shape=jax.ShapeDtypeStruct((depth, rows, n), F32),
        compiler_params=_cparams(("arbitrary", "arbitrary")),
        name="ada_mod",
    )(cc, ada_w, ada_b.reshape(depth, 1, n))


def _modulated_norm(x, gain, shift, scale):
    ms = jnp.mean(x * x, axis=-1, keepdims=True)
    return (x * lax.rsqrt(ms + EPS) * gain) * (1.0 + scale) + shift


def _log_sigmoid(x):
    return jnp.minimum(x, 0.0) - jnp.log1p(jnp.exp(-jnp.abs(x)))


def _head_mean_sq(z):
    first = lax.broadcasted_iota(jnp.int32, (1, LANES), 1) < NA_HD
    out = []
    for t in range(NA_W // LANES):
        sq = z[:, t * LANES:(t + 1) * LANES]
        sq = sq * sq
        lo = jnp.sum(jnp.where(first, sq, 0.0), axis=-1, keepdims=True)
        hi = jnp.sum(jnp.where(first, 0.0, sq), axis=-1, keepdims=True)
        out.append(jnp.where(first, lo, hi))
    return jnp.concatenate(out, axis=1) * (1.0 / NA_HD)


def _inproj_kernel(x_ref, mod_ref, gain_ref, w_ref, cs_ref, qkg_ref, rope_ref, aw_ref, ab_ref, tri_ref,
                   uv_ref, nqkv_ref, gqk_ref, gvg_ref, gb_ref, *, rope, tm):
    x = x_ref[...]
    h = _modulated_norm(x, gain_ref[...], mod_ref[0:1, :], mod_ref[1:2, :]).astype(BF16)

    zf = _dot(h, w_ref[:, C_FX:C_NQ]).astype(BF16)
    zva = _dot(h, w_ref[:, C_NV:C_GQ])
    nqkv_ref[:, 2 * NA_W:3 * NA_W] = zva[:, :NA_W].astype(BF16)
    a = zva[:, NA_W:].astype(BF16)
    zqk = _dot(h, w_ref[:, C_NQ:C_NV])
    logits = _dot(a, aw_ref[...]) + ab_ref[...]
    zg = _dot(h, w_ref[:, C_GQ:C_GV])
    uv = _dot(zf, cs_ref[...])
    for j in range(uv_ref.shape[0]):
        uv_ref[j] = uv[:, j * LANES:(j + 1) * LANES]
    gvg_ref[...] = _dot(h, w_ref[:, C_GV:N_IN_PAD]).astype(BF16)

    glog = (_log_sigmoid(logits) * (1.0 / GLA_GATE_NORM)).astype(BF16)
    for s in range(tm // SUB):
        rows = slice(s * SUB, (s + 1) * SUB)
        gb_ref[rows, 0:GLA_QK_PAD] = _dot(tri_ref[0], glog[rows, 0:GLA_QK_PAD])
        gb_ref[rows, GLA_QK_PAD:] = _dot(tri_ref[1], glog[rows, GLA_QK_PAD:])

    for i in range(2):
        z = zqk[:, i * NA_W:(i + 1) * NA_W]
        z = z * lax.rsqrt(_head_mean_sq(z) + EPS) * qkg_ref[i:i + 1, :]
        nqkv_ref[:, i * NA_W:(i + 1) * NA_W] = z.astype(BF16)

    for i in range(2):
        z = zg[:, i * GLA_QK_PAD:(i + 1) * GLA_QK_PAD]
        if rope:
            z = (z * rope_ref[0]
                 + pltpu.roll(z, GLA_QK_PAD - GLA_DK // 4, 1) * rope_ref[1]
                 + pltpu.roll(z, GLA_DK // 4, 1) * rope_ref[2])
        if i == 0:
            z = z * (GLA_DK ** -0.5)
        gqk_ref[:, i * GLA_QK_PAD:(i + 1) * GLA_QK_PAD] = z.astype(BF16)


def _inproj(x, l, mod, mod_row, p, consts, rope_tab, *, tm):
    bsz, t, d = x.shape
    assert t % tm == 0 and tm % SUB == 0
    rope = rope_tab is not None
    if not rope:
        rope_tab = jnp.zeros((3, tm, GLA_QK_PAD), F32)
    rope_idx = (lambda b, i: (0, i, 0)) if rope else (lambda b, i: (0, 0, 0))
    row = lambda w_: pl.BlockSpec((None, tm, w_), lambda b, i: (b, i, 0))
    uv_tiles = 2 * FN_W // LANES
    out_w = (3 * NA_W, 2 * GLA_QK_PAD, 2 * GLA_V_PAD, 2 * GLA_QK_PAD)
    out_dt = (BF16, BF16, BF16, F32)
    return pl.pallas_call(
        functools.partial(_inproj_kernel, rope=rope, tm=tm),
        grid=(bsz, t // tm),
        in_specs=[
            row(d),
            _mod_spec(d, l, mod_row),
            _layer_spec((1, d), l),
            _layer_spec((d, N_IN_PAD), l),
            _const_spec(consts["cs"].shape),
            _layer_spec((2, NA_W), l),
            pl.BlockSpec((3, tm, GLA_QK_PAD), rope_idx),
            _layer_spec((GLA_A_PAD, 2 * GLA_QK_PAD), l),
            _layer_spec((1, 2 * GLA_QK_PAD), l),
            _const_spec((2, SUB, SUB)),
        ],
        out_specs=[pl.BlockSpec((None, uv_tiles, tm, LANES), lambda b, i: (b, 0, i, 0))]
        + [row(w_) for w_ in out_w],
        out_shape=[jax.ShapeDtypeStruct((bsz, uv_tiles, t, LANES), F32)]
        + [jax.ShapeDtypeStruct((bsz, t, w_), dt) for w_, dt in zip(out_w, out_dt)],
        compiler_params=_cparams(("arbitrary", "arbitrary")),
        name="inproj_rope" if rope else "inproj_ctx",
    )(x, mod, p["gain_m"], p["w"], consts["cs"], p["qkg"], rope_tab, p["aw"], p["ab"], consts["tri"])


def _fourier_kernel(tab_ref, uv_ref, w_ref, o_ref, *, scale):
    half = uv_ref.shape[1] // 2

    def positions(parity):
        tiles = [uv_ref[j, pl.ds(parity, half, stride=2), :].astype(BF16) for j in range(uv_ref.shape[0])]
        return jnp.concatenate(tiles[:2], axis=1), jnp.concatenate(tiles[2:], axis=1)

    ue, ve = positions(0)
    uo, vo = positions(1)
    even = _dot(tab_ref[0], ue) - _dot(tab_ref[1], ve)
    odd = _dot(tab_ref[2], uo) - _dot(tab_ref[3], vo)
    w = w_ref[...]
    o_ref[0:half, :] = _dot(((even + odd) * scale).astype(BF16), w).astype(BF16)
    o_ref[half:, :] = _dot(((even - odd) * scale).astype(BF16), w).astype(BF16)


def _fourier(uv, l, tabs, fnet_w):
    bsz, tiles, t, _ = uv.shape
    scale = float((t * FN_GD) ** -0.5)
    return pl.pallas_call(
        functools.partial(_fourier_kernel, scale=scale),
        grid=(bsz,),
        in_specs=[
            _const_spec(tabs.shape),
            pl.BlockSpec((None, tiles, t, LANES), lambda b: (b, 0, 0, 0)),
            _layer_spec((FN_W, FN_W), l),
        ],
        out_specs=pl.BlockSpec((None, t, FN_W), lambda b: (b, 0, 0)),
        out_shape=jax.ShapeDtypeStruct((bsz, t, FN_W), BF16),
        compiler_params=_cparams(("arbitrary",)),
        name=f"fourier_{t}",
    )(tabs, uv, fnet_w)


def _na_softmax_t(s):
    return jnp.exp2(s - jnp.max(s, axis=0, keepdims=True)).astype(BF16)


def _na_heads_out_t(o0, o1):
    row = lax.broadcasted_iota(jnp.int32, o0.shape, 0)
    out_t = jnp.where(row < NA_HD, o0 / o0[NA_HD:NA_HD + 1, :], o1 / o1[0:1, :])
    return out_t.T.astype(BF16)


def _na_group_kinds():
    margin = NA_WIN - NA_KH
    return ((0, lambda i: 0), (NA_KH // 2, lambda i: i), (margin + NA_KH - NA_GROUP, lambda i: margin))


def _na_kernel(q_ref, k_ref, v_ref, qc_ref, kc_ref, vc_ref, rpb_ref, o_ref, oc_ref,
               vt_ref, sc_ref, bias_ref, *, n_rows, need_ctx):
    lane = lax.broadcasted_iota(jnp.int32, (1, LANES), 1)
    first = lane < NA_HD
    zero = jnp.zeros((), BF16)
    one = jnp.ones((), BF16)
    ne = q_ref.shape[0]
    kcs, vcts = [], []
    for e in range(ne):
        v = v_ref[e]
        vt_ref[e, 0] = jnp.where(first, v, one).T
        vt_ref[e, 1] = jnp.where(first, one, v).T
        kcs.append(kc_ref[e])
        vc = vc_ref[e]
        vcts.append((jnp.where(first, vc, one).T, jnp.where(first, one, vc).T))

    def both_heads(q):
        return jnp.concatenate([jnp.where(first, q, zero), jnp.where(first, zero, q)], axis=0)

    n_groups = n_rows // NA_GROUP
    gq = NA_GROUP * GRID_W
    n_loc = NA_WIN * GRID_W
    margin = NA_WIN - NA_KH

    @pl.when(pl.program_id(1) == 0)
    def _():
        wrow = lax.broadcasted_iota(jnp.int32, (n_loc, GRID_W), 0)
        for hh in range(2):
            for kind, (q_off, a_lo_of) in enumerate(_na_group_kinds()):
                for j in range(NA_GROUP // 2):
                    halves = []
                    for i in (2 * j, 2 * j + 1):
                        start = (NA_KH - 1 - q_off - i + margin) * GRID_W
                        lo = a_lo_of(i) * GRID_W
                        band = (wrow >= lo) & (wrow < lo + NA_KH * GRID_W)
                        halves.append(jnp.where(band, rpb_ref[hh, start:start + n_loc, :], NA_MASKED))
                    bias_ref[hh, kind, :, j * LANES:(j + 1) * LANES] = jnp.concatenate(halves, axis=1)

    def window(g):
        ws = jnp.clip(g * NA_GROUP - NA_KH // 2, 0, n_rows - NA_WIN)
        return (pl.ds(pl.multiple_of(g * gq, gq), gq), pl.ds(pl.multiple_of(ws * GRID_W, gq), n_loc))

    def scores(g, slot):
        qs, ks = window(g)
        kind = jnp.where(g == 0, 0, jnp.where(g == n_groups - 1, 2, 1))
        for e in range(ne):
            q = both_heads(q_ref[e, qs, :])
            s_loc = _dot_nt(k_ref[e, ks, :], q)
            s_ctx = _dot_nt(kcs[e], q)
            for hh in range(2):
                cols = slice(hh * gq, (hh + 1) * gq)
                sc_ref[e, slot, 0:n_loc, cols] = s_loc[:, cols] + bias_ref[hh, kind]
            sc_ref[e, slot, n_loc:, :] = s_ctx

    def finish(g, slot):
        qs, ks = window(g)
        for e in range(ne):
            p = _na_softmax_t(sc_ref[e, slot])
            outs = [_dot(jnp.concatenate([vt_ref[e, hh, :, ks], vcts[e][hh]], axis=1),
                         p[:, hh * gq:(hh + 1) * gq]) for hh in range(2)]
            o_ref[e, qs, :] = _na_heads_out_t(outs[0], outs[1])

    scores(0, 0)

    def pair(i, carry):
        scores(2 * i + 1, 1)
        finish(2 * i, 0)
        scores(2 * i + 2, 0)
        finish(2 * i + 1, 1)
        return carry

    lax.fori_loop(0, n_groups // 2 - 1, pair, 0)
    scores(n_groups - 1, 1)
    finish(n_groups - 2, 0)
    finish(n_groups - 1, 1)

    if need_ctx:
        n_ctx = qc_ref.shape[1]
        for e in range(ne):
            p = _na_softmax_t(_dot_nt(kcs[e], both_heads(qc_ref[e])))
            outs = [_dot(vcts[e][hh], p[:, hh * n_ctx:(hh + 1) * n_ctx]) for hh in range(2)]
            oc_ref[e] = _na_heads_out_t(outs[0], outs[1])
    else:
        oc_ref[...] = jnp.zeros(oc_ref.shape, oc_ref.dtype)


def _na(nqkv, nqkv_c, l, rpb_rows, *, need_ctx):
    bsz, t, _ = nqkv.shape
    n = nqkv_c.shape[1]
    n_rows = t // GRID_W
    assert n_rows % (2 * NA_GROUP) == 0 and n_rows >= NA_WIN + NA_GROUP
    ne = NA_BATCH if bsz % NA_BATCH == 0 else 1
    pairs = NA_HEADS // 2
    n_loc = NA_WIN * GRID_W
    gq = NA_GROUP * GRID_W
    nb = NA_W // LANES
    lat = lambda off: pl.BlockSpec((ne, t, LANES), lambda j, b: (b, 0, off + j))
    ctx = lambda off: pl.BlockSpec((ne, n, LANES), lambda j, b: (b, 0, off + j))
    return pl.pallas_call(
        functools.partial(_na_kernel, n_rows=n_rows, need_ctx=need_ctx),
        grid=(pairs, bsz // ne),
        in_specs=[lat(0), lat(nb), lat(2 * nb), ctx(0), ctx(nb), ctx(2 * nb),
                  pl.BlockSpec((None, 2) + rpb_rows.shape[2:], lambda j, b: (l, j, 0, 0))],
        out_specs=[pl.BlockSpec((ne, t, LANES), lambda j, b: (b, 0, j)),
                   pl.BlockSpec((ne, n, LANES), lambda j, b: (b, 0, j))],
        out_shape=[jax.ShapeDtypeStruct((bsz, t, NA_W), BF16),
                   jax.ShapeDtypeStruct((bsz, n, NA_W), BF16)],
        scratch_shapes=[pltpu.VMEM((ne, 2, LANES, t), BF16),
                        pltpu.VMEM((ne, 2, n_loc + n, 2 * gq), F32),
                        pltpu.VMEM((2, len(_na_group_kinds()), n_loc, gq), F32)],
        compiler_params=_cparams(("arbitrary", "arbitrary")),
        name="na_attn",
    )(nqkv, nqkv, nqkv, nqkv_c, nqkv_c, nqkv_c, rpb_rows)


def _gla_steps(jobs, st_ref, causal_ref, vmask_ref):
    c = GLA_CHUNK
    head = lax.broadcasted_iota(jnp.int32, (1, GLA_QK_PAD), 1) // GLA_DK_PAD
    prep = []
    for qk_ref, vg_ref, b_ref, out_ref, ci, d in jobs:
        rows = pl.ds(pl.multiple_of(ci * GLA_BLOCK, GLA_BLOCK), GLA_BLOCK)
        k = qk_ref[rows, GLA_QK_PAD:].astype(F32)
        v = vg_ref[rows, 0:GLA_V_PAD]
        b = b_ref[rows, d * GLA_QK_PAD:(d + 1) * GLA_QK_PAD]
        lo, hi = b[0:c, :], b[c:, :]
        if d == 0:
            tot1, tot2 = lo[c - 1:c, :], hi[c - 1:c, :]
            b_rel = jnp.concatenate([lo - tot1, hi], axis=0)
        else:
            tot1, tot2 = hi[0:1, :], lo[0:1, :]
            b_rel = jnp.concatenate([lo, hi - tot1], axis=0)
        k_in = k * jnp.exp(-b_rel)
        k_end = k_in * jnp.exp(tot2)
        q_in = q_st = None
        if out_ref is not None:
            q_rel = qk_ref[rows, 0:GLA_QK_PAD].astype(F32) * jnp.exp(b_rel)
            q_in = q_rel.astype(BF16)
            q_st = (q_rel * jnp.exp(tot1)).astype(BF16)
        decay = jnp.exp(jnp.broadcast_to(tot1 + tot2, (GLA_DV_PAD, GLA_QK_PAD)).T)
        prep.append((rows, q_in, q_st, k_in, k_end, v, decay))

    att = [None if q_in is None else
           _dot_nt(q_in, jnp.concatenate([jnp.where(head == h, k_in, 0.0) for h in range(GLA_HEADS)],
                                         axis=0).astype(BF16))
           for (rows, q_in, q_st, k_in, k_end, v, decay) in prep]

    kv = []
    for (rows, q_in, q_st, k_in, k_end, v, decay) in prep:
        k_t = k_end.astype(BF16).T
        kv.append(jnp.concatenate(
            [_dot(k_t[h * GLA_DK_PAD:(h + 1) * GLA_DK_PAD, :], v[:, h * GLA_DV_PAD:(h + 1) * GLA_DV_PAD])
             for h in range(GLA_HEADS)], axis=0))

    inter = []
    zero = jnp.zeros((GLA_DK_PAD, GLA_DV_PAD), BF16)
    for j, ((qk_ref, vg_ref, b_ref, out_ref, ci, d), (rows, q_in, q_st, k_in, k_end, v, decay)) in enumerate(
            zip(jobs, prep)):
        st = st_ref[d]
        if q_st is None:
            inter.append(None)
        else:
            sb = st.astype(BF16)
            s_bd = jnp.concatenate(
                [jnp.concatenate([sb[h * GLA_DK_PAD:(h + 1) * GLA_DK_PAD, :] if h2 == h else zero
                                  for h2 in range(GLA_HEADS)], axis=1) for h in range(GLA_HEADS)], axis=0)
            inter.append(_dot(q_st, s_bd))
        st_ref[d] = st * decay + kv[j]

    for j, ((qk_ref, vg_ref, b_ref, out_ref, ci, d), (rows, q_in, q_st, k_in, k_end, v, decay)) in enumerate(
            zip(jobs, prep)):
        if out_ref is not None:
            a = (att[j] * causal_ref[d]).astype(BF16)
            intra = []
            for pair in range(GLA_HEADS // 2):
                cols = slice(pair * 2 * GLA_DV_PAD, (pair + 1) * 2 * GLA_DV_PAD)
                v_pair = jnp.concatenate([v[:, cols]] * 2, axis=0) * vmask_ref[...]
                intra.append(_dot(a[:, pair * 2 * GLA_BLOCK:(pair + 1) * 2 * GLA_BLOCK], v_pair))
            out_ref[rows, :] = (inter[j] + jnp.concatenate(intra, axis=1)).astype(out_ref.dtype)


def _gla_kernel(qk_ref, vg_ref, b_ref, qkc_ref, vgc_ref, bc_ref, causal_ref, vmask_ref, *rest, t, n, need_ctx):
    if need_ctx:
        of_ref, ob_ref, ocf_ref, ocb_ref, st_ref = rest
    else:
        of_ref, ob_ref, st_ref = rest
    st_ref[...] = jnp.zeros(st_ref.shape, F32)
    nb_ctx = n // GLA_BLOCK
    nb_lat = t // GLA_BLOCK

    def jobs(refs, outs, n_blocks, i):
        return [refs + (outs[0], i, 0), refs + (outs[1], n_blocks - 1 - i, 1)]

    ctx_outs = (ocf_ref, ocb_ref) if need_ctx else (None, None)
    for i in range(nb_ctx):
        _gla_steps(jobs((qkc_ref, vgc_ref, bc_ref), ctx_outs, nb_ctx, i), st_ref, causal_ref, vmask_ref)

    def step(i, carry):
        _gla_steps(jobs((qk_ref, vg_ref, b_ref), (of_ref, ob_ref), nb_lat, i), st_ref, causal_ref, vmask_ref)
        return carry

    lax.fori_loop(0, nb_lat, step, 0, unroll=GLA_UNROLL)


def _gla(gqk, gvg, gb, gqk_c, gvg_c, gb_c, causal, vmask, *, need_ctx):
    bsz, t, _ = gqk.shape
    n = gqk_c.shape[1]
    assert t % GLA_BLOCK == 0 and n % GLA_BLOCK == 0
    row = lambda rows, w_: pl.BlockSpec((None, rows, w_), lambda b: (b, 0, 0))
    out_rows = (t, t, n, n) if need_ctx else (t, t)
    return pl.pallas_call(
        functools.partial(_gla_kernel, t=t, n=n, need_ctx=need_ctx),
        grid=(bsz,),
        in_specs=[row(t, 2 * GLA_QK_PAD), row(t, GLA_V_PAD), row(t, 2 * GLA_QK_PAD),
                  row(n, 2 * GLA_QK_PAD), row(n, GLA_V_PAD), row(n, 2 * GLA_QK_PAD),
                  _const_spec(causal.shape), _const_spec(vmask.shape)],
        out_specs=[row(r, GLA_V_PAD) for r in out_rows],
        out_shape=[jax.ShapeDtypeStruct((bsz, r, GLA_V_PAD), BF16) for r in out_rows],
        scratch_shapes=[pltpu.VMEM((2, GLA_QK_PAD, GLA_DV_PAD), F32)],
        compiler_params=_cparams(("arbitrary",)),
        name="gla_scan",
    )(gqk, gvg, gb, gqk_c, gvg_c, gb_c, causal, vmask)


def _gla_out(of_ref, ob_ref, g_ref, gain_ref, rows):
    heads = []
    for h in range(GLA_HEADS):
        cols = slice(h * GLA_DV_PAD, (h + 1) * GLA_DV_PAD)
        o = of_ref[rows, cols].astype(F32) + ob_ref[rows, cols].astype(F32)
        ms = jnp.sum(o * o, axis=-1, keepdims=True) * (1.0 / GLA_DV)
        g = g_ref[rows, cols].astype(F32)
        heads.append((o * lax.rsqrt(ms + EPS) * gain_ref[:, cols] * (g * jax.nn.sigmoid(g))).astype(BF16))
    return jnp.concatenate(heads, axis=1)


def _post_kernel(x_ref, yfn_ref, yna_ref, of_ref, ob_ref, g_ref, mod_ref, gain_o_ref, gain_ref,
                 wo_ref, w1_ref, w3_ref, w2_ref, o_ref):
    tm = x_ref.shape[0]
    tiles = [slice(r0, r0 + POST_SUB) for r0 in range(0, tm, POST_SUB)]
    ys = [(_dot(yfn_ref[rows, :], wo_ref[0:FN_W, :])
           + _dot(yna_ref[rows, :], wo_ref[FN_W:FN_W + NA_W, :])
           + _dot(_gla_out(of_ref, ob_ref, g_ref, gain_o_ref, rows), wo_ref[FN_W + NA_W:, :]))
          for rows in tiles]
    for rows, y in zip(tiles, ys):
        x1 = x_ref[rows, :] + mod_ref[2:3, :] * y
        h = _modulated_norm(x1, gain_ref[...], mod_ref[3:4, :], mod_ref[4:5, :]).astype(BF16)
        acts = []
        for c in range(D_FF // MXU_W):
            cols = slice(c * MXU_W, (c + 1) * MXU_W)
            u = _dot(h, w1_ref[:, cols])
            g = _dot(h, w3_ref[:, cols])
            acts.append((u * jax.nn.sigmoid(u) * g).astype(BF16))
        acc = _dot(jnp.concatenate(acts, axis=1), w2_ref[...])
        o_ref[rows, :] = x1 + mod_ref[5:6, :] * acc


def _post(x, yfn, yna, o_fwd, o_bwd, gvg, l, mod, mod_row, p, *, tm):
    bsz, t, d = x.shape
    assert t % tm == 0 and tm % POST_SUB == 0
    row = lambda w_: pl.BlockSpec((None, tm, w_), lambda b, i: (b, i, 0))
    gate =pl.BlockSpec((None, tm, GLA_V_PAD), lambda b, i: (b, i, 1))
    return pl.pallas_call(
        _post_kernel,
        grid=(bsz, t // tm),
        in_specs=[row(d), row(FN_W), row(NA_W), row(GLA_V_PAD), row(GLA_V_PAD), gate,
                  _mod_spec(d, l, mod_row), _layer_spec((1, GLA_V_PAD), l),
                  _layer_spec((1, d), l), _layer_spec((MIX_PAD, d), l), _layer_spec((d, D_FF), l),
                  _layer_spec((d, D_FF), l), _layer_spec((D_FF, d), l)],
        out_specs=row(d),
        out_shape=jax.ShapeDtypeStruct((bsz, t, d), F32),
        compiler_params=_cparams(("arbitrary", "arbitrary")),
        name=f"post_{t}",
    )(x, yfn, yna, o_fwd, o_bwd, gvg, mod, p["gain_o"], p["gain_f"], p["wo"], p["w1"], p["w3"], p["w2"])


def _dft_tables(n):
    idx = (np.arange(n)[:, None] * np.arange(n)[None, :]) % n
    ang = 2.0 * np.pi * idx / n
    return np.cos(ang), np.sin(ang)


def _shape_consts(t, n):
    c64, s64 = _dft_tables(FN_GD)
    eye = np.eye(FN_GROUPS)
    cs = np.concatenate([np.kron(eye, c64), np.kron(eye, s64)], axis=1)
    chunk_id = np.arange(SUB) // GLA_CHUNK
    same = chunk_id[:, None] == chunk_id[None, :]
    pos = np.arange(SUB)
    tri = np.stack([same & (pos[None, :] <= pos[:, None]), same & (pos[None, :] >= pos[:, None])])
    cpos = np.arange(GLA_BLOCK)
    fwd = cpos[None, :] <= cpos[:, None]
    gla_causal = np.stack([np.tile(fwd, (1, GLA_HEADS)), np.tile(fwd.T, (1, GLA_HEADS))])
    gla_vmask = ((np.arange(2 * GLA_BLOCK)[:, None] // GLA_BLOCK)
                 == (np.arange(2 * GLA_DV_PAD)[None, :] // GLA_DV_PAD))

    m = GLA_DK // 4
    inv = ROPE_BASE ** (-np.arange(m) / m)
    tok = np.arange(t)
    cos_h = np.ones((t, GLA_DK_PAD))
    sin_lo = np.zeros((t, GLA_DK_PAD))
    sin_hi = np.zeros((t, GLA_DK_PAD))
    for blk, p in enumerate((tok // GRID_W, tok % GRID_W)):
        ang = p[:, None] * inv[None, :]
        o = blk * 2 * m
        cos_h[:, o:o + m] = np.cos(ang)
        cos_h[:, o + m:o + 2 * m] = np.cos(ang)
        sin_lo[:, o:o + m] = -np.sin(ang)
        sin_hi[:, o + m:o + 2 * m] = np.sin(ang)
    rope = np.stack([np.tile(a, (1, GLA_HEADS)) for a in (cos_h, sin_lo, sin_hi)])

    def position_tables(length):
        cos, sin = _dft_tables(length)
        half = length // 2
        return np.stack([cos[:half, 0::2], sin[:half, 0::2], cos[:half, 1::2], sin[:half, 1::2]])

    to_bf16 = lambda a: jnp.asarray(a, F32).astype(BF16)
    return {
        "cs": to_bf16(cs), "tri": jnp.asarray(tri, BF16),
        "gla_causal": jnp.asarray(gla_causal, F32), "gla_vmask": jnp.asarray(gla_vmask, BF16),
        "rope": jnp.asarray(rope, F32),
        "dft_lat": to_bf16(position_tables(t)), "dft_ctx": to_bf16(position_tables(n)),
    }


def _bias_kernel(rpb_ref, onehot_ref, ok_ref, o_ref):
    r = rpb_ref[...]
    onehot = onehot_ref[...]
    t = jnp.zeros(o_ref.shape, F32)
    for _ in range(3):
        piece = r.astype(BF16)
        t = t + _dot(piece, onehot)
        r = r - piece.astype(F32)
    o_ref[...] = jnp.where(ok_ref[...] > 0.0, t * LOG2E, NA_MASKED)


def _window_bias_rows(na_rpb):
    depth, heads, n_dr, n_dc = na_rpb.shape
    cq = np.arange(GRID_W)
    c0 = np.clip(cq - NA_KW // 2, 0, GRID_W - NA_KW)
    col_ok = (cq[None, :] >= c0[:, None]) & (cq[None, :] < c0[:, None] + NA_KW)
    dc = np.clip(cq[None, :] - cq[:, None], -(NA_KW - 1), NA_KW - 1) + NA_KW - 1
    n_dc_pad = n_dc + 1
    n_tab = depth * heads * n_dr
    rows_pad = -(-n_tab // 8) * 8
    onehot = (np.arange(n_dc_pad)[:, None, None] == dc[None]).reshape(n_dc_pad, GRID_W * GRID_W)
    rpb2 = jnp.pad(na_rpb.reshape(n_tab, n_dc), ((0, rows_pad - n_tab), (0, 1)))
    t = pl.pallas_call(
        _bias_kernel,
        out_shape=jax.ShapeDtypeStruct((rows_pad, GRID_W * GRID_W), F32),
        name="na_bias",
    )(rpb2, jnp.asarray(onehot, BF16), jnp.asarray(col_ok.reshape(1, -1), F32))
    t = t[:n_tab].reshape(depth * heads, n_dr, GRID_W, GRID_W)
    t = t.transpose(0, 1, 3, 2).reshape(depth * heads, n_dr * GRID_W, GRID_W)
    margin = NA_WIN - NA_KH
    t = jnp.pad(t, ((0, 0), (margin * GRID_W, margin * GRID_W), (0, 0)))
    return t.reshape(depth, heads, (n_dr + 2 * margin) * GRID_W, GRID_W)


def _pad_heads(w, heads, dim, pad, axis=-1):
    axis = axis % w.ndim
    shape = w.shape
    w = w.reshape(shape[:axis] + (heads, dim) + shape[axis + 1:])
    widths = [(0, 0)] * w.ndim
    widths[axis + 1] = (0, pad - dim)
    return jnp.pad(w, widths).reshape(shape[:axis] + (heads * pad,) + shape[axis + 1:])


def _params(norm_mix, norm_ffn, w_in, fnet_w, na_q_norm, na_k_norm, na_rpb, alpha_w, alpha_b, o_norm, w_out,
            ffn_w1, ffn_w3, ffn_w2):
    depth = w_in.shape[0]
    o = FN_W + 3 * NA_W
    gq = w_in[..., o:o + GLA_QK_W]
    gk = w_in[..., o + GLA_QK_W:o + 2 * GLA_QK_W]
    o2 = o + 2 * GLA_QK_W
    gv = w_in[..., o2:o2 + GLA_V_W]
    gg = w_in[..., o2 + GLA_V_W:o2 + 2 * GLA_V_W]
    ga = w_in[..., o2 + 2 * GLA_V_W:]
    w = jnp.concatenate([
        w_in[..., :o],
        jnp.pad(ga, ((0, 0), (0, 0), (0, GLA_A_PAD - 2 * GLA_RANK))),
        _pad_heads(gq, GLA_HEADS, GLA_DK, GLA_DK_PAD), _pad_heads(gk, GLA_HEADS, GLA_DK, GLA_DK_PAD),
        _pad_heads(gv, GLA_HEADS, GLA_DV, GLA_DV_PAD), _pad_heads(gg, GLA_HEADS, GLA_DV, GLA_DV_PAD),
    ], axis=-1).astype(BF16)

    qkg = jnp.stack([jnp.tile(na_q_norm, (1, NA_HEADS)) * (NA_HD ** -0.5 * LOG2E),
                     jnp.tile(na_k_norm, (1, NA_HEADS))], axis=1)

    aw_pad = _pad_heads(alpha_w, GLA_HEADS, GLA_DK, GLA_DK_PAD)
    aw = jnp.concatenate([jnp.pad(aw_pad[:, 0], ((0, 0), (0, 0), (0, GLA_QK_PAD))),
                          jnp.pad(aw_pad[:, 1], ((0, 0), (0, 0), (GLA_QK_PAD, 0)))], axis=1)
    aw = jnp.pad(aw, ((0, 0), (0, GLA_A_PAD - 2 * GLA_RANK), (0, 0))).astype(BF16)
    ab = _pad_heads(alpha_b, GLA_HEADS, GLA_DK, GLA_DK_PAD).reshape(depth, 1, 2 * GLA_QK_PAD)

    gain_o = _pad_heads(jnp.tile(o_norm, (1, GLA_HEADS)), GLA_HEADS, GLA_DV, GLA_DV_PAD)[:, None, :]
    wo = jnp.concatenate([w_out[:, :FN_W + NA_W],
                          _pad_heads(w_out[:, FN_W + NA_W:], GLA_HEADS, GLA_DV, GLA_DV_PAD, axis=1)],
                         axis=1).astype(BF16)
    return {
        "gain_m": norm_mix[:, None, :], "gain_f": norm_ffn[:, None, :], "w": w, "qkg": qkg, "aw": aw, "ab": ab,
        "rpb_rows": _window_bias_rows(na_rpb), "gain_o": gain_o, "wo": wo, "fw": fnet_w.astype(BF16),
        "w1": ffn_w1.astype(BF16), "w3": ffn_w3.astype(BF16), "w2": ffn_w2.astype(BF16),
    }


def kernel(x, c, ctx, c_ctx, ada_w, ada_b, norm_mix, norm_ffn, w_in, fnet_w, na_q_norm, na_k_norm, na_rpb,
           gla_alpha_w, gla_alpha_b, gla_o_norm, w_out, ffn_w1, ffn_w3, ffn_w2):
    bsz, t, d = x.shape
    n = ctx.shape[1]
    depth = ada_w.shape[0]
    ctx_row = bsz
    assert bsz < MOD_ROWS and n % SUB == 0 and (bsz * n) % CTX_TM == 0
    consts = _shape_consts(t, n)
    p = _params(norm_mix, norm_ffn, w_in, fnet_w, na_q_norm, na_k_norm, na_rpb, gla_alpha_w, gla_alpha_b,
                gla_o_norm, w_out, ffn_w1, ffn_w3, ffn_w2)

    cc = jnp.concatenate([c, c_ctx[None, :], jnp.zeros((MOD_ROWS - bsz - 1, d), F32)])
    mod = _ada_mod(cc, ada_w, ada_b).reshape(depth, MOD_ROWS, 6, d)

    cx = ctx
    flat = lambda a: a.reshape(1, bsz * n, a.shape[-1])
    for l in range(depth):
        need_ctx = l < depth - 1
        uv, nqkv, gqk, gvg, gb = _inproj(x, l, mod, None, p, consts, consts["rope"], tm=INPROJ_TM)
        uv_c, nqkv_c, gqk_c, gvg_c, gb_c = _inproj(cx, l, mod, ctx_row, p, consts, None, tm=n)

        y_fn = _fourier(uv, l, consts["dft_lat"], p["fw"])
        y_na, y_na_c = _na(nqkv, nqkv_c, l, p["rpb_rows"], need_ctx=need_ctx)
        o_gla = _gla(gqk, gvg, gb, gqk_c, gvg_c, gb_c, consts["gla_causal"], consts["gla_vmask"],
                     need_ctx=need_ctx)
        x = _post(x, y_fn, y_na, o_gla[0], o_gla[1], gvg, l, mod, None, p, tm=POST_TM)
        if need_ctx:
            y_fn_c = _fourier(uv_c, l, consts["dft_ctx"], p["fw"])
            cx = _post(flat(cx), flat(y_fn_c), flat(y_na_c), flat(o_gla[2]), flat(o_gla[3]), flat(gvg_c),
                       l, mod, ctx_row, p, tm=CTX_TM).reshape(bsz, n, d)
    return x
```

```python
import functools

import numpy as np
import jax
import jax.numpy as jnp
from jax import lax
from jax.experimental import pallas as pl
from jax.experimental.pallas import tpu as pltpu

F32 = jnp.float32
BF16 = jnp.bfloat16

D_MODEL = 1024
GRID_W = 64
FN_W = 256
FN_GROUPS = 4
FN_GD = 64
NA_HD = 64
NA_W = 384
NA_HEADS = 6
NA_KH = 8
NA_KW = 16
NA_GROUP = 4
NA_WIN = NA_GROUP + NA_KH
NA_BATCH = 4
LOG2E = float(np.log2(np.e))
NA_MASKED = -1e30
GLA_HEADS = 4
GLA_DK = 48
GLA_DV = 96
GLA_QK_W = GLA_HEADS * GLA_DK
GLA_V_W = GLA_HEADS * GLA_DV
GLA_RANK = 16
GLA_GATE_NORM = 16.0
GLA_CHUNK = 64
GLA_BLOCK = 2 * GLA_CHUNK
GLA_UNROLL = 16
ROPE_BASE = 10000.0
D_FF = 2816
EPS = 1e-6
MOD_ROWS = 16

LANES = 128
MXU_W = 256
VMEM_LIMIT_BYTES = 56 * 1024 * 1024

GLA_DK_PAD = 64
GLA_DV_PAD = 128
GLA_QK_PAD = GLA_HEADS * GLA_DK_PAD
GLA_V_PAD = GLA_HEADS * GLA_DV_PAD
GLA_A_PAD = LANES
MIX_PAD = FN_W + NA_W + GLA_V_PAD

C_FX = 0
C_NQ = C_FX + FN_W
C_NK = C_NQ + NA_W
C_NV = C_NK + NA_W
C_GA = C_NV + NA_W
C_GQ = C_GA + GLA_A_PAD
C_GK = C_GQ + GLA_QK_PAD
C_GV = C_GK + GLA_QK_PAD
C_GG = C_GV + GLA_V_PAD
N_IN_PAD = C_GG + GLA_V_PAD

SUB = 256

INPROJ_TM = 1024
POST_TM = 1024
POST_SUB = 256
CTX_TM = 512
ADA_TN = 1024


def _cparams(sem):
    return pltpu.CompilerParams(dimension_semantics=sem, vmem_limit_bytes=VMEM_LIMIT_BYTES)


def _const_spec(shape):
    nd = len(shape)
    return pl.BlockSpec(tuple(shape), lambda *_: (0,) * nd, pipeline_mode=pl.Buffered(1))


def _layer_spec(shape, l):
    nd = len(shape)
    return pl.BlockSpec((None,) + tuple(shape), lambda *_: (l,) + (0,) * nd, pipeline_mode=pl.Buffered(1))


def _mod_spec(d, l, mod_row):
    if mod_row is None:
        return pl.BlockSpec((None, None, 6, d), lambda b, i: (l, b, 0, 0))
    return pl.BlockSpec((None, None, 6, d), lambda b, i: (l, mod_row, 0, 0))


def _dot(a, b):
    return jnp.dot(a, b, preferred_element_type=F32)


def _dot_nt(a, b):
    return lax.dot_general(a, b, (((1,), (1,)), ((), ())), preferred_element_type=F32)


def _ada_kernel(c_ref, w_ref, b_ref, o_ref):
    a = c_ref[...]
    a = (a * jax.nn.sigmoid(a)).astype(BF16)
    o_ref[...] = _dot(a, w_ref[...].astype(BF16)) + b_ref[...]


def _ada_mod(cc, ada_w, ada_b):
    depth, d, n = ada_w.shape
    rows = cc.shape[0]
    tn = ADA_TN
    return pl.pallas_call(
        _ada_kernel,
        grid=(depth, n // tn),
        in_specs=[
            pl.BlockSpec((rows, d), lambda l, j: (0, 0)),
            pl.BlockSpec((None, d, tn), lambda l, j: (l, 0, j)),
            pl.BlockSpec((None, 1, tn), lambda l, j: (l, 0, j)),
        ],
        out_specs=pl.BlockSpec((None, rows, tn), lambda l, j: (l, 0, j)),
        out_shape=jax.ShapeDtypeStruct((depth, rows, n), F32),
        compiler_params=_cparams(("arbitrary", "arbitrary")),
        name="ada_mod",
    )(cc, ada_w, ada_b.reshape(depth, 1, n))


def _modulated_norm(x, gain, shift, scale):
    ms = jnp.mean(x * x, axis=-1, keepdims=True)
    return (x * lax.rsqrt(ms + EPS) * gain) * (1.0 + scale) + shift


def _log_sigmoid(x):
    return jnp.minimum(x, 0.0) - jnp.log1p(jnp.exp(-jnp.abs(x)))


def _head_mean_sq(z):
    first = lax.broadcasted_iota(jnp.int32, (1, LANES), 1) < NA_HD
    out = []
    for t in range(NA_W // LANES):
        sq = z[:, t * LANES:(t + 1) * LANES]
        sq = sq * sq
        lo = jnp.sum(jnp.where(first, sq, 0.0), axis=-1, keepdims=True)
        hi = jnp.sum(jnp.where(first, 0.0, sq), axis=-1, keepdims=True)
        out.append(jnp.where(first, lo, hi))
    return jnp.concatenate(out, axis=1) * (1.0 / NA_HD)


def _inproj_kernel(x_ref, mod_ref, gain_ref, w_ref, cs_ref, qkg_ref, rope_ref, aw_ref, ab_ref, tri_ref,
                   uv_ref, nqkv_ref, gqk_ref, gvg_ref, gb_ref, uvs_ref, *, rope, tm):
    x = x_ref[...]
    h = _modulated_norm(x, gain_ref[...], mod_ref[0:1, :], mod_ref[1:2, :]).astype(BF16)

    zf = _dot(h, w_ref[:, C_FX:C_NQ]).astype(BF16)
    zva = _dot(h, w_ref[:, C_NV:C_GQ])
    nqkv_ref[:, 2 * NA_W:3 * NA_W] = zva[:, :NA_W].astype(BF16)
    a = zva[:, NA_W:].astype(BF16)
    zqk = _dot(h, w_ref[:, C_NQ:C_NV])
    logits = _dot(a, aw_ref[...]) + ab_ref[...]
    zg = _dot(h, w_ref[:, C_GQ:C_GV])
    uv = _dot(zf, cs_ref[...])
    for j in range(uvs_ref.shape[0]):
        uvs_ref[j] = uv[:, j * LANES:(j + 1) * LANES]
    for parity in range(2):
        uv_ref[parity] = jnp.concatenate(
            [uvs_ref[j, pl.ds(parity, tm // 2, stride=2), :] for j in range(uvs_ref.shape[0])],
            axis=1).astype(BF16)
    gvg_ref[...] = _dot(h, w_ref[:, C_GV:N_IN_PAD]).astype(BF16)

    glog = (_log_sigmoid(logits) * (1.0 / GLA_GATE_NORM)).astype(BF16)
    for s in range(tm // SUB):
        rows = slice(s * SUB, (s + 1) * SUB)
        gb_ref[rows, 0:GLA_QK_PAD] = _dot(tri_ref[0], glog[rows, 0:GLA_QK_PAD])
        gb_ref[rows, GLA_QK_PAD:] = _dot(tri_ref[1], glog[rows, GLA_QK_PAD:])

    for i in range(2):
        z = zqk[:, i * NA_W:(i + 1) * NA_W]
        z = z * lax.rsqrt(_head_mean_sq(z) + EPS) * qkg_ref[i:i + 1, :]
        nqkv_ref[:, i * NA_W:(i + 1) * NA_W] = z.astype(BF16)

    for i in range(2):
        z = zg[:, i * GLA_QK_PAD:(i + 1) * GLA_QK_PAD]
        if rope:
            z = (z * rope_ref[0]
                 + pltpu.roll(z, GLA_QK_PAD - GLA_DK // 4, 1) * rope_ref[1]
                 + pltpu.roll(z, GLA_DK // 4, 1) * rope_ref[2])
        if i == 0:
            z = z * (GLA_DK ** -0.5)
        gqk_ref[:, i * GLA_QK_PAD:(i + 1) * GLA_QK_PAD] = z.astype(BF16)


def _inproj(x, l, mod, mod_row, p, consts, rope_tab, *, tm):
    bsz, t, d = x.shape
    assert t % tm == 0 and tm % SUB == 0
    rope = rope_tab is not None
    if not rope:
        rope_tab = jnp.zeros((3, tm, GLA_QK_PAD), F32)
    rope_idx = (lambda b, i: (0, i, 0)) if rope else (lambda b, i: (0, 0, 0))
    row = lambda w_: pl.BlockSpec((None, tm, w_), lambda b, i: (b, i, 0))
    uv_tiles = 2 * FN_W // LANES
    out_w = (3 * NA_W, 2 * GLA_QK_PAD, 2 * GLA_V_PAD, 2 * GLA_QK_PAD)
    out_dt = (BF16, BF16, BF16, F32)
    return pl.pallas_call(
        functools.partial(_inproj_kernel, rope=rope, tm=tm),
        grid=(bsz, t // tm),
        in_specs=[
            row(d),
            _mod_spec(d, l, mod_row),
            _layer_spec((1, d), l),
            _layer_spec((d, N_IN_PAD), l),
            _const_spec(consts["cs"].shape),
            _layer_spec((2, NA_W), l),
            pl.BlockSpec((3, tm, GLA_QK_PAD), rope_idx),
            _layer_spec((GLA_A_PAD, 2 * GLA_QK_PAD), l),
            _layer_spec((1, 2 * GLA_QK_PAD), l),
            _const_spec((2, SUB, SUB)),
        ],
        out_specs=[pl.BlockSpec((None, 2, tm // 2, 2 * FN_W), lambda b, i: (b, 0, i, 0))]
        + [row(w_) for w_ in out_w],
        out_shape=[jax.ShapeDtypeStruct((bsz, 2, t // 2, 2 * FN_W), BF16)]
        + [jax.ShapeDtypeStruct((bsz, t, w_), dt) for w_, dt in zip(out_w, out_dt)],
        scratch_shapes=[pltpu.VMEM((uv_tiles, tm, LANES), F32)],
        compiler_params=_cparams(("arbitrary", "arbitrary")),
        name="inproj_rope" if rope else "inproj_ctx",
    )(x, mod, p["gain_m"], p["w"], consts["cs"], p["qkg"], rope_tab, p["aw"], p["ab"], consts["tri"])


def _fourier_kernel(tab_ref, uv_ref, w_ref, o_ref, *, scale):
    half = uv_ref.shape[1]
    ue, ve = uv_ref[0, :, 0:FN_W], uv_ref[0, :, FN_W:]
    uo, vo = uv_ref[1, :, 0:FN_W], uv_ref[1, :, FN_W:]
    even = _dot(tab_ref[0], ue) - _dot(tab_ref[1], ve)
    odd = _dot(tab_ref[2], uo) - _dot(tab_ref[3], vo)
    w = w_ref[...]
    o_ref[0:half, :] = _dot(((even + odd) * scale).astype(BF16), w).astype(BF16)
    o_ref[half:, :] = _dot(((even - odd) * scale).astype(BF16), w).astype(BF16)


def _fourier(uv, l, tabs, fnet_w):
    bsz, _, half_t, _ = uv.shape
    t = 2 * half_t
    scale = float((t * FN_GD) ** -0.5)
    return pl.pallas_call(
        functools.partial(_fourier_kernel, scale=scale),
        grid=(bsz,),
        in_specs=[
            _const_spec(tabs.shape),
            pl.BlockSpec((None, 2, half_t, 2 * FN_W), lambda b: (b, 0, 0, 0)),
            _layer_spec((FN_W, FN_W), l),
        ],
        out_specs=pl.BlockSpec((None, t, FN_W), lambda b: (b, 0, 0)),
        out_shape=jax.ShapeDtypeStruct((bsz, t, FN_W), BF16),
        compiler_params=_cparams(("arbitrary",)),
        name=f"fourier_{t}",
    )(tabs, uv, fnet_w)


def _na_softmax_t(s):
    return jnp.exp2(s - jnp.max(s, axis=0, keepdims=True)).astype(BF16)


def _na_heads_out_t(o0, o1):
    row = lax.broadcasted_iota(jnp.int32, o0.shape, 0)
    out_t = jnp.where(row < NA_HD, o0 / o0[NA_HD:NA_HD + 1, :], o1 / o1[0:1, :])
    return out_t.T.astype(BF16)


def _na_group_kinds():
    margin = NA_WIN - NA_KH
    return ((0, lambda i: 0), (NA_KH // 2, lambda i: i), (margin + NA_KH - NA_GROUP, lambda i: margin))


def _na_kernel(q_ref, k_ref, v_ref, qc_ref, kc_ref, vc_ref, rpb_ref, o_ref, oc_ref,
               vt_ref, sc_ref, bias_ref, *, n_rows, need_ctx):
    lane = lax.broadcasted_iota(jnp.int32, (1, LANES), 1)
    first = lane < NA_HD
    zero = jnp.zeros((), BF16)
    one = jnp.ones((), BF16)
    ne = q_ref.shape[0]
    kcs, vcts = [], []
    for e in range(ne):
        v = v_ref[e]
        vt_ref[e, 0] = jnp.where(first, v, one).T
        vt_ref[e, 1] = jnp.where(first, one, v).T
        kcs.append(kc_ref[e])
        vc = vc_ref[e]
        vcts.append((jnp.where(first, vc, one).T, jnp.where(first, one, vc).T))

    def both_heads(q):
        return jnp.concatenate([jnp.where(first, q, zero), jnp.where(first, zero, q)], axis=0)

    n_groups = n_rows // NA_GROUP
    gq = NA_GROUP * GRID_W
    n_loc = NA_WIN * GRID_W
    margin = NA_WIN - NA_KH

    @pl.when(pl.program_id(1) == 0)
    def _():
        wrow = lax.broadcasted_iota(jnp.int32, (n_loc, GRID_W), 0)
        for hh in range(2):
            for kind, (q_off, a_lo_of) in enumerate(_na_group_kinds()):
                for j in range(NA_GROUP // 2):
                    halves = []
                    for i in (2 * j, 2 * j + 1):
                        start = (NA_KH - 1 - q_off - i + margin) * GRID_W
                        lo = a_lo_of(i) * GRID_W
                        band = (wrow >= lo) & (wrow < lo + NA_KH * GRID_W)
                        halves.append(jnp.where(band, rpb_ref[hh, start:start + n_loc, :], NA_MASKED))
                    bias_ref[hh, kind, :, j * LANES:(j + 1) * LANES] = jnp.concatenate(halves, axis=1)

    def window(g):
        ws = jnp.clip(g * NA_GROUP - NA_KH // 2, 0, n_rows - NA_WIN)
        return (pl.ds(pl.multiple_of(g * gq, gq), gq), pl.ds(pl.multiple_of(ws * GRID_W, gq), n_loc))

    def scores(g, slot):
        qs, ks = window(g)
        kind = jnp.where(g == 0, 0, jnp.where(g == n_groups - 1, 2, 1))
        for e in range(ne):
            q = both_heads(q_ref[e, qs, :])
            s_loc = _dot_nt(k_ref[e, ks, :], q)
            s_ctx = _dot_nt(kcs[e], q)
            for hh in range(2):
                cols = slice(hh * gq, (hh + 1) * gq)
                sc_ref[e, slot, 0:n_loc, cols] = s_loc[:, cols] + bias_ref[hh, kind]
            sc_ref[e, slot, n_loc:, :] = s_ctx

    def finish(g, slot):
        qs, ks = window(g)
        for e in range(ne):
            p = _na_softmax_t(sc_ref[e, slot])
            outs = [_dot(jnp.concatenate([vt_ref[e, hh, :, ks], vcts[e][hh]], axis=1),
                         p[:, hh * gq:(hh + 1) * gq]) for hh in range(2)]
            o_ref[e, qs, :] = _na_heads_out_t(outs[0], outs[1])

    scores(0, 0)

    def pair(i, carry):
        scores(2 * i + 1, 1)
        finish(2 * i, 0)
        scores(2 * i + 2, 0)
        finish(2 * i + 1, 1)
        return carry

    lax.fori_loop(0, n_groups // 2 - 1, pair, 0)
    scores(n_groups - 1, 1)
    finish(n_groups - 2, 0)
    finish(n_groups - 1, 1)

    if need_ctx:
        n_ctx = qc_ref.shape[1]
        for e in range(ne):
            p = _na_softmax_t(_dot_nt(kcs[e], both_heads(qc_ref[e])))
            outs = [_dot(vcts[e][hh], p[:, hh * n_ctx:(hh + 1) * n_ctx]) for hh in range(2)]
            oc_ref[e] = _na_heads_out_t(outs[0], outs[1])
    else:
        oc_ref[...] = jnp.zeros(oc_ref.shape, oc_ref.dtype)


def _na(nqkv, nqkv_c, l, rpb_rows, *, need_ctx):
    bsz, t, _ = nqkv.shape
    n = nqkv_c.shape[1]
    n_rows = t // GRID_W
    assert n_rows % (2 * NA_GROUP) == 0 and n_rows >= NA_WIN + NA_GROUP
    ne = NA_BATCH if bsz % NA_BATCH == 0 else 1
    pairs = NA_HEADS // 2
    n_loc = NA_WIN * GRID_W
    gq = NA_GROUP * GRID_W
    nb = NA_W // LANES
    lat = lambda off: pl.BlockSpec((ne, t, LANES), lambda j, b: (b, 0, off + j))
    ctx = lambda off: pl.BlockSpec((ne, n, LANES), lambda j, b: (b, 0, off + j))
    return pl.pallas_call(
        functools.partial(_na_kernel, n_rows=n_rows, need_ctx=need_ctx),
        grid=(pairs, bsz // ne),
        in_specs=[lat(0), lat(nb), lat(2 * nb), ctx(0), ctx(nb), ctx(2 * nb),
                  pl.BlockSpec((None, 2) + rpb_rows.shape[2:], lambda j, b: (l, j, 0, 0))],
        out_specs=[pl.BlockSpec((ne, t, LANES), lambda j, b: (b, 0, j)),
                   pl.BlockSpec((ne, n, LANES), lambda j, b: (b, 0, j))],
        out_shape=[jax.ShapeDtypeStruct((bsz, t, NA_W), BF16),
                   jax.ShapeDtypeStruct((bsz, n, NA_W), BF16)],
        scratch_shapes=[pltpu.VMEM((ne, 2, LANES, t), BF16),
                        pltpu.VMEM((ne, 2, n_loc + n, 2 * gq), F32),
                        pltpu.VMEM((2, len(_na_group_kinds()), n_loc, gq), F32)],
        compiler_params=_cparams(("arbitrary", "arbitrary")),
        name="na_attn",
    )(nqkv, nqkv, nqkv, nqkv_c, nqkv_c, nqkv_c, rpb_rows)


def _gla_steps(jobs, st_ref, causal_ref, vmask_ref):
    c = GLA_CHUNK
    head = lax.broadcasted_iota(jnp.int32, (1, GLA_QK_PAD), 1) // GLA_DK_PAD
    prep = []
    for qk_ref, vg_ref, b_ref, out_ref, ci, d in jobs:
        rows = pl.ds(pl.multiple_of(ci * GLA_BLOCK, GLA_BLOCK), GLA_BLOCK)
        k = qk_ref[rows, GLA_QK_PAD:].astype(F32)
        v = vg_ref[rows, 0:GLA_V_PAD]
        b = b_ref[rows, d * GLA_QK_PAD:(d + 1) * GLA_QK_PAD]
        lo, hi = b[0:c, :], b[c:, :]
        if d == 0:
            tot1, tot2 = lo[c - 1:c, :], hi[c - 1:c, :]
            b_rel = jnp.concatenate([lo - tot1, hi], axis=0)
        else:
            tot1, tot2 = hi[0:1, :], lo[0:1, :]
            b_rel = jnp.concatenate([lo, hi - tot1], axis=0)
        k_in = k * jnp.exp(-b_rel)
        k_end = k_in * jnp.exp(tot2)
        q_in = q_st = None
        if out_ref is not None:
            q_rel = qk_ref[rows, 0:GLA_QK_PAD].astype(F32) * jnp.exp(b_rel)
            q_in = q_rel.astype(BF16)
            q_st = (q_rel * jnp.exp(tot1)).astype(BF16)
        decay = jnp.exp(jnp.broadcast_to(tot1 + tot2, (GLA_DV_PAD, GLA_QK_PAD)).T)
        prep.append((rows, q_in, q_st, k_in, k_end, v, decay))

    att = [None if q_in is None else
           _dot_nt(q_in, jnp.concatenate([jnp.where(head == h, k_in, 0.0) for h in range(GLA_HEADS)],
                                         axis=0).astype(BF16))
           for (rows, q_in, q_st, k_in, k_end, v, decay) in prep]

    kv = []
    for (rows, q_in, q_st, k_in, k_end, v, decay) in prep:
        k_t = k_end.astype(BF16).T
        kv.append(jnp.concatenate(
            [_dot(k_t[h * GLA_DK_PAD:(h + 1) * GLA_DK_PAD, :], v[:, h * GLA_DV_PAD:(h + 1) * GLA_DV_PAD])
             for h in range(GLA_HEADS)], axis=0))

    inter = []
    zero = jnp.zeros((GLA_DK_PAD, GLA_DV_PAD), BF16)
    for j, ((qk_ref, vg_ref, b_ref, out_ref, ci, d), (rows, q_in, q_st, k_in, k_end, v, decay)) in enumerate(
            zip(jobs, prep)):
        st = st_ref[d]
        if q_st is None:
            inter.append(None)
        else:
            sb = st.astype(BF16)
            s_bd = jnp.concatenate(
                [jnp.concatenate([sb[h * GLA_DK_PAD:(h + 1) * GLA_DK_PAD, :] if h2 == h else zero
                                  for h2 in range(GLA_HEADS)], axis=1) for h in range(GLA_HEADS)], axis=0)
            inter.append(_dot(q_st, s_bd))
        st_ref[d] = st * decay + kv[j]

    for j, ((qk_ref, vg_ref, b_ref, out_ref, ci, d), (rows, q_in, q_st, k_in, k_end, v, decay)) in enumerate(
            zip(jobs, prep)):
        if out_ref is not None:
            a = (att[j] * causal_ref[d]).astype(BF16)
            intra = []
            for pair in range(GLA_HEADS // 2):
                cols = slice(pair * 2 * GLA_DV_PAD, (pair + 1) * 2 * GLA_DV_PAD)
                v_pair = jnp.concatenate([v[:, cols]] * 2, axis=0) * vmask_ref[...]
                intra.append(_dot(a[:, pair * 2 * GLA_BLOCK:(pair + 1) * 2 * GLA_BLOCK], v_pair))
            out_ref[rows, :] = (inter[j] + jnp.concatenate(intra, axis=1)).astype(out_ref.dtype)


def _gla_kernel(qk_ref, vg_ref, b_ref, qkc_ref, vgc_ref, bc_ref, causal_ref, vmask_ref, *rest, t, n, need_ctx):
    if need_ctx:
        of_ref, ob_ref, ocf_ref, ocb_ref, st_ref = rest
    else:
        of_ref, ob_ref, st_ref = rest
    st_ref[...] = jnp.zeros(st_ref.shape, F32)
    nb_ctx = n // GLA_BLOCK
    nb_lat = t // GLA_BLOCK

    def jobs(refs, outs, n_blocks, i):
        return [refs + (outs[0], i, 0), refs + (outs[1], n_blocks - 1 - i, 1)]

    ctx_outs = (ocf_ref, ocb_ref) if need_ctx else (None, None)
    for i in range(nb_ctx):
        _gla_steps(jobs((qkc_ref, vgc_ref, bc_ref), ctx_outs, nb_ctx, i), st_ref, causal_ref, vmask_ref)

    def step(i, carry):
        _gla_steps(jobs((qk_ref, vg_ref, b_ref), (of_ref, ob_ref), nb_lat, i), st_ref, causal_ref, vmask_ref)
        return carry

    lax.fori_loop(0, nb_lat, step, 0, unroll=GLA_UNROLL)


def _gla(gqk, gvg, gb, gqk_c, gvg_c, gb_c, causal, vmask, *, need_ctx):
    bsz, t, _ = gqk.shape
    n = gqk_c.shape[1]
    assert t % GLA_BLOCK == 0 and n % GLA_BLOCK == 0
    row = lambda rows, w_: pl.BlockSpec((None, rows, w_), lambda b: (b, 0, 0))
    out_rows = (t, t, n, n) if need_ctx else (t, t)
    return pl.pallas_call(
        functools.partial(_gla_kernel, t=t, n=n, need_ctx=need_ctx),
        grid=(bsz,),
        in_specs=[row(t, 2 * GLA_QK_PAD), row(t, GLA_V_PAD), row(t, 2 * GLA_QK_PAD),
                  row(n, 2 * GLA_QK_PAD), row(n, GLA_V_PAD), row(n, 2 * GLA_QK_PAD),
                  _const_spec(causal.shape), _const_spec(vmask.shape)],
        out_specs=[row(r, GLA_V_PAD) for r in out_rows],
        out_shape=[jax.ShapeDtypeStruct((bsz, r, GLA_V_PAD), BF16) for r in out_rows],
        scratch_shapes=[pltpu.VMEM((2, GLA_QK_PAD, GLA_DV_PAD), F32)],
        compiler_params=_cparams(("arbitrary",)),
        name="gla_scan",
    )(gqk, gvg, gb, gqk_c, gvg_c, gb_c, causal, vmask)


def _gla_out(of_ref, ob_ref, g_ref, gain_ref, rows):
    heads = []
    for h in range(GLA_HEADS):
        cols = slice(h * GLA_DV_PAD, (h + 1) * GLA_DV_PAD)
        o = of_ref[rows, cols].astype(F32) + ob_ref[rows, cols].astype(F32)
        ms = jnp.sum(o * o, axis=-1, keepdims=True) * (1.0 / GLA_DV)
        g = g_ref[rows, cols].astype(F32)
        heads.append((o * lax.rsqrt(ms + EPS) * gain_ref[:, cols] * (g * jax.nn.sigmoid(g))).astype(BF16))
    return jnp.concatenate(heads, axis=1)


def _post_kernel(x_ref, yfn_ref, yna_ref, of_ref, ob_ref, g_ref, mod_ref, gain_o_ref, gain_ref,
                 wo_ref, w1_ref, w3_ref, w2_ref, o_ref):
    tm = x_ref.shape[0]
    tiles = [slice(r0, r0 + POST_SUB) for r0 in range(0, tm, POST_SUB)]
    ys = [(_dot(yfn_ref[rows, :], wo_ref[0:FN_W, :])
           + _dot(yna_ref[rows, :], wo_ref[FN_W:FN_W + NA_W, :])
           + _dot(_gla_out(of_ref, ob_ref, g_ref, gain_o_ref, rows), wo_ref[FN_W + NA_W:, :]))
          for rows in tiles]
    for rows, y in zip(tiles, ys):
        x1 = x_ref[rows, :] + mod_ref[2:3, :] * y
        h = _modulated_norm(x1, gain_ref[...], mod_ref[3:4, :], mod_ref[4:5, :]).astype(BF16)
        acts = []
        for c in range(D_FF // MXU_W):
            cols = slice(c * MXU_W, (c + 1) * MXU_W)
            u = _dot(h, w1_ref[:, cols])
            g = _dot(h, w3_ref[:, cols])
            acts.append((u * jax.nn.sigmoid(u) * g).astype(BF16))
        acc = _dot(jnp.concatenate(acts, axis=1), w2_ref[...])
        o_ref[rows, :] = x1 + mod_ref[5:6, :] * acc


def _post(x, yfn, yna, o_fwd, o_bwd, gvg, l, mod, mod_row, p, *, tm):
    bsz, t, d = x.shape
    assert t % tm == 0 and tm % POST_SUB == 0
    row = lambda w_: pl.BlockSpec((None, tm, w_), lambda b, i: (b, i, 0))
    gate =pl.BlockSpec((None, tm, GLA_V_PAD), lambda b, i: (b, i, 1))
    return pl.pallas_call(
        _post_kernel,
        grid=(bsz, t // tm),
        in_specs=[row(d), row(FN_W), row(NA_W), row(GLA_V_PAD), row(GLA_V_PAD), gate,
                  _mod_spec(d, l, mod_row), _layer_spec((1, GLA_V_PAD), l),
                  _layer_spec((1, d), l), _layer_spec((MIX_PAD, d), l), _layer_spec((d, D_FF), l),
                  _layer_spec((d, D_FF), l), _layer_spec((D_FF, d), l)],
        out_specs=row(d),
        out_shape=jax.ShapeDtypeStruct((bsz, t, d), F32),
        compiler_params=_cparams(("arbitrary", "arbitrary")),
        name=f"post_{t}",
    )(x, yfn, yna, o_fwd, o_bwd, gvg, mod, p["gain_o"], p["gain_f"], p["wo"], p["w1"], p["w3"], p["w2"])


def _dft_tables(n):
    idx = (np.arange(n)[:, None] * np.arange(n)[None, :]) % n
    ang = 2.0 * np.pi * idx / n
    return np.cos(ang), np.sin(ang)


def _shape_consts(t, n):
    c64, s64 = _dft_tables(FN_GD)
    eye = np.eye(FN_GROUPS)
    cs = np.concatenate([np.kron(eye, c64), np.kron(eye, s64)], axis=1)
    chunk_id = np.arange(SUB) // GLA_CHUNK
    same = chunk_id[:, None] == chunk_id[None, :]
    pos = np.arange(SUB)
    tri = np.stack([same & (pos[None, :] <= pos[:, None]), same & (pos[None, :] >= pos[:, None])])
    cpos = np.arange(GLA_BLOCK)
    fwd = cpos[None, :] <= cpos[:, None]
    gla_causal = np.stack([np.tile(fwd, (1, GLA_HEADS)), np.tile(fwd.T, (1, GLA_HEADS))])
    gla_vmask = ((np.arange(2 * GLA_BLOCK)[:, None] // GLA_BLOCK)
                 == (np.arange(2 * GLA_DV_PAD)[None, :] // GLA_DV_PAD))

    m = GLA_DK // 4
    inv = ROPE_BASE ** (-np.arange(m) / m)
    tok = np.arange(t)
    cos_h = np.ones((t, GLA_DK_PAD))
    sin_lo = np.zeros((t, GLA_DK_PAD))
    sin_hi = np.zeros((t, GLA_DK_PAD))
    for blk, p in enumerate((tok // GRID_W, tok % GRID_W)):
        ang = p[:, None] * inv[None, :]
        o = blk * 2 * m
        cos_h[:, o:o + m] = np.cos(ang)
        cos_h[:, o + m:o + 2 * m] = np.cos(ang)
        sin_lo[:, o:o + m] = -np.sin(ang)
        sin_hi[:, o + m:o + 2 * m] = np.sin(ang)
    rope = np.stack([np.tile(a, (1, GLA_HEADS)) for a in (cos_h, sin_lo, sin_hi)])

    def position_tables(length):
        cos, sin = _dft_tables(length)
        half = length // 2
        return np.stack([cos[:half, 0::2], sin[:half, 0::2], cos[:half, 1::2], sin[:half, 1::2]])

    to_bf16 = lambda a: jnp.asarray(a, F32).astype(BF16)
    return {
        "cs": to_bf16(cs), "tri": jnp.asarray(tri, BF16),
        "gla_causal": jnp.asarray(gla_causal, F32), "gla_vmask": jnp.asarray(gla_vmask, BF16),
        "rope": jnp.asarray(rope, F32),
        "dft_lat": to_bf16(position_tables(t)), "dft_ctx": to_bf16(position_tables(n)),
    }


def _bias_kernel(rpb_ref, onehot_ref, ok_ref, o_ref):
    r = rpb_ref[...]
    onehot = onehot_ref[...]
    t = jnp.zeros(o_ref.shape, F32)
    for _ in range(3):
        piece = r.astype(BF16)
        t = t + _dot(piece, onehot)
        r = r - piece.astype(F32)
    o_ref[...] = jnp.where(ok_ref[...] > 0.0, t * LOG2E, NA_MASKED)


def _window_bias_rows(na_rpb):
    depth, heads, n_dr, n_dc = na_rpb.shape
    cq = np.arange(GRID_W)
    c0 = np.clip(cq - NA_KW // 2, 0, GRID_W - NA_KW)
    col_ok = (cq[None, :] >= c0[:, None]) & (cq[None, :] < c0[:, None] + NA_KW)
    dc = np.clip(cq[None, :] - cq[:, None], -(NA_KW - 1), NA_KW - 1) + NA_KW - 1
    n_dc_pad = n_dc + 1
    n_tab = depth * heads * n_dr
    rows_pad = -(-n_tab // 8) * 8
    onehot = (np.arange(n_dc_pad)[:, None, None] == dc[None]).reshape(n_dc_pad, GRID_W * GRID_W)
    rpb2 = jnp.pad(na_rpb.reshape(n_tab, n_dc), ((0, rows_pad - n_tab), (0, 1)))
    t = pl.pallas_call(
        _bias_kernel,
        out_shape=jax.ShapeDtypeStruct((rows_pad, GRID_W * GRID_W), F32),
        name="na_bias",
    )(rpb2, jnp.asarray(onehot, BF16), jnp.asarray(col_ok.reshape(1, -1), F32))
    t = t[:n_tab].reshape(depth * heads, n_dr, GRID_W, GRID_W)
    t = t.transpose(0, 1, 3, 2).reshape(depth * heads, n_dr * GRID_W, GRID_W)
    margin = NA_WIN - NA_KH
    t = jnp.pad(t, ((0, 0), (margin * GRID_W, margin * GRID_W), (0, 0)))
    return t.reshape(depth, heads, (n_dr + 2 * margin) * GRID_W, GRID_W)


def _pad_heads(w, heads, dim, pad, axis=-1):
    axis = axis % w.ndim
    shape = w.shape
    w = w.reshape(shape[:axis] + (heads, dim) + shape[axis + 1:])
    widths = [(0, 0)] * w.ndim
    widths[axis + 1] = (0, pad - dim)
    return jnp.pad(w, widths).reshape(shape[:axis] + (heads * pad,) + shape[axis + 1:])


def _params(norm_mix, norm_ffn, w_in, fnet_w, na_q_norm, na_k_norm, na_rpb, alpha_w, alpha_b, o_norm, w_out,
            ffn_w1, ffn_w3, ffn_w2):
    depth = w_in.shape[0]
    o = FN_W + 3 * NA_W
    gq = w_in[..., o:o + GLA_QK_W]
    gk = w_in[..., o + GLA_QK_W:o + 2 * GLA_QK_W]
    o2 = o + 2 * GLA_QK_W
    gv = w_in[..., o2:o2 + GLA_V_W]
    gg = w_in[..., o2 + GLA_V_W:o2 + 2 * GLA_V_W]
    ga = w_in[..., o2 + 2 * GLA_V_W:]
    w = jnp.concatenate([
        w_in[..., :o],
        jnp.pad(ga, ((0, 0), (0, 0), (0, GLA_A_PAD - 2 * GLA_RANK))),
        _pad_heads(gq, GLA_HEADS, GLA_DK, GLA_DK_PAD), _pad_heads(gk, GLA_HEADS, GLA_DK, GLA_DK_PAD),
        _pad_heads(gv, GLA_HEADS, GLA_DV, GLA_DV_PAD), _pad_heads(gg, GLA_HEADS, GLA_DV, GLA_DV_PAD),
    ], axis=-1).astype(BF16)

    qkg = jnp.stack([jnp.tile(na_q_norm, (1, NA_HEADS)) * (NA_HD ** -0.5 * LOG2E),
                     jnp.tile(na_k_norm, (1, NA_HEADS))], axis=1)

    aw_pad = _pad_heads(alpha_w, GLA_HEADS, GLA_DK, GLA_DK_PAD)
    aw = jnp.concatenate([jnp.pad(aw_pad[:, 0], ((0, 0), (0, 0), (0, GLA_QK_PAD))),
                          jnp.pad(aw_pad[:, 1], ((0, 0), (0, 0), (GLA_QK_PAD, 0)))], axis=1)
    aw = jnp.pad(aw, ((0, 0), (0, GLA_A_PAD - 2 * GLA_RANK), (0, 0))).astype(BF16)
    ab = _pad_heads(alpha_b, GLA_HEADS, GLA_DK, GLA_DK_PAD).reshape(depth, 1, 2 * GLA_QK_PAD)

    gain_o = _pad_heads(jnp.tile(o_norm, (1, GLA_HEADS)), GLA_HEADS, GLA_DV, GLA_DV_PAD)[:, None, :]
    wo = jnp.concatenate([w_out[:, :FN_W + NA_W],
                          _pad_heads(w_out[:, FN_W + NA_W:], GLA_HEADS, GLA_DV, GLA_DV_PAD, axis=1)],
                         axis=1).astype(BF16)
    return {
        "gain_m": norm_mix[:, None, :], "gain_f": norm_ffn[:, None, :], "w": w, "qkg": qkg, "aw": aw, "ab": ab,
        "rpb_rows": _window_bias_rows(na_rpb), "gain_o": gain_o, "wo": wo, "fw": fnet_w.astype(BF16),
        "w1": ffn_w1.astype(BF16), "w3": ffn_w3.astype(BF16), "w2": ffn_w2.astype(BF16),
    }


def kernel(x, c, ctx, c_ctx, ada_w, ada_b, norm_mix, norm_ffn, w_in, fnet_w, na_q_norm, na_k_norm, na_rpb,
           gla_alpha_w, gla_alpha_b, gla_o_norm, w_out, ffn_w1, ffn_w3, ffn_w2):
    bsz, t, d = x.shape
    n = ctx.shape[1]
    depth = ada_w.shape[0]
    ctx_row = bsz
    assert bsz < MOD_ROWS and n % SUB == 0 and (bsz * n) % CTX_TM == 0
    consts = _shape_consts(t, n)
    p = _params(norm_mix, norm_ffn, w_in, fnet_w, na_q_norm, na_k_norm, na_rpb, gla_alpha_w, gla_alpha_b,
                gla_o_norm, w_out, ffn_w1, ffn_w3, ffn_w2)

    cc = jnp.concatenate([c, c_ctx[None, :], jnp.zeros((MOD_ROWS - bsz - 1, d), F32)])
    mod = _ada_mod(cc, ada_w, ada_b).reshape(depth, MOD_ROWS, 6, d)

    cx = ctx
    flat = lambda a: a.reshape(1, bsz * n, a.shape[-1])
    for l in range(depth):
        need_ctx = l < depth - 1
        uv, nqkv, gqk, gvg, gb = _inproj(x, l, mod, None, p, consts, consts["rope"], tm=INPROJ_TM)
        uv_c, nqkv_c, gqk_c, gvg_c, gb_c = _inproj(cx, l, mod, ctx_row, p, consts, None, tm=n)

        y_fn = _fourier(uv, l, consts["dft_lat"], p["fw"])
        y_na, y_na_c = _na(nqkv, nqkv_c, l, p["rpb_rows"], need_ctx=need_ctx)
        o_gla = _gla(gqk, gvg, gb, gqk_c, gvg_c, gb_c, consts["gla_causal"], consts["gla_vmask"],
                     need_ctx=need_ctx)
        x = _post(x, y_fn, y_na, o_gla[0], o_gla[1], gvg, l, mod, None, p, tm=POST_TM)
        if need_ctx:
            y_fn_c = _fourier(uv_c, l, consts["dft_ctx"], p["fw"])
            cx = _post(flat(cx), flat(y_fn_c), flat(y_na_c), flat(o_gla[2]), flat(o_gla[3]), flat(gvg_c),
                       l, mod, ctx_row, p, tm=CTX_TM).reshape(bsz, n, d)
    return x
```

```python
import functools

import numpy as np
import jax
import jax.numpy as jnp
from jax import lax
from jax.experimental import pallas as pl
from jax.experimental.pallas import tpu as pltpu

F32 = jnp.float32
BF16 = jnp.bfloat16

D_MODEL = 1024
GRID_W = 64
FN_W = 256
FN_GROUPS = 4
FN_GD = 64
NA_HD = 64
NA_W = 384
NA_HEADS = 6
NA_KH = 8
NA_KW = 16
NA_GROUP = 4
NA_WIN = NA_GROUP + NA_KH
NA_BATCH = 4
LOG2E = float(np.log2(np.e))
NA_MASKED = -1e30
GLA_HEADS = 4
GLA_DK = 48
GLA_DV = 96
GLA_QK_W = GLA_HEADS * GLA_DK
GLA_V_W = GLA_HEADS * GLA_DV
GLA_RANK = 16
GLA_GATE_NORM = 16.0
GLA_CHUNK = 64
GLA_BLOCK = 2 * GLA_CHUNK
GLA_UNROLL = 16
ROPE_BASE = 10000.0
D_FF = 2816
EPS = 1e-6
MOD_ROWS = 16

LANES = 128
MXU_W = 256
VMEM_LIMIT_BYTES = 56 * 1024 * 1024

GLA_DK_PAD = 64
GLA_DV_PAD = 128
GLA_QK_PAD = GLA_HEADS * GLA_DK_PAD
GLA_V_PAD = GLA_HEADS * GLA_DV_PAD
GLA_A_PAD = LANES
MIX_PAD = FN_W + NA_W + GLA_V_PAD

C_FX = 0
C_NQ = C_FX + FN_W
C_NK = C_NQ + NA_W
C_NV = C_NK + NA_W
C_GA = C_NV + NA_W
C_GQ = C_GA + GLA_A_PAD
C_GK = C_GQ + GLA_QK_PAD
C_GV = C_GK + GLA_QK_PAD
C_GG = C_GV + GLA_V_PAD
N_IN_PAD = C_GG + GLA_V_PAD

SUB = 256

INPROJ_TM = 1024
POST_TM = 1024
POST_SUB = 256
CTX_TM = 256
ADA_TN = 1024


def _cparams(sem):
    return pltpu.CompilerParams(dimension_semantics=sem, vmem_limit_bytes=VMEM_LIMIT_BYTES)


def _const_spec(shape):
    nd = len(shape)
    return pl.BlockSpec(tuple(shape), lambda *_: (0,) * nd, pipeline_mode=pl.Buffered(1))


def _layer_spec(shape, l):
    nd = len(shape)
    return pl.BlockSpec((None,) + tuple(shape), lambda *_: (l,) + (0,) * nd, pipeline_mode=pl.Buffered(1))


def _mod_spec(d, l, mod_row):
    if mod_row is None:
        return pl.BlockSpec((None, None, 6, d), lambda b, i: (l, b, 0, 0))
    return pl.BlockSpec((None, None, 6, d), lambda b, i: (l, mod_row, 0, 0))


def _dot(a, b):
    return jnp.dot(a, b, preferred_element_type=F32)


def _dot_nt(a, b):
    return lax.dot_general(a, b, (((1,), (1,)), ((), ())), preferred_element_type=F32)


def _ada_kernel(c_ref, w_ref, b_ref, o_ref):
    a = c_ref[...]
    a = (a * jax.nn.sigmoid(a)).astype(BF16)
    o_ref[...] = _dot(a, w_ref[...].astype(BF16)) + b_ref[...]


def _ada_mod(cc, ada_w, ada_b):
    depth, d, n = ada_w.shape
    rows = cc.shape[0]
    tn = ADA_TN
    return pl.pallas_call(
        _ada_kernel,
        grid=(depth, n // tn),
        in_specs=[
            pl.BlockSpec((rows, d), lambda l, j: (0, 0)),
            pl.BlockSpec((None, d, tn), lambda l, j: (l, 0, j)),
            pl.BlockSpec((None, 1, tn), lambda l, j: (l, 0, j)),
        ],
        out_specs=pl.BlockSpec((None, rows, tn), lambda l, j: (l, 0, j)),
        out_shape=jax.ShapeDtypeStruct((depth, rows, n), F32),
        compiler_params=_cparams(("arbitrary", "arbitrary")),
        name="ada_mod",
    )(cc, ada_w, ada_b.reshape(depth, 1, n))


def _modulated_norm(x, gain, shift, scale):
    ms = jnp.mean(x * x, axis=-1, keepdims=True)
    return (x * lax.rsqrt(ms + EPS) * gain) * (1.0 + scale) + shift


def _log_sigmoid(x):
    return jnp.minimum(x, 0.0) - jnp.log1p(jnp.exp(-jnp.abs(x)))


def _head_mean_sq(z):
    first = lax.broadcasted_iota(jnp.int32, (1, LANES), 1) < NA_HD
    out = []
    for t in range(NA_W // LANES):
        sq = z[:, t * LANES:(t + 1) * LANES]
        sq = sq * sq
        lo = jnp.sum(jnp.where(first, sq, 0.0), axis=-1, keepdims=True)
        hi = jnp.sum(jnp.where(first, 0.0, sq), axis=-1, keepdims=True)
        out.append(jnp.where(first, lo, hi))
    return jnp.concatenate(out, axis=1) * (1.0 / NA_HD)


def _inproj_kernel(x_ref, mod_ref, gain_ref, w_ref, cs_ref, qkg_ref, rope_ref, aw_ref, ab_ref, tri_ref,
                   uv_ref, nqkv_ref, gqk_ref, gvg_ref, gb_ref, *, rope, tm):
    x = x_ref[...]
    h = _modulated_norm(x, gain_ref[...], mod_ref[0:1, :], mod_ref[1:2, :]).astype(BF16)

    zf = _dot(h, w_ref[:, C_FX:C_NQ]).astype(BF16)
    zva = _dot(h, w_ref[:, C_NV:C_GQ])
    nqkv_ref[:, 2 * NA_W:3 * NA_W] = zva[:, :NA_W].astype(BF16)
    a = zva[:, NA_W:].astype(BF16)
    zqk = _dot(h, w_ref[:, C_NQ:C_NV])
    logits = _dot(a, aw_ref[...]) + ab_ref[...]
    zg = _dot(h, w_ref[:, C_GQ:C_GV])
    uv = _dot(zf, cs_ref[...])
    for j in range(uv_ref.shape[0]):
        uv_ref[j] = uv[:, j * LANES:(j + 1) * LANES]
    gvg_ref[...] = _dot(h, w_ref[:, C_GV:N_IN_PAD]).astype(BF16)

    glog = (_log_sigmoid(logits) * (1.0 / GLA_GATE_NORM)).astype(BF16)
    for s in range(tm // SUB):
        rows = slice(s * SUB, (s + 1) * SUB)
        gb_ref[rows, 0:GLA_QK_PAD] = _dot(tri_ref[0], glog[rows, 0:GLA_QK_PAD])
        gb_ref[rows, GLA_QK_PAD:] = _dot(tri_ref[1], glog[rows, GLA_QK_PAD:])

    for i in range(2):
        z = zqk[:, i * NA_W:(i + 1) * NA_W]
        z = z * lax.rsqrt(_head_mean_sq(z) + EPS) * qkg_ref[i:i + 1, :]
        nqkv_ref[:, i * NA_W:(i + 1) * NA_W] = z.astype(BF16)

    for i in range(2):
        z = zg[:, i * GLA_QK_PAD:(i + 1) * GLA_QK_PAD]
        if rope:
            z = (z * rope_ref[0]
                 + pltpu.roll(z, GLA_QK_PAD - GLA_DK // 4, 1) * rope_ref[1]
                 + pltpu.roll(z, GLA_DK // 4, 1) * rope_ref[2])
        if i == 0:
            z = z * (GLA_DK ** -0.5)
        gqk_ref[:, i * GLA_QK_PAD:(i + 1) * GLA_QK_PAD] = z.astype(BF16)


def _inproj(x, l, mod, mod_row, p, consts, rope_tab, *, tm):
    bsz, t, d = x.shape
    assert t % tm == 0 and tm % SUB == 0
    rope = rope_tab is not None
    if not rope:
        rope_tab = jnp.zeros((3, tm, GLA_QK_PAD), F32)
    rope_idx = (lambda b, i: (0, i, 0)) if rope else (lambda b, i: (0, 0, 0))
    row = lambda w_: pl.BlockSpec((None, tm, w_), lambda b, i: (b, i, 0))
    uv_tiles = 2 * FN_W // LANES
    out_w = (3 * NA_W, 2 * GLA_QK_PAD, 2 * GLA_V_PAD, 2 * GLA_QK_PAD)
    out_dt = (BF16, BF16, BF16, F32)
    return pl.pallas_call(
        functools.partial(_inproj_kernel, rope=rope, tm=tm),
        grid=(bsz, t // tm),
        in_specs=[
            row(d),
            _mod_spec(d, l, mod_row),
            _layer_spec((1, d), l),
            _layer_spec((d, N_IN_PAD), l),
            _const_spec(consts["cs"].shape),
            _layer_spec((2, NA_W), l),
            pl.BlockSpec((3, tm, GLA_QK_PAD), rope_idx),
            _layer_spec((GLA_A_PAD, 2 * GLA_QK_PAD), l),
            _layer_spec((1, 2 * GLA_QK_PAD), l),
            _const_spec((2, SUB, SUB)),
        ],
        out_specs=[pl.BlockSpec((None, uv_tiles, tm, LANES), lambda b, i: (b, 0, i, 0))]
        + [row(w_) for w_ in out_w],
        out_shape=[jax.ShapeDtypeStruct((bsz, uv_tiles, t, LANES), F32)]
        + [jax.ShapeDtypeStruct((bsz, t, w_), dt) for w_, dt in zip(out_w, out_dt)],
        compiler_params=_cparams(("arbitrary", "arbitrary")),
        name="inproj_rope" if rope else "inproj_ctx",
    )(x, mod, p["gain_m"], p["w"], consts["cs"], p["qkg"], rope_tab, p["aw"], p["ab"], consts["tri"])


def _fourier_kernel(tab_ref, uv_ref, w_ref, o_ref, *, scale):
    half = uv_ref.shape[1] // 2

    def positions(parity):
        tiles = [uv_ref[j, pl.ds(parity, half, stride=2), :].astype(BF16) for j in range(uv_ref.shape[0])]
        return jnp.concatenate(tiles[:2], axis=1), jnp.concatenate(tiles[2:], axis=1)

    ue, ve = positions(0)
    uo, vo = positions(1)
    even = _dot(tab_ref[0], ue) - _dot(tab_ref[1], ve)
    odd = _dot(tab_ref[2], uo) - _dot(tab_ref[3], vo)
    w = w_ref[...]
    o_ref[0:half, :] = _dot(((even + odd) * scale).astype(BF16), w).astype(BF16)
    o_ref[half:, :] = _dot(((even - odd) * scale).astype(BF16), w).astype(BF16)


def _fourier(uv, l, tabs, fnet_w):
    bsz, tiles, t, _ = uv.shape
    scale = float((t * FN_GD) ** -0.5)
    return pl.pallas_call(
        functools.partial(_fourier_kernel, scale=scale),
        grid=(bsz,),
        in_specs=[
            _const_spec(tabs.shape),
            pl.BlockSpec((None, tiles, t, LANES), lambda b: (b, 0, 0, 0)),
            _layer_spec((FN_W, FN_W), l),
        ],
        out_specs=pl.BlockSpec((None, t, FN_W), lambda b: (b, 0, 0)),
        out_shape=jax.ShapeDtypeStruct((bsz, t, FN_W), BF16),
        compiler_params=_cparams(("arbitrary",)),
        name=f"fourier_{t}",
    )(tabs, uv, fnet_w)


def _na_softmax_t(s):
    return jnp.exp2(s - jnp.max(s, axis=0, keepdims=True)).astype(BF16)


def _na_heads_out_t(o0, o1):
    row = lax.broadcasted_iota(jnp.int32, o0.shape, 0)
    out_t = jnp.where(row < NA_HD, o0 / o0[NA_HD:NA_HD + 1, :], o1 / o1[0:1, :])
    return out_t.T.astype(BF16)


def _na_group_kinds():
    margin = NA_WIN - NA_KH
    return ((0, lambda i: 0), (NA_KH // 2, lambda i: i), (margin + NA_KH - NA_GROUP, lambda i: margin))


def _na_kernel(q_ref, k_ref, v_ref, qc_ref, kc_ref, vc_ref, rpb_ref, o_ref, oc_ref,
               vt_ref, sc_ref, bias_ref, *, n_rows, need_ctx):
    lane = lax.broadcasted_iota(jnp.int32, (1, LANES), 1)
    first = lane < NA_HD
    zero = jnp.zeros((), BF16)
    one = jnp.ones((), BF16)
    ne = q_ref.shape[0]
    kcs, vcts = [], []
    for e in range(ne):
        v = v_ref[e]
        vt_ref[e, 0] = jnp.where(first, v, one).T
        vt_ref[e, 1] = jnp.where(first, one, v).T
        kcs.append(kc_ref[e])
        vc = vc_ref[e]
        vcts.append((jnp.where(first, vc, one).T, jnp.where(first, one, vc).T))

    def both_heads(q):
        return jnp.concatenate([jnp.where(first, q, zero), jnp.where(first, zero, q)], axis=0)

    n_groups = n_rows // NA_GROUP
    gq = NA_GROUP * GRID_W
    n_loc = NA_WIN * GRID_W
    margin = NA_WIN - NA_KH

    @pl.when(pl.program_id(1) == 0)
    def _():
        wrow = lax.broadcasted_iota(jnp.int32, (n_loc, GRID_W), 0)
        for hh in range(2):
            for kind, (q_off, a_lo_of) in enumerate(_na_group_kinds()):
                for j in range(NA_GROUP // 2):
                    halves = []
                    for i in (2 * j, 2 * j + 1):
                        start = (NA_KH - 1 - q_off - i + margin) * GRID_W
                        lo = a_lo_of(i) * GRID_W
                        band = (wrow >= lo) & (wrow < lo + NA_KH * GRID_W)
                        halves.append(jnp.where(band, rpb_ref[hh, start:start + n_loc, :], NA_MASKED))
                    bias_ref[hh, kind, :, j * LANES:(j + 1) * LANES] = jnp.concatenate(halves, axis=1)

    def window(g):
        ws = jnp.clip(g * NA_GROUP - NA_KH // 2, 0, n_rows - NA_WIN)
        return (pl.ds(pl.multiple_of(g * gq, gq), gq), pl.ds(pl.multiple_of(ws * GRID_W, gq), n_loc))

    def scores(g, slot):
        qs, ks = window(g)
        kind = jnp.where(g == 0, 0, jnp.where(g == n_groups - 1, 2, 1))
        for e in range(ne):
            q = both_heads(q_ref[e, qs, :])
            s_loc = _dot_nt(k_ref[e, ks, :], q)
            s_ctx = _dot_nt(kcs[e], q)
            for hh in range(2):
                cols = slice(hh * gq, (hh + 1) * gq)
                sc_ref[e, slot, 0:n_loc, cols] = s_loc[:, cols] + bias_ref[hh, kind]
            sc_ref[e, slot, n_loc:, :] = s_ctx

    def finish(g, slot):
        qs, ks = window(g)
        for e in range(ne):
            p = _na_softmax_t(sc_ref[e, slot])
            outs = [_dot(jnp.concatenate([vt_ref[e, hh, :, ks], vcts[e][hh]], axis=1),
                         p[:, hh * gq:(hh + 1) * gq]) for hh in range(2)]
            o_ref[e, qs, :] = _na_heads_out_t(outs[0], outs[1])

    scores(0, 0)

    def pair(i, carry):
        scores(2 * i + 1, 1)
        finish(2 * i, 0)
        scores(2 * i + 2, 0)
        finish(2 * i + 1, 1)
        return carry

    lax.fori_loop(0, n_groups // 2 - 1, pair, 0)
    scores(n_groups - 1, 1)
    finish(n_groups - 2, 0)
    finish(n_groups - 1, 1)

    if need_ctx:
        n_ctx = qc_ref.shape[1]
        for e in range(ne):
            p = _na_softmax_t(_dot_nt(kcs[e], both_heads(qc_ref[e])))
            outs = [_dot(vcts[e][hh], p[:, hh * n_ctx:(hh + 1) * n_ctx]) for hh in range(2)]
            oc_ref[e] = _na_heads_out_t(outs[0], outs[1])
    else:
        oc_ref[...] = jnp.zeros(oc_ref.shape, oc_ref.dtype)


def _na(nqkv, nqkv_c, l, rpb_rows, *, need_ctx):
    bsz, t, _ = nqkv.shape
    n = nqkv_c.shape[1]
    n_rows = t // GRID_W
    assert n_rows % (2 * NA_GROUP) == 0 and n_rows >= NA_WIN + NA_GROUP
    ne = NA_BATCH if bsz % NA_BATCH == 0 else 1
    pairs = NA_HEADS // 2
    n_loc = NA_WIN * GRID_W
    gq = NA_GROUP * GRID_W
    nb = NA_W // LANES
    lat = lambda off: pl.BlockSpec((ne, t, LANES), lambda j, b: (b, 0, off + j))
    ctx = lambda off: pl.BlockSpec((ne, n, LANES), lambda j, b: (b, 0, off + j))
    return pl.pallas_call(
        functools.partial(_na_kernel, n_rows=n_rows, need_ctx=need_ctx),
        grid=(pairs, bsz // ne),
        in_specs=[lat(0), lat(nb), lat(2 * nb), ctx(0), ctx(nb), ctx(2 * nb),
                  pl.BlockSpec((None, 2) + rpb_rows.shape[2:], lambda j, b: (l, j, 0, 0))],
        out_specs=[pl.BlockSpec((ne, t, LANES), lambda j, b: (b, 0, j)),
                   pl.BlockSpec((ne, n, LANES), lambda j, b: (b, 0, j))],
        out_shape=[jax.ShapeDtypeStruct((bsz, t, NA_W), BF16),
                   jax.ShapeDtypeStruct((bsz, n, NA_W), BF16)],
        scratch_shapes=[pltpu.VMEM((ne, 2, LANES, t), BF16),
                        pltpu.VMEM((ne, 2, n_loc + n, 2 * gq), F32),
                        pltpu.VMEM((2, len(_na_group_kinds()), n_loc, gq), F32)],
        compiler_params=_cparams(("arbitrary", "arbitrary")),
        name="na_attn",
    )(nqkv, nqkv, nqkv, nqkv_c, nqkv_c, nqkv_c, rpb_rows)


def _gla_steps(jobs, st_ref, causal_ref, vmask_ref):
    c = GLA_CHUNK
    head = lax.broadcasted_iota(jnp.int32, (1, GLA_QK_PAD), 1) // GLA_DK_PAD
    prep = []
    for qk_ref, vg_ref, b_ref, out_ref, ci, d in jobs:
        rows = pl.ds(pl.multiple_of(ci * GLA_BLOCK, GLA_BLOCK), GLA_BLOCK)
        k = qk_ref[rows, GLA_QK_PAD:].astype(F32)
        v = vg_ref[rows, 0:GLA_V_PAD]
        b = b_ref[rows, d * GLA_QK_PAD:(d + 1) * GLA_QK_PAD]
        lo, hi = b[0:c, :], b[c:, :]
        if d == 0:
            tot1, tot2 = lo[c - 1:c, :], hi[c - 1:c, :]
            b_rel = jnp.concatenate([lo - tot1, hi], axis=0)
        else:
            tot1, tot2 = hi[0:1, :], lo[0:1, :]
            b_rel = jnp.concatenate([lo, hi - tot1], axis=0)
        k_in = k * jnp.exp(-b_rel)
        k_end = k_in * jnp.exp(tot2)
        q_in = q_st = None
        if out_ref is not None:
            q_rel = qk_ref[rows, 0:GLA_QK_PAD].astype(F32) * jnp.exp(b_rel)
            q_in = q_rel.astype(BF16)
            q_st = (q_rel * jnp.exp(tot1)).astype(BF16)
        decay = jnp.exp(jnp.broadcast_to(tot1 + tot2, (GLA_DV_PAD, GLA_QK_PAD)).T)
        prep.append((rows, q_in, q_st, k_in, k_end, v, decay))

    att = [None if q_in is None else
           _dot_nt(q_in, jnp.concatenate([jnp.where(head == h, k_in, 0.0) for h in range(GLA_HEADS)],
                                         axis=0).astype(BF16))
           for (rows, q_in, q_st, k_in, k_end, v, decay) in prep]

    kv = []
    for (rows, q_in, q_st, k_in, k_end, v, decay) in prep:
        k_t = k_end.astype(BF16).T
        kv.append(jnp.concatenate(
            [_dot(k_t[h * GLA_DK_PAD:(h + 1) * GLA_DK_PAD, :], v[:, h * GLA_DV_PAD:(h + 1) * GLA_DV_PAD])
             for h in range(GLA_HEADS)], axis=0))

    inter = []
    zero = jnp.zeros((GLA_DK_PAD, GLA_DV_PAD), BF16)
    for j, ((qk_ref, vg_ref, b_ref, out_ref, ci, d), (rows, q_in, q_st, k_in, k_end, v, decay)) in enumerate(
            zip(jobs, prep)):
        st = st_ref[d]
        if q_st is None:
            inter.append(None)
        else:
            sb = st.astype(BF16)
            s_bd = jnp.concatenate(
                [jnp.concatenate([sb[h * GLA_DK_PAD:(h + 1) * GLA_DK_PAD, :] if h2 == h else zero
                                  for h2 in range(GLA_HEADS)], axis=1) for h in range(GLA_HEADS)], axis=0)
            inter.append(_dot(q_st, s_bd))
        st_ref[d] = st * decay + kv[j]

    for j, ((qk_ref, vg_ref, b_ref, out_ref, ci, d), (rows, q_in, q_st, k_in, k_end, v, decay)) in enumerate(
            zip(jobs, prep)):
        if out_ref is not None:
            a = (att[j] * causal_ref[d]).astype(BF16)
            intra = []
            for pair in range(GLA_HEADS // 2):
                cols = slice(pair * 2 * GLA_DV_PAD, (pair + 1) * 2 * GLA_DV_PAD)
                v_pair = jnp.concatenate([v[:, cols]] * 2, axis=0) * vmask_ref[...]
                intra.append(_dot(a[:, pair * 2 * GLA_BLOCK:(pair + 1) * 2 * GLA_BLOCK], v_pair))
            out_ref[rows, :] = (inter[j] + jnp.concatenate(intra, axis=1)).astype(out_ref.dtype)


def _gla_kernel(qk_ref, vg_ref, b_ref, qkc_ref, vgc_ref, bc_ref, causal_ref, vmask_ref, *rest, t, n, need_ctx):
    if need_ctx:
        of_ref, ob_ref, ocf_ref, ocb_ref, st_ref = rest
    else:
        of_ref, ob_ref, st_ref = rest
    st_ref[...] = jnp.zeros(st_ref.shape, F32)
    nb_ctx = n // GLA_BLOCK
    nb_lat = t // GLA_BLOCK

    def jobs(refs, outs, n_blocks, i):
        return [refs + (outs[0], i, 0), refs + (outs[1], n_blocks - 1 - i, 1)]

    ctx_outs = (ocf_ref, ocb_ref) if need_ctx else (None, None)
    for i in range(nb_ctx):
        _gla_steps(jobs((qkc_ref, vgc_ref, bc_ref), ctx_outs, nb_ctx, i), st_ref, causal_ref, vmask_ref)

    def step(i, carry):
        _gla_steps(jobs((qk_ref, vg_ref, b_ref), (of_ref, ob_ref), nb_lat, i), st_ref, causal_ref, vmask_ref)
        return carry

    lax.fori_loop(0, nb_lat, step, 0, unroll=GLA_UNROLL)


def _gla(gqk, gvg, gb, gqk_c, gvg_c, gb_c, causal, vmask, *, need_ctx):
    bsz, t, _ = gqk.shape
    n = gqk_c.shape[1]
    assert t % GLA_BLOCK == 0 and n % GLA_BLOCK == 0
    row = lambda rows, w_: pl.BlockSpec((None, rows, w_), lambda b: (b, 0, 0))
    out_rows = (t, t, n, n) if need_ctx else (t, t)
    return pl.pallas_call(
        functools.partial(_gla_kernel, t=t, n=n, need_ctx=need_ctx),
        grid=(bsz,),
        in_specs=[row(t, 2 * GLA_QK_PAD), row(t, GLA_V_PAD), row(t, 2 * GLA_QK_PAD),
                  row(n, 2 * GLA_QK_PAD), row(n, GLA_V_PAD), row(n, 2 * GLA_QK_PAD),
                  _const_spec(causal.shape), _const_spec(vmask.shape)],
        out_specs=[row(r, GLA_V_PAD) for r in out_rows],
        out_shape=[jax.ShapeDtypeStruct((bsz, r, GLA_V_PAD), BF16) for r in out_rows],
        scratch_shapes=[pltpu.VMEM((2, GLA_QK_PAD, GLA_DV_PAD), F32)],
        compiler_params=_cparams(("arbitrary",)),
        name="gla_scan",
    )(gqk, gvg, gb, gqk_c, gvg_c, gb_c, causal, vmask)


def _gla_out(of_ref, ob_ref, g_ref, gain_ref, rows):
    heads = []
    for h in range(GLA_HEADS):
        cols = slice(h * GLA_DV_PAD, (h + 1) * GLA_DV_PAD)
        o = of_ref[rows, cols].astype(F32) + ob_ref[rows, cols].astype(F32)
        ms = jnp.sum(o * o, axis=-1, keepdims=True) * (1.0 / GLA_DV)
        g = g_ref[rows, cols].astype(F32)
        heads.append((o * lax.rsqrt(ms + EPS) * gain_ref[:, cols] * (g * jax.nn.sigmoid(g))).astype(BF16))
    return jnp.concatenate(heads, axis=1)


def _post_kernel(x_ref, yfn_ref, yna_ref, of_ref, ob_ref, g_ref, mod_ref, gain_o_ref, gain_ref,
                 wo_ref, w1_ref, w3_ref, w2_ref, o_ref):
    tm = x_ref.shape[0]
    tiles = [slice(r0, r0 + POST_SUB) for r0 in range(0, tm, POST_SUB)]
    ys = [(_dot(yfn_ref[rows, :], wo_ref[0:FN_W, :])
           + _dot(yna_ref[rows, :], wo_ref[FN_W:FN_W + NA_W, :])
           + _dot(_gla_out(of_ref, ob_ref, g_ref, gain_o_ref, rows), wo_ref[FN_W + NA_W:, :]))
          for rows in tiles]
    for rows, y in zip(tiles, ys):
        x1 = x_ref[rows, :] + mod_ref[2:3, :] * y
        h = _modulated_norm(x1, gain_ref[...], mod_ref[3:4, :], mod_ref[4:5, :]).astype(BF16)
        acts = []
        for c in range(D_FF // MXU_W):
            cols = slice(c * MXU_W, (c + 1) * MXU_W)
            u = _dot(h, w1_ref[:, cols])
            g = _dot(h, w3_ref[:, cols])
            acts.append((u * jax.nn.sigmoid(u) * g).astype(BF16))
        acc = _dot(jnp.concatenate(acts, axis=1), w2_ref[...])
        o_ref[rows, :] = x1 + mod_ref[5:6, :] * acc


def _post(x, yfn, yna, o_fwd, o_bwd, gvg, l, mod, mod_row, p, *, tm):
    bsz, t, d = x.shape
    assert t % tm == 0 and tm % POST_SUB == 0
    row = lambda w_: pl.BlockSpec((None, tm, w_), lambda b, i: (b, i, 0))
    gate =pl.BlockSpec((None, tm, GLA_V_PAD), lambda b, i: (b, i, 1))
    return pl.pallas_call(
        _post_kernel,
        grid=(bsz, t // tm),
        in_specs=[row(d), row(FN_W), row(NA_W), row(GLA_V_PAD), row(GLA_V_PAD), gate,
                  _mod_spec(d, l, mod_row), _layer_spec((1, GLA_V_PAD), l),
                  _layer_spec((1, d), l), _layer_spec((MIX_PAD, d), l), _layer_spec((d, D_FF), l),
                  _layer_spec((d, D_FF), l), _layer_spec((D_FF, d), l)],
        out_specs=row(d),
        out_shape=jax.ShapeDtypeStruct((bsz, t, d), F32),
        compiler_params=_cparams(("arbitrary", "arbitrary")),
        name=f"post_{t}",
    )(x, yfn, yna, o_fwd, o_bwd, gvg, mod, p["gain_o"], p["gain_f"], p["wo"], p["w1"], p["w3"], p["w2"])


def _dft_tables(n):
    idx = (np.arange(n)[:, None] * np.arange(n)[None, :]) % n
    ang = 2.0 * np.pi * idx / n
    return np.cos(ang), np.sin(ang)


def _shape_consts(t, n):
    c64, s64 = _dft_tables(FN_GD)
    eye = np.eye(FN_GROUPS)
    cs = np.concatenate([np.kron(eye, c64), np.kron(eye, s64)], axis=1)
    chunk_id = np.arange(SUB) // GLA_CHUNK
    same = chunk_id[:, None] == chunk_id[None, :]
    pos = np.arange(SUB)
    tri = np.stack([same & (pos[None, :] <= pos[:, None]), same & (pos[None, :] >= pos[:, None])])
    cpos = np.arange(GLA_BLOCK)
    fwd = cpos[None, :] <= cpos[:, None]
    gla_causal = np.stack([np.tile(fwd, (1, GLA_HEADS)), np.tile(fwd.T, (1, GLA_HEADS))])
    gla_vmask = ((np.arange(2 * GLA_BLOCK)[:, None] // GLA_BLOCK)
                 == (np.arange(2 * GLA_DV_PAD)[None, :] // GLA_DV_PAD))

    m = GLA_DK // 4
    inv = ROPE_BASE ** (-np.arange(m) / m)
    tok = np.arange(t)
    cos_h = np.ones((t, GLA_DK_PAD))
    sin_lo = np.zeros((t, GLA_DK_PAD))
    sin_hi = np.zeros((t, GLA_DK_PAD))
    for blk, p in enumerate((tok // GRID_W, tok % GRID_W)):
        ang = p[:, None] * inv[None, :]
        o = blk * 2 * m
        cos_h[:, o:o + m] = np.cos(ang)
        cos_h[:, o + m:o + 2 * m] = np.cos(ang)
        sin_lo[:, o:o + m] = -np.sin(ang)
        sin_hi[:, o + m:o + 2 * m] = np.sin(ang)
    rope = np.stack([np.tile(a, (1, GLA_HEADS)) for a in (cos_h, sin_lo, sin_hi)])

    def position_tables(length):
        cos, sin = _dft_tables(length)
        half = length // 2
        return np.stack([cos[:half, 0::2], sin[:half, 0::2], cos[:half, 1::2], sin[:half, 1::2]])

    to_bf16 = lambda a: jnp.asarray(a, F32).astype(BF16)
    return {
        "cs": to_bf16(cs), "tri": jnp.asarray(tri, BF16),
        "gla_causal": jnp.asarray(gla_causal, F32), "gla_vmask": jnp.asarray(gla_vmask, BF16),
        "rope": jnp.asarray(rope, F32),
        "dft_lat": to_bf16(position_tables(t)), "dft_ctx": to_bf16(position_tables(n)),
    }


def _bias_kernel(rpb_ref, onehot_ref, ok_ref, o_ref):
    r = rpb_ref[...]
    onehot = onehot_ref[...]
    t = jnp.zeros(o_ref.shape, F32)
    for _ in range(3):
        piece = r.astype(BF16)
        t = t + _dot(piece, onehot)
        r = r - piece.astype(F32)
    o_ref[...] = jnp.where(ok_ref[...] > 0.0, t * LOG2E, NA_MASKED)


def _window_bias_rows(na_rpb):
    depth, heads, n_dr, n_dc = na_rpb.shape
    cq = np.arange(GRID_W)
    c0 = np.clip(cq - NA_KW // 2, 0, GRID_W - NA_KW)
    col_ok = (cq[None, :] >= c0[:, None]) & (cq[None, :] < c0[:, None] + NA_KW)
    dc = np.clip(cq[None, :] - cq[:, None], -(NA_KW - 1), NA_KW - 1) + NA_KW - 1
    n_dc_pad = n_dc + 1
    n_tab = depth * heads * n_dr
    rows_pad = -(-n_tab // 8) * 8
    onehot = (np.arange(n_dc_pad)[:, None, None] == dc[None]).reshape(n_dc_pad, GRID_W * GRID_W)
    rpb2 = jnp.pad(na_rpb.reshape(n_tab, n_dc), ((0, rows_pad - n_tab), (0, 1)))
    t = pl.pallas_call(
        _bias_kernel,
        out_shape=jax.ShapeDtypeStruct((rows_pad, GRID_W * GRID_W), F32),
        name="na_bias",
    )(rpb2, jnp.asarray(onehot, BF16), jnp.asarray(col_ok.reshape(1, -1), F32))
    t = t[:n_tab].reshape(depth * heads, n_dr, GRID_W, GRID_W)
    t = t.transpose(0, 1, 3, 2).reshape(depth * heads, n_dr * GRID_W, GRID_W)
    margin = NA_WIN - NA_KH
    t = jnp.pad(t, ((0, 0), (margin * GRID_W, margin * GRID_W), (0, 0)))
    return t.reshape(depth, heads, (n_dr + 2 * margin) * GRID_W, GRID_W)


def _pad_heads(w, heads, dim, pad, axis=-1):
    axis = axis % w.ndim
    shape = w.shape
    w = w.reshape(shape[:axis] + (heads, dim) + shape[axis + 1:])
    widths = [(0, 0)] * w.ndim
    widths[axis + 1] = (0, pad - dim)
    return jnp.pad(w, widths).reshape(shape[:axis] + (heads * pad,) + shape[axis + 1:])


def _params(norm_mix, norm_ffn, w_in, fnet_w, na_q_norm, na_k_norm, na_rpb, alpha_w, alpha_b, o_norm, w_out,
            ffn_w1, ffn_w3, ffn_w2):
    depth = w_in.shape[0]
    o = FN_W + 3 * NA_W
    gq = w_in[..., o:o + GLA_QK_W]
    gk = w_in[..., o + GLA_QK_W:o + 2 * GLA_QK_W]
    o2 = o + 2 * GLA_QK_W
    gv = w_in[..., o2:o2 + GLA_V_W]
    gg = w_in[..., o2 + GLA_V_W:o2 + 2 * GLA_V_W]
    ga = w_in[..., o2 + 2 * GLA_V_W:]
    w = jnp.concatenate([
        w_in[..., :o],
        jnp.pad(ga, ((0, 0), (0, 0), (0, GLA_A_PAD - 2 * GLA_RANK))),
        _pad_heads(gq, GLA_HEADS, GLA_DK, GLA_DK_PAD), _pad_heads(gk, GLA_HEADS, GLA_DK, GLA_DK_PAD),
        _pad_heads(gv, GLA_HEADS, GLA_DV, GLA_DV_PAD), _pad_heads(gg, GLA_HEADS, GLA_DV, GLA_DV_PAD),
    ], axis=-1).astype(BF16)

    qkg = jnp.stack([jnp.tile(na_q_norm, (1, NA_HEADS)) * (NA_HD ** -0.5 * LOG2E),
                     jnp.tile(na_k_norm, (1, NA_HEADS))], axis=1)

    aw_pad = _pad_heads(alpha_w, GLA_HEADS, GLA_DK, GLA_DK_PAD)
    aw = jnp.concatenate([jnp.pad(aw_pad[:, 0], ((0, 0), (0, 0), (0, GLA_QK_PAD))),
                          jnp.pad(aw_pad[:, 1], ((0, 0), (0, 0), (GLA_QK_PAD, 0)))], axis=1)
    aw = jnp.pad(aw, ((0, 0), (0, GLA_A_PAD - 2 * GLA_RANK), (0, 0))).astype(BF16)
    ab = _pad_heads(alpha_b, GLA_HEADS, GLA_DK, GLA_DK_PAD).reshape(depth, 1, 2 * GLA_QK_PAD)

    gain_o = _pad_heads(jnp.tile(o_norm, (1, GLA_HEADS)), GLA_HEADS, GLA_DV, GLA_DV_PAD)[:, None, :]
    wo = jnp.concatenate([w_out[:, :FN_W + NA_W],
                          _pad_heads(w_out[:, FN_W + NA_W:], GLA_HEADS, GLA_DV, GLA_DV_PAD, axis=1)],
                         axis=1).astype(BF16)
    return {
        "gain_m": norm_mix[:, None, :], "gain_f": norm_ffn[:, None, :], "w": w, "qkg": qkg, "aw": aw, "ab": ab,
        "rpb_rows": _window_bias_rows(na_rpb), "gain_o": gain_o, "wo": wo, "fw": fnet_w.astype(BF16),
        "w1": ffn_w1.astype(BF16), "w3": ffn_w3.astype(BF16), "w2": ffn_w2.astype(BF16),
    }


def kernel(x, c, ctx, c_ctx, ada_w, ada_b, norm_mix, norm_ffn, w_in, fnet_w, na_q_norm, na_k_norm, na_rpb,
           gla_alpha_w, gla_alpha_b, gla_o_norm, w_out, ffn_w1, ffn_w3, ffn_w2):
    bsz, t, d = x.shape
    n = ctx.shape[1]
    depth = ada_w.shape[0]
    ctx_row = bsz
    assert bsz < MOD_ROWS and n % SUB == 0 and (bsz * n) % CTX_TM == 0
    consts = _shape_consts(t, n)
    p = _params(norm_mix, norm_ffn, w_in, fnet_w, na_q_norm, na_k_norm, na_rpb, gla_alpha_w, gla_alpha_b,
                gla_o_norm, w_out, ffn_w1, ffn_w3, ffn_w2)

    cc = jnp.concatenate([c, c_ctx[None, :], jnp.zeros((MOD_ROWS - bsz - 1, d), F32)])
    mod = _ada_mod(cc, ada_w, ada_b).reshape(depth, MOD_ROWS, 6, d)

    cx = ctx
    flat = lambda a: a.reshape(1, bsz * n, a.shape[-1])
    for l in range(depth):
        need_ctx = l < depth - 1
        uv, nqkv, gqk, gvg, gb = _inproj(x, l, mod, None, p, consts, consts["rope"], tm=INPROJ_TM)
        uv_c, nqkv_c, gqk_c, gvg_c, gb_c = _inproj(cx, l, mod, ctx_row, p, consts, None, tm=n)

        y_fn = _fourier(uv, l, consts["dft_lat"], p["fw"])
        y_na, y_na_c = _na(nqkv, nqkv_c, l, p["rpb_rows"], need_ctx=need_ctx)
        o_gla = _gla(gqk, gvg, gb, gqk_c, gvg_c, gb_c, consts["gla_causal"], consts["gla_vmask"],
                     need_ctx=need_ctx)
        x = _post(x, y_fn, y_na, o_gla[0], o_gla[1], gvg, l, mod, None, p, tm=POST_TM)
        if need_ctx:
            y_fn_c = _fourier(uv_c, l, consts["dft_ctx"], p["fw"])
            cx = _post(flat(cx), flat(y_fn_c), flat(y_na_c), flat(o_gla[2]), flat(o_gla[3]), flat(gvg_c),
                       l, mod, ctx_row, p, tm=CTX_TM).reshape(bsz, n, d)
    return x
```
